```python
import math
import jax, jax.numpy as jnp
from jax import lax
import numpy as np

D_MODEL = 1024
BATCH = 32
SEQ = 2048
DEPTH = 2

CHUNK = 64
N_META = 16
ROPE_THETA = 10000.0
EPS = 1e-6
A_HEADS = 4
A_DK = 128
A_DV = 128
A_CONV = 4
B_HEADS = 4
B_DH = 128
IDX_HEADS = 8
IDX_DIM = 64
TOPK_MAX = 256
TOPK_DIV = 4
Q_BLOCK = 64
D_FF = 2816
FFN_CONV = 3

A_WIDTH = A_HEADS * A_DV
B_WIDTH = B_HEADS * B_DH
IN_SPLITS = (A_HEADS * A_DK, A_HEADS * A_DK, A_WIDTH, A_WIDTH, A_HEADS, A_HEADS,
             B_WIDTH, B_DH, B_DH, IDX_HEADS * IDX_DIM, IDX_DIM, IDX_HEADS)
IN_COLS = sum(IN_SPLITS)
SPLIT_POINTS = tuple(int(c) for c in np.cumsum(IN_SPLITS)[:-1])

kernel_name = "hybrid_gdn_dsa_convffn_meta"


def rms_norm(x, g):
    xf = x.astype(jnp.float32)
    y = xf * lax.rsqrt(jnp.mean(xf * xf, axis=-1, keepdims=True) + EPS)
    return (y * g.astype(jnp.float32)).astype(x.dtype)


def l2_norm(x):
    xf = x.astype(jnp.float32)
    return xf * lax.rsqrt(jnp.sum(xf * xf, axis=-1, keepdims=True) + EPS)


def rope_tables(T, dim):
    inv_freq = 1.0 / (ROPE_THETA ** (jnp.arange(0, dim, 2, dtype=jnp.float32) / dim))
    ang = jnp.arange(T, dtype=jnp.float32)[:, None] * inv_freq[None, :]
    return jnp.cos(ang), jnp.sin(ang)


def apply_rope(x, cos, sin):
    xf = x.astype(jnp.float32)
    x1, x2 = jnp.split(xf, 2, axis=-1)
    c, s = cos[:, None, :], sin[:, None, :]
    return jnp.concatenate([x1 * c - x2 * s, x2 * c + x1 * s], axis=-1).astype(x.dtype)


def causal_dwconv(x, w):
    width = w.shape[0]
    T = x.shape[1]
    xp = jnp.pad(x, ((0, 0), (width - 1, 0), (0, 0)))
    out = xp[:, 0:T] * w[0]
    for i in range(1, width):
        out = out + xp[:, i:i + T] * w[i]
    return out


def gated_deltanet(q, k, v, z, a, b, conv_w, a_log, dt_bias, norm_g):
    f32 = jnp.float32
    Bsz, T, _ = q.shape
    qkv = jax.nn.silu(causal_dwconv(jnp.concatenate([q, k, v], axis=-1), conv_w))
    q, k, v = jnp.split(qkv, [A_HEADS * A_DK, 2 * A_HEADS * A_DK], axis=-1)
    q = l2_norm(q.reshape(Bsz, T, A_HEADS, A_DK)) * (A_DK ** -0.5)
    k = l2_norm(k.reshape(Bsz, T, A_HEADS, A_DK))
    v = v.reshape(Bsz, T, A_HEADS, A_DV).astype(f32)
    g = -jnp.exp(a_log.astype(f32)) * jax.nn.softplus(a.astype(f32) + dt_bias.astype(f32))
    beta = jax.nn.sigmoid(b.astype(f32))
    pad = (-T) % CHUNK
    Tp = T + pad
    N = Tp // CHUNK

    def chunks(t):
        t = jnp.pad(t, ((0, 0), (pad, 0)) + ((0, 0),) * (t.ndim - 2))
        t = t.reshape((Bsz, N, CHUNK) + t.shape[2:])
        return jnp.moveaxis(t, 3, 1)

    qc, kc, vc, gch, bc = chunks(q), chunks(k), chunks(v), chunks(g), chunks(beta)
    gcum = jnp.cumsum(gch, axis=-1)
    pos = jnp.arange(CHUNK)
    incl = pos[:, None] >= pos[None, :]
    strict = pos[:, None] > pos[None, :]
    decay = jnp.exp(jnp.where(incl, gcum[..., :, None] - gcum[..., None, :], -jnp.inf))
    kk = jnp.einsum('bhncd,bhnsd->bhncs', kc, kc)
    m = jnp.where(strict, bc[..., None] * kk * decay, 0.0)
    lhs = m + jnp.eye(CHUNK, dtype=f32)
    rhs = jnp.concatenate([vc * bc[..., None], kc * (bc * jnp.exp(gcum))[..., None]], axis=-1)
    sol = lax.linalg.triangular_solve(lhs, rhs, left_side=True, lower=True, unit_diagonal=True)
    u, w = sol[..., :A_DV], sol[..., A_DV:]
    qk = jnp.einsum('bhncd,bhnsd->bhncs', qc, kc) * decay

    def step(S, inp):
        q_c, k_c, u_c, w_c, g_c, qk_c = inp
        v_new = u_c - jnp.einsum('bhcd,bhde->bhce', w_c, S)
        o = (jnp.einsum('bhcd,bhde->bhce', q_c * jnp.exp(g_c)[..., None], S)
             + jnp.einsum('bhcs,bhse->bhce', qk_c, v_new))
        g_last = g_c[..., -1]
        S = (S * jnp.exp(g_last)[..., None, None]
             + jnp.einsum('bhcd,bhce->bhde', k_c * jnp.exp(g_last[..., None] - g_c)[..., None], v_new))
        return S, o

    xs = tuple(jnp.moveaxis(t, 2, 0) for t in (qc, kc, u, w, gcum, qk))
    S0 = jnp.zeros((Bsz, A_HEADS, A_DK, A_DV), f32)
    _, o = lax.scan(step, S0, xs)
    o = jnp.transpose(o, (1, 0, 3, 2, 4)).reshape(Bsz, Tp, A_HEADS, A_DV)[:, pad:]
    o = rms_norm(o, norm_g) * jax.nn.silu(z.reshape(Bsz, T, A_HEADS, A_DV).astype(f32))
    return o.reshape(Bsz, T, A_WIDTH).astype(z.dtype)


def dsa_attention(q, k, v, iq, ik, iw, q_norm, k_norm, kidx_norm, cos_a, sin_a, cos_i, sin_i):
    f32 = jnp.float32
    Bsz, T, _ = q.shape
    S = T - N_META
    topk = min(TOPK_MAX, S // TOPK_DIV)
    q = apply_rope(rms_norm(q.reshape(Bsz, T, B_HEADS, B_DH), q_norm), cos_a, sin_a)
    k = apply_rope(rms_norm(k.reshape(Bsz, T, 1, B_DH), k_norm), cos_a, sin_a)[:, :, 0]
    iq = apply_rope(iq.reshape(Bsz, T, IDX_HEADS, IDX_DIM), cos_i, sin_i)
    ik = apply_rope(rms_norm(ik.reshape(Bsz, T, 1, IDX_DIM), kidx_norm), cos_i, sin_i)[:, :, 0]
    iw = iw * (IDX_HEADS ** -0.5 * IDX_DIM ** -0.5)
    scale = B_DH ** -0.5

    q_m, k_m, v_m = q[:, :N_META], k[:, :N_META], v[:, :N_META]
    s_mm = jnp.einsum('bqhd,bkd->bhqk', q_m, k_m).astype(f32) * scale
    o_meta = jnp.einsum('bhqk,bkd->bqhd', jax.nn.softmax(s_mm, axis=-1).astype(v.dtype), v_m)

    nb = S // Q_BLOCK
    q_r = q[:, N_META:].reshape(Bsz, nb, Q_BLOCK, B_HEADS, B_DH).swapaxes(0, 1)
    iq_r = iq[:, N_META:].reshape(Bsz, nb, Q_BLOCK, IDX_HEADS, IDX_DIM).swapaxes(0, 1)
    iw_r = iw[:, N_META:].reshape(Bsz, nb, Q_BLOCK, IDX_HEADS).swapaxes(0, 1)
    ik_r = ik[:, N_META:]
    kv_r = jnp.concatenate([k[:, N_META:], v[:, N_META:]], axis=-1)
    key_pos = jnp.arange(S)

    def one_block(args):
        j, qb, iqb, iwb = args
        q_pos = j * Q_BLOCK + jnp.arange(Q_BLOCK)
        limit = (q_pos // CHUNK + 1) * CHUNK
        valid = key_pos[None, :] < limit[:, None]
        logit_i = jnp.einsum('bqhd,bsd->bhqs', iqb, ik_r)
        score = jnp.einsum('bhqs,bqh->bqs', jax.nn.relu(logit_i), iwb).astype(f32)
        score = jnp.where(valid[None], score, -jnp.inf)
        _, sel = lax.top_k(score, topk)
        sel_valid = sel < limit[None, :, None]
        kv_sel = jax.vmap(lambda arr, ix: arr[ix])(kv_r, sel)
        k_sel, v_sel = kv_sel[..., :B_DH], kv_sel[..., B_DH:]
        s_meta = jnp.einsum('bqhd,bmd->bhqm', qb, k_m).astype(f32)
        s_sel = jnp.einsum('bqhd,bqkd->bhqk', qb, k_sel).astype(f32)
        s_sel = jnp.where(sel_valid[:, None], s_sel, -jnp.inf)
        p = jax.nn.softmax(jnp.concatenate([s_meta, s_sel], axis=-1) * scale, axis=-1).astype(v.dtype)
        return (jnp.einsum('bhqm,bmd->bqhd', p[..., :N_META], v_m)
                + jnp.einsum('bhqk,bqkd->bqhd', p[..., N_META:], v_sel))

    o_r = lax.map(one_block, (jnp.arange(nb), q_r, iq_r, iw_r))
    o_r = o_r.swapaxes(0, 1).reshape(Bsz, S, B_HEADS, B_DH)
    return jnp.concatenate([o_meta, o_r], axis=1).reshape(Bsz, T, B_WIDTH)


def hybrid_mixer(n, w_in, conv_a, a_log, dt_bias, a_out_norm, q_norm, k_norm, kidx_norm,
                 w_branch_a, w_branch_b, w_gate, b_gate, w_out, cos_a, sin_a, cos_i, sin_i):
    Bsz, T, D = n.shape
    proj = n @ w_in
    aq, ak, av, az, aa, ab, bq, bk, bv, iq, ik, iw = jnp.split(proj, SPLIT_POINTS, axis=-1)
    o_a = gated_deltanet(aq, ak, av, az, aa, ab, conv_a, a_log, dt_bias, a_out_norm)
    o_b = dsa_attention(bq, bk, bv, iq, ik, iw, q_norm, k_norm, kidx_norm, cos_a, sin_a, cos_i, sin_i)
    gates = jax.nn.sigmoid(n @ w_gate + b_gate).reshape(Bsz, T, 2, D)
    y = gates[:, :, 0] * (o_a @ w_branch_a) + gates[:, :, 1] * (o_b @ w_branch_b)
    return y @ w_out


def conv_gated_mlp(n, w_up, conv_w, w_down):
    gate, up = jnp.split(n @ w_up, 2, axis=-1)
    gate = causal_dwconv(gate, conv_w)
    return (jax.nn.silu(gate) * up) @ w_down


def setup_inputs(seed: int = 0) -> dict:
    key = jax.random.key(seed)
    ks = jax.random.split(key, 24)
    L, D = DEPTH, D_MODEL
    f32 = jnp.float32

    def nrm(k, shape, scale):
        return jax.random.normal(k, shape, f32) * scale

    def gain(k, shape):
        return 1.0 + nrm(k, shape, 0.02)

    a_init = jax.random.uniform(ks[5], (L, A_HEADS), f32, 1.0, 16.0)
    dt = jnp.exp(jax.random.uniform(ks[6], (L, A_HEADS), f32, math.log(1e-3), math.log(1e-1)))
    return {
        "x": nrm(ks[0], (BATCH, SEQ, D), 1.0),
        "meta_tokens": nrm(ks[1], (N_META, D), 0.5),
        "norm_mix": gain(ks[2], (L, D)),
        "w_in": nrm(ks[3], (L, D, IN_COLS), D ** -0.5),
        "conv_a": nrm(ks[4], (L, A_CONV, 2 * A_HEADS * A_DK + A_WIDTH), 0.5),
        "a_log": jnp.log(a_init),
        "dt_bias": dt + jnp.log(-jnp.expm1(-dt)),
        "a_out_norm": gain(ks[7], (L, A_DV)),
        "q_norm": gain(ks[8], (L, B_DH)),
        "k_norm": gain(ks[9], (L, B_DH)),
        "kidx_norm": gain(ks[10], (L, IDX_DIM)),
        "w_branch_a": nrm(ks[11], (L, A_WIDTH, D), A_WIDTH ** -0.5),
        "w_branch_b": nrm(ks[12], (L, B_WIDTH, D), B_WIDTH ** -0.5),
        "w_gate": nrm(ks[13], (L, D, 2 * D), D ** -0.5),
        "b_gate": nrm(ks[14], (L, 2 * D), 0.1),
        "w_out": nrm(ks[15], (L, D, D), D ** -0.5),
        "norm_ffn": gain(ks[16], (L, D)),
        "w_up": nrm(ks[17], (L, D, 2 * D_FF), D ** -0.5),
        "conv_ffn": nrm(ks[18], (L, FFN_CONV, D_FF), 0.5),
        "w_down": nrm(ks[19], (L, D_FF, D), D_FF ** -0.5),
    }


def reference(x, meta_tokens, norm_mix, w_in, conv_a, a_log, dt_bias, a_out_norm, q_norm, k_norm,
              kidx_norm, w_branch_a, w_branch_b, w_gate, b_gate, w_out, norm_ffn, w_up, conv_ffn,
              w_down) -> jnp.ndarray:
    Bsz = x.shape[0]
    T = N_META + x.shape[1]
    cos_a, sin_a = rope_tables(T, B_DH)
    cos_i, sin_i = rope_tables(T, IDX_DIM)
    meta = jnp.broadcast_to(meta_tokens.astype(x.dtype)[None], (Bsz, N_META, x.shape[2]))
    h = jnp.concatenate([meta, x], axis=1)
    for l in range(DEPTH):
        n = rms_norm(h, norm_mix[l])
        h = h + hybrid_mixer(n, w_in[l], conv_a[l], a_log[l], dt_bias[l], a_out_norm[l], q_norm[l],
                             k_norm[l], kidx_norm[l], w_branch_a[l], w_branch_b[l], w_gate[l], b_gate[l],
                             w_out[l], cos_a, sin_a, cos_i, sin_i).astype(h.dtype)
        n = rms_norm(h, norm_ffn[l])
        h = h + conv_gated_mlp(n, w_up[l], conv_ffn[l], w_down[l]).astype(h.dtype)
    return h[:, N_META:]
```

```python
import functools
import math

import jax
import jax.numpy as jnp
import numpy as np
from jax import lax
from jax.experimental import pallas as pl
from jax.experimental.pallas import tpu as pltpu

F32 = jnp.float32
BF16 = jnp.bfloat16
I32 = jnp.int32

CHUNK = 64
N_META = 16
PAD = CHUNK - N_META
ROPE_THETA = 10000.0
EPS = 1e-6
A_HEADS = 4
A_DK = 128
A_CONV = 4
B_HEADS = 4
B_DH = 128
IDX_HEADS = 8
IDX_DIM = 64
TOPK_MAX = 256
TOPK_DIV = 4
FFN_CONV = 3
LANES = 128
KEY_CHUNK = 256
FF_CHUNK = 256
INT_MIN = -(2 ** 31)
VMEM_LIMIT_BYTES = 56 * 1024 * 1024

NT_DIMS = (((1,), (1,)), ((), ()))


def _dot(a, b):
    return jnp.dot(a, b, preferred_element_type=F32)


def _dot_nt(a, b):
    return lax.dot_general(a, b, NT_DIMS, preferred_element_type=F32)


def _dot_exact(a, b):
    return jnp.dot(a, b, preferred_element_type=F32, precision=lax.Precision.HIGHEST)


def _rms(x, gain):
    return x * lax.rsqrt(jnp.mean(x * x, axis=-1, keepdims=True) + EPS) * gain


def _sigmoid(x):
    return 1.0 / (1.0 + jnp.exp(-x))


def _silu(x):
    return x * _sigmoid(x)


def _softplus(x):
    return jnp.maximum(x, 0.0) + jnp.log(1.0 + jnp.exp(-jnp.abs(x)))


def _row_block(n_chunks):
    g = max(d for d in range(1, 12) if n_chunks % d == 0)
    return g * CHUNK


def _params(sem):
    return pltpu.CompilerParams(dimension_semantics=sem, vmem_limit_bytes=VMEM_LIMIT_BYTES)


def _const_spec(shape):
    nd = len(shape)
    return pl.BlockSpec(shape, lambda *_: (0,) * nd)


PROJ_WIDTHS = (1536, 512, 128, 512, 256, 512, 128)


def _proj_body(h_ref, g_ref, w_ref, cosa_ref, sina_ref, cosi_ref, sinlo_ref, sinhi_ref,
               qn_ref, kn_ref, kin_ref,
               qkv_ref, z_ref, sm_ref, qr_ref, kr_ref, v_ref, iqr_ref, ike_ref, iko_ref):
    n = _rms(h_ref[0], g_ref[...]).astype(BF16)
    offs = np.cumsum((0,) + PROJ_WIDTHS)

    def mm(i):
        return _dot(n, w_ref[:, offs[i]:offs[i + 1]])

    qkv_ref[0] = mm(0)
    z_ref[0] = mm(1)
    sm_ref[0] = mm(2)

    ca, sa = cosa_ref[...], sina_ref[...]
    q = mm(3)
    parts = []
    for hd in range(B_HEADS):
        qh = _rms(q[:, hd * B_DH:(hd + 1) * B_DH], qn_ref[...])
        parts.append(qh * ca + pltpu.roll(qh, B_DH // 2, 1) * sa)
    qr_ref[0] = jnp.concatenate(parts, axis=1).astype(BF16)

    kv = mm(4)
    k = _rms(kv[:, :B_DH], kn_ref[...])
    kr_ref[0] = (k * ca + pltpu.roll(k, B_DH // 2, 1) * sa).astype(BF16)
    v_ref[0] = kv[:, B_DH:].astype(BF16)

    ci, slo, shi = cosi_ref[...], sinlo_ref[...], sinhi_ref[...]
    iq = mm(5)
    parts = []
    for hp in range(IDX_HEADS // 2):
        x = iq[:, hp * LANES:(hp + 1) * LANES]
        parts.append(x * ci + pltpu.roll(x, LANES - IDX_DIM // 2, 1) * slo + pltpu.roll(x, IDX_DIM // 2, 1) * shi)
    iqr_ref[0] = jnp.concatenate(parts, axis=1).astype(BF16)

    ik = _rms(mm(6), kin_ref[...])
    ik = ik * ci + pltpu.roll(ik, IDX_DIM // 2, 1) * (slo + shi)
    lane = lax.broadcasted_iota(I32, (1, LANES), 1)
    ike_ref[0] = jnp.where(lane < IDX_DIM, ik, 0.0).astype(BF16)
    iko_ref[0] = jnp.where(lane >= IDX_DIM, ik, 0.0).astype(BF16)


def _proj_call(h, gain, w, tables, qn, kn, kin, rb):
    bsz, tp, d = h.shape
    nt = tp // rb
    row = lambda w_, dt: (jax.ShapeDtypeStruct((bsz, tp, w_), dt),
                          pl.BlockSpec((1, rb, w_), lambda b, t: (b, t, 0)))
    outs = [row(1536, F32), row(512, F32), row(128, F32), row(512, BF16), row(128, BF16),
            row(128, BF16), row(512, BF16), row(128, BF16), row(128, BF16)]
    tab_spec = pl.BlockSpec((rb, LANES), lambda b, t: (t, 0))
    return pl.pallas_call(
        _proj_body,
        grid=(bsz, nt),
        in_specs=[pl.BlockSpec((1, rb, d), lambda b, t: (b, t, 0)), _const_spec((1, d)), _const_spec(w.shape)]
                 + [tab_spec] * 5 + [_const_spec((1, LANES))] * 3,
        out_specs=[o[1] for o in outs],
        out_shape=[o[0] for o in outs],
        compiler_params=_params(("parallel", "arbitrary")),
        name="proj",
    )(h, gain, w, *tables, qn, kn, kin)


def _gdn_body(qkv_ref, z_ref, sm_ref, cw_ref, alog_ref, dtb_ref, ng_ref, ltri_ref, eg_ref, eb_ref,
              out_ref, xe_ref, s01_ref, s23_ref, *, rb):
    t = pl.program_id(1)
    nwide = 3 * A_HEADS * A_DK

    @pl.when(t == 0)
    def _():
        xe_ref[0:8, :] = jnp.zeros((8, nwide), F32)
        s01_ref[...] = jnp.zeros_like(s01_ref)
        s23_ref[...] = jnp.zeros_like(s23_ref)

    xe_ref[8:8 + rb, :] = qkv_ref[0]

    hw = A_HEADS * A_DK
    pw = A_HEADS * CHUNK
    ii = lax.broadcasted_iota(I32, (CHUNK, pw), 0)
    jj = lax.broadcasted_iota(I32, (CHUNK, pw), 1) & (CHUNK - 1)
    colhead = lax.broadcasted_iota(I32, (CHUNK, pw), 1) >> 6
    eye_p = ii == jj
    bd_mask = (lax.broadcasted_iota(I32, (pw, pw), 0) >> 6) == (lax.broadcasted_iota(I32, (pw, pw), 1) >> 6)
    bdk_mask = (lax.broadcasted_iota(I32, (pw, hw), 0) >> 6) == (lax.broadcasted_iota(I32, (pw, hw), 1) >> 7)
    pair_mask = (lax.broadcasted_iota(I32, (pw, pw), 0) >> 7) == (lax.broadcasted_iota(I32, (pw, pw), 1) >> 7)
    lane = lax.broadcasted_iota(I32, (1, LANES), 1)
    rows64 = lax.broadcasted_iota(I32, (CHUNK, 1), 0)
    cw = cw_ref[...]
    neg_a = -jnp.exp(alog_ref[...])
    dtb = dtb_ref[...]
    ng = ng_ref[...]

    def block_diag(xp):
        return jnp.where(bd_mask, jnp.concatenate([xp.astype(BF16)] * A_HEADS, axis=0), 0)

    def stack_heads(xp):
        return jnp.concatenate([jnp.where(colhead == hd, xp, 0.0) for hd in range(A_HEADS)], axis=0)

    def l2n(x):
        parts = []
        for hd in range(A_HEADS):
            xh = x[:, hd * A_DK:(hd + 1) * A_DK]
            parts.append(xh * lax.rsqrt(jnp.sum(xh * xh, axis=-1, keepdims=True) + EPS))
        return jnp.concatenate(parts, axis=1)

    def chunk_step(c, carry):
        r0 = pl.multiple_of(c * CHUNK, CHUNK)
        xw = xe_ref[pl.ds(r0, CHUNK + 8), :]
        acc = pltpu.roll(xw, 3, 0)[8:] * cw[0:1]
        acc = acc + pltpu.roll(xw, 2, 0)[8:] * cw[1:2]
        acc = acc + pltpu.roll(xw, 1, 0)[8:] * cw[2:3]
        acc = acc + xw[8:] * cw[3:4]
        xq = _silu(acc)
        q = l2n(xq[:, :hw]) * (A_DK ** -0.5)
        k = l2n(xq[:, hw:2 * hw])
        v = xq[:, 2 * hw:]

        sm = sm_ref[0, pl.ds(r0, CHUNK), :]
        g = jnp.where(lane < A_HEADS, neg_a * _softplus(sm + dtb), 0.0)
        g = jnp.where(t * rb + r0 + rows64 >= PAD, g, 0.0)
        gcs = _dot_exact(ltri_ref[...], g)
        gx = _dot_exact(gcs, eg_ref[...])
        bx = _dot_exact(_sigmoid(sm), eb_ref[...])
        gp, gq = gx[:, :pw], gx[:, pw:]
        bp, bq = bx[:, :pw], bx[:, pw:]
        grow = jnp.sum(jnp.where(eye_p, gp, 0.0), axis=0, keepdims=True)
        decay = jnp.where(ii >= jj, jnp.exp(gp - grow), 0.0)

        kb = k.astype(BF16)
        bdk = jnp.where(bdk_mask, jnp.concatenate([kb] * A_HEADS, axis=0), 0)
        kq = _dot_nt(jnp.concatenate([kb, q.astype(BF16)], axis=0), bdk)
        kk_p, qk_p = kq[:CHUNK], kq[CHUNK:]

        a = -jnp.where(ii > jj, bp * kk_p * decay, 0.0)
        tinv = jnp.where(eye_p, 1.0, 0.0) + a
        pk = _dot(a.astype(BF16), block_diag(a))
        for _ in range(4):
            pt = _dot(jnp.concatenate([pk, tinv], axis=0).astype(BF16), block_diag(pk))
            tinv = tinv + pt[CHUNK:]
            pk = pt[:CHUNK]
        tinv = tinv + _dot(tinv.astype(BF16), block_diag(pk))

        eg = jnp.exp(gq)
        vb = v * bq
        kbg = k * bq * eg
        rv = jnp.concatenate(
            [jnp.concatenate([vb[:, hd * A_DK:(hd + 1) * A_DK], kbg[:, hd * A_DK:(hd + 1) * A_DK]], axis=1)
             for hd in range(A_HEADS)], axis=0)
        uw = _dot(stack_heads(tinv).astype(BF16), rv.astype(BF16))
        u, w = uw[:, :A_DK], uw[:, A_DK:]

        qg = q * eg
        glast = gq[CHUNK - 1:CHUNK, :]
        kd = k * jnp.exp(glast - gq)
        egl = jnp.exp(glast)
        qkd = stack_heads(qk_p * decay).astype(BF16)

        def pair_lanes(x, a_, b_):
            return jnp.concatenate([x[a_ * CHUNK:(a_ + 1) * CHUNK], x[b_ * CHUNK:(b_ + 1) * CHUNK]], axis=1)

        def unpair(r):
            return [r[:, :A_DK], r[:, A_DK:]]

        s01 = s01_ref[...]
        s23 = s23_ref[...]
        r01 = _dot(jnp.concatenate([pair_lanes(w, 0, 1), qg[:, :2 * A_DK]], axis=0).astype(BF16), s01.astype(BF16))
        r23 = _dot(jnp.concatenate([pair_lanes(w, 2, 3), qg[:, 2 * A_DK:]], axis=0).astype(BF16), s23.astype(BF16))
        ws = jnp.concatenate(unpair(r01[:CHUNK]) + unpair(r23[:CHUNK]), axis=0)
        qs = jnp.concatenate(unpair(r01[CHUNK:]) + unpair(r23[CHUNK:]), axis=0)
        vn = u - ws
        vnb = vn.astype(BF16)
        o_rs = qs + _dot(qkd, vnb)

        zeros = jnp.zeros((CHUNK, 2 * A_DK), F32)

        def state_update(s_old, kd_pair, vn_pair, egl_pair):
            kdt = jnp.concatenate([kd_pair, zeros], axis=0).T
            upd = _dot(kdt.astype(BF16), jnp.concatenate([vn_pair, zeros], axis=0).astype(BF16))
            return s_old * egl_pair + jnp.where(pair_mask, upd, 0.0)

        s01_ref[...] = state_update(s01, kd[:, :2 * A_DK], pair_lanes(vn, 0, 1), egl[:, :2 * A_DK])
        s23_ref[...] = state_update(s23, kd[:, 2 * A_DK:], pair_lanes(vn, 2, 3), egl[:, 2 * A_DK:])

        zz = z_ref[0, pl.ds(r0, CHUNK), :]
        parts = []
        for hd in range(A_HEADS):
            parts.append(_rms(o_rs[hd * CHUNK:(hd + 1) * CHUNK], ng))
        out_ref[0, pl.ds(r0, CHUNK), :] = jnp.concatenate(parts, axis=1) * _silu(zz)
        return carry

    lax.fori_loop(0, rb // CHUNK, chunk_step, 0)
    xe_ref[0:8, :] = xe_ref[rb:rb + 8, :]


def _gdn_call(qkv, z, sm, cw, alog, dtb, ng, ltri, eg, eb, rb):
    bsz, tp, _ = qkv.shape
    nt = tp // rb
    hw = A_HEADS * A_DK
    row = lambda w_: pl.BlockSpec((1, rb, w_), lambda b, t: (b, t, 0))
    return pl.pallas_call(
        functools.partial(_gdn_body, rb=rb),
        grid=(bsz, nt),
        in_specs=[row(3 * hw), row(hw), row(LANES), _const_spec(cw.shape), _const_spec((1, LANES)),
                  _const_spec((1, LANES)), _const_spec((1, LANES)), _const_spec(ltri.shape),
                  _const_spec(eg.shape), _const_spec(eb.shape)],
        out_specs=row(hw),
        out_shape=jax.ShapeDtypeStruct((bsz, tp, hw), F32),
        scratch_shapes=[pltpu.VMEM((rb + 8, 3 * hw), F32),
                        pltpu.VMEM((2 * A_DK, 2 * A_DK), F32), pltpu.VMEM((2 * A_DK, 2 * A_DK), F32)],
        compiler_params=_params(("parallel", "arbitrary")),
        name="gdn",
    )(qkv, z, sm, cw, alog, dtb, ng, ltri, eg, eb)


def _dsa_body(qr_ref, kr_ref, v_ref, iqr_ref, ike_ref, iko_ref, sm_ref, lstrict_ref, out_ref,
              vt_ref, vmt_ref, skey_ref, bias_ref, st_ref, *, seq, topk):
    p = pl.program_id(1)
    scale = B_DH ** -0.5
    iw_scale = IDX_HEADS ** -0.5 * IDX_DIM ** -0.5
    qpair = 2 * CHUNK
    neg_inf = -jnp.inf

    @pl.when(p == 0)
    def _():
        def vblk(i, carry):
            r = pl.multiple_of(CHUNK + i * LANES, CHUNK)
            vt_ref[i] = v_ref[0, pl.ds(r, LANES), :].astype(F32).T.astype(BF16)
            return carry

        lax.fori_loop(0, seq // LANES, vblk, 0)
        vmt_ref[...] = v_ref[0, 0:LANES, :].astype(F32).T.astype(BF16)

        qm = qr_ref[0, 0:CHUNK, :]
        km = kr_ref[0, 0:CHUNK, :]
        vm = v_ref[0, 0:CHUNK, :]
        colv = lax.broadcasted_iota(I32, (1, CHUNK), 1) >= PAD
        parts = []
        for hd in range(B_HEADS):
            s = _dot_nt(qm[:, hd * B_DH:(hd + 1) * B_DH], km) * scale
            s = jnp.where(colv, s, neg_inf)
            e = jnp.exp(s - jnp.max(s, axis=-1, keepdims=True))
            pr = e / jnp.sum(e, axis=-1, keepdims=True)
            parts.append(_dot(pr.astype(BF16), vm))
        out_ref[0, 0:CHUNK, :] = jnp.concatenate(parts, axis=1)

    r0 = pl.multiple_of(CHUNK + p * qpair, CHUNK)
    qb = qr_ref[0, pl.ds(r0, qpair), :]
    q_hq = jnp.concatenate([qb[:, hd * B_DH:(hd + 1) * B_DH] for hd in range(B_HEADS)], axis=0)
    ib = iqr_ref[0, pl.ds(r0, qpair), :]
    iq_hq = jnp.concatenate([ib[:, hp * LANES:(hp + 1) * LANES] for hp in range(IDX_HEADS // 2)], axis=0)
    smt = sm_ref[0, pl.ds(r0, qpair), :].T
    wts = [smt[2 * A_HEADS + hd:2 * A_HEADS + hd + 1, :] * iw_scale for hd in range(IDX_HEADS)]
    lane = lax.broadcasted_iota(I32, (1, LANES), 1)
    limit = jnp.where(lane < CHUNK, (2 * p + 1) * CHUNK, (2 * p + 2) * CHUNK)
    nkc = (p + 2) // 2
    krows = lax.broadcasted_iota(I32, (KEY_CHUNK, 1), 0)

    def score_chunk(kc, carry):
        k0 = pl.multiple_of(CHUNK + kc * KEY_CHUNK, CHUNK)
        le = _dot_nt(ike_ref[0, pl.ds(k0, KEY_CHUNK), :], iq_hq)
        lo = _dot_nt(iko_ref[0, pl.ds(k0, KEY_CHUNK), :], iq_hq)
        sc = jnp.zeros((KEY_CHUNK, LANES), F32)
        for hp in range(IDX_HEADS // 2):
            sc = sc + wts[2 * hp] * jnp.maximum(le[:, hp * LANES:(hp + 1) * LANES], 0.0)
            sc = sc + wts[2 * hp + 1] * jnp.maximum(lo[:, hp * LANES:(hp + 1) * LANES], 0.0)
        bits = lax.bitcast_convert_type(sc, I32)
        bits = jnp.where(sc == 0.0, 0, bits)
        key = bits ^ ((bits >> 31) & 0x7FFFFFFF)
        valid = kc * KEY_CHUNK + krows < limit
        skey_ref[pl.ds(pl.multiple_of(kc * KEY_CHUNK, KEY_CHUNK), KEY_CHUNK), :] = jnp.where(valid, key, INT_MIN)
        return carry

    lax.fori_loop(0, nkc, score_chunk, 0)

    def count_ge(thr):
        def body(i, acc):
            blk = skey_ref[pl.ds(pl.multiple_of(i * CHUNK, CHUNK), CHUNK), :]
            hit = jnp.where(blk >= thr, 1, 0).astype(I32)
            return acc + jnp.sum(hit.reshape(8, 8, LANES), axis=0)

        acc = lax.fori_loop(0, nkc * (KEY_CHUNK // CHUNK), body, jnp.zeros((8, LANES), I32))
        return jnp.sum(acc, axis=0, keepdims=True)

    def bit_body(i, tu):
        cand = tu | jnp.left_shift(jnp.int32(1), 31 - i)
        return jnp.where(count_ge(cand ^ INT_MIN) >= topk, cand, tu)

    tu = lax.fori_loop(0, 32, bit_body, jnp.zeros((1, LANES), I32))
    thr = jnp.maximum(tu ^ INT_MIN, INT_MIN + 1)
    n_ge = count_ge(thr)
    has_tie = jnp.max(n_ge) > topk

    @pl.when(jnp.logical_not(has_tie))
    def _():
        def body(kc, carry):
            rows = pl.ds(pl.multiple_of(kc * KEY_CHUNK, KEY_CHUNK), KEY_CHUNK)
            bias_ref[rows, :] = jnp.where(skey_ref[rows, :] >= thr, 0.0, neg_inf)
            return carry

        lax.fori_loop(0, nkc, body, 0)

    @pl.when(has_tie)
    def _():
        need = (topk - count_ge(thr + 1)).astype(F32)

        def body(kc, seen):
            rows = pl.ds(pl.multiple_of(kc * KEY_CHUNK, KEY_CHUNK), KEY_CHUNK)
            blk = skey_ref[rows, :]
            eq = jnp.where(blk == thr, 1.0, 0.0)
            rank = _dot(lstrict_ref[...], eq.astype(BF16)) + seen
            keep = (blk > thr) | ((blk == thr) & (rank < need))
            bias_ref[rows, :] = jnp.where(keep, 0.0, neg_inf)
            return seen + jnp.sum(eq, axis=0, keepdims=True)

        lax.fori_loop(0, nkc, body, jnp.zeros((1, LANES), F32))

    def logit_chunk(kc, m8):
        k0 = pl.multiple_of(CHUNK + kc * KEY_CHUNK, CHUNK)
        rows = pl.ds(pl.multiple_of(kc * KEY_CHUNK, KEY_CHUNK), KEY_CHUNK)
        s = _dot_nt(kr_ref[0, pl.ds(k0, KEY_CHUNK), :], q_hq) * scale
        s = s + jnp.concatenate([bias_ref[rows, :]] * B_HEADS, axis=1)
        st_ref[rows, :] = s
        return jnp.maximum(m8, jnp.max(s.reshape(KEY_CHUNK // 8, 8, B_HEADS * LANES), axis=0))

    m8 = lax.fori_loop(0, nkc, logit_chunk, jnp.full((8, B_HEADS * LANES), neg_inf, F32))
    mrows = lax.broadcasted_iota(I32, (LANES, 1), 0)
    s_meta = _dot_nt(kr_ref[0, 0:LANES, :], q_hq) * scale
    s_meta = jnp.where((mrows >= PAD) & (mrows < CHUNK), s_meta, neg_inf)
    m = jnp.maximum(jnp.max(m8, axis=0, keepdims=True), jnp.max(s_meta, axis=0, keepdims=True))

    p_meta = jnp.exp(s_meta - m)
    l0 = jnp.sum(p_meta, axis=0, keepdims=True)
    o0 = _dot(vmt_ref[...], p_meta.astype(BF16))

    def pv_chunk(kc, carry):
        l, o_t = carry
        rows = pl.ds(pl.multiple_of(kc * KEY_CHUNK, KEY_CHUNK), KEY_CHUNK)
        pr = jnp.exp(st_ref[rows, :] - m)
        vblk = jnp.concatenate([vt_ref[2 * kc], vt_ref[2 * kc + 1]], axis=1)
        return l + jnp.sum(pr, axis=0, keepdims=True), o_t + _dot(vblk, pr.astype(BF16))

    l, o_t = lax.fori_loop(0, nkc, pv_chunk, (l0, o0))
    o_hq = (o_t / l).T
    for hd in range(B_HEADS):
        out_ref[0, pl.ds(r0, qpair), hd * B_DH:(hd + 1) * B_DH] = o_hq[hd * LANES:(hd + 1) * LANES, :]


def _dsa_call(qr, kr, v, iqr, ike, iko, sm, lstrict, seq, topk):
    bsz, tp, _ = qr.shape
    npairs = seq // (2 * CHUNK)
    full = lambda w_: pl.BlockSpec((1, tp, w_), lambda b, p: (b, 0, 0))
    hw = B_HEADS * B_DH
    return pl.pallas_call(
        functools.partial(_dsa_body, seq=seq, topk=topk),
        grid=(bsz, npairs),
        in_specs=[full(hw), full(B_DH), full(B_DH), full(hw), full(LANES), full(LANES), full(LANES),
                  _const_spec(lstrict.shape)],
        out_specs=full(hw),
        out_shape=jax.ShapeDtypeStruct((bsz, tp, hw), F32),
        scratch_shapes=[pltpu.VMEM((seq // LANES, B_DH, LANES), BF16), pltpu.VMEM((B_DH, LANES), BF16),
                        pltpu.VMEM((seq, LANES), I32), pltpu.VMEM((seq, LANES), F32),
                        pltpu.VMEM((seq, hw), F32)],
        compiler_params=_params(("parallel", "arbitrary")),
        name="dsa",
    )(qr, kr, v, iqr, ike, iko, sm, lstrict)


def _mix_body(h_ref, oa_ref, ob_ref, g_ref, wg_ref, bg_ref, wa_ref, wb_ref, wo_ref, out_ref, *, rb):
    x = h_ref[0]
    d = x.shape[-1]
    n = _rms(x, g_ref[...]).astype(BF16)
    gates = _sigmoid(_dot(n, wg_ref[...]) + bg_ref[...])
    y = gates[:, :d] * _dot(oa_ref[0].astype(BF16), wa_ref[...]) + gates[:, d:] * _dot(ob_ref[0].astype(BF16), wb_ref[...])
    out = x + _dot(y.astype(BF16), wo_ref[...])
    rows = pl.program_id(1) * rb + lax.broadcasted_iota(I32, (rb, 1), 0)
    out_ref[0] = jnp.where(rows >= PAD, out, 0.0)


def _mix_call(h, oa, ob, gain, wg, bg, wa, wb, wo, rb):
    bsz, tp, d = h.shape
    row = lambda w_: pl.BlockSpec((1, rb, w_), lambda b, t: (b, t, 0))
    return pl.pallas_call(
        functools.partial(_mix_body, rb=rb),
        grid=(bsz, tp // rb),
        in_specs=[row(d), row(oa.shape[-1]), row(ob.shape[-1]), _const_spec((1, d)), _const_spec(wg.shape),
                  _const_spec(bg.shape), _const_spec(wa.shape), _const_spec(wb.shape), _const_spec(wo.shape)],
        out_specs=row(d),
        out_shape=jax.ShapeDtypeStruct((bsz, tp, d), F32),
        compiler_params=_params(("parallel", "arbitrary")),
        name="mix",
    )(h, oa, ob, gain, wg, bg, wa, wb, wo)


def _ffn_body(h_ref, g_ref, wup_ref, cw_ref, wd_ref, out_ref, carry_ref, *, rb, dff):
    t = pl.program_id(1)

    @pl.when(t == 0)
    def _():
        carry_ref[...] = jnp.zeros_like(carry_ref)

    x = h_ref[0]
    n = _rms(x, g_ref[...]).astype(BF16)
    rows = lax.broadcasted_iota(I32, (rb, 1), 0)
    acc = jnp.zeros(x.shape, F32)
    for c in range(dff // FF_CHUNK):
        cols = slice(c * FF_CHUNK, (c + 1) * FF_CHUNK)
        gate = _dot(n, wup_ref[:, cols])
        up = _dot(n, wup_ref[:, dff + c * FF_CHUNK:dff + (c + 1) * FF_CHUNK])
        prev = carry_ref[:, cols]
        g1 = jnp.where(rows == 0, prev[7:8], pltpu.roll(gate, 1, 0))
        g2 = jnp.where(rows == 0, prev[6:7], jnp.where(rows == 1, prev[7:8], pltpu.roll(gate, 2, 0)))
        cw = cw_ref[:, cols]
        conv = g2 * cw[0:1] + g1 * cw[1:2] + gate * cw[2:3]
        carry_ref[:, cols] = gate[rb - 8:rb]
        acc = acc + _dot((_silu(conv) * up).astype(BF16), wd_ref[cols, :])
    out = x + acc
    out_ref[0] = jnp.where(t * rb + rows >= PAD, out, 0.0)


def _ffn_call(h, gain, wup, cw, wd, rb):
    bsz, tp, d = h.shape
    dff = wd.shape[0]
    row = pl.BlockSpec((1, rb, d), lambda b, t: (b, t, 0))
    return pl.pallas_call(
        functools.partial(_ffn_body, rb=rb, dff=dff),
        grid=(bsz, tp // rb),
        in_specs=[row, _const_spec((1, d)), _const_spec(wup.shape), _const_spec(cw.shape), _const_spec(wd.shape)],
        out_specs=row,
        out_shape=jax.ShapeDtypeStruct((bsz, tp, d), F32),
        scratch_shapes=[pltpu.VMEM((8, dff), F32)],
        compiler_params=_params(("parallel", "arbitrary")),
        name="ffn",
    )(h, gain, wup, cw, wd)


def _rope_tables(tp):
    pos = (jnp.arange(tp, dtype=F32) - PAD)[:, None]

    def cs(dim):
        inv = 1.0 / (ROPE_THETA ** (jnp.arange(0, dim, 2, dtype=F32) / dim))
        ang = pos * inv[None, :]
        return jnp.cos(ang), jnp.sin(ang)

    ca, sa = cs(B_DH)
    ci, si = cs(IDX_DIM)
    zi = jnp.zeros_like(si)
    return (jnp.concatenate([ca, ca], axis=1), jnp.concatenate([-sa, sa], axis=1),
            jnp.concatenate([ci] * 4, axis=1), jnp.concatenate([-si, zi, -si, zi], axis=1),
            jnp.concatenate([zi, si, zi, si], axis=1))


def _pad_lanes(vec):
    return jnp.zeros((1, LANES), F32).at[0, :vec.shape[0]].set(vec.astype(F32))


def _head_expanders():
    pw, hw = A_HEADS * CHUNK, A_HEADS * A_DK
    head = np.concatenate([np.arange(pw) // CHUNK, np.arange(hw) // A_DK])
    src = np.arange(LANES)[:, None]
    return (jnp.asarray(src == head[None, :], F32), jnp.asarray(src == A_HEADS + head[None, :], F32))


def kernel(x, meta_tokens, norm_mix, w_in, conv_a, a_log, dt_bias, a_out_norm, q_norm, k_norm, kidx_norm,
           w_branch_a, w_branch_b, w_gate, b_gate, w_out, norm_ffn, w_up, conv_ffn, w_down):
    bsz, seq, d = x.shape
    depth = w_in.shape[0]
    assert seq % KEY_CHUNK == 0
    tp = CHUNK + seq
    rb = _row_block(tp // CHUNK)
    topk = min(TOPK_MAX, seq // TOPK_DIV)

    meta = jnp.broadcast_to(meta_tokens.astype(x.dtype)[None], (bsz, N_META, d))
    h = jnp.concatenate([jnp.zeros((bsz, PAD, d), x.dtype), meta, x], axis=1)

    tables = _rope_tables(tp)
    eg, eb = _head_expanders()
    tri = np.arange(CHUNK)
    ltri = jnp.asarray(tri[:, None] >= tri[None, :], F32)
    trk = np.arange(KEY_CHUNK)
    lstrict = jnp.asarray(trk[:, None] > trk[None, :], BF16)

    hw = A_HEADS * A_DK
    o = np.cumsum((0, hw, hw, hw, hw, A_HEADS, A_HEADS, B_HEADS * B_DH, B_DH, B_DH,
                   IDX_HEADS * IDX_DIM, IDX_DIM, IDX_HEADS))
    for l in range(depth):
        w = w_in[l]
        small = jnp.concatenate([w[:, o[4]:o[6]], w[:, o[11]:o[12]],
                                 jnp.zeros((d, LANES - 2 * A_HEADS - IDX_HEADS), w.dtype)], axis=1)
        w_p = jnp.concatenate([w[:, o[0]:o[3]], w[:, o[3]:o[4]], small, w[:, o[6]:o[7]], w[:, o[7]:o[9]],
                               w[:, o[9]:o[10]], w[:, o[10]:o[11]], w[:, o[10]:o[11]]], axis=1).astype(BF16)
        qkv, z, sm, qr, kr, v, iqr, ike, iko = _proj_call(
            h, norm_mix[l][None], w_p, tables, q_norm[l][None], k_norm[l][None],
            jnp.concatenate([kidx_norm[l], kidx_norm[l]])[None], rb)
        o_a = _gdn_call(qkv, z, sm, conv_a[l], _pad_lanes(a_log[l]), _pad_lanes(dt_bias[l]),
                        a_out_norm[l][None], ltri, eg, eb, rb)
        o_b = _dsa_call(qr, kr, v, iqr, ike, iko, sm, lstrict, seq, topk)
        h = _mix_call(h, o_a, o_b, norm_mix[l][None], w_gate[l].astype(BF16), b_gate[l][None],
                      w_branch_a[l].astype(BF16), w_branch_b[l].astype(BF16), w_out[l].astype(BF16), rb)
        h = _ffn_call(h, norm_ffn[l][None], w_up[l].astype(BF16), conv_ffn[l], w_down[l].astype(BF16), rb)
    return h[:, CHUNK:]
```

```python
import functools
import math

import jax
import jax.numpy as jnp
import numpy as np
from jax import lax
from jax.experimental import pallas as pl
from jax.experimental.pallas import tpu as pltpu

F32 = jnp.float32
BF16 = jnp.bfloat16
I32 = jnp.int32

CHUNK = 64
N_META = 16
PAD = CHUNK - N_META
ROPE_THETA = 10000.0
EPS = 1e-6
A_HEADS = 4
A_DK = 128
A_CONV = 4
B_HEADS = 4
B_DH = 128
IDX_HEADS = 8
IDX_DIM = 64
TOPK_MAX = 256
TOPK_DIV = 4
FFN_CONV = 3
LANES = 128
KEY_CHUNK = 256
FF_CHUNK = 256
INT_MIN = -(2 ** 31)
VMEM_LIMIT_BYTES = 56 * 1024 * 1024

NT_DIMS = (((1,), (1,)), ((), ()))


def _dot(a, b):
    return jnp.dot(a, b, preferred_element_type=F32)


def _dot_nt(a, b):
    return lax.dot_general(a, b, NT_DIMS, preferred_element_type=F32)


def _dot_exact(a, b):
    return jnp.dot(a, b, preferred_element_type=F32, precision=lax.Precision.HIGHEST)


def _rms(x, gain):
    return x * lax.rsqrt(jnp.mean(x * x, axis=-1, keepdims=True) + EPS) * gain


def _sigmoid(x):
    return 1.0 / (1.0 + jnp.exp(-x))


def _silu(x):
    return x * _sigmoid(x)


def _softplus(x):
    return jnp.maximum(x, 0.0) + jnp.log(1.0 + jnp.exp(-jnp.abs(x)))


def _row_block(n_chunks):
    g = max(d for d in range(1, 12) if n_chunks % d == 0)
    return g * CHUNK


def _params(sem):
    return pltpu.CompilerParams(dimension_semantics=sem, vmem_limit_bytes=VMEM_LIMIT_BYTES)


def _const_spec(shape):
    nd = len(shape)
    return pl.BlockSpec(shape, lambda *_: (0,) * nd)


PROJ_WIDTHS = (1536, 512, 128, 512, 256, 512, 128)


def _proj_body(h_ref, g_ref, w_ref, cosa_ref, sina_ref, cosi_ref, sinlo_ref, sinhi_ref,
               qn_ref, kn_ref, kin_ref,
               qkv_ref, z_ref, sm_ref, qr_ref, kr_ref, v_ref, iqr_ref, ike_ref, iko_ref):
    n = _rms(h_ref[0], g_ref[...]).astype(BF16)
    offs = np.cumsum((0,) + PROJ_WIDTHS)

    def mm(i):
        return _dot(n, w_ref[:, offs[i]:offs[i + 1]])

    qkv_ref[0] = mm(0)
    z_ref[0] = mm(1)
    sm_ref[0] = mm(2)

    ca, sa = cosa_ref[...], sina_ref[...]
    q = mm(3)
    parts = []
    for hd in range(B_HEADS):
        qh = _rms(q[:, hd * B_DH:(hd + 1) * B_DH], qn_ref[...])
        parts.append(qh * ca + pltpu.roll(qh, B_DH // 2, 1) * sa)
    qr_ref[0] = jnp.concatenate(parts, axis=1).astype(BF16)

    kv = mm(4)
    k = _rms(kv[:, :B_DH], kn_ref[...])
    kr_ref[0] = (k * ca + pltpu.roll(k, B_DH // 2, 1) * sa).astype(BF16)
    v_ref[0] = kv[:, B_DH:].astype(BF16)

    ci, slo, shi = cosi_ref[...], sinlo_ref[...], sinhi_ref[...]
    iq = mm(5)
    parts = []
    for hp in range(IDX_HEADS // 2):
        x = iq[:, hp * LANES:(hp + 1) * LANES]
        parts.append(x * ci + pltpu.roll(x, LANES - IDX_DIM // 2, 1) * slo + pltpu.roll(x, IDX_DIM // 2, 1) * shi)
    iqr_ref[0] = jnp.concatenate(parts, axis=1).astype(BF16)

    ik = _rms(mm(6), kin_ref[...])
    ik = ik * ci + pltpu.roll(ik, IDX_DIM // 2, 1) * (slo + shi)
    lane = lax.broadcasted_iota(I32, (1, LANES), 1)
    ike_ref[0] = jnp.where(lane < IDX_DIM, ik, 0.0).astype(BF16)
    iko_ref[0] = jnp.where(lane >= IDX_DIM, ik, 0.0).astype(BF16)


def _proj_call(h, gain, w, tables, qn, kn, kin, rb):
    bsz, tp, d = h.shape
    nt = tp // rb
    row = lambda w_, dt: (jax.ShapeDtypeStruct((bsz, tp, w_), dt),
                          pl.BlockSpec((1, rb, w_), lambda b, t: (b, t, 0)))
    outs = [row(1536, F32), row(512, F32), row(128, F32), row(512, BF16), row(128, BF16),
            row(128, BF16), row(512, BF16), row(128, BF16), row(128, BF16)]
    tab_spec = pl.BlockSpec((rb, LANES), lambda b, t: (t, 0))
    return pl.pallas_call(
        _proj_body,
        grid=(bsz, nt),
        in_specs=[pl.BlockSpec((1, rb, d), lambda b, t: (b, t, 0)), _const_spec((1, d)), _const_spec(w.shape)]
                 + [tab_spec] * 5 + [_const_spec((1, LANES))] * 3,
        out_specs=[o[1] for o in outs],
        out_shape=[o[0] for o in outs],
        compiler_params=_params(("parallel", "arbitrary")),
        name="proj",
    )(h, gain, w, *tables, qn, kn, kin)


def _gdn_body(qkv_ref, z_ref, sm_ref, cw_ref, alog_ref, dtb_ref, ng_ref, ltri_ref, eg_ref, eb_ref,
              out_ref, xe_ref, s01_ref, s23_ref, *, rb):
    t = pl.program_id(1)
    nwide = 3 * A_HEADS * A_DK

    @pl.when(t == 0)
    def _():
        xe_ref[0:8, :] = jnp.zeros((8, nwide), F32)
        s01_ref[...] = jnp.zeros_like(s01_ref)
        s23_ref[...] = jnp.zeros_like(s23_ref)

    xe_ref[8:8 + rb, :] = qkv_ref[0]

    hw = A_HEADS * A_DK
    pw = A_HEADS * CHUNK
    ii = lax.broadcasted_iota(I32, (CHUNK, pw), 0)
    jj = lax.broadcasted_iota(I32, (CHUNK, pw), 1) & (CHUNK - 1)
    colhead = lax.broadcasted_iota(I32, (CHUNK, pw), 1) >> 6
    eye_p = ii == jj
    bd_mask = (lax.broadcasted_iota(I32, (pw, pw), 0) >> 6) == (lax.broadcasted_iota(I32, (pw, pw), 1) >> 6)
    bdk_mask = (lax.broadcasted_iota(I32, (pw, hw), 0) >> 6) == (lax.broadcasted_iota(I32, (pw, hw), 1) >> 7)
    pair_mask = (lax.broadcasted_iota(I32, (pw, pw), 0) >> 7) == (lax.broadcasted_iota(I32, (pw, pw), 1) >> 7)
    lane = lax.broadcasted_iota(I32, (1, LANES), 1)
    rows64 = lax.broadcasted_iota(I32, (CHUNK, 1), 0)
    cw = cw_ref[...]
    neg_a = -jnp.exp(alog_ref[...])
    dtb = dtb_ref[...]
    ng = ng_ref[...]

    def block_diag(xp):
        return jnp.where(bd_mask, jnp.concatenate([xp.astype(BF16)] * A_HEADS, axis=0), 0)

    def stack_heads(xp):
        return jnp.concatenate([jnp.where(colhead == hd, xp, 0.0) for hd in range(A_HEADS)], axis=0)

    def l2n(x):
        parts = []
        for hd in range(A_HEADS):
            xh = x[:, hd * A_DK:(hd + 1) * A_DK]
            parts.append(xh * lax.rsqrt(jnp.sum(xh * xh, axis=-1, keepdims=True) + EPS))
        return jnp.concatenate(parts, axis=1)

    def chunk_step(c, carry):
        r0 = pl.multiple_of(c * CHUNK, CHUNK)
        xw = xe_ref[pl.ds(r0, CHUNK + 8), :]
        acc = pltpu.roll(xw, 3, 0)[8:] * cw[0:1]
        acc = acc + pltpu.roll(xw, 2, 0)[8:] * cw[1:2]
        acc = acc + pltpu.roll(xw, 1, 0)[8:] * cw[2:3]
        acc = acc + xw[8:] * cw[3:4]
        xq = _silu(acc)
        q = l2n(xq[:, :hw]) * (A_DK ** -0.5)
        k = l2n(xq[:, hw:2 * hw])
        v = xq[:, 2 * hw:]

        sm = sm_ref[0, pl.ds(r0, CHUNK), :]
        g = jnp.where(lane < A_HEADS, neg_a * _softplus(sm + dtb), 0.0)
        g = jnp.where(t * rb + r0 + rows64 >= PAD, g, 0.0)
        gcs = _dot_exact(ltri_ref[...], g)
        gx = _dot_exact(gcs, eg_ref[...])
        bx = _dot_exact(_sigmoid(sm), eb_ref[...])
        gp, gq = gx[:, :pw], gx[:, pw:]
        bp, bq = bx[:, :pw], bx[:, pw:]
        grow = jnp.sum(jnp.where(eye_p, gp, 0.0), axis=0, keepdims=True)
        decay = jnp.where(ii >= jj, jnp.exp(gp - grow), 0.0)

        kb = k.astype(BF16)
        bdk = jnp.where(bdk_mask, jnp.concatenate([kb] * A_HEADS, axis=0), 0)
        kq = _dot_nt(jnp.concatenate([kb, q.astype(BF16)], axis=0), bdk)
        kk_p, qk_p = kq[:CHUNK], kq[CHUNK:]

        a = -jnp.where(ii > jj, bp * kk_p * decay, 0.0)
        tinv = jnp.where(eye_p, 1.0, 0.0) + a
        pk = _dot(a.astype(BF16), block_diag(a))
        for _ in range(4):
            pt = _dot(jnp.concatenate([pk, tinv], axis=0).astype(BF16), block_diag(pk))
            tinv = tinv + pt[CHUNK:]
            pk = pt[:CHUNK]
        tinv = tinv + _dot(tinv.astype(BF16), block_diag(pk))

        eg = jnp.exp(gq)
        vb = v * bq
        kbg = k * bq * eg
        rv = jnp.concatenate(
            [jnp.concatenate([vb[:, hd * A_DK:(hd + 1) * A_DK], kbg[:, hd * A_DK:(hd + 1) * A_DK]], axis=1)
             for hd in range(A_HEADS)], axis=0)
        uw = _dot(stack_heads(tinv).astype(BF16), rv.astype(BF16))
        u, w = uw[:, :A_DK], uw[:, A_DK:]

        qg = q * eg
        glast = gq[CHUNK - 1:CHUNK, :]
        kd = k * jnp.exp(glast - gq)
        egl = jnp.exp(glast)
        qkd = stack_heads(qk_p * decay).astype(BF16)

        def pair_lanes(x, a_, b_):
            return jnp.concatenate([x[a_ * CHUNK:(a_ + 1) * CHUNK], x[b_ * CHUNK:(b_ + 1) * CHUNK]], axis=1)

        def unpair(r):
            return [r[:, :A_DK], r[:, A_DK:]]

        s01 = s01_ref[...]
        s23 = s23_ref[...]
        r01 = _dot(jnp.concatenate([pair_lanes(w, 0, 1), qg[:, :2 * A_DK]], axis=0).astype(BF16), s01.astype(BF16))
        r23 = _dot(jnp.concatenate([pair_lanes(w, 2, 3), qg[:, 2 * A_DK:]], axis=0).astype(BF16), s23.astype(BF16))
        ws = jnp.concatenate(unpair(r01[:CHUNK]) + unpair(r23[:CHUNK]), axis=0)
        qs = jnp.concatenate(unpair(r01[CHUNK:]) + unpair(r23[CHUNK:]), axis=0)
        vn = u - ws
        vnb = vn.astype(BF16)
        o_rs = qs + _dot(qkd, vnb)

        zeros = jnp.zeros((CHUNK, 2 * A_DK), F32)

        def state_update(s_old, kd_pair, vn_pair, egl_pair):
            kdt = jnp.concatenate([kd_pair, zeros], axis=0).T
            upd = _dot(kdt.astype(BF16), jnp.concatenate([vn_pair, zeros], axis=0).astype(BF16))
            return s_old * egl_pair + jnp.where(pair_mask, upd, 0.0)

        s01_ref[...] = state_update(s01, kd[:, :2 * A_DK], pair_lanes(vn, 0, 1), egl[:, :2 * A_DK])
        s23_ref[...] = state_update(s23, kd[:, 2 * A_DK:], pair_lanes(vn, 2, 3), egl[:, 2 * A_DK:])

        zz = z_ref[0, pl.ds(r0, CHUNK), :]
        parts = []
        for hd in range(A_HEADS):
            parts.append(_rms(o_rs[hd * CHUNK:(hd + 1) * CHUNK], ng))
        out_ref[0, pl.ds(r0, CHUNK), :] = jnp.concatenate(parts, axis=1) * _silu(zz)
        return carry

    lax.fori_loop(0, rb // CHUNK, chunk_step, 0)
    xe_ref[0:8, :] = xe_ref[rb:rb + 8, :]


def _gdn_call(qkv, z, sm, cw, alog, dtb, ng, ltri, eg, eb, rb):
    bsz, tp, _ = qkv.shape
    nt = tp // rb
    hw = A_HEADS * A_DK
    row = lambda w_: pl.BlockSpec((1, rb, w_), lambda b, t: (b, t, 0))
    return pl.pallas_call(
        functools.partial(_gdn_body, rb=rb),
        grid=(bsz, nt),
        in_specs=[row(3 * hw), row(hw), row(LANES), _const_spec(cw.shape), _const_spec((1, LANES)),
                  _const_spec((1, LANES)), _const_spec((1, LANES)), _const_spec(ltri.shape),
                  _const_spec(eg.shape), _const_spec(eb.shape)],
        out_specs=row(hw),
        out_shape=jax.ShapeDtypeStruct((bsz, tp, hw), F32),
        scratch_shapes=[pltpu.VMEM((rb + 8, 3 * hw), F32),
                        pltpu.VMEM((2 * A_DK, 2 * A_DK), F32), pltpu.VMEM((2 * A_DK, 2 * A_DK), F32)],
        compiler_params=_params(("parallel", "arbitrary")),
        name="gdn",
    )(qkv, z, sm, cw, alog, dtb, ng, ltri, eg, eb)


def _dsa_body(qr_ref, kr_ref, v_ref, iqr_ref, ike_ref, iko_ref, sm_ref, lstrict_ref, out_ref,
              vt_ref, vmt_ref, skey_ref, planes_ref, bias_ref, st_ref, *, seq, topk):
    p = pl.program_id(1)
    scale = B_DH ** -0.5
    iw_scale = IDX_HEADS ** -0.5 * IDX_DIM ** -0.5
    qpair = 2 * CHUNK
    neg_inf = -jnp.inf

    @pl.when(p == 0)
    def _():
        planes_ref[...] = jnp.zeros_like(planes_ref)

        def vblk(i, carry):
            r = pl.multiple_of(CHUNK + i * LANES, CHUNK)
            vt_ref[i] = v_ref[0, pl.ds(r, LANES), :].astype(F32).T.astype(BF16)
            return carry

        lax.fori_loop(0, seq // LANES, vblk, 0)
        vmt_ref[...] = v_ref[0, 0:LANES, :].astype(F32).T.astype(BF16)

        qm = qr_ref[0, 0:CHUNK, :]
        km = kr_ref[0, 0:CHUNK, :]
        vm = v_ref[0, 0:CHUNK, :]
        colv = lax.broadcasted_iota(I32, (1, CHUNK), 1) >= PAD
        parts = []
        for hd in range(B_HEADS):
            s = _dot_nt(qm[:, hd * B_DH:(hd + 1) * B_DH], km) * scale
            s = jnp.where(colv, s, neg_inf)
            e = jnp.exp(s - jnp.max(s, axis=-1, keepdims=True))
            pr = e / jnp.sum(e, axis=-1, keepdims=True)
            parts.append(_dot(pr.astype(BF16), vm))
        out_ref[0, 0:CHUNK, :] = jnp.concatenate(parts, axis=1)

    r0 = pl.multiple_of(CHUNK + p * qpair, CHUNK)
    qb = qr_ref[0, pl.ds(r0, qpair), :]
    q_hq = jnp.concatenate([qb[:, hd * B_DH:(hd + 1) * B_DH] for hd in range(B_HEADS)], axis=0)
    ib = iqr_ref[0, pl.ds(r0, qpair), :]
    iq_hq = jnp.concatenate([ib[:, hp * LANES:(hp + 1) * LANES] for hp in range(IDX_HEADS // 2)], axis=0)
    smt = sm_ref[0, pl.ds(r0, qpair), :].T
    wts = [smt[2 * A_HEADS + hd:2 * A_HEADS + hd + 1, :] * iw_scale for hd in range(IDX_HEADS)]
    lane = lax.broadcasted_iota(I32, (1, LANES), 1)
    limit = jnp.where(lane < CHUNK, (2 * p + 1) * CHUNK, (2 * p + 2) * CHUNK)
    nkc = (p + 2) // 2
    krows = lax.broadcasted_iota(I32, (KEY_CHUNK, 1), 0)

    def score_chunk(kc, carry):
        k0 = pl.multiple_of(CHUNK + kc * KEY_CHUNK, CHUNK)
        le = _dot_nt(ike_ref[0, pl.ds(k0, KEY_CHUNK), :], iq_hq)
        lo = _dot_nt(iko_ref[0, pl.ds(k0, KEY_CHUNK), :], iq_hq)
        sc = jnp.zeros((KEY_CHUNK, LANES), F32)
        for hp in range(IDX_HEADS // 2):
            sc = sc + wts[2 * hp] * jnp.maximum(le[:, hp * LANES:(hp + 1) * LANES], 0.0)
            sc = sc + wts[2 * hp + 1] * jnp.maximum(lo[:, hp * LANES:(hp + 1) * LANES], 0.0)
        bits = lax.bitcast_convert_type(sc, I32)
        bits = jnp.where(sc == 0.0, 0, bits)
        key = bits ^ ((bits >> 31) & 0x7FFFFFFF)
        valid = kc * KEY_CHUNK + krows < limit
        key = jnp.where(valid, key, INT_MIN)
        skey_ref[pl.ds(pl.multiple_of(kc * KEY_CHUNK, KEY_CHUNK), KEY_CHUNK), :] = key
        a = [key[8 * r:8 * r + 8, :] ^ INT_MIN for r in range(32)]
        j, msk = 16, 0x0000FFFF
        while j:
            k = 0
            while k < 32:
                tt = (a[k] ^ lax.shift_right_logical(a[k + j], jnp.int32(j))) & msk
                a[k] = a[k] ^ tt
                a[k + j] = a[k + j] ^ (tt << j)
                k = (k + j + 1) & ~j
            j >>= 1
            msk = (msk ^ (msk << j)) & 0xFFFFFFFF
        for b in range(32):
            planes_ref[b, pl.ds(pl.multiple_of(kc * 8, 8), 8), :] = a[b]
        return carry

    lax.fori_loop(0, nkc, score_chunk, 0)

    grp = seq // 32
    alive0 = jnp.where(lax.broadcasted_iota(I32, (grp, 1), 0) < nkc * 8, jnp.full((grp, LANES), -1, I32), 0)

    def lane_count(words):
        c = lax.population_count(words)
        if grp > 8:
            c = jnp.sum(c.reshape(grp // 8, 8, LANES), axis=0)
        return jnp.sum(c, axis=0, keepdims=True)

    def bit_body(i, carry):
        tu, above, alive = carry
        hit = alive & planes_ref[i]
        c1 = lane_count(hit)
        take = above + c1 >= topk
        tu = jnp.where(take, tu | jnp.left_shift(jnp.int32(1), 31 - i), tu)
        return tu, jnp.where(take, above, above + c1), jnp.where(take, hit, alive ^ hit)

    zero_row = jnp.zeros((1, LANES), I32)
    tu, above, alive = lax.fori_loop(0, 32, bit_body, (zero_row, zero_row, alive0))
    thr = tu ^ INT_MIN
    found = tu != 0
    tie_lane = found & (above + lane_count(alive) > topk)
    has_tie = jnp.max(jnp.where(tie_lane, 1, 0)) > 0

    @pl.when(jnp.logical_not(has_tie))
    def _():
        thr_c = jnp.maximum(thr, INT_MIN + 1)

        def body(kc, carry):
            rows = pl.ds(pl.multiple_of(kc * KEY_CHUNK, KEY_CHUNK), KEY_CHUNK)
            bias_ref[rows, :] = jnp.where(skey_ref[rows, :] >= thr_c, 0.0, neg_inf)
            return carry

        lax.fori_loop(0, nkc, body, 0)

    @pl.when(has_tie)
    def _():
        need = (topk - above).astype(F32)

        def body(kc, seen):
            rows = pl.ds(pl.multiple_of(kc * KEY_CHUNK, KEY_CHUNK), KEY_CHUNK)
            blk = skey_ref[rows, :]
            tied = (blk == thr) & found
            eq = jnp.where(tied, 1.0, 0.0)
            rank = _dot(lstrict_ref[...], eq.astype(BF16)) + seen
            keep = (blk > thr) | (tied & (rank < need))
            bias_ref[rows, :] = jnp.where(keep, 0.0, neg_inf)
            return seen + jnp.sum(eq, axis=0, keepdims=True)

        lax.fori_loop(0, nkc, body, jnp.zeros((1, LANES), F32))

    def logit_chunk(kc, m8):
        k0 = pl.multiple_of(CHUNK + kc * KEY_CHUNK, CHUNK)
        rows = pl.ds(pl.multiple_of(kc * KEY_CHUNK, KEY_CHUNK), KEY_CHUNK)
        s = _dot_nt(kr_ref[0, pl.ds(k0, KEY_CHUNK), :], q_hq) * scale
        s = s + jnp.concatenate([bias_ref[rows, :]] * B_HEADS, axis=1)
        st_ref[rows, :] = s
        return jnp.maximum(m8, jnp.max(s.reshape(KEY_CHUNK // 8, 8, B_HEADS * LANES), axis=0))

    m8 = lax.fori_loop(0, nkc, logit_chunk, jnp.full((8, B_HEADS * LANES), neg_inf, F32))
    mrows = lax.broadcasted_iota(I32, (LANES, 1), 0)
    s_meta = _dot_nt(kr_ref[0, 0:LANES, :], q_hq) * scale
    s_meta = jnp.where((mrows >= PAD) & (mrows < CHUNK), s_meta, neg_inf)
    m = jnp.maximum(jnp.max(m8, axis=0, keepdims=True), jnp.max(s_meta, axis=0, keepdims=True))

    p_meta = jnp.exp(s_meta - m)
    l0 = jnp.sum(p_meta, axis=0, keepdims=True)
    o0 = _dot(vmt_ref[...], p_meta.astype(BF16))

    def pv_chunk(kc, carry):
        l, o_t = carry
        rows = pl.ds(pl.multiple_of(kc * KEY_CHUNK, KEY_CHUNK), KEY_CHUNK)
        pr = jnp.exp(st_ref[rows, :] - m)
        vblk = jnp.concatenate([vt_ref[2 * kc], vt_ref[2 * kc + 1]], axis=1)
        return l + jnp.sum(pr, axis=0, keepdims=True), o_t + _dot(vblk, pr.astype(BF16))

    l, o_t = lax.fori_loop(0, nkc, pv_chunk, (l0, o0))
    o_hq = (o_t / l).T
    for hd in range(B_HEADS):
        out_ref[0, pl.ds(r0, qpair), hd * B_DH:(hd + 1) * B_DH] = o_hq[hd * LANES:(hd + 1) * LANES, :]


def _dsa_call(qr, kr, v, iqr, ike, iko, sm, lstrict, seq, topk):
    bsz, tp, _ = qr.shape
    npairs = seq // (2 * CHUNK)
    full = lambda w_: pl.BlockSpec((1, tp, w_), lambda b, p: (b, 0, 0))
    hw = B_HEADS * B_DH
    return pl.pallas_call(
        functools.partial(_dsa_body, seq=seq, topk=topk),
        grid=(bsz, npairs),
        in_specs=[full(hw), full(B_DH), full(B_DH), full(hw), full(LANES), full(LANES), full(LANES),
                  _const_spec(lstrict.shape)],
        out_specs=full(hw),
        out_shape=jax.ShapeDtypeStruct((bsz, tp, hw), F32),
        scratch_shapes=[pltpu.VMEM((seq // LANES, B_DH, LANES), BF16), pltpu.VMEM((B_DH, LANES), BF16),
                        pltpu.VMEM((seq, LANES), I32), pltpu.VMEM((32, seq // 32, LANES), I32),
                        pltpu.VMEM((seq, LANES), F32),
                        pltpu.VMEM((seq, hw), F32)],
        compiler_params=_params(("parallel", "arbitrary")),
        name="dsa",
    )(qr, kr, v, iqr, ike, iko, sm, lstrict)


def _mix_body(h_ref, oa_ref, ob_ref, g_ref, wg_ref, bg_ref, wa_ref, wb_ref, wo_ref, out_ref, *, rb):
    x = h_ref[0]
    d = x.shape[-1]
    n = _rms(x, g_ref[...]).astype(BF16)
    gates = _sigmoid(_dot(n, wg_ref[...]) + bg_ref[...])
    y = gates[:, :d] * _dot(oa_ref[0].astype(BF16), wa_ref[...]) + gates[:, d:] * _dot(ob_ref[0].astype(BF16), wb_ref[...])
    out = x + _dot(y.astype(BF16), wo_ref[...])
    rows = pl.program_id(1) * rb + lax.broadcasted_iota(I32, (rb, 1), 0)
    out_ref[0] = jnp.where(rows >= PAD, out, 0.0)


def _mix_call(h, oa, ob, gain, wg, bg, wa, wb, wo, rb):
    bsz, tp, d = h.shape
    row = lambda w_: pl.BlockSpec((1, rb, w_), lambda b, t: (b, t, 0))
    return pl.pallas_call(
        functools.partial(_mix_body, rb=rb),
        grid=(bsz, tp // rb),
        in_specs=[row(d), row(oa.shape[-1]), row(ob.shape[-1]), _const_spec((1, d)), _const_spec(wg.shape),
                  _const_spec(bg.shape), _const_spec(wa.shape), _const_spec(wb.shape), _const_spec(wo.shape)],
        out_specs=row(d),
        out_shape=jax.ShapeDtypeStruct((bsz, tp, d), F32),
        compiler_params=_params(("parallel", "arbitrary")),
        name="mix",
    )(h, oa, ob, gain, wg, bg, wa, wb, wo)


def _ffn_body(h_ref, g_ref, wup_ref, cw_ref, wd_ref, out_ref, carry_ref, *, rb, dff):
    t = pl.program_id(1)

    @pl.when(t == 0)
    def _():
        carry_ref[...] = jnp.zeros_like(carry_ref)

    x = h_ref[0]
    n = _rms(x, g_ref[...]).astype(BF16)
    rows = lax.broadcasted_iota(I32, (rb, 1), 0)
    acc = jnp.zeros(x.shape, F32)
    for c in range(dff // FF_CHUNK):
        cols = slice(c * FF_CHUNK, (c + 1) * FF_CHUNK)
        gate = _dot(n, wup_ref[:, cols])
        up = _dot(n, wup_ref[:, dff + c * FF_CHUNK:dff + (c + 1) * FF_CHUNK])
        prev = carry_ref[:, cols]
        g1 = jnp.where(rows == 0, prev[7:8], pltpu.roll(gate, 1, 0))
        g2 = jnp.where(rows == 0, prev[6:7], jnp.where(rows == 1, prev[7:8], pltpu.roll(gate, 2, 0)))
        cw = cw_ref[:, cols]
        conv = g2 * cw[0:1] + g1 * cw[1:2] + gate * cw[2:3]
        carry_ref[:, cols] = gate[rb - 8:rb]
        acc = acc + _dot((_silu(conv) * up).astype(BF16), wd_ref[cols, :])
    out = x + acc
    out_ref[0] = jnp.where(t * rb + rows >= PAD, out, 0.0)


def _ffn_call(h, gain, wup, cw, wd, rb):
    bsz, tp, d = h.shape
    dff = wd.shape[0]
    row = pl.BlockSpec((1, rb, d), lambda b, t: (b, t, 0))
    return pl.pallas_call(
        functools.partial(_ffn_body, rb=rb, dff=dff),
        grid=(bsz, tp // rb),
        in_specs=[row, _const_spec((1, d)), _const_spec(wup.shape), _const_spec(cw.shape), _const_spec(wd.shape)],
        out_specs=row,
        out_shape=jax.ShapeDtypeStruct((bsz, tp, d), F32),
        scratch_shapes=[pltpu.VMEM((8, dff), F32)],
        compiler_params=_params(("parallel", "arbitrary")),
        name="ffn",
    )(h, gain, wup, cw, wd)


def _rope_tables(tp):
    pos = (jnp.arange(tp, dtype=F32) - PAD)[:, None]

    def cs(dim):
        inv = 1.0 / (ROPE_THETA ** (jnp.arange(0, dim, 2, dtype=F32) / dim))
        ang = pos * inv[None, :]
        return jnp.cos(ang), jnp.sin(ang)

    ca, sa = cs(B_DH)
    ci, si = cs(IDX_DIM)
    zi = jnp.zeros_like(si)
    return (jnp.concatenate([ca, ca], axis=1), jnp.concatenate([-sa, sa], axis=1),
            jnp.concatenate([ci] * 4, axis=1), jnp.concatenate([-si, zi, -si, zi], axis=1),
            jnp.concatenate([zi, si, zi, si], axis=1))


def _pad_lanes(vec):
    return jnp.zeros((1, LANES), F32).at[0, :vec.shape[0]].set(vec.astype(F32))


def _head_expanders():
    pw, hw = A_HEADS * CHUNK, A_HEADS * A_DK
    head = np.concatenate([np.arange(pw) // CHUNK, np.arange(hw) // A_DK])
    src = np.arange(LANES)[:, None]
    return (jnp.asarray(src == head[None, :], F32), jnp.asarray(src == A_HEADS + head[None, :], F32))


def kernel(x, meta_tokens, norm_mix, w_in, conv_a, a_log, dt_bias, a_out_norm, q_norm, k_norm, kidx_norm,
           w_branch_a, w_branch_b, w_gate, b_gate, w_out, norm_ffn, w_up, conv_ffn, w_down):
    bsz, seq, d = x.shape
    depth = w_in.shape[0]
    assert seq % KEY_CHUNK == 0
    tp = CHUNK + seq
    rb = _row_block(tp // CHUNK)
    topk = min(TOPK_MAX, seq // TOPK_DIV)

    meta = jnp.broadcast_to(meta_tokens.astype(x.dtype)[None], (bsz, N_META, d))
    h = jnp.concatenate([jnp.zeros((bsz, PAD, d), x.dtype), meta, x], axis=1)

    tables = _rope_tables(tp)
    eg, eb = _head_expanders()
    tri = np.arange(CHUNK)
    ltri = jnp.asarray(tri[:, None] >= tri[None, :], F32)
    trk = np.arange(KEY_CHUNK)
    lstrict = jnp.asarray(trk[:, None] > trk[None, :], BF16)

    hw = A_HEADS * A_DK
    o = np.cumsum((0, hw, hw, hw, hw, A_HEADS, A_HEADS, B_HEADS * B_DH, B_DH, B_DH,
                   IDX_HEADS * IDX_DIM, IDX_DIM, IDX_HEADS))
    for l in range(depth):
        w = w_in[l]
        small = jnp.concatenate([w[:, o[4]:o[6]], w[:, o[11]:o[12]],
                                 jnp.zeros((d, LANES - 2 * A_HEADS - IDX_HEADS), w.dtype)], axis=1)
        w_p = jnp.concatenate([w[:, o[0]:o[3]], w[:, o[3]:o[4]], small, w[:, o[6]:o[7]], w[:, o[7]:o[9]],
                               w[:, o[9]:o[10]], w[:, o[10]:o[11]], w[:, o[10]:o[11]]], axis=1).astype(BF16)
        qkv, z, sm, qr, kr, v, iqr, ike, iko = _proj_call(
            h, norm_mix[l][None], w_p, tables, q_norm[l][None], k_norm[l][None],
            jnp.concatenate([kidx_norm[l], kidx_norm[l]])[None], rb)
        o_a = _gdn_call(qkv, z, sm, conv_a[l], _pad_lanes(a_log[l]), _pad_lanes(dt_bias[l]),
                        a_out_norm[l][None], ltri, eg, eb, rb)
        o_b = _dsa_call(qr, kr, v, iqr, ike, iko, sm, lstrict, seq, topk)
        h = _mix_call(h, o_a, o_b, norm_mix[l][None], w_gate[l].astype(BF16), b_gate[l][None],
                      w_branch_a[l].astype(BF16), w_branch_b[l].astype(BF16), w_out[l].astype(BF16), rb)
        h = _ffn_call(h, norm_ffn[l][None], w_up[l].astype(BF16), conv_ffn[l], w_down[l].astype(BF16), rb)
    return h[:, CHUNK:]
```

```python
import functools
import math

import jax
import jax.numpy as jnp
import numpy as np
from jax import lax
from jax.experimental import pallas as pl
from jax.experimental.pallas import tpu as pltpu

F32 = jnp.float32
BF16 = jnp.bfloat16
I32 = jnp.int32

CHUNK = 64
N_META = 16
PAD = CHUNK - N_META
ROPE_THETA = 10000.0
EPS = 1e-6
A_HEADS = 4
A_DK = 128
A_CONV = 4
B_HEADS = 4
B_DH = 128
IDX_HEADS = 8
IDX_DIM = 64
TOPK_MAX = 256
TOPK_DIV = 4
FFN_CONV = 3
LANES = 128
KEY_CHUNK = 256
FF_CHUNK = 256
INT_MIN = -(2 ** 31)
VMEM_LIMIT_BYTES = 56 * 1024 * 1024

NT_DIMS = (((1,), (1,)), ((), ()))


def _dot(a, b):
    return jnp.dot(a, b, preferred_element_type=F32)


def _dot_nt(a, b):
    return lax.dot_general(a, b, NT_DIMS, preferred_element_type=F32)


def _dot_exact(a, b):
    return jnp.dot(a, b, preferred_element_type=F32, precision=lax.Precision.HIGHEST)


def _rms(x, gain):
    return x * lax.rsqrt(jnp.mean(x * x, axis=-1, keepdims=True) + EPS) * gain


def _sigmoid(x):
    return 1.0 / (1.0 + jnp.exp(-x))


def _silu(x):
    return x * _sigmoid(x)


def _softplus(x):
    return jnp.maximum(x, 0.0) + jnp.log(1.0 + jnp.exp(-jnp.abs(x)))


def _row_block(n_chunks):
    g = max(d for d in range(1, 12) if n_chunks % d == 0)
    return g * CHUNK


def _params(sem):
    return pltpu.CompilerParams(dimension_semantics=sem, vmem_limit_bytes=VMEM_LIMIT_BYTES)


def _const_spec(shape):
    nd = len(shape)
    return pl.BlockSpec(shape, lambda *_: (0,) * nd)


PROJ_WIDTHS = (1536, 512, 128, 512, 256, 512, 128)


def _proj_body(h_ref, g_ref, w_ref, cosa_ref, sina_ref, cosi_ref, sinlo_ref, sinhi_ref,
               qn_ref, kn_ref, kin_ref,
               qkv_ref, z_ref, sm_ref, qr_ref, kr_ref, v_ref, iqr_ref, ike_ref, iko_ref):
    n = _rms(h_ref[0], g_ref[...]).astype(BF16)
    offs = np.cumsum((0,) + PROJ_WIDTHS)

    def mm(i):
        return _dot(n, w_ref[:, offs[i]:offs[i + 1]])

    qkv_ref[0] = mm(0)
    z_ref[0] = mm(1)
    sm_ref[0] = mm(2)

    ca, sa = cosa_ref[...], sina_ref[...]
    q = mm(3)
    parts = []
    for hd in range(B_HEADS):
        qh = _rms(q[:, hd * B_DH:(hd + 1) * B_DH], qn_ref[...])
        parts.append(qh * ca + pltpu.roll(qh, B_DH // 2, 1) * sa)
    qr_ref[0] = jnp.concatenate(parts, axis=1).astype(BF16)

    kv = mm(4)
    k = _rms(kv[:, :B_DH], kn_ref[...])
    kr_ref[0] = (k * ca + pltpu.roll(k, B_DH // 2, 1) * sa).astype(BF16)
    v_ref[0] = kv[:, B_DH:].astype(BF16)

    ci, slo, shi = cosi_ref[...], sinlo_ref[...], sinhi_ref[...]
    iq = mm(5)
    parts = []
    for hp in range(IDX_HEADS // 2):
        x = iq[:, hp * LANES:(hp + 1) * LANES]
        parts.append(x * ci + pltpu.roll(x, LANES - IDX_DIM // 2, 1) * slo + pltpu.roll(x, IDX_DIM // 2, 1) * shi)
    iqr_ref[0] = jnp.concatenate(parts, axis=1).astype(BF16)

    ik = _rms(mm(6), kin_ref[...])
    ik = ik * ci + pltpu.roll(ik, IDX_DIM // 2, 1) * (slo + shi)
    lane = lax.broadcasted_iota(I32, (1, LANES), 1)
    ike_ref[0] = jnp.where(lane < IDX_DIM, ik, 0.0).astype(BF16)
    iko_ref[0] = jnp.where(lane >= IDX_DIM, ik, 0.0).astype(BF16)


def _proj_call(h, gain, w, tables, qn, kn, kin, rb):
    bsz, tp, d = h.shape
    nt = tp // rb
    row = lambda w_, dt: (jax.ShapeDtypeStruct((bsz, tp, w_), dt),
                          pl.BlockSpec((1, rb, w_), lambda b, t: (b, t, 0)))
    outs = [row(1536, F32), row(512, F32), row(128, F32), row(512, BF16), row(128, BF16),
            row(128, BF16), row(512, BF16), row(128, BF16), row(128, BF16)]
    tab_spec = pl.BlockSpec((rb, LANES), lambda b, t: (t, 0))
    return pl.pallas_call(
        _proj_body,
        grid=(bsz, nt),
        in_specs=[pl.BlockSpec((1, rb, d), lambda b, t: (b, t, 0)), _const_spec((1, d)), _const_spec(w.shape)]
                 + [tab_spec] * 5 + [_const_spec((1, LANES))] * 3,
        out_specs=[o[1] for o in outs],
        out_shape=[o[0] for o in outs],
        compiler_params=_params(("parallel", "arbitrary")),
        name="proj",
    )(h, gain, w, *tables, qn, kn, kin)


def _split3(x):
    x1 = x.astype(BF16)
    r1 = x - x1.astype(F32)
    x2 = r1.astype(BF16)
    x3 = (r1 - x2.astype(F32)).astype(BF16)
    return jnp.concatenate([x1, x2, x3], axis=0)


def _gdn_body(qkv_ref, z_ref, sm_ref, cw_ref, alog_ref, dtb_ref, ng_ref, lt3_ref, id3_ref, egb_ref,
              out_ref, xe_ref, s01_ref, s23_ref, u_ref, wq_ref, qkd_ref, kdt_ref, egl_ref, *, rb):
    t = pl.program_id(1)
    nwide = 3 * A_HEADS * A_DK

    @pl.when(t == 0)
    def _():
        xe_ref[0:8, :] = jnp.zeros((8, nwide), F32)
        s01_ref[...] = jnp.zeros_like(s01_ref)
        s23_ref[...] = jnp.zeros_like(s23_ref)

    xe_ref[8:8 + rb, :] = qkv_ref[0]
    cw = cw_ref[...]

    hw = A_HEADS * A_DK
    pw = A_HEADS * CHUNK
    ii = lax.broadcasted_iota(I32, (CHUNK, pw), 0)
    jj = lax.broadcasted_iota(I32, (CHUNK, pw), 1) & (CHUNK - 1)
    colhead = lax.broadcasted_iota(I32, (CHUNK, pw), 1) >> 6
    eye_p = ii == jj
    bd_mask = (lax.broadcasted_iota(I32, (pw, pw), 0) >> 6) == (lax.broadcasted_iota(I32, (pw, pw), 1) >> 6)
    bdk_mask = (lax.broadcasted_iota(I32, (pw, hw), 0) >> 6) == (lax.broadcasted_iota(I32, (pw, hw), 1) >> 7)
    pair_mask = (lax.broadcasted_iota(I32, (pw, pw), 0) >> 7) == (lax.broadcasted_iota(I32, (pw, pw), 1) >> 7)
    lane = lax.broadcasted_iota(I32, (1, LANES), 1)
    rows64 = lax.broadcasted_iota(I32, (CHUNK, 1), 0)
    neg_a = -jnp.exp(alog_ref[...])
    dtb = dtb_ref[...]
    ng = ng_ref[...]

    def block_diag(xp):
        return jnp.where(bd_mask, jnp.concatenate([xp.astype(BF16)] * A_HEADS, axis=0), 0)

    def stack_heads(xp):
        return jnp.concatenate([jnp.where(colhead == hd, xp, 0.0) for hd in range(A_HEADS)], axis=0)

    def l2n(x):
        parts = []
        for hd in range(A_HEADS):
            xh = x[:, hd * A_DK:(hd + 1) * A_DK]
            parts.append(xh * lax.rsqrt(jnp.sum(xh * xh, axis=-1, keepdims=True) + EPS))
        return jnp.concatenate(parts, axis=1)

    zeros = jnp.zeros((CHUNK, 2 * A_DK), F32)

    def pair_lanes(x, a_, b_):
        return jnp.concatenate([x[a_ * CHUNK:(a_ + 1) * CHUNK], x[b_ * CHUNK:(b_ + 1) * CHUNK]], axis=1)

    def local_load(c):
        r0 = c * CHUNK if isinstance(c, int) else pl.multiple_of(c * CHUNK, CHUNK)
        return xe_ref[pl.ds(r0, CHUNK + 8), :], sm_ref[0, pl.ds(r0, CHUNK), :], r0

    def local_compute(xw, sm, r0):
        acc = pltpu.roll(xw, 3, 0)[8:] * cw[0:1]
        acc = acc + pltpu.roll(xw, 2, 0)[8:] * cw[1:2]
        acc = acc + pltpu.roll(xw, 1, 0)[8:] * cw[2:3]
        acc = acc + xw[8:] * cw[3:4]
        xq = _silu(acc)
        q = l2n(xq[:, :hw]) * (A_DK ** -0.5)
        k = l2n(xq[:, hw:2 * hw])
        v = xq[:, 2 * hw:]

        g = jnp.where(lane < A_HEADS, neg_a * _softplus(sm + dtb), 0.0)
        g = jnp.where(t * rb + r0 + rows64 >= PAD, g, 0.0)
        x3 = _split3(jnp.where(lane < A_HEADS, g, _sigmoid(sm)))
        copies = _dot(x3, egb_ref[...]).astype(BF16)
        yield
        gx = _dot(lt3_ref[...], copies[:, :pw + hw])
        bx = _dot(id3_ref[...], copies[:, pw + hw:])
        yield
        gp, gq = gx[:, :pw], gx[:, pw:]
        bp, bq = bx[:, :pw], bx[:, pw:]
        grow = jnp.sum(jnp.where(eye_p, gp, 0.0), axis=0, keepdims=True)
        decay = jnp.where(ii >= jj, jnp.exp(gp - grow), 0.0)

        kb = k.astype(BF16)
        bdk = jnp.where(bdk_mask, jnp.concatenate([kb] * A_HEADS, axis=0), 0)
        kq = _dot_nt(jnp.concatenate([kb, q.astype(BF16)], axis=0), bdk)
        yield
        kk_p, qk_p = kq[:CHUNK], kq[CHUNK:]

        a = -jnp.where(ii > jj, bp * kk_p * decay, 0.0)
        tinv = jnp.where(eye_p, 1.0, 0.0) + a
        pk = _dot(a.astype(BF16), block_diag(a))
        yield
        for _ in range(4):
            pt = _dot(jnp.concatenate([pk, tinv], axis=0).astype(BF16), block_diag(pk))
            yield
            tinv = tinv + pt[CHUNK:]
            pk = pt[:CHUNK]
        tinv = tinv + _dot(tinv.astype(BF16), block_diag(pk))
        yield

        eg = jnp.exp(gq)
        vb = v * bq
        kbg = k * bq * eg
        rv = jnp.concatenate(
            [jnp.concatenate([vb[:, hd * A_DK:(hd + 1) * A_DK], kbg[:, hd * A_DK:(hd + 1) * A_DK]], axis=1)
             for hd in range(A_HEADS)], axis=0)
        uw = _dot(stack_heads(tinv).astype(BF16), rv.astype(BF16))
        yield
        u, w = uw[:, :A_DK], uw[:, A_DK:]

        qg = q * eg
        glast = gq[CHUNK - 1:CHUNK, :]
        kd = k * jnp.exp(glast - gq)
        return (u,
                jnp.concatenate([pair_lanes(w, 0, 1), qg[:, :2 * A_DK]], axis=0).astype(BF16),
                jnp.concatenate([pair_lanes(w, 2, 3), qg[:, 2 * A_DK:]], axis=0).astype(BF16),
                stack_heads(qk_p * decay).astype(BF16),
                jnp.concatenate([kd[:, :2 * A_DK], zeros], axis=0).T.astype(BF16),
                jnp.concatenate([kd[:, 2 * A_DK:], zeros], axis=0).T.astype(BF16),
                jnp.broadcast_to(jnp.exp(glast), (8, hw)))

    def local_store(c, vals):
        u_ref[c], wq_ref[c, 0], wq_ref[c, 1], qkd_ref[c], kdt_ref[c, 0], kdt_ref[c, 1], egl_ref[c] = vals

    def local_chain(c):
        vals = yield from local_compute(*local_load(c))
        local_store(c, vals)

    def state_chain(chunks):
        def unpair(r):
            return [r[:, :A_DK], r[:, A_DK:]]

        def state_update(s_old, kdt, vn_pair, egl_pair):
            upd = _dot(kdt, jnp.concatenate([vn_pair, zeros], axis=0).astype(BF16))
            return s_old * egl_pair + jnp.where(pair_mask, upd, 0.0)

        s01 = s01_ref[...]
        s23 = s23_ref[...]
        for c in chunks:
            r0 = c * CHUNK if isinstance(c, int) else pl.multiple_of(c * CHUNK, CHUNK)
            r01 = _dot(wq_ref[c, 0], s01.astype(BF16))
            r23 = _dot(wq_ref[c, 1], s23.astype(BF16))
            yield
            ws = jnp.concatenate(unpair(r01[:CHUNK]) + unpair(r23[:CHUNK]), axis=0)
            qs = jnp.concatenate(unpair(r01[CHUNK:]) + unpair(r23[CHUNK:]), axis=0)
            vn = u_ref[c] - ws
            o_rs = qs + _dot(qkd_ref[c], vn.astype(BF16))
            egl = egl_ref[c][0:1, :]
            s01 = state_update(s01, kdt_ref[c, 0], pair_lanes(vn, 0, 1), egl[:, :2 * A_DK])
            s23 = state_update(s23, kdt_ref[c, 1], pair_lanes(vn, 2, 3), egl[:, 2 * A_DK:])
            yield
            zz = z_ref[0, pl.ds(r0, CHUNK), :]
            parts = [_rms(o_rs[hd * CHUNK:(hd + 1) * CHUNK], ng) for hd in range(A_HEADS)]
            out_ref[0, pl.ds(r0, CHUNK), :] = jnp.concatenate(parts, axis=1) * _silu(zz)
        s01_ref[...] = s01
        s23_ref[...] = s23

    def run_lockstep(chains):
        live = list(chains)
        while live:
            for gen in list(live):
                try:
                    next(gen)
                except StopIteration:
                    live.remove(gen)

    nc = rb // CHUNK
    npairs = nc // 2
    if npairs == 0:
        run_lockstep([local_chain(0)])
        run_lockstep([state_chain([0])])
    else:
        run_lockstep([local_chain(0), local_chain(1)])

        def steady(i, carry):
            run_lockstep([local_chain(2 * i), local_chain(2 * i + 1), state_chain([2 * i - 2, 2 * i - 1])])
            return carry

        lax.fori_loop(1, npairs, steady, 0)
        last = [2 * npairs - 2, 2 * npairs - 1]
        if nc % 2:
            run_lockstep([local_chain(nc - 1), state_chain(last)])
            run_lockstep([state_chain([nc - 1])])
        else:
            run_lockstep([state_chain(last)])
    xe_ref[0:8, :] = xe_ref[rb:rb + 8, :]


def _gdn_call(qkv, z, sm, cw, alog, dtb, ng, lt3, id3, egb, rb):
    bsz, tp, _ = qkv.shape
    nt = tp // rb
    nc = rb // CHUNK
    hw = A_HEADS * A_DK
    pw = A_HEADS * CHUNK
    row = lambda w_: pl.BlockSpec((1, rb, w_), lambda b, t: (b, t, 0))
    return pl.pallas_call(
        functools.partial(_gdn_body, rb=rb),
        grid=(bsz, nt),
        in_specs=[row(3 * hw), row(hw), row(LANES), _const_spec(cw.shape), _const_spec((1, LANES)),
                  _const_spec((1, LANES)), _const_spec((1, LANES)), _const_spec(lt3.shape),
                  _const_spec(id3.shape), _const_spec(egb.shape)],
        out_specs=row(hw),
        out_shape=jax.ShapeDtypeStruct((bsz, tp, hw), F32),
        scratch_shapes=[pltpu.VMEM((rb + 8, 3 * hw), F32),
                        pltpu.VMEM((2 * A_DK, 2 * A_DK), F32), pltpu.VMEM((2 * A_DK, 2 * A_DK), F32),
                        pltpu.VMEM((nc, pw, A_DK), F32), pltpu.VMEM((nc, 2, 2 * CHUNK, 2 * A_DK), BF16),
                        pltpu.VMEM((nc, pw, pw), BF16), pltpu.VMEM((nc, 2, 2 * A_DK, 2 * CHUNK), BF16),
                        pltpu.VMEM((nc, 8, hw), F32)],
        compiler_params=_params(("parallel", "arbitrary")),
        name="gdn",
    )(qkv, z, sm, cw, alog, dtb, ng, lt3, id3, egb)


def _dsa_body(qr_ref, kr_ref, v_ref, iqr_ref, ike_ref, iko_ref, sm_ref, lstrict_ref, out_ref,
              vt_ref, vmt_ref, skey_ref, planes_ref, bias_ref, st_ref, *, seq, topk):
    p = pl.program_id(1)
    scale = B_DH ** -0.5
    iw_scale = IDX_HEADS ** -0.5 * IDX_DIM ** -0.5
    qpair = 2 * CHUNK
    neg_inf = -jnp.inf

    @pl.when(p == 0)
    def _():
        planes_ref[...] = jnp.zeros_like(planes_ref)

        def vblk(i, carry):
            r = pl.multiple_of(CHUNK + i * LANES, CHUNK)
            vt_ref[i] = v_ref[0, pl.ds(r, LANES), :].astype(F32).T.astype(BF16)
            return carry

        lax.fori_loop(0, seq // LANES, vblk, 0)
        vmt_ref[...] = v_ref[0, 0:LANES, :].astype(F32).T.astype(BF16)

        qm = qr_ref[0, 0:CHUNK, :]
        km = kr_ref[0, 0:CHUNK, :]
        vm = v_ref[0, 0:CHUNK, :]
        colv = lax.broadcasted_iota(I32, (1, CHUNK), 1) >= PAD
        parts = []
        for hd in range(B_HEADS):
            s = _dot_nt(qm[:, hd * B_DH:(hd + 1) * B_DH], km) * scale
            s = jnp.where(colv, s, neg_inf)
            e = jnp.exp(s - jnp.max(s, axis=-1, keepdims=True))
            pr = e / jnp.sum(e, axis=-1, keepdims=True)
            parts.append(_dot(pr.astype(BF16), vm))
        out_ref[0, 0:CHUNK, :] = jnp.concatenate(parts, axis=1)

    r0 = pl.multiple_of(CHUNK + p * qpair, CHUNK)
    qb = qr_ref[0, pl.ds(r0, qpair), :]
    q_hq = jnp.concatenate([qb[:, hd * B_DH:(hd + 1) * B_DH] for hd in range(B_HEADS)], axis=0)
    ib = iqr_ref[0, pl.ds(r0, qpair), :]
    iq_hq = jnp.concatenate([ib[:, hp * LANES:(hp + 1) * LANES] for hp in range(IDX_HEADS // 2)], axis=0)
    smt = sm_ref[0, pl.ds(r0, qpair), :].T
    wts = [smt[2 * A_HEADS + hd:2 * A_HEADS + hd + 1, :] * iw_scale for hd in range(IDX_HEADS)]
    lane = lax.broadcasted_iota(I32, (1, LANES), 1)
    limit = jnp.where(lane < CHUNK, (2 * p + 1) * CHUNK, (2 * p + 2) * CHUNK)
    nkc = (p + 2) // 2
    krows = lax.broadcasted_iota(I32, (KEY_CHUNK, 1), 0)

    def score_chunk(kc, carry):
        k0 = pl.multiple_of(CHUNK + kc * KEY_CHUNK, CHUNK)
        le = _dot_nt(ike_ref[0, pl.ds(k0, KEY_CHUNK), :], iq_hq)
        lo = _dot_nt(iko_ref[0, pl.ds(k0, KEY_CHUNK), :], iq_hq)
        sc = jnp.zeros((KEY_CHUNK, LANES), F32)
        for hp in range(IDX_HEADS // 2):
            sc = sc + wts[2 * hp] * jnp.maximum(le[:, hp * LANES:(hp + 1) * LANES], 0.0)
            sc = sc + wts[2 * hp + 1] * jnp.maximum(lo[:, hp * LANES:(hp + 1) * LANES], 0.0)
        bits = lax.bitcast_convert_type(sc, I32)
        bits = jnp.where(sc == 0.0, 0, bits)
        key = bits ^ ((bits >> 31) & 0x7FFFFFFF)
        valid = kc * KEY_CHUNK + krows < limit
        key = jnp.where(valid, key, INT_MIN)
        skey_ref[pl.ds(pl.multiple_of(kc * KEY_CHUNK, KEY_CHUNK), KEY_CHUNK), :] = key
        a = [key[8 * r:8 * r + 8, :] ^ INT_MIN for r in range(32)]
        j, msk = 16, 0x0000FFFF
        while j:
            k = 0
            while k < 32:
                tt = (a[k] ^ lax.shift_right_logical(a[k + j], jnp.int32(j))) & msk
                a[k] = a[k] ^ tt
                a[k + j] = a[k + j] ^ (tt << j)
                k = (k + j + 1) & ~j
            j >>= 1
            msk = (msk ^ (msk << j)) & 0xFFFFFFFF
        for b in range(32):
            planes_ref[b, pl.ds(pl.multiple_of(kc * 8, 8), 8), :] = a[b]
        return carry

    lax.fori_loop(0, nkc, score_chunk, 0)

    grp = seq // 32
    alive0 = jnp.where(lax.broadcasted_iota(I32, (grp, 1), 0) < nkc * 8, jnp.full((grp, LANES), -1, I32), 0)

    def lane_count(words):
        c = lax.population_count(words)
        if grp > 8:
            c = jnp.sum(c.reshape(grp // 8, 8, LANES), axis=0)
        return jnp.sum(c, axis=0, keepdims=True)

    def bit_body(i, carry):
        tu, above, alive = carry
        hit = alive & planes_ref[i]
        c1 = lane_count(hit)
        take = above + c1 >= topk
        tu = jnp.where(take, tu | jnp.left_shift(jnp.int32(1), 31 - i), tu)
        return tu, jnp.where(take, above, above + c1), jnp.where(take, hit, alive ^ hit)

    zero_row = jnp.zeros((1, LANES), I32)
    tu, above, alive = lax.fori_loop(0, 32, bit_body, (zero_row, zero_row, alive0))
    thr = tu ^ INT_MIN
    found = tu != 0
    tie_lane = found & (above + lane_count(alive) > topk)
    has_tie = jnp.max(jnp.where(tie_lane, 1, 0)) > 0

    @pl.when(jnp.logical_not(has_tie))
    def _():
        thr_c = jnp.maximum(thr, INT_MIN + 1)

        def body(kc, carry):
            rows = pl.ds(pl.multiple_of(kc * KEY_CHUNK, KEY_CHUNK), KEY_CHUNK)
            bias_ref[rows, :] = jnp.where(skey_ref[rows, :] >= thr_c, 0.0, neg_inf)
            return carry

        lax.fori_loop(0, nkc, body, 0)

    @pl.when(has_tie)
    def _():
        need = (topk - above).astype(F32)

        def body(kc, seen):
            rows = pl.ds(pl.multiple_of(kc * KEY_CHUNK, KEY_CHUNK), KEY_CHUNK)
            blk = skey_ref[rows, :]
            tied = (blk == thr) & found
            eq = jnp.where(tied, 1.0, 0.0)
            rank = _dot(lstrict_ref[...], eq.astype(BF16)) + seen
            keep = (blk > thr) | (tied & (rank < need))
            bias_ref[rows, :] = jnp.where(keep, 0.0, neg_inf)
            return seen + jnp.sum(eq, axis=0, keepdims=True)

        lax.fori_loop(0, nkc, body, jnp.zeros((1, LANES), F32))

    def logit_chunk(kc, m8):
        k0 = pl.multiple_of(CHUNK + kc * KEY_CHUNK, CHUNK)
        rows = pl.ds(pl.multiple_of(kc * KEY_CHUNK, KEY_CHUNK), KEY_CHUNK)
        s = _dot_nt(kr_ref[0, pl.ds(k0, KEY_CHUNK), :], q_hq) * scale
        s = s + jnp.concatenate([bias_ref[rows, :]] * B_HEADS, axis=1)
        st_ref[rows, :] = s
        return jnp.maximum(m8, jnp.max(s.reshape(KEY_CHUNK // 8, 8, B_HEADS * LANES), axis=0))

    m8 = lax.fori_loop(0, nkc, logit_chunk, jnp.full((8, B_HEADS * LANES), neg_inf, F32))
    mrows = lax.broadcasted_iota(I32, (LANES, 1), 0)
    s_meta = _dot_nt(kr_ref[0, 0:LANES, :], q_hq) * scale
    s_meta = jnp.where((mrows >= PAD) & (mrows < CHUNK), s_meta, neg_inf)
    m = jnp.maximum(jnp.max(m8, axis=0, keepdims=True), jnp.max(s_meta, axis=0, keepdims=True))

    p_meta = jnp.exp(s_meta - m)
    l0 = jnp.sum(p_meta, axis=0, keepdims=True)
    o0 = _dot(vmt_ref[...], p_meta.astype(BF16))

    def pv_chunk(kc, carry):
        l, o_t = carry
        rows = pl.ds(pl.multiple_of(kc * KEY_CHUNK, KEY_CHUNK), KEY_CHUNK)
        pr = jnp.exp(st_ref[rows, :] - m)
        vblk = jnp.concatenate([vt_ref[2 * kc], vt_ref[2 * kc + 1]], axis=1)
        return l + jnp.sum(pr, axis=0, keepdims=True), o_t + _dot(vblk, pr.astype(BF16))

    l, o_t = lax.fori_loop(0, nkc, pv_chunk, (l0, o0))
    o_hq = (o_t / l).T
    for hd in range(B_HEADS):
        out_ref[0, pl.ds(r0, qpair), hd * B_DH:(hd + 1) * B_DH] = o_hq[hd * LANES:(hd + 1) * LANES, :]


def _dsa_call(qr, kr, v, iqr, ike, iko, sm, lstrict, seq, topk):
    bsz, tp, _ = qr.shape
    npairs = seq // (2 * CHUNK)
    full = lambda w_: pl.BlockSpec((1, tp, w_), lambda b, p: (b, 0, 0))
    hw = B_HEADS * B_DH
    return pl.pallas_call(
        functools.partial(_dsa_body, seq=seq, topk=topk),
        grid=(bsz, npairs),
        in_specs=[full(hw), full(B_DH), full(B_DH), full(hw), full(LANES), full(LANES), full(LANES),
                  _const_spec(lstrict.shape)],
        out_specs=full(hw),
        out_shape=jax.ShapeDtypeStruct((bsz, tp, hw), F32),
        scratch_shapes=[pltpu.VMEM((seq // LANES, B_DH, LANES), BF16), pltpu.VMEM((B_DH, LANES), BF16),
                        pltpu.VMEM((seq, LANES), I32), pltpu.VMEM((32, seq // 32, LANES), I32),
                        pltpu.VMEM((seq, LANES), F32),
                        pltpu.VMEM((seq, hw), F32)],
        compiler_params=_params(("parallel", "arbitrary")),
        name="dsa",
    )(qr, kr, v, iqr, ike, iko, sm, lstrict)


def _mix_body(h_ref, oa_ref, ob_ref, g_ref, wg_ref, bg_ref, wa_ref, wb_ref, wo_ref, out_ref, *, rb):
    x = h_ref[0]
    d = x.shape[-1]
    n = _rms(x, g_ref[...]).astype(BF16)
    gates = _sigmoid(_dot(n, wg_ref[...]) + bg_ref[...])
    y = gates[:, :d] * _dot(oa_ref[0].astype(BF16), wa_ref[...]) + gates[:, d:] * _dot(ob_ref[0].astype(BF16), wb_ref[...])
    out = x + _dot(y.astype(BF16), wo_ref[...])
    rows = pl.program_id(1) * rb + lax.broadcasted_iota(I32, (rb, 1), 0)
    out_ref[0] = jnp.where(rows >= PAD, out, 0.0)


def _mix_call(h, oa, ob, gain, wg, bg, wa, wb, wo, rb):
    bsz, tp, d = h.shape
    row = lambda w_: pl.BlockSpec((1, rb, w_), lambda b, t: (b, t, 0))
    return pl.pallas_call(
        functools.partial(_mix_body, rb=rb),
        grid=(bsz, tp // rb),
        in_specs=[row(d), row(oa.shape[-1]), row(ob.shape[-1]), _const_spec((1, d)), _const_spec(wg.shape),
                  _const_spec(bg.shape), _const_spec(wa.shape), _const_spec(wb.shape), _const_spec(wo.shape)],
        out_specs=row(d),
        out_shape=jax.ShapeDtypeStruct((bsz, tp, d), F32),
        compiler_params=_params(("parallel", "arbitrary")),
        name="mix",
    )(h, oa, ob, gain, wg, bg, wa, wb, wo)


def _ffn_body(h_ref, g_ref, wup_ref, cw_ref, wd_ref, out_ref, carry_ref, *, rb, dff):
    t = pl.program_id(1)

    @pl.when(t == 0)
    def _():
        carry_ref[...] = jnp.zeros_like(carry_ref)

    x = h_ref[0]
    n = _rms(x, g_ref[...]).astype(BF16)
    rows = lax.broadcasted_iota(I32, (rb, 1), 0)
    acc = jnp.zeros(x.shape, F32)
    for c in range(dff // FF_CHUNK):
        cols = slice(c * FF_CHUNK, (c + 1) * FF_CHUNK)
        gate = _dot(n, wup_ref[:, cols])
        up = _dot(n, wup_ref[:, dff + c * FF_CHUNK:dff + (c + 1) * FF_CHUNK])
        prev = carry_ref[:, cols]
        g1 = jnp.where(rows == 0, prev[7:8], pltpu.roll(gate, 1, 0))
        g2 = jnp.where(rows == 0, prev[6:7], jnp.where(rows == 1, prev[7:8], pltpu.roll(gate, 2, 0)))
        cw = cw_ref[:, cols]
        conv = g2 * cw[0:1] + g1 * cw[1:2] + gate * cw[2:3]
        carry_ref[:, cols] = gate[rb - 8:rb]
        acc = acc + _dot((_silu(conv) * up).astype(BF16), wd_ref[cols, :])
    out = x + acc
    out_ref[0] = jnp.where(t * rb + rows >= PAD, out, 0.0)


def _ffn_call(h, gain, wup, cw, wd, rb):
    bsz, tp, d = h.shape
    dff = wd.shape[0]
    row = pl.BlockSpec((1, rb, d), lambda b, t: (b, t, 0))
    return pl.pallas_call(
        functools.partial(_ffn_body, rb=rb, dff=dff),
        grid=(bsz, tp // rb),
        in_specs=[row, _const_spec((1, d)), _const_spec(wup.shape), _const_spec(cw.shape), _const_spec(wd.shape)],
        out_specs=row,
        out_shape=jax.ShapeDtypeStruct((bsz, tp, d), F32),
        scratch_shapes=[pltpu.VMEM((8, dff), F32)],
        compiler_params=_params(("parallel", "arbitrary")),
        name="ffn",
    )(h, gain, wup, cw, wd)


def _rope_tables(tp):
    pos = (jnp.arange(tp, dtype=F32) - PAD)[:, None]

    def cs(dim):
        inv = 1.0 / (ROPE_THETA ** (jnp.arange(0, dim, 2, dtype=F32) / dim))
        ang = pos * inv[None, :]
        return jnp.cos(ang), jnp.sin(ang)

    ca, sa = cs(B_DH)
    ci, si = cs(IDX_DIM)
    zi = jnp.zeros_like(si)
    return (jnp.concatenate([ca, ca], axis=1), jnp.concatenate([-sa, sa], axis=1),
            jnp.concatenate([ci] * 4, axis=1), jnp.concatenate([-si, zi, -si, zi], axis=1),
            jnp.concatenate([zi, si, zi, si], axis=1))


def _pad_lanes(vec):
    return jnp.zeros((1, LANES), F32).at[0, :vec.shape[0]].set(vec.astype(F32))


def _head_expander():
    pw, hw = A_HEADS * CHUNK, A_HEADS * A_DK
    head = np.concatenate([np.arange(pw) // CHUNK, np.arange(hw) // A_DK])
    src = np.arange(LANES)[:, None]
    return jnp.asarray(np.concatenate([src == head[None, :], src == A_HEADS + head[None, :]], axis=1), BF16)


def kernel(x, meta_tokens, norm_mix, w_in, conv_a, a_log, dt_bias, a_out_norm, q_norm, k_norm, kidx_norm,
           w_branch_a, w_branch_b, w_gate, b_gate, w_out, norm_ffn, w_up, conv_ffn, w_down):
    bsz, seq, d = x.shape
    depth = w_in.shape[0]
    assert seq % KEY_CHUNK == 0
    tp = CHUNK + seq
    rb = _row_block(tp // CHUNK)
    topk = min(TOPK_MAX, seq // TOPK_DIV)

    meta = jnp.broadcast_to(meta_tokens.astype(x.dtype)[None], (bsz, N_META, d))
    h = jnp.concatenate([jnp.zeros((bsz, PAD, d), x.dtype), meta, x], axis=1)

    tables = _rope_tables(tp)
    egb = _head_expander()
    tri = np.arange(CHUNK)
    lt3 = jnp.asarray(np.tile(tri[:, None] >= tri[None, :], (1, 3)), BF16)
    id3 = jnp.asarray(np.tile(np.eye(CHUNK), (1, 3)), BF16)
    trk = np.arange(KEY_CHUNK)
    lstrict = jnp.asarray(trk[:, None] > trk[None, :], BF16)

    hw = A_HEADS * A_DK
    o = np.cumsum((0, hw, hw, hw, hw, A_HEADS, A_HEADS, B_HEADS * B_DH, B_DH, B_DH,
                   IDX_HEADS * IDX_DIM, IDX_DIM, IDX_HEADS))
    for l in range(depth):
        w = w_in[l]
        small = jnp.concatenate([w[:, o[4]:o[6]], w[:, o[11]:o[12]],
                                 jnp.zeros((d, LANES - 2 * A_HEADS - IDX_HEADS), w.dtype)], axis=1)
        w_p = jnp.concatenate([w[:, o[0]:o[3]], w[:, o[3]:o[4]], small, w[:, o[6]:o[7]], w[:, o[7]:o[9]],
                               w[:, o[9]:o[10]], w[:, o[10]:o[11]], w[:, o[10]:o[11]]], axis=1).astype(BF16)
        qkv, z, sm, qr, kr, v, iqr, ike, iko = _proj_call(
            h, norm_mix[l][None], w_p, tables, q_norm[l][None], k_norm[l][None],
            jnp.concatenate([kidx_norm[l], kidx_norm[l]])[None], rb)
        o_a = _gdn_call(qkv, z, sm, conv_a[l], _pad_lanes(a_log[l]), _pad_lanes(dt_bias[l]),
                        a_out_norm[l][None], lt3, id3, egb, rb)
        o_b = _dsa_call(qr, kr, v, iqr, ike, iko, sm, lstrict, seq, topk)
        h = _mix_call(h, o_a, o_b, norm_mix[l][None], w_gate[l].astype(BF16), b_gate[l][None],
                      w_branch_a[l].astype(BF16), w_branch_b[l].astype(BF16), w_out[l].astype(BF16), rb)
        h = _ffn_call(h, norm_ffn[l][None], w_up[l].astype(BF16), conv_ffn[l], w_down[l].astype(BF16), rb)
    return h[:, CHUNK:]
```

```python
import functools
import math

import jax
import jax.numpy as jnp
import numpy as np
from jax import lax
from jax.experimental import pallas as pl
from jax.experimental.pallas import tpu as pltpu

F32 = jnp.float32
BF16 = jnp.bfloat16
I32 = jnp.int32

CHUNK = 64
N_META = 16
PAD = CHUNK - N_META
ROPE_THETA = 10000.0
EPS = 1e-6
A_HEADS = 4
A_DK = 128
A_CONV = 4
B_HEADS = 4
B_DH = 128
IDX_HEADS = 8
IDX_DIM = 64
TOPK_MAX = 256
TOPK_DIV = 4
FFN_CONV = 3
LANES = 128
KEY_CHUNK = 512
PLANE_ROWS = 256
FF_CHUNK = 256
INT_MIN = -(2 ** 31)
VMEM_LIMIT_BYTES = 56 * 1024 * 1024

NT_DIMS = (((1,), (1,)), ((), ()))


def _dot(a, b):
    return jnp.dot(a, b, preferred_element_type=F32)


def _dot_nt(a, b):
    return lax.dot_general(a, b, NT_DIMS, preferred_element_type=F32)


def _dot_exact(a, b):
    return jnp.dot(a, b, preferred_element_type=F32, precision=lax.Precision.HIGHEST)


def _rms(x, gain):
    return x * lax.rsqrt(jnp.mean(x * x, axis=-1, keepdims=True) + EPS) * gain


def _sigmoid(x):
    return 1.0 / (1.0 + jnp.exp(-x))


def _silu(x):
    return x * _sigmoid(x)


def _softplus(x):
    return jnp.maximum(x, 0.0) + jnp.log(1.0 + jnp.exp(-jnp.abs(x)))


def _row_block(n_chunks):
    g = max(d for d in range(1, 12) if n_chunks % d == 0)
    return g * CHUNK


def _params(sem):
    return pltpu.CompilerParams(dimension_semantics=sem, vmem_limit_bytes=VMEM_LIMIT_BYTES)


def _const_spec(shape):
    nd = len(shape)
    return pl.BlockSpec(shape, lambda *_: (0,) * nd)


PROJ_WIDTHS = (1536, 512, 128, 512, 256, 512, 128)


def _proj_body(h_ref, g_ref, w_ref, cosa_ref, sina_ref, cosi_ref, sinlo_ref, sinhi_ref,
               qn_ref, kn_ref, kin_ref,
               qkv_ref, z_ref, sm_ref, qr_ref, kr_ref, v_ref, iqr_ref, ike_ref, iko_ref):
    n = _rms(h_ref[0], g_ref[...]).astype(BF16)
    offs = np.cumsum((0,) + PROJ_WIDTHS)

    def mm(i):
        return _dot(n, w_ref[:, offs[i]:offs[i + 1]])

    qkv_ref[0] = mm(0)
    z_ref[0] = mm(1)
    sm_ref[0] = mm(2)

    ca, sa = cosa_ref[...], sina_ref[...]
    q = mm(3)
    parts = []
    for hd in range(B_HEADS):
        qh = _rms(q[:, hd * B_DH:(hd + 1) * B_DH], qn_ref[...])
        parts.append(qh * ca + pltpu.roll(qh, B_DH // 2, 1) * sa)
    qr_ref[0] = (jnp.concatenate(parts, axis=1) * (B_DH ** -0.5)).astype(BF16)

    kv = mm(4)
    k = _rms(kv[:, :B_DH], kn_ref[...])
    kr_ref[0] = (k * ca + pltpu.roll(k, B_DH // 2, 1) * sa).astype(BF16)
    v_ref[0] = kv[:, B_DH:].astype(BF16)

    ci, slo, shi = cosi_ref[...], sinlo_ref[...], sinhi_ref[...]
    iq = mm(5)
    parts = []
    for hp in range(IDX_HEADS // 2):
        x = iq[:, hp * LANES:(hp + 1) * LANES]
        parts.append(x * ci + pltpu.roll(x, LANES - IDX_DIM // 2, 1) * slo + pltpu.roll(x, IDX_DIM // 2, 1) * shi)
    iqr_ref[0] = jnp.concatenate(parts, axis=1).astype(BF16)

    ik = _rms(mm(6), kin_ref[...])
    ik = ik * ci + pltpu.roll(ik, IDX_DIM // 2, 1) * (slo + shi)
    lane = lax.broadcasted_iota(I32, (1, LANES), 1)
    ike_ref[0] = jnp.where(lane < IDX_DIM, ik, 0.0).astype(BF16)
    iko_ref[0] = jnp.where(lane >= IDX_DIM, ik, 0.0).astype(BF16)


def _proj_call(h, gain, w, tables, qn, kn, kin, rb):
    bsz, tp, d = h.shape
    nt = tp // rb
    row = lambda w_, dt: (jax.ShapeDtypeStruct((bsz, tp, w_), dt),
                          pl.BlockSpec((1, rb, w_), lambda b, t: (b, t, 0)))
    outs = [row(1536, F32), row(512, F32), row(128, F32), row(512, BF16), row(128, BF16),
            row(128, BF16), row(512, BF16), row(128, BF16), row(128, BF16)]
    tab_spec = pl.BlockSpec((rb, LANES), lambda b, t: (t, 0))
    return pl.pallas_call(
        _proj_body,
        grid=(bsz, nt),
        in_specs=[pl.BlockSpec((1, rb, d), lambda b, t: (b, t, 0)), _const_spec((1, d)), _const_spec(w.shape)]
                 + [tab_spec] * 5 + [_const_spec((1, LANES))] * 3,
        out_specs=[o[1] for o in outs],
        out_shape=[o[0] for o in outs],
        compiler_params=_params(("parallel", "arbitrary")),
        name="proj",
    )(h, gain, w, *tables, qn, kn, kin)


def _split3(x):
    x1 = x.astype(BF16)
    r1 = x - x1.astype(F32)
    x2 = r1.astype(BF16)
    x3 = (r1 - x2.astype(F32)).astype(BF16)
    return jnp.concatenate([x1, x2, x3], axis=0)


def _gdn_body(qkv_ref, z_ref, sm_ref, cw_ref, alog_ref, dtb_ref, ng_ref, lt3_ref, id3_ref, egb_ref,
              out_ref, xe_ref, s01_ref, s23_ref, u_ref, wq_ref, qkd_ref, kdt_ref, egl_ref, *, rb):
    t = pl.program_id(1)
    nwide = 3 * A_HEADS * A_DK

    @pl.when(t == 0)
    def _():
        xe_ref[0:8, :] = jnp.zeros((8, nwide), F32)
        s01_ref[...] = jnp.zeros_like(s01_ref)
        s23_ref[...] = jnp.zeros_like(s23_ref)

    xe_ref[8:8 + rb, :] = qkv_ref[0]
    cw = cw_ref[...]

    hw = A_HEADS * A_DK
    pw = A_HEADS * CHUNK
    ii = lax.broadcasted_iota(I32, (CHUNK, pw), 0)
    jj = lax.broadcasted_iota(I32, (CHUNK, pw), 1) & (CHUNK - 1)
    colhead = lax.broadcasted_iota(I32, (CHUNK, pw), 1) >> 6
    eye_p = ii == jj
    bd_mask = (lax.broadcasted_iota(I32, (pw, pw), 0) >> 6) == (lax.broadcasted_iota(I32, (pw, pw), 1) >> 6)
    bdk_mask = (lax.broadcasted_iota(I32, (pw, hw), 0) >> 6) == (lax.broadcasted_iota(I32, (pw, hw), 1) >> 7)
    pair_mask = (lax.broadcasted_iota(I32, (pw, pw), 0) >> 7) == (lax.broadcasted_iota(I32, (pw, pw), 1) >> 7)
    lane = lax.broadcasted_iota(I32, (1, LANES), 1)
    rows64 = lax.broadcasted_iota(I32, (CHUNK, 1), 0)
    neg_a = -jnp.exp(alog_ref[...])
    dtb = dtb_ref[...]
    ng = ng_ref[...]

    def block_diag(xp):
        return jnp.where(bd_mask, jnp.concatenate([xp.astype(BF16)] * A_HEADS, axis=0), 0)

    def stack_heads(xp):
        return jnp.concatenate([jnp.where(colhead == hd, xp, 0.0) for hd in range(A_HEADS)], axis=0)

    def l2n(x):
        parts = []
        for hd in range(A_HEADS):
            xh = x[:, hd * A_DK:(hd + 1) * A_DK]
            parts.append(xh * lax.rsqrt(jnp.sum(xh * xh, axis=-1, keepdims=True) + EPS))
        return jnp.concatenate(parts, axis=1)

    zeros = jnp.zeros((CHUNK, 2 * A_DK), F32)

    def pair_lanes(x, a_, b_):
        return jnp.concatenate([x[a_ * CHUNK:(a_ + 1) * CHUNK], x[b_ * CHUNK:(b_ + 1) * CHUNK]], axis=1)

    def local_load(c):
        r0 = c * CHUNK if isinstance(c, int) else pl.multiple_of(c * CHUNK, CHUNK)
        return xe_ref[pl.ds(r0, CHUNK + 8), :], sm_ref[0, pl.ds(r0, CHUNK), :], r0

    def local_compute(xw, sm, r0):
        acc = pltpu.roll(xw, 3, 0)[8:] * cw[0:1]
        acc = acc + pltpu.roll(xw, 2, 0)[8:] * cw[1:2]
        acc = acc + pltpu.roll(xw, 1, 0)[8:] * cw[2:3]
        acc = acc + xw[8:] * cw[3:4]
        xq = _silu(acc)
        q = l2n(xq[:, :hw]) * (A_DK ** -0.5)
        k = l2n(xq[:, hw:2 * hw])
        v = xq[:, 2 * hw:]

        g = jnp.where(lane < A_HEADS, neg_a * _softplus(sm + dtb), 0.0)
        g = jnp.where(t * rb + r0 + rows64 >= PAD, g, 0.0)
        x3 = _split3(jnp.where(lane < A_HEADS, g, _sigmoid(sm)))
        copies = _dot(x3, egb_ref[...]).astype(BF16)
        yield
        gx = _dot(lt3_ref[...], copies[:, :pw + hw])
        bx = _dot(id3_ref[...], copies[:, pw + hw:])
        yield
        gp, gq = gx[:, :pw], gx[:, pw:]
        bp, bq = bx[:, :pw], bx[:, pw:]
        grow = jnp.sum(jnp.where(eye_p, gp, 0.0), axis=0, keepdims=True)
        decay = jnp.where(ii >= jj, jnp.exp(gp - grow), 0.0)

        kb = k.astype(BF16)
        bdk = jnp.where(bdk_mask, jnp.concatenate([kb] * A_HEADS, axis=0), 0)
        kq = _dot_nt(jnp.concatenate([kb, q.astype(BF16)], axis=0), bdk)
        yield
        kk_p, qk_p = kq[:CHUNK], kq[CHUNK:]

        a = -jnp.where(ii > jj, bp * kk_p * decay, 0.0)
        tinv = jnp.where(eye_p, 1.0, 0.0) + a
        pk = _dot(a.astype(BF16), block_diag(a))
        yield
        for _ in range(4):
            pt = _dot(jnp.concatenate([pk, tinv], axis=0).astype(BF16), block_diag(pk))
            yield
            tinv = tinv + pt[CHUNK:]
            pk = pt[:CHUNK]
        tinv = tinv + _dot(tinv.astype(BF16), block_diag(pk))
        yield

        eg = jnp.exp(gq)
        vb = v * bq
        kbg = k * bq * eg
        rv = jnp.concatenate(
            [jnp.concatenate([vb[:, hd * A_DK:(hd + 1) * A_DK], kbg[:, hd * A_DK:(hd + 1) * A_DK]], axis=1)
             for hd in range(A_HEADS)], axis=0)
        uw = _dot(stack_heads(tinv).astype(BF16), rv.astype(BF16))
        yield
        u, w = uw[:, :A_DK], uw[:, A_DK:]

        qg = q * eg
        glast = gq[CHUNK - 1:CHUNK, :]
        kd = k * jnp.exp(glast - gq)
        return (u,
                jnp.concatenate([pair_lanes(w, 0, 1), qg[:, :2 * A_DK]], axis=0).astype(BF16),
                jnp.concatenate([pair_lanes(w, 2, 3), qg[:, 2 * A_DK:]], axis=0).astype(BF16),
                stack_heads(qk_p * decay).astype(BF16),
                jnp.concatenate([kd[:, :2 * A_DK], zeros], axis=0).T.astype(BF16),
                jnp.concatenate([kd[:, 2 * A_DK:], zeros], axis=0).T.astype(BF16),
                jnp.broadcast_to(jnp.exp(glast), (8, hw)))

    def local_store(c, vals):
        u_ref[c], wq_ref[c, 0], wq_ref[c, 1], qkd_ref[c], kdt_ref[c, 0], kdt_ref[c, 1], egl_ref[c] = vals

    def local_chain(c):
        vals = yield from local_compute(*local_load(c))
        local_store(c, vals)

    def state_chain(chunks):
        def unpair(r):
            return [r[:, :A_DK], r[:, A_DK:]]

        def state_update(s_old, kdt, vn_pair, egl_pair):
            upd = _dot(kdt, jnp.concatenate([vn_pair, zeros], axis=0).astype(BF16))
            return s_old * egl_pair + jnp.where(pair_mask, upd, 0.0)

        s01 = s01_ref[...]
        s23 = s23_ref[...]
        for c in chunks:
            r0 = c * CHUNK if isinstance(c, int) else pl.multiple_of(c * CHUNK, CHUNK)
            r01 = _dot(wq_ref[c, 0], s01.astype(BF16))
            r23 = _dot(wq_ref[c, 1], s23.astype(BF16))
            yield
            ws = jnp.concatenate(unpair(r01[:CHUNK]) + unpair(r23[:CHUNK]), axis=0)
            qs = jnp.concatenate(unpair(r01[CHUNK:]) + unpair(r23[CHUNK:]), axis=0)
            vn = u_ref[c] - ws
            o_rs = qs + _dot(qkd_ref[c], vn.astype(BF16))
            egl = egl_ref[c][0:1, :]
            s01 = state_update(s01, kdt_ref[c, 0], pair_lanes(vn, 0, 1), egl[:, :2 * A_DK])
            s23 = state_update(s23, kdt_ref[c, 1], pair_lanes(vn, 2, 3), egl[:, 2 * A_DK:])
            yield
            zz = z_ref[0, pl.ds(r0, CHUNK), :]
            parts = [_rms(o_rs[hd * CHUNK:(hd + 1) * CHUNK], ng) for hd in range(A_HEADS)]
            out_ref[0, pl.ds(r0, CHUNK), :] = jnp.concatenate(parts, axis=1) * _silu(zz)
        s01_ref[...] = s01
        s23_ref[...] = s23

    def run_lockstep(chains):
        live = list(chains)
        while live:
            for gen in list(live):
                try:
                    next(gen)
                except StopIteration:
                    live.remove(gen)

    nc = rb // CHUNK
    npairs = nc // 2
    if npairs == 0:
        run_lockstep([local_chain(0)])
        run_lockstep([state_chain([0])])
    else:
        run_lockstep([local_chain(0), local_chain(1)])

        def steady(i, carry):
            run_lockstep([local_chain(2 * i), local_chain(2 * i + 1), state_chain([2 * i - 2, 2 * i - 1])])
            return carry

        lax.fori_loop(1, npairs, steady, 0)
        last = [2 * npairs - 2, 2 * npairs - 1]
        if nc % 2:
            run_lockstep([local_chain(nc - 1), state_chain(last)])
            run_lockstep([state_chain([nc - 1])])
        else:
            run_lockstep([state_chain(last)])
    xe_ref[0:8, :] = xe_ref[rb:rb + 8, :]


def _gdn_call(qkv, z, sm, cw, alog, dtb, ng, lt3, id3, egb, rb):
    bsz, tp, _ = qkv.shape
    nt = tp // rb
    nc = rb // CHUNK
    hw = A_HEADS * A_DK
    pw = A_HEADS * CHUNK
    row = lambda w_: pl.BlockSpec((1, rb, w_), lambda b, t: (b, t, 0))
    return pl.pallas_call(
        functools.partial(_gdn_body, rb=rb),
        grid=(bsz, nt),
        in_specs=[row(3 * hw), row(hw), row(LANES), _const_spec(cw.shape), _const_spec((1, LANES)),
                  _const_spec((1, LANES)), _const_spec((1, LANES)), _const_spec(lt3.shape),
                  _const_spec(id3.shape), _const_spec(egb.shape)],
        out_specs=row(hw),
        out_shape=jax.ShapeDtypeStruct((bsz, tp, hw), F32),
        scratch_shapes=[pltpu.VMEM((rb + 8, 3 * hw), F32),
                        pltpu.VMEM((2 * A_DK, 2 * A_DK), F32), pltpu.VMEM((2 * A_DK, 2 * A_DK), F32),
                        pltpu.VMEM((nc, pw, A_DK), F32), pltpu.VMEM((nc, 2, 2 * CHUNK, 2 * A_DK), BF16),
                        pltpu.VMEM((nc, pw, pw), BF16), pltpu.VMEM((nc, 2, 2 * A_DK, 2 * CHUNK), BF16),
                        pltpu.VMEM((nc, 8, hw), F32)],
        compiler_params=_params(("parallel", "arbitrary")),
        name="gdn",
    )(qkv, z, sm, cw, alog, dtb, ng, lt3, id3, egb)


def _dsa_body(qr_ref, kr_ref, v_ref, iqr_ref, ike_ref, iko_ref, sm_ref, lstrict_ref, out_ref,
              vt_ref, vmt_ref, skey_ref, planes_ref, st_ref, *, seq, topk):
    p = pl.program_id(1)
    iw_scale = IDX_HEADS ** -0.5 * IDX_DIM ** -0.5
    qpair = 2 * CHUNK
    neg_inf = -jnp.inf

    @pl.when(p == 0)
    def _():
        planes_ref[...] = jnp.zeros_like(planes_ref)

        def vblk(i, carry):
            r = pl.multiple_of(CHUNK + i * LANES, CHUNK)
            vt_ref[i] = v_ref[0, pl.ds(r, LANES), :].astype(F32).T.astype(BF16)
            return carry

        lax.fori_loop(0, seq // LANES, vblk, 0)
        vmt_ref[...] = v_ref[0, 0:LANES, :].astype(F32).T.astype(BF16)

        qm = qr_ref[0, 0:CHUNK, :]
        km = kr_ref[0, 0:CHUNK, :]
        vm = v_ref[0, 0:CHUNK, :]
        colv = lax.broadcasted_iota(I32, (1, CHUNK), 1) >= PAD
        parts = []
        for hd in range(B_HEADS):
            s = _dot_nt(qm[:, hd * B_DH:(hd + 1) * B_DH], km)
            s = jnp.where(colv, s, neg_inf)
            e = jnp.exp(s - jnp.max(s, axis=-1, keepdims=True))
            pr = e / jnp.sum(e, axis=-1, keepdims=True)
            parts.append(_dot(pr.astype(BF16), vm))
        out_ref[0, 0:CHUNK, :] = jnp.concatenate(parts, axis=1)

    r0 = pl.multiple_of(CHUNK + p * qpair, CHUNK)
    qb = qr_ref[0, pl.ds(r0, qpair), :]
    q_hq = jnp.concatenate([qb[:, hd * B_DH:(hd + 1) * B_DH] for hd in range(B_HEADS)], axis=0)
    ib = iqr_ref[0, pl.ds(r0, qpair), :]
    iq_hq = jnp.concatenate([ib[:, hp * LANES:(hp + 1) * LANES] for hp in range(IDX_HEADS // 2)], axis=0)
    smt = sm_ref[0, pl.ds(r0, qpair), :].T
    wts = [smt[2 * A_HEADS + hd:2 * A_HEADS + hd + 1, :] * iw_scale for hd in range(IDX_HEADS)]
    lane = lax.broadcasted_iota(I32, (1, LANES), 1)
    limit = jnp.where(lane < CHUNK, (2 * p + 1) * CHUNK, (2 * p + 2) * CHUNK)
    nkc = ((p + 1) * qpair + KEY_CHUNK - 1) // KEY_CHUNK
    krows = lax.broadcasted_iota(I32, (KEY_CHUNK, 1), 0)

    def score_chunk(kc, carry):
        k0 = pl.multiple_of(CHUNK + kc * KEY_CHUNK, CHUNK)
        le = _dot_nt(ike_ref[0, pl.ds(k0, KEY_CHUNK), :], iq_hq)
        lo = _dot_nt(iko_ref[0, pl.ds(k0, KEY_CHUNK), :], iq_hq)
        st_ref[pl.ds(pl.multiple_of(kc * KEY_CHUNK, KEY_CHUNK), KEY_CHUNK), :] = _dot_nt(
            kr_ref[0, pl.ds(k0, KEY_CHUNK), :], q_hq)
        sc = jnp.zeros((KEY_CHUNK, LANES), F32)
        for hp in range(IDX_HEADS // 2):
            sc = sc + wts[2 * hp] * jnp.maximum(le[:, hp * LANES:(hp + 1) * LANES], 0.0)
            sc = sc + wts[2 * hp + 1] * jnp.maximum(lo[:, hp * LANES:(hp + 1) * LANES], 0.0)
        bits = lax.bitcast_convert_type(sc, I32)
        bits = jnp.where(sc == 0.0, 0, bits)
        key = bits ^ ((bits >> 31) & 0x7FFFFFFF)
        valid = kc * KEY_CHUNK + krows < limit
        key = jnp.where(valid, key, INT_MIN)
        skey_ref[pl.ds(pl.multiple_of(kc * KEY_CHUNK, KEY_CHUNK), KEY_CHUNK), :] = key
        for gi in range(KEY_CHUNK // PLANE_ROWS):
            a = [key[gi * PLANE_ROWS + 8 * r:gi * PLANE_ROWS + 8 * r + 8, :] ^ INT_MIN for r in range(32)]
            j, msk = 16, 0x0000FFFF
            while j:
                k = 0
                while k < 32:
                    tt = (a[k] ^ lax.shift_right_logical(a[k + j], jnp.int32(j))) & msk
                    a[k] = a[k] ^ tt
                    a[k + j] = a[k + j] ^ (tt << j)
                    k = (k + j + 1) & ~j
                j >>= 1
                msk = (msk ^ (msk << j)) & 0xFFFFFFFF
            g8 = pl.multiple_of((kc * (KEY_CHUNK // PLANE_ROWS) + gi) * 8, 8)
            for b in range(32):
                planes_ref[b, pl.ds(g8, 8), :] = a[b]
        return carry

    lax.fori_loop(0, nkc, score_chunk, 0)

    grp = seq // 32
    alive0 = jnp.where(lax.broadcasted_iota(I32, (grp, 1), 0) < nkc * (KEY_CHUNK // 32),
                       jnp.full((grp, LANES), -1, I32), 0)

    def lane_count(words):
        c = lax.population_count(words)
        if grp > 8:
            c = jnp.sum(c.reshape(grp // 8, 8, LANES), axis=0)
        return jnp.sum(c, axis=0, keepdims=True)

    def bit_body(i, carry):
        tu, above, alive = carry
        hit = alive & planes_ref[i]
        c1 = lane_count(hit)
        take = above + c1 >= topk
        tu = jnp.where(take, tu | jnp.left_shift(jnp.int32(1), 31 - i), tu)
        return tu, jnp.where(take, above, above + c1), jnp.where(take, hit, alive ^ hit)

    zero_row = jnp.zeros((1, LANES), I32)
    tu, above, alive = lax.fori_loop(0, 32, bit_body, (zero_row, zero_row, alive0))
    thr = tu ^ INT_MIN
    found = tu != 0
    tie_lane = found & (above + lane_count(alive) > topk)
    has_tie = jnp.max(jnp.where(tie_lane, 1, 0)) > 0

    @pl.when(has_tie)
    def _():
        need = (topk - above).astype(F32)

        def body(kc, seen):
            rows = pl.ds(pl.multiple_of(kc * KEY_CHUNK, KEY_CHUNK), KEY_CHUNK)
            blk = skey_ref[rows, :]
            tied = (blk == thr) & found
            eq = jnp.where(tied, 1.0, 0.0)
            rank = _dot(lstrict_ref[...], eq.astype(BF16)) + seen
            skey_ref[rows, :] = jnp.where(tied & (rank >= need), INT_MIN, blk)
            return seen + jnp.sum(eq, axis=0, keepdims=True)

        lax.fori_loop(0, nkc, body, jnp.zeros((1, LANES), F32))

    thr_c = jnp.maximum(thr, INT_MIN + 1)
    mrows = lax.broadcasted_iota(I32, (LANES, 1), 0)
    s_meta = _dot_nt(kr_ref[0, 0:LANES, :], q_hq)
    s_meta = jnp.where((mrows >= PAD) & (mrows < CHUNK), s_meta, neg_inf)
    m0 = jnp.max(s_meta, axis=0, keepdims=True)
    p_meta = jnp.exp(s_meta - m0)
    l0 = jnp.sum(p_meta, axis=0, keepdims=True)
    o0 = _dot(vmt_ref[...], p_meta.astype(BF16))

    def pv_chunk(kc, carry):
        m, l, o_t = carry
        rows = pl.ds(pl.multiple_of(kc * KEY_CHUNK, KEY_CHUNK), KEY_CHUNK)
        bias = jnp.where(skey_ref[rows, :] >= thr_c, 0.0, neg_inf)
        s = st_ref[rows, :] + jnp.concatenate([bias] * B_HEADS, axis=1)
        m_new = jnp.maximum(m, jnp.max(jnp.max(s.reshape(KEY_CHUNK // 8, 8, B_HEADS * LANES), axis=0),
                                       axis=0, keepdims=True))
        pr = jnp.exp(s - m_new)
        alpha = jnp.exp(m - m_new)
        vblk = jnp.concatenate([vt_ref[(KEY_CHUNK // LANES) * kc + i] for i in range(KEY_CHUNK // LANES)], axis=1)
        return (m_new, l * alpha + jnp.sum(pr, axis=0, keepdims=True),
                o_t * alpha + _dot(vblk, pr.astype(BF16)))

    _, l, o_t = lax.fori_loop(0, nkc, pv_chunk, (m0, l0, o0))
    o_hq = (o_t / l).T
    for hd in range(B_HEADS):
        out_ref[0, pl.ds(r0, qpair), hd * B_DH:(hd + 1) * B_DH] = o_hq[hd * LANES:(hd + 1) * LANES, :]


def _dsa_call(qr, kr, v, iqr, ike, iko, sm, lstrict, seq, topk):
    bsz, tp, _ = qr.shape
    npairs = seq // (2 * CHUNK)
    full = lambda w_: pl.BlockSpec((1, tp, w_), lambda b, p: (b, 0, 0))
    hw = B_HEADS * B_DH
    return pl.pallas_call(
        functools.partial(_dsa_body, seq=seq, topk=topk),
        grid=(bsz, npairs),
        in_specs=[full(hw), full(B_DH), full(B_DH), full(hw), full(LANES), full(LANES), full(LANES),
                  _const_spec(lstrict.shape)],
        out_specs=full(hw),
        out_shape=jax.ShapeDtypeStruct((bsz, tp, hw), F32),
        scratch_shapes=[pltpu.VMEM((seq // LANES, B_DH, LANES), BF16), pltpu.VMEM((B_DH, LANES), BF16),
                        pltpu.VMEM((seq, LANES), I32), pltpu.VMEM((32, seq // 32, LANES), I32),
                        pltpu.VMEM((seq, hw), F32)],
        compiler_params=_params(("parallel", "arbitrary")),
        name="dsa",
    )(qr, kr, v, iqr, ike, iko, sm, lstrict)


def _mix_body(h_ref, oa_ref, ob_ref, g_ref, wg_ref, bg_ref, wa_ref, wb_ref, wo_ref, out_ref, *, rb):
    x = h_ref[0]
    d = x.shape[-1]
    n = _rms(x, g_ref[...]).astype(BF16)
    gates = _sigmoid(_dot(n, wg_ref[...]) + bg_ref[...])
    y = gates[:, :d] * _dot(oa_ref[0].astype(BF16), wa_ref[...]) + gates[:, d:] * _dot(ob_ref[0].astype(BF16), wb_ref[...])
    out = x + _dot(y.astype(BF16), wo_ref[...])
    rows = pl.program_id(1) * rb + lax.broadcasted_iota(I32, (rb, 1), 0)
    out_ref[0] = jnp.where(rows >= PAD, out, 0.0)


def _mix_call(h, oa, ob, gain, wg, bg, wa, wb, wo, rb):
    bsz, tp, d = h.shape
    row = lambda w_: pl.BlockSpec((1, rb, w_), lambda b, t: (b, t, 0))
    return pl.pallas_call(
        functools.partial(_mix_body, rb=rb),
        grid=(bsz, tp // rb),
        in_specs=[row(d), row(oa.shape[-1]), row(ob.shape[-1]), _const_spec((1, d)), _const_spec(wg.shape),
                  _const_spec(bg.shape), _const_spec(wa.shape), _const_spec(wb.shape), _const_spec(wo.shape)],
        out_specs=row(d),
        out_shape=jax.ShapeDtypeStruct((bsz, tp, d), F32),
        compiler_params=_params(("parallel", "arbitrary")),
        name="mix",
    )(h, oa, ob, gain, wg, bg, wa, wb, wo)


def _ffn_body(h_ref, g_ref, wup_ref, cw_ref, wd_ref, out_ref, carry_ref, *, rb, dff):
    t = pl.program_id(1)

    @pl.when(t == 0)
    def _():
        carry_ref[...] = jnp.zeros_like(carry_ref)

    x = h_ref[0]
    n = _rms(x, g_ref[...]).astype(BF16)
    rows = lax.broadcasted_iota(I32, (rb, 1), 0)
    acc = jnp.zeros(x.shape, F32)
    for c in range(dff // FF_CHUNK):
        cols = slice(c * FF_CHUNK, (c + 1) * FF_CHUNK)
        gate = _dot(n, wup_ref[:, cols])
        up = _dot(n, wup_ref[:, dff + c * FF_CHUNK:dff + (c + 1) * FF_CHUNK])
        prev = carry_ref[:, cols]
        g1 = jnp.where(rows == 0, prev[7:8], pltpu.roll(gate, 1, 0))
        g2 = jnp.where(rows == 0, prev[6:7], jnp.where(rows == 1, prev[7:8], pltpu.roll(gate, 2, 0)))
        cw = cw_ref[:, cols]
        conv = g2 * cw[0:1] + g1 * cw[1:2] + gate * cw[2:3]
        carry_ref[:, cols] = gate[rb - 8:rb]
        acc = acc + _dot((_silu(conv) * up).astype(BF16), wd_ref[cols, :])
    out = x + acc
    out_ref[0] = jnp.where(t * rb + rows >= PAD, out, 0.0)


def _ffn_call(h, gain, wup, cw, wd, rb):
    bsz, tp, d = h.shape
    dff = wd.shape[0]
    row = pl.BlockSpec((1, rb, d), lambda b, t: (b, t, 0))
    return pl.pallas_call(
        functools.partial(_ffn_body, rb=rb, dff=dff),
        grid=(bsz, tp // rb),
        in_specs=[row, _const_spec((1, d)), _const_spec(wup.shape), _const_spec(cw.shape), _const_spec(wd.shape)],
        out_specs=row,
        out_shape=jax.ShapeDtypeStruct((bsz, tp, d), F32),
        scratch_shapes=[pltpu.VMEM((8, dff), F32)],
        compiler_params=_params(("parallel", "arbitrary")),
        name="ffn",
    )(h, gain, wup, cw, wd)


def _rope_tables(tp):
    pos = (jnp.arange(tp, dtype=F32) - PAD)[:, None]

    def cs(dim):
        inv = 1.0 / (ROPE_THETA ** (jnp.arange(0, dim, 2, dtype=F32) / dim))
        ang = pos * inv[None, :]
        return jnp.cos(ang), jnp.sin(ang)

    ca, sa = cs(B_DH)
    ci, si = cs(IDX_DIM)
    zi = jnp.zeros_like(si)
    return (jnp.concatenate([ca, ca], axis=1), jnp.concatenate([-sa, sa], axis=1),
            jnp.concatenate([ci] * 4, axis=1), jnp.concatenate([-si, zi, -si, zi], axis=1),
            jnp.concatenate([zi, si, zi, si], axis=1))


def _pad_lanes(vec):
    return jnp.zeros((1, LANES), F32).at[0, :vec.shape[0]].set(vec.astype(F32))


def _head_expander():
    pw, hw = A_HEADS * CHUNK, A_HEADS * A_DK
    head = np.concatenate([np.arange(pw) // CHUNK, np.arange(hw) // A_DK])
    src = np.arange(LANES)[:, None]
    return jnp.asarray(np.concatenate([src == head[None, :], src == A_HEADS + head[None, :]], axis=1), BF16)


def kernel(x, meta_tokens, norm_mix, w_in, conv_a, a_log, dt_bias, a_out_norm, q_norm, k_norm, kidx_norm,
           w_branch_a, w_branch_b, w_gate, b_gate, w_out, norm_ffn, w_up, conv_ffn, w_down):
    bsz, seq, d = x.shape
    depth = w_in.shape[0]
    assert seq % KEY_CHUNK == 0
    tp = CHUNK + seq
    rb = _row_block(tp // CHUNK)
    topk = min(TOPK_MAX, seq // TOPK_DIV)

    meta = jnp.broadcast_to(meta_tokens.astype(x.dtype)[None], (bsz, N_META, d))
    h = jnp.concatenate([jnp.zeros((bsz, PAD, d), x.dtype), meta, x], axis=1)

    tables = _rope_tables(tp)
    egb = _head_expander()
    tri = np.arange(CHUNK)
    lt3 = jnp.asarray(np.tile(tri[:, None] >= tri[None, :], (1, 3)), BF16)
    id3 = jnp.asarray(np.tile(np.eye(CHUNK), (1, 3)), BF16)
    trk = np.arange(KEY_CHUNK)
    lstrict = jnp.asarray(trk[:, None] > trk[None, :], BF16)

    hw = A_HEADS * A_DK
    o = np.cumsum((0, hw, hw, hw, hw, A_HEADS, A_HEADS, B_HEADS * B_DH, B_DH, B_DH,
                   IDX_HEADS * IDX_DIM, IDX_DIM, IDX_HEADS))
    for l in range(depth):
        w = w_in[l]
        small = jnp.concatenate([w[:, o[4]:o[6]], w[:, o[11]:o[12]],
                                 jnp.zeros((d, LANES - 2 * A_HEADS - IDX_HEADS), w.dtype)], axis=1)
        w_p = jnp.concatenate([w[:, o[0]:o[3]], w[:, o[3]:o[4]], small, w[:, o[6]:o[7]], w[:, o[7]:o[9]],
                               w[:, o[9]:o[10]], w[:, o[10]:o[11]], w[:, o[10]:o[11]]], axis=1).astype(BF16)
        qkv, z, sm, qr, kr, v, iqr, ike, iko = _proj_call(
            h, norm_mix[l][None], w_p, tables, q_norm[l][None], k_norm[l][None],
            jnp.concatenate([kidx_norm[l], kidx_norm[l]])[None], rb)
        o_a = _gdn_call(qkv, z, sm, conv_a[l], _pad_lanes(a_log[l]), _pad_lanes(dt_bias[l]),
                        a_out_norm[l][None], lt3, id3, egb, rb)
        o_b = _dsa_call(qr, kr, v, iqr, ike, iko, sm, lstrict, seq, topk)
        h = _mix_call(h, o_a, o_b, norm_mix[l][None], w_gate[l].astype(BF16), b_gate[l][None],
                      w_branch_a[l].astype(BF16), w_branch_b[l].astype(BF16), w_out[l].astype(BF16), rb)
        h = _ffn_call(h, norm_ffn[l][None], w_up[l].astype(BF16), conv_ffn[l], w_down[l].astype(BF16), rb)
    return h[:, CHUNK:]
```

```python
import functools
import math

import jax
import jax.numpy as jnp
import numpy as np
from jax import lax
from jax.experimental import pallas as pl
from jax.experimental.pallas import tpu as pltpu

F32 = jnp.float32
BF16 = jnp.bfloat16
I32 = jnp.int32

CHUNK = 64
N_META = 16
PAD = CHUNK - N_META
ROPE_THETA = 10000.0
EPS = 1e-6
A_HEADS = 4
A_DK = 128
A_CONV = 4
B_HEADS = 4
B_DH = 128
IDX_HEADS = 8
IDX_DIM = 64
TOPK_MAX = 256
TOPK_DIV = 4
FFN_CONV = 3
LANES = 128
KEY_CHUNK = 512
PLANE_ROWS = 256
QUERY_GROUPS = 4
FF_CHUNK = 256
INT_MIN = -(2 ** 31)
VMEM_LIMIT_BYTES = 56 * 1024 * 1024

NT_DIMS = (((1,), (1,)), ((), ()))


def _dot(a, b):
    return jnp.dot(a, b, preferred_element_type=F32)


def _dot_nt(a, b):
    return lax.dot_general(a, b, NT_DIMS, preferred_element_type=F32)


def _dot_exact(a, b):
    return jnp.dot(a, b, preferred_element_type=F32, precision=lax.Precision.HIGHEST)


def _rms(x, gain):
    return x * lax.rsqrt(jnp.mean(x * x, axis=-1, keepdims=True) + EPS) * gain


def _sigmoid(x):
    return 1.0 / (1.0 + jnp.exp(-x))


def _silu(x):
    return x * _sigmoid(x)


def _softplus(x):
    return jnp.maximum(x, 0.0) + jnp.log(1.0 + jnp.exp(-jnp.abs(x)))


def _row_block(n_chunks):
    g = max(d for d in range(1, 12) if n_chunks % d == 0)
    return g * CHUNK


def _params(sem):
    return pltpu.CompilerParams(dimension_semantics=sem, vmem_limit_bytes=VMEM_LIMIT_BYTES)


def _const_spec(shape):
    nd = len(shape)
    return pl.BlockSpec(shape, lambda *_: (0,) * nd)


PROJ_WIDTHS = (1536, 512, 128, 512, 256, 512, 128)


def _proj_body(h_ref, g_ref, w_ref, cosa_ref, sina_ref, cosi_ref, sinlo_ref, sinhi_ref,
               qn_ref, kn_ref, kin_ref,
               qkv_ref, z_ref, sm_ref, qr_ref, kr_ref, v_ref, iqr_ref, ike_ref, iko_ref):
    n = _rms(h_ref[0], g_ref[...]).astype(BF16)
    offs = np.cumsum((0,) + PROJ_WIDTHS)

    def mm(i):
        return _dot(n, w_ref[:, offs[i]:offs[i + 1]])

    qkv_ref[0] = mm(0)
    z_ref[0] = mm(1)
    sm_ref[0] = mm(2)

    ca, sa = cosa_ref[...], sina_ref[...]
    q = mm(3)
    parts = []
    for hd in range(B_HEADS):
        qh = _rms(q[:, hd * B_DH:(hd + 1) * B_DH], qn_ref[...])
        parts.append(qh * ca + pltpu.roll(qh, B_DH // 2, 1) * sa)
    qr_ref[0] = (jnp.concatenate(parts, axis=1) * (B_DH ** -0.5)).astype(BF16)

    kv = mm(4)
    k = _rms(kv[:, :B_DH], kn_ref[...])
    kr_ref[0] = (k * ca + pltpu.roll(k, B_DH // 2, 1) * sa).astype(BF16)
    v_ref[0] = kv[:, B_DH:].astype(BF16)

    ci, slo, shi = cosi_ref[...], sinlo_ref[...], sinhi_ref[...]
    iq = mm(5)
    parts = []
    for hp in range(IDX_HEADS // 2):
        x = iq[:, hp * LANES:(hp + 1) * LANES]
        parts.append(x * ci + pltpu.roll(x, LANES - IDX_DIM // 2, 1) * slo + pltpu.roll(x, IDX_DIM // 2, 1) * shi)
    iqr_ref[0] = jnp.concatenate(parts, axis=1).astype(BF16)

    ik = _rms(mm(6), kin_ref[...])
    ik = ik * ci + pltpu.roll(ik, IDX_DIM // 2, 1) * (slo + shi)
    lane = lax.broadcasted_iota(I32, (1, LANES), 1)
    ike_ref[0] = jnp.where(lane < IDX_DIM, ik, 0.0).astype(BF16)
    iko_ref[0] = jnp.where(lane >= IDX_DIM, ik, 0.0).astype(BF16)


def _proj_call(h, gain, w, tables, qn, kn, kin, rb):
    bsz, tp, d = h.shape
    nt = tp // rb
    row = lambda w_, dt: (jax.ShapeDtypeStruct((bsz, tp, w_), dt),
                          pl.BlockSpec((1, rb, w_), lambda b, t: (b, t, 0)))
    outs = [row(1536, F32), row(512, F32), row(128, F32), row(512, BF16), row(128, BF16),
            row(128, BF16), row(512, BF16), row(128, BF16), row(128, BF16)]
    tab_spec = pl.BlockSpec((rb, LANES), lambda b, t: (t, 0))
    return pl.pallas_call(
        _proj_body,
        grid=(bsz, nt),
        in_specs=[pl.BlockSpec((1, rb, d), lambda b, t: (b, t, 0)), _const_spec((1, d)), _const_spec(w.shape)]
                 + [tab_spec] * 5 + [_const_spec((1, LANES))] * 3,
        out_specs=[o[1] for o in outs],
        out_shape=[o[0] for o in outs],
        compiler_params=_params(("parallel", "arbitrary")),
        name="proj",
    )(h, gain, w, *tables, qn, kn, kin)


def _split3(x):
    x1 = x.astype(BF16)
    r1 = x - x1.astype(F32)
    x2 = r1.astype(BF16)
    x3 = (r1 - x2.astype(F32)).astype(BF16)
    return jnp.concatenate([x1, x2, x3], axis=0)


def _gdn_body(qkv_ref, z_ref, sm_ref, cw_ref, alog_ref, dtb_ref, ng_ref, lt3_ref, id3_ref, egb_ref,
              out_ref, xe_ref, s01_ref, s23_ref, u_ref, wq_ref, qkd_ref, kdt_ref, egl_ref, *, rb):
    t = pl.program_id(1)
    nwide = 3 * A_HEADS * A_DK

    @pl.when(t == 0)
    def _():
        xe_ref[0:8, :] = jnp.zeros((8, nwide), F32)
        s01_ref[...] = jnp.zeros_like(s01_ref)
        s23_ref[...] = jnp.zeros_like(s23_ref)

    xe_ref[8:8 + rb, :] = qkv_ref[0]
    cw = cw_ref[...]

    hw = A_HEADS * A_DK
    pw = A_HEADS * CHUNK
    ii = lax.broadcasted_iota(I32, (CHUNK, pw), 0)
    jj = lax.broadcasted_iota(I32, (CHUNK, pw), 1) & (CHUNK - 1)
    colhead = lax.broadcasted_iota(I32, (CHUNK, pw), 1) >> 6
    eye_p = ii == jj
    bd_mask = (lax.broadcasted_iota(I32, (pw, pw), 0) >> 6) == (lax.broadcasted_iota(I32, (pw, pw), 1) >> 6)
    bdk_mask = (lax.broadcasted_iota(I32, (pw, hw), 0) >> 6) == (lax.broadcasted_iota(I32, (pw, hw), 1) >> 7)
    pair_mask = (lax.broadcasted_iota(I32, (pw, pw), 0) >> 7) == (lax.broadcasted_iota(I32, (pw, pw), 1) >> 7)
    lane = lax.broadcasted_iota(I32, (1, LANES), 1)
    rows64 = lax.broadcasted_iota(I32, (CHUNK, 1), 0)
    neg_a = -jnp.exp(alog_ref[...])
    dtb = dtb_ref[...]
    ng = ng_ref[...]

    def block_diag(xp):
        return jnp.where(bd_mask, jnp.concatenate([xp.astype(BF16)] * A_HEADS, axis=0), 0)

    def stack_heads(xp):
        return jnp.concatenate([jnp.where(colhead == hd, xp, 0.0) for hd in range(A_HEADS)], axis=0)

    def l2n(x):
        parts = []
        for hd in range(A_HEADS):
            xh = x[:, hd * A_DK:(hd + 1) * A_DK]
            parts.append(xh * lax.rsqrt(jnp.sum(xh * xh, axis=-1, keepdims=True) + EPS))
        return jnp.concatenate(parts, axis=1)

    zeros = jnp.zeros((CHUNK, 2 * A_DK), F32)

    def pair_lanes(x, a_, b_):
        return jnp.concatenate([x[a_ * CHUNK:(a_ + 1) * CHUNK], x[b_ * CHUNK:(b_ + 1) * CHUNK]], axis=1)

    def local_load(c):
        r0 = c * CHUNK if isinstance(c, int) else pl.multiple_of(c * CHUNK, CHUNK)
        return xe_ref[pl.ds(r0, CHUNK + 8), :], sm_ref[0, pl.ds(r0, CHUNK), :], r0

    def local_compute(xw, sm, r0):
        acc = pltpu.roll(xw, 3, 0)[8:] * cw[0:1]
        acc = acc + pltpu.roll(xw, 2, 0)[8:] * cw[1:2]
        acc = acc + pltpu.roll(xw, 1, 0)[8:] * cw[2:3]
        acc = acc + xw[8:] * cw[3:4]
        xq = _silu(acc)
        q = l2n(xq[:, :hw]) * (A_DK ** -0.5)
        k = l2n(xq[:, hw:2 * hw])
        v = xq[:, 2 * hw:]

        g = jnp.where(lane < A_HEADS, neg_a * _softplus(sm + dtb), 0.0)
        g = jnp.where(t * rb + r0 + rows64 >= PAD, g, 0.0)
        x3 = _split3(jnp.where(lane < A_HEADS, g, _sigmoid(sm)))
        copies = _dot(x3, egb_ref[...]).astype(BF16)
        yield
        gx = _dot(lt3_ref[...], copies[:, :pw + hw])
        bx = _dot(id3_ref[...], copies[:, pw + hw:])
        yield
        gp, gq = gx[:, :pw], gx[:, pw:]
        bp, bq = bx[:, :pw], bx[:, pw:]
        grow = jnp.sum(jnp.where(eye_p, gp, 0.0), axis=0, keepdims=True)
        decay = jnp.where(ii >= jj, jnp.exp(gp - grow), 0.0)

        kb = k.astype(BF16)
        bdk = jnp.where(bdk_mask, jnp.concatenate([kb] * A_HEADS, axis=0), 0)
        kq = _dot_nt(jnp.concatenate([kb, q.astype(BF16)], axis=0), bdk)
        yield
        kk_p, qk_p = kq[:CHUNK], kq[CHUNK:]

        a = -jnp.where(ii > jj, bp * kk_p * decay, 0.0)
        tinv = jnp.where(eye_p, 1.0, 0.0) + a
        pk = _dot(a.astype(BF16), block_diag(a))
        yield
        for _ in range(4):
            pt = _dot(jnp.concatenate([pk, tinv], axis=0).astype(BF16), block_diag(pk))
            yield
            tinv = tinv + pt[CHUNK:]
            pk = pt[:CHUNK]
        tinv = tinv + _dot(tinv.astype(BF16), block_diag(pk))
        yield

        eg = jnp.exp(gq)
        vb = v * bq
        kbg = k * bq * eg
        rv = jnp.concatenate(
            [jnp.concatenate([vb[:, hd * A_DK:(hd + 1) * A_DK], kbg[:, hd * A_DK:(hd + 1) * A_DK]], axis=1)
             for hd in range(A_HEADS)], axis=0)
        uw = _dot(stack_heads(tinv).astype(BF16), rv.astype(BF16))
        yield
        u, w = uw[:, :A_DK], uw[:, A_DK:]

        qg = q * eg
        glast = gq[CHUNK - 1:CHUNK, :]
        kd = k * jnp.exp(glast - gq)
        return (u,
                jnp.concatenate([pair_lanes(w, 0, 1), qg[:, :2 * A_DK]], axis=0).astype(BF16),
                jnp.concatenate([pair_lanes(w, 2, 3), qg[:, 2 * A_DK:]], axis=0).astype(BF16),
                stack_heads(qk_p * decay).astype(BF16),
                jnp.concatenate([kd[:, :2 * A_DK], zeros], axis=0).T.astype(BF16),
                jnp.concatenate([kd[:, 2 * A_DK:], zeros], axis=0).T.astype(BF16),
                jnp.broadcast_to(jnp.exp(glast), (8, hw)))

    def local_store(c, vals):
        u_ref[c], wq_ref[c, 0], wq_ref[c, 1], qkd_ref[c], kdt_ref[c, 0], kdt_ref[c, 1], egl_ref[c] = vals

    def local_chain(c):
        vals = yield from local_compute(*local_load(c))
        local_store(c, vals)

    def state_chain(chunks):
        def unpair(r):
            return [r[:, :A_DK], r[:, A_DK:]]

        def state_update(s_old, kdt, vn_pair, egl_pair):
            upd = _dot(kdt, jnp.concatenate([vn_pair, zeros], axis=0).astype(BF16))
            return s_old * egl_pair + jnp.where(pair_mask, upd, 0.0)

        s01 = s01_ref[...]
        s23 = s23_ref[...]
        for c in chunks:
            r0 = c * CHUNK if isinstance(c, int) else pl.multiple_of(c * CHUNK, CHUNK)
            r01 = _dot(wq_ref[c, 0], s01.astype(BF16))
            r23 = _dot(wq_ref[c, 1], s23.astype(BF16))
            yield
            ws = jnp.concatenate(unpair(r01[:CHUNK]) + unpair(r23[:CHUNK]), axis=0)
            qs = jnp.concatenate(unpair(r01[CHUNK:]) + unpair(r23[CHUNK:]), axis=0)
            vn = u_ref[c] - ws
            o_rs = qs + _dot(qkd_ref[c], vn.astype(BF16))
            egl = egl_ref[c][0:1, :]
            s01 = state_update(s01, kdt_ref[c, 0], pair_lanes(vn, 0, 1), egl[:, :2 * A_DK])
            s23 = state_update(s23, kdt_ref[c, 1], pair_lanes(vn, 2, 3), egl[:, 2 * A_DK:])
            yield
            zz = z_ref[0, pl.ds(r0, CHUNK), :]
            parts = [_rms(o_rs[hd * CHUNK:(hd + 1) * CHUNK], ng) for hd in range(A_HEADS)]
            out_ref[0, pl.ds(r0, CHUNK), :] = jnp.concatenate(parts, axis=1) * _silu(zz)
        s01_ref[...] = s01
        s23_ref[...] = s23

    def run_lockstep(chains):
        live = list(chains)
        while live:
            for gen in list(live):
                try:
                    next(gen)
                except StopIteration:
                    live.remove(gen)

    nc = rb // CHUNK
    npairs = nc // 2
    if npairs == 0:
        run_lockstep([local_chain(0)])
        run_lockstep([state_chain([0])])
    else:
        run_lockstep([local_chain(0), local_chain(1)])

        def steady(i, carry):
            run_lockstep([local_chain(2 * i), local_chain(2 * i + 1), state_chain([2 * i - 2, 2 * i - 1])])
            return carry

        lax.fori_loop(1, npairs, steady, 0)
        last = [2 * npairs - 2, 2 * npairs - 1]
        if nc % 2:
            run_lockstep([local_chain(nc - 1), state_chain(last)])
            run_lockstep([state_chain([nc - 1])])
        else:
            run_lockstep([state_chain(last)])
    xe_ref[0:8, :] = xe_ref[rb:rb + 8, :]


def _gdn_call(qkv, z, sm, cw, alog, dtb, ng, lt3, id3, egb, rb):
    bsz, tp, _ = qkv.shape
    nt = tp // rb
    nc = rb // CHUNK
    hw = A_HEADS * A_DK
    pw = A_HEADS * CHUNK
    row = lambda w_: pl.BlockSpec((1, rb, w_), lambda b, t: (b, t, 0))
    return pl.pallas_call(
        functools.partial(_gdn_body, rb=rb),
        grid=(bsz, nt),
        in_specs=[row(3 * hw), row(hw), row(LANES), _const_spec(cw.shape), _const_spec((1, LANES)),
                  _const_spec((1, LANES)), _const_spec((1, LANES)), _const_spec(lt3.shape),
                  _const_spec(id3.shape), _const_spec(egb.shape)],
        out_specs=row(hw),
        out_shape=jax.ShapeDtypeStruct((bsz, tp, hw), F32),
        scratch_shapes=[pltpu.VMEM((rb + 8, 3 * hw), F32),
                        pltpu.VMEM((2 * A_DK, 2 * A_DK), F32), pltpu.VMEM((2 * A_DK, 2 * A_DK), F32),
                        pltpu.VMEM((nc, pw, A_DK), F32), pltpu.VMEM((nc, 2, 2 * CHUNK, 2 * A_DK), BF16),
                        pltpu.VMEM((nc, pw, pw), BF16), pltpu.VMEM((nc, 2, 2 * A_DK, 2 * CHUNK), BF16),
                        pltpu.VMEM((nc, 8, hw), F32)],
        compiler_params=_params(("parallel", "arbitrary")),
        name="gdn",
    )(qkv, z, sm, cw, alog, dtb, ng, lt3, id3, egb)


def _dsa_body(qr_ref, kr_ref, v_ref, iqr_ref, ike_ref, iko_ref, sm_ref, lstrict_ref, out_ref,
              vt_ref, vmt_ref, skey_ref, planes_ref, st_ref, ot_ref, *, seq, topk):
    g = pl.program_id(1)
    iw_scale = IDX_HEADS ** -0.5 * IDX_DIM ** -0.5
    qpair = 2 * CHUNK
    neg_inf = -jnp.inf

    @pl.when(g == 0)
    def _():
        planes_ref[...] = jnp.zeros_like(planes_ref)

        def vblk(i, carry):
            r = pl.multiple_of(CHUNK + i * LANES, CHUNK)
            vt_ref[i] = v_ref[0, pl.ds(r, LANES), :].astype(F32).T.astype(BF16)
            return carry

        lax.fori_loop(0, seq // LANES, vblk, 0)
        vmt_ref[...] = v_ref[0, 0:LANES, :].astype(F32).T.astype(BF16)

        qm = qr_ref[0, 0:CHUNK, :]
        km = kr_ref[0, 0:CHUNK, :]
        vm = v_ref[0, 0:CHUNK, :]
        colv = lax.broadcasted_iota(I32, (1, CHUNK), 1) >= PAD
        parts = []
        for hd in range(B_HEADS):
            s = _dot_nt(qm[:, hd * B_DH:(hd + 1) * B_DH], km)
            s = jnp.where(colv, s, neg_inf)
            e = jnp.exp(s - jnp.max(s, axis=-1, keepdims=True))
            pr = e / jnp.sum(e, axis=-1, keepdims=True)
            parts.append(_dot(pr.astype(BF16), vm))
        out_ref[0, 0:CHUNK, :] = jnp.concatenate(parts, axis=1)

    lane = lax.broadcasted_iota(I32, (1, LANES), 1)
    groups = []
    for j in range(QUERY_GROUPS):
        p = g * QUERY_GROUPS + j
        r0 = pl.multiple_of(CHUNK + p * qpair, CHUNK)
        qb = qr_ref[0, pl.ds(r0, qpair), :]
        ib = iqr_ref[0, pl.ds(r0, qpair), :]
        smt = sm_ref[0, pl.ds(r0, qpair), :].T
        groups.append(dict(
            j=j, r0=r0,
            q_hq=jnp.concatenate([qb[:, hd * B_DH:(hd + 1) * B_DH] for hd in range(B_HEADS)], axis=0),
            iq_hq=jnp.concatenate([ib[:, hp * LANES:(hp + 1) * LANES] for hp in range(IDX_HEADS // 2)], axis=0),
            wts=[smt[2 * A_HEADS + hd:2 * A_HEADS + hd + 1, :] * iw_scale for hd in range(IDX_HEADS)],
            limit=jnp.where(lane < CHUNK, (2 * p + 1) * CHUNK, (2 * p + 2) * CHUNK)))
    nkc = ((g + 1) * QUERY_GROUPS * qpair + KEY_CHUNK - 1) // KEY_CHUNK
    krows = lax.broadcasted_iota(I32, (KEY_CHUNK, 1), 0)

    def score_chunk(kc, carry):
        k0 = pl.multiple_of(CHUNK + kc * KEY_CHUNK, CHUNK)
        rows = pl.ds(pl.multiple_of(kc * KEY_CHUNK, KEY_CHUNK), KEY_CHUNK)
        ke, ko, kk = (ref[0, pl.ds(k0, KEY_CHUNK), :] for ref in (ike_ref, iko_ref, kr_ref))
        logits = []
        for grp_ in groups:
            logits.append((_dot_nt(ke, grp_["iq_hq"]), _dot_nt(ko, grp_["iq_hq"])))
            st_ref[grp_["j"], rows, :] = _dot_nt(kk, grp_["q_hq"])
        for grp_, (le, lo) in zip(groups, logits):
            score_tail(kc, rows, grp_, le, lo)
        return carry

    def score_tail(kc, rows, grp_, le, lo):
        j, wts, limit = grp_["j"], grp_["wts"], grp_["limit"]
        sc = jnp.zeros((KEY_CHUNK, LANES), F32)
        for hp in range(IDX_HEADS // 2):
            sc = sc + wts[2 * hp] * jnp.maximum(le[:, hp * LANES:(hp + 1) * LANES], 0.0)
            sc = sc + wts[2 * hp + 1] * jnp.maximum(lo[:, hp * LANES:(hp + 1) * LANES], 0.0)
        bits = lax.bitcast_convert_type(sc, I32)
        bits = jnp.where(sc == 0.0, 0, bits)
        key = bits ^ ((bits >> 31) & 0x7FFFFFFF)
        valid = kc * KEY_CHUNK + krows < limit
        key = jnp.where(valid, key, INT_MIN)
        skey_ref[j, rows, :] = key
        for gi in range(KEY_CHUNK // PLANE_ROWS):
            a = [key[gi * PLANE_ROWS + 8 * r:gi * PLANE_ROWS + 8 * r + 8, :] ^ INT_MIN for r in range(32)]
            sh, msk = 16, 0x0000FFFF
            while sh:
                k = 0
                while k < 32:
                    tt = (a[k] ^ lax.shift_right_logical(a[k + sh], jnp.int32(sh))) & msk
                    a[k] = a[k] ^ tt
                    a[k + sh] = a[k + sh] ^ (tt << sh)
                    k = (k + sh + 1) & ~sh
                sh >>= 1
                msk = (msk ^ (msk << sh)) & 0xFFFFFFFF
            g8 = pl.multiple_of((kc * (KEY_CHUNK // PLANE_ROWS) + gi) * 8, 8)
            for b in range(32):
                planes_ref[j, b, pl.ds(g8, 8), :] = a[b]

    lax.fori_loop(0, nkc, score_chunk, 0)

    grp = seq // 32
    alive0 = jnp.where(lax.broadcasted_iota(I32, (grp, 1), 0) < nkc * (KEY_CHUNK // 32),
                       jnp.full((grp, LANES), -1, I32), 0)

    def lane_count(words):
        c = lax.population_count(words)
        if grp > 8:
            c = jnp.sum(c.reshape(grp // 8, 8, LANES), axis=0)
        return jnp.sum(c, axis=0, keepdims=True)

    def bit_body(i, carry):
        bit = jnp.left_shift(jnp.int32(1), 31 - i)
        out = []
        for grp_, (tu, above, alive) in zip(groups, carry):
            hit = alive & planes_ref[grp_["j"], i]
            c1 = lane_count(hit)
            take = above + c1 >= topk
            out.append((jnp.where(take, tu | bit, tu), jnp.where(take, above, above + c1),
                        jnp.where(take, hit, alive ^ hit)))
        return tuple(out)

    zero_row = jnp.zeros((1, LANES), I32)
    found_keys = lax.fori_loop(0, 32, bit_body, ((zero_row, zero_row, alive0),) * QUERY_GROUPS)

    mrows = lax.broadcasted_iota(I32, (LANES, 1), 0)
    k_meta = kr_ref[0, 0:LANES, :]
    init = []
    for grp_, (tu, above, alive) in zip(groups, found_keys):
        j = grp_["j"]
        thr = tu ^ INT_MIN
        found = tu != 0
        tie_lane = found & (above + lane_count(alive) > topk)

        @pl.when(jnp.max(jnp.where(tie_lane, 1, 0)) > 0)
        def _(j=j, thr=thr, found=found, above=above):
            need = (topk - above).astype(F32)

            def body(kc, seen):
                rows = pl.ds(pl.multiple_of(kc * KEY_CHUNK, KEY_CHUNK), KEY_CHUNK)
                blk = skey_ref[j, rows, :]
                tied = (blk == thr) & found
                eq = jnp.where(tied, 1.0, 0.0)
                rank = _dot(lstrict_ref[...], eq.astype(BF16)) + seen
                skey_ref[j, rows, :] = jnp.where(tied & (rank >= need), INT_MIN, blk)
                return seen + jnp.sum(eq, axis=0, keepdims=True)

            lax.fori_loop(0, nkc, body, jnp.zeros((1, LANES), F32))

        grp_["thr"] = jnp.maximum(thr, INT_MIN + 1)
        s_meta = _dot_nt(k_meta, grp_["q_hq"])
        s_meta = jnp.where((mrows >= PAD) & (mrows < CHUNK), s_meta, neg_inf)
        m0 = jnp.max(s_meta, axis=0, keepdims=True)
        p_meta = jnp.exp(s_meta - m0)
        ot_ref[j] = _dot(vmt_ref[...], p_meta.astype(BF16))
        init.append((m0, jnp.sum(p_meta, axis=0, keepdims=True)))

    def pv_chunk(kc, carry):
        rows = pl.ds(pl.multiple_of(kc * KEY_CHUNK, KEY_CHUNK), KEY_CHUNK)
        vblk = jnp.concatenate([vt_ref[(KEY_CHUNK // LANES) * kc + i] for i in range(KEY_CHUNK // LANES)], axis=1)
        out, probs = [], []
        for grp_, (m, l) in zip(groups, carry):
            bias = jnp.where(skey_ref[grp_["j"], rows, :] >= grp_["thr"], 0.0, neg_inf)
            s = st_ref[grp_["j"], rows, :] + jnp.concatenate([bias] * B_HEADS, axis=1)
            m_new = jnp.maximum(m, jnp.max(jnp.max(s.reshape(KEY_CHUNK // 8, 8, B_HEADS * LANES), axis=0),
                                           axis=0, keepdims=True))
            pr = jnp.exp(s - m_new)
            alpha = jnp.exp(m - m_new)
            out.append((m_new, l * alpha + jnp.sum(pr, axis=0, keepdims=True)))
            probs.append((alpha, pr.astype(BF16)))
        for grp_, (alpha, prb) in zip(groups, probs):
            ot_ref[grp_["j"]] = ot_ref[grp_["j"]] * alpha + _dot(vblk, prb)
        return tuple(out)

    sums = lax.fori_loop(0, nkc, pv_chunk, tuple(init))
    for grp_, (_, l) in zip(groups, sums):
        o_hq = (ot_ref[grp_["j"]] / l).T
        for hd in range(B_HEADS):
            out_ref[0, pl.ds(grp_["r0"], qpair), hd * B_DH:(hd + 1) * B_DH] = o_hq[hd * LANES:(hd + 1) * LANES, :]


def _dsa_call(qr, kr, v, iqr, ike, iko, sm, lstrict, seq, topk):
    bsz, tp, _ = qr.shape
    nsteps, rem = divmod(seq, 2 * CHUNK * QUERY_GROUPS)
    assert rem == 0
    full = lambda w_: pl.BlockSpec((1, tp, w_), lambda b, p: (b, 0, 0))
    hw = B_HEADS * B_DH
    nq = QUERY_GROUPS
    return pl.pallas_call(
        functools.partial(_dsa_body, seq=seq, topk=topk),
        grid=(bsz, nsteps),
        in_specs=[full(hw), full(B_DH), full(B_DH), full(hw), full(LANES), full(LANES), full(LANES),
                  _const_spec(lstrict.shape)],
        out_specs=full(hw),
        out_shape=jax.ShapeDtypeStruct((bsz, tp, hw), F32),
        scratch_shapes=[pltpu.VMEM((seq // LANES, B_DH, LANES), BF16), pltpu.VMEM((B_DH, LANES), BF16),
                        pltpu.VMEM((nq, seq, LANES), I32), pltpu.VMEM((nq, 32, seq // 32, LANES), I32),
                        pltpu.VMEM((nq, seq, hw), F32), pltpu.VMEM((nq, B_DH, hw), F32)],
        compiler_params=_params(("parallel", "arbitrary")),
        name="dsa",
    )(qr, kr, v, iqr, ike, iko, sm, lstrict)


def _mix_body(h_ref, oa_ref, ob_ref, g_ref, wg_ref, bg_ref, wa_ref, wb_ref, wo_ref, out_ref, *, rb):
    x = h_ref[0]
    d = x.shape[-1]
    n = _rms(x, g_ref[...]).astype(BF16)
    gates = _sigmoid(_dot(n, wg_ref[...]) + bg_ref[...])
    y = gates[:, :d] * _dot(oa_ref[0].astype(BF16), wa_ref[...]) + gates[:, d:] * _dot(ob_ref[0].astype(BF16), wb_ref[...])
    out = x + _dot(y.astype(BF16), wo_ref[...])
    rows = pl.program_id(1) * rb + lax.broadcasted_iota(I32, (rb, 1), 0)
    out_ref[0] = jnp.where(rows >= PAD, out, 0.0)


def _mix_call(h, oa, ob, gain, wg, bg, wa, wb, wo, rb):
    bsz, tp, d = h.shape
    row = lambda w_: pl.BlockSpec((1, rb, w_), lambda b, t: (b, t, 0))
    return pl.pallas_call(
        functools.partial(_mix_body, rb=rb),
        grid=(bsz, tp // rb),
        in_specs=[row(d), row(oa.shape[-1]), row(ob.shape[-1]), _const_spec((1, d)), _const_spec(wg.shape),
                  _const_spec(bg.shape), _const_spec(wa.shape), _const_spec(wb.shape), _const_spec(wo.shape)],
        out_specs=row(d),
        out_shape=jax.ShapeDtypeStruct((bsz, tp, d), F32),
        compiler_params=_params(("parallel", "arbitrary")),
        name="mix",
    )(h, oa, ob, gain, wg, bg, wa, wb, wo)


def _ffn_body(h_ref, g_ref, wup_ref, cw_ref, wd_ref, out_ref, carry_ref, *, rb, dff):
    t = pl.program_id(1)

    @pl.when(t == 0)
    def _():
        carry_ref[...] = jnp.zeros_like(carry_ref)

    x = h_ref[0]
    n = _rms(x, g_ref[...]).astype(BF16)
    rows = lax.broadcasted_iota(I32, (rb, 1), 0)
    acc = jnp.zeros(x.shape, F32)
    for c in range(dff // FF_CHUNK):
        cols = slice(c * FF_CHUNK, (c + 1) * FF_CHUNK)
        gate = _dot(n, wup_ref[:, cols])
        up = _dot(n, wup_ref[:, dff + c * FF_CHUNK:dff + (c + 1) * FF_CHUNK])
        prev = carry_ref[:, cols]
        g1 = jnp.where(rows == 0, prev[7:8], pltpu.roll(gate, 1, 0))
        g2 = jnp.where(rows == 0, prev[6:7], jnp.where(rows == 1, prev[7:8], pltpu.roll(gate, 2, 0)))
        cw = cw_ref[:, cols]
        conv = g2 * cw[0:1] + g1 * cw[1:2] + gate * cw[2:3]
        carry_ref[:, cols] = gate[rb - 8:rb]
        acc = acc + _dot((_silu(conv) * up).astype(BF16), wd_ref[cols, :])
    out = x + acc
    out_ref[0] = jnp.where(t * rb + rows >= PAD, out, 0.0)


def _ffn_call(h, gain, wup, cw, wd, rb):
    bsz, tp, d = h.shape
    dff = wd.shape[0]
    row = pl.BlockSpec((1, rb, d), lambda b, t: (b, t, 0))
    return pl.pallas_call(
        functools.partial(_ffn_body, rb=rb, dff=dff),
        grid=(bsz, tp // rb),
        in_specs=[row, _const_spec((1, d)), _const_spec(wup.shape), _const_spec(cw.shape), _const_spec(wd.shape)],
        out_specs=row,
        out_shape=jax.ShapeDtypeStruct((bsz, tp, d), F32),
        scratch_shapes=[pltpu.VMEM((8, dff), F32)],
        compiler_params=_params(("parallel", "arbitrary")),
        name="ffn",
    )(h, gain, wup, cw, wd)


def _rope_tables(tp):
    pos = (jnp.arange(tp, dtype=F32) - PAD)[:, None]

    def cs(dim):
        inv = 1.0 / (ROPE_THETA ** (jnp.arange(0, dim, 2, dtype=F32) / dim))
        ang = pos * inv[None, :]
        return jnp.cos(ang), jnp.sin(ang)

    ca, sa = cs(B_DH)
    ci, si = cs(IDX_DIM)
    zi = jnp.zeros_like(si)
    return (jnp.concatenate([ca, ca], axis=1), jnp.concatenate([-sa, sa], axis=1),
            jnp.concatenate([ci] * 4, axis=1), jnp.concatenate([-si, zi, -si, zi], axis=1),
            jnp.concatenate([zi, si, zi, si], axis=1))


def _pad_lanes(vec):
    return jnp.zeros((1, LANES), F32).at[0, :vec.shape[0]].set(vec.astype(F32))


def _head_expander():
    pw, hw = A_HEADS * CHUNK, A_HEADS * A_DK
    head = np.concatenate([np.arange(pw) // CHUNK, np.arange(hw) // A_DK])
    src = np.arange(LANES)[:, None]
    return jnp.asarray(np.concatenate([src == head[None, :], src == A_HEADS + head[None, :]], axis=1), BF16)


def kernel(x, meta_tokens, norm_mix, w_in, conv_a, a_log, dt_bias, a_out_norm, q_norm, k_norm, kidx_norm,
           w_branch_a, w_branch_b, w_gate, b_gate, w_out, norm_ffn, w_up, conv_ffn, w_down):
    bsz, seq, d = x.shape
    depth = w_in.shape[0]
    assert seq % KEY_CHUNK == 0
    tp = CHUNK + seq
    rb = _row_block(tp // CHUNK)
    topk = min(TOPK_MAX, seq // TOPK_DIV)

    meta = jnp.broadcast_to(meta_tokens.astype(x.dtype)[None], (bsz, N_META, d))
    h = jnp.concatenate([jnp.zeros((bsz, PAD, d), x.dtype), meta, x], axis=1)

    tables = _rope_tables(tp)
    egb = _head_expander()
    tri = np.arange(CHUNK)
    lt3 = jnp.asarray(np.tile(tri[:, None] >= tri[None, :], (1, 3)), BF16)
    id3 = jnp.asarray(np.tile(np.eye(CHUNK), (1, 3)), BF16)
    trk = np.arange(KEY_CHUNK)
    lstrict = jnp.asarray(trk[:, None] > trk[None, :], BF16)

    hw = A_HEADS * A_DK
    o = np.cumsum((0, hw, hw, hw, hw, A_HEADS, A_HEADS, B_HEADS * B_DH, B_DH, B_DH,
                   IDX_HEADS * IDX_DIM, IDX_DIM, IDX_HEADS))
    for l in range(depth):
        w = w_in[l]
        small = jnp.concatenate([w[:, o[4]:o[6]], w[:, o[11]:o[12]],
                                 jnp.zeros((d, LANES - 2 * A_HEADS - IDX_HEADS), w.dtype)], axis=1)
        w_p = jnp.concatenate([w[:, o[0]:o[3]], w[:, o[3]:o[4]], small, w[:, o[6]:o[7]], w[:, o[7]:o[9]],
                               w[:, o[9]:o[10]], w[:, o[10]:o[11]], w[:, o[10]:o[11]]], axis=1).astype(BF16)
        qkv, z, sm, qr, kr, v, iqr, ike, iko = _proj_call(
            h, norm_mix[l][None], w_p, tables, q_norm[l][None], k_norm[l][None],
            jnp.concatenate([kidx_norm[l], kidx_norm[l]])[None], rb)
        o_a = _gdn_call(qkv, z, sm, conv_a[l], _pad_lanes(a_log[l]), _pad_lanes(dt_bias[l]),
                        a_out_norm[l][None], lt3, id3, egb, rb)
        o_b = _dsa_call(qr, kr, v, iqr, ike, iko, sm, lstrict, seq, topk)
        h = _mix_call(h, o_a, o_b, norm_mix[l][None], w_gate[l].astype(BF16), b_gate[l][None],
                      w_branch_a[l].astype(BF16), w_branch_b[l].astype(BF16), w_out[l].astype(BF16), rb)
        h = _ffn_call(h, norm_ffn[l][None], w_up[l].astype(BF16), conv_ffn[l], w_down[l].astype(BF16), rb)
    return h[:, CHUNK:]
```

```python
import functools
import math

import jax
import jax.numpy as jnp
import numpy as np
from jax import lax
from jax.experimental import pallas as pl
from jax.experimental.pallas import tpu as pltpu

F32 = jnp.float32
BF16 = jnp.bfloat16
I32 = jnp.int32

CHUNK = 64
N_META = 16
PAD = CHUNK - N_META
ROPE_THETA = 10000.0
EPS = 1e-6
A_HEADS = 4
A_DK = 128
A_CONV = 4
B_HEADS = 4
B_DH = 128
IDX_HEADS = 8
IDX_DIM = 64
TOPK_MAX = 256
TOPK_DIV = 4
FFN_CONV = 3
LANES = 128
KEY_CHUNK = 512
PLANE_ROWS = 256
QUERY_GROUPS = 4
FF_CHUNK = 256
INT_MIN = -(2 ** 31)
VMEM_LIMIT_BYTES = 56 * 1024 * 1024

NT_DIMS = (((1,), (1,)), ((), ()))


def _dot(a, b):
    return jnp.dot(a, b, preferred_element_type=F32)


def _dot_nt(a, b):
    return lax.dot_general(a, b, NT_DIMS, preferred_element_type=F32)


def _dot_exact(a, b):
    return jnp.dot(a, b, preferred_element_type=F32, precision=lax.Precision.HIGHEST)


def _rms(x, gain):
    return x * lax.rsqrt(jnp.mean(x * x, axis=-1, keepdims=True) + EPS) * gain


def _sigmoid(x):
    return 1.0 / (1.0 + jnp.exp(-x))


def _silu(x):
    return x * _sigmoid(x)


def _softplus(x):
    return jnp.maximum(x, 0.0) + jnp.log(1.0 + jnp.exp(-jnp.abs(x)))


def _row_block(n_chunks):
    g = max(d for d in range(1, 12) if n_chunks % d == 0)
    return g * CHUNK


def _params(sem):
    return pltpu.CompilerParams(dimension_semantics=sem, vmem_limit_bytes=VMEM_LIMIT_BYTES)


def _const_spec(shape):
    nd = len(shape)
    return pl.BlockSpec(shape, lambda *_: (0,) * nd)


PROJ_WIDTHS = (1536, 512, 128, 512, 256, 512, 128)


def _proj_body(h_ref, g_ref, w_ref, cosa_ref, sina_ref, cosi_ref, sinlo_ref, sinhi_ref,
               qn_ref, kn_ref, kin_ref,
               qkv_ref, z_ref, sm_ref, qr_ref, kr_ref, v_ref, iqr_ref, ike_ref, iko_ref):
    n = _rms(h_ref[0], g_ref[...]).astype(BF16)
    offs = np.cumsum((0,) + PROJ_WIDTHS)

    def mm(i):
        return _dot(n, w_ref[:, offs[i]:offs[i + 1]])

    ca, sa = cosa_ref[...], sina_ref[...]
    ci, slo, shi = cosi_ref[...], sinlo_ref[...], sinhi_ref[...]
    q = mm(3)
    iq = mm(5)
    parts = []
    for hd in range(B_HEADS):
        qh = _rms(q[:, hd * B_DH:(hd + 1) * B_DH], qn_ref[...])
        parts.append(qh * ca + pltpu.roll(qh, B_DH // 2, 1) * sa)
    qr_ref[0] = (jnp.concatenate(parts, axis=1) * (B_DH ** -0.5)).astype(BF16)

    kv = mm(4)
    ik = mm(6)
    parts = []
    for hp in range(IDX_HEADS // 2):
        x = iq[:, hp * LANES:(hp + 1) * LANES]
        parts.append(x * ci + pltpu.roll(x, LANES - IDX_DIM // 2, 1) * slo + pltpu.roll(x, IDX_DIM // 2, 1) * shi)
    iqr_ref[0] = jnp.concatenate(parts, axis=1).astype(BF16)

    qkv_ref[0] = mm(0)
    k = _rms(kv[:, :B_DH], kn_ref[...])
    kr_ref[0] = (k * ca + pltpu.roll(k, B_DH // 2, 1) * sa).astype(BF16)
    v_ref[0] = kv[:, B_DH:].astype(BF16)
    ik = _rms(ik, kin_ref[...])
    ik = ik * ci + pltpu.roll(ik, IDX_DIM // 2, 1) * (slo + shi)
    lane = lax.broadcasted_iota(I32, (1, LANES), 1)
    ike_ref[0] = jnp.where(lane < IDX_DIM, ik, 0.0).astype(BF16)
    iko_ref[0] = jnp.where(lane >= IDX_DIM, ik, 0.0).astype(BF16)
    z_ref[0] = mm(1)
    sm_ref[0] = mm(2)


def _proj_call(h, gain, w, tables, qn, kn, kin, rb):
    bsz, tp, d = h.shape
    nt = tp // rb
    row = lambda w_, dt: (jax.ShapeDtypeStruct((bsz, tp, w_), dt),
                          pl.BlockSpec((1, rb, w_), lambda b, t: (b, t, 0)))
    outs = [row(1536, F32), row(512, F32), row(128, F32), row(512, BF16), row(128, BF16),
            row(128, BF16), row(512, BF16), row(128, BF16), row(128, BF16)]
    tab_spec = pl.BlockSpec((rb, LANES), lambda b, t: (t, 0))
    return pl.pallas_call(
        _proj_body,
        grid=(bsz, nt),
        in_specs=[pl.BlockSpec((1, rb, d), lambda b, t: (b, t, 0)), _const_spec((1, d)), _const_spec(w.shape)]
                 + [tab_spec] * 5 + [_const_spec((1, LANES))] * 3,
        out_specs=[o[1] for o in outs],
        out_shape=[o[0] for o in outs],
        compiler_params=_params(("parallel", "arbitrary")),
        name="proj",
    )(h, gain, w, *tables, qn, kn, kin)


def _split3(x):
    x1 = x.astype(BF16)
    r1 = x - x1.astype(F32)
    x2 = r1.astype(BF16)
    x3 = (r1 - x2.astype(F32)).astype(BF16)
    return jnp.concatenate([x1, x2, x3], axis=0)


def _gdn_body(qkv_ref, z_ref, sm_ref, cw_ref, alog_ref, dtb_ref, ng_ref, lt3_ref, id3_ref, egb_ref,
              out_ref, xe_ref, s01_ref, s23_ref, u_ref, wq_ref, qkd_ref, kdt_ref, egl_ref, *, rb):
    t = pl.program_id(1)
    nwide = 3 * A_HEADS * A_DK

    @pl.when(t == 0)
    def _():
        xe_ref[0:8, :] = jnp.zeros((8, nwide), F32)
        s01_ref[...] = jnp.zeros_like(s01_ref)
        s23_ref[...] = jnp.zeros_like(s23_ref)

    xe_ref[8:8 + rb, :] = qkv_ref[0]
    cw = cw_ref[...]

    hw = A_HEADS * A_DK
    pw = A_HEADS * CHUNK
    ii = lax.broadcasted_iota(I32, (CHUNK, pw), 0)
    jj = lax.broadcasted_iota(I32, (CHUNK, pw), 1) & (CHUNK - 1)
    colhead = lax.broadcasted_iota(I32, (CHUNK, pw), 1) >> 6
    eye_p = ii == jj
    bd_mask = (lax.broadcasted_iota(I32, (pw, pw), 0) >> 6) == (lax.broadcasted_iota(I32, (pw, pw), 1) >> 6)
    bdk_mask = (lax.broadcasted_iota(I32, (pw, hw), 0) >> 6) == (lax.broadcasted_iota(I32, (pw, hw), 1) >> 7)
    pair_mask = (lax.broadcasted_iota(I32, (pw, pw), 0) >> 7) == (lax.broadcasted_iota(I32, (pw, pw), 1) >> 7)
    lane = lax.broadcasted_iota(I32, (1, LANES), 1)
    rows64 = lax.broadcasted_iota(I32, (CHUNK, 1), 0)
    neg_a = -jnp.exp(alog_ref[...])
    dtb = dtb_ref[...]
    ng = ng_ref[...]

    def block_diag(xp):
        return jnp.where(bd_mask, jnp.concatenate([xp.astype(BF16)] * A_HEADS, axis=0), 0)

    def stack_heads(xp):
        return jnp.concatenate([jnp.where(colhead == hd, xp, 0.0) for hd in range(A_HEADS)], axis=0)

    def l2n(x):
        parts = []
        for hd in range(A_HEADS):
            xh = x[:, hd * A_DK:(hd + 1) * A_DK]
            parts.append(xh * lax.rsqrt(jnp.sum(xh * xh, axis=-1, keepdims=True) + EPS))
        return jnp.concatenate(parts, axis=1)

    zeros = jnp.zeros((CHUNK, 2 * A_DK), F32)

    def pair_lanes(x, a_, b_):
        return jnp.concatenate([x[a_ * CHUNK:(a_ + 1) * CHUNK], x[b_ * CHUNK:(b_ + 1) * CHUNK]], axis=1)

    def local_load(c):
        r0 = c * CHUNK if isinstance(c, int) else pl.multiple_of(c * CHUNK, CHUNK)
        return xe_ref[pl.ds(r0, CHUNK + 8), :], sm_ref[0, pl.ds(r0, CHUNK), :], r0

    def local_compute(xw, sm, r0):
        acc = pltpu.roll(xw, 3, 0)[8:] * cw[0:1]
        acc = acc + pltpu.roll(xw, 2, 0)[8:] * cw[1:2]
        acc = acc + pltpu.roll(xw, 1, 0)[8:] * cw[2:3]
        acc = acc + xw[8:] * cw[3:4]
        xq = _silu(acc)
        q = l2n(xq[:, :hw]) * (A_DK ** -0.5)
        k = l2n(xq[:, hw:2 * hw])
        v = xq[:, 2 * hw:]

        g = jnp.where(lane < A_HEADS, neg_a * _softplus(sm + dtb), 0.0)
        g = jnp.where(t * rb + r0 + rows64 >= PAD, g, 0.0)
        x3 = _split3(jnp.where(lane < A_HEADS, g, _sigmoid(sm)))
        copies = _dot(x3, egb_ref[...]).astype(BF16)
        yield
        gx = _dot(lt3_ref[...], copies[:, :pw + hw])
        bx = _dot(id3_ref[...], copies[:, pw + hw:])
        yield
        gp, gq = gx[:, :pw], gx[:, pw:]
        bp, bq = bx[:, :pw], bx[:, pw:]
        grow = jnp.sum(jnp.where(eye_p, gp, 0.0), axis=0, keepdims=True)
        decay = jnp.where(ii >= jj, jnp.exp(gp - grow), 0.0)

        kb = k.astype(BF16)
        bdk = jnp.where(bdk_mask, jnp.concatenate([kb] * A_HEADS, axis=0), 0)
        kq = _dot_nt(jnp.concatenate([kb, q.astype(BF16)], axis=0), bdk)
        yield
        kk_p, qk_p = kq[:CHUNK], kq[CHUNK:]

        a = -jnp.where(ii > jj, bp * kk_p * decay, 0.0)
        tinv = jnp.where(eye_p, 1.0, 0.0) + a
        pk = _dot(a.astype(BF16), block_diag(a))
        yield
        for _ in range(4):
            pt = _dot(jnp.concatenate([pk, tinv], axis=0).astype(BF16), block_diag(pk))
            yield
            tinv = tinv + pt[CHUNK:]
            pk = pt[:CHUNK]
        tinv = tinv + _dot(tinv.astype(BF16), block_diag(pk))
        yield

        eg = jnp.exp(gq)
        vb = v * bq
        kbg = k * bq * eg
        rv = jnp.concatenate(
            [jnp.concatenate([vb[:, hd * A_DK:(hd + 1) * A_DK], kbg[:, hd * A_DK:(hd + 1) * A_DK]], axis=1)
             for hd in range(A_HEADS)], axis=0)
        uw = _dot(stack_heads(tinv).astype(BF16), rv.astype(BF16))
        yield
        u, w = uw[:, :A_DK], uw[:, A_DK:]

        qg = q * eg
        glast = gq[CHUNK - 1:CHUNK, :]
        kd = k * jnp.exp(glast - gq)
        return (u,
                jnp.concatenate([pair_lanes(w, 0, 1), qg[:, :2 * A_DK]], axis=0).astype(BF16),
                jnp.concatenate([pair_lanes(w, 2, 3), qg[:, 2 * A_DK:]], axis=0).astype(BF16),
                stack_heads(qk_p * decay).astype(BF16),
                jnp.concatenate([kd[:, :2 * A_DK], zeros], axis=0).T.astype(BF16),
                jnp.concatenate([kd[:, 2 * A_DK:], zeros], axis=0).T.astype(BF16),
                jnp.broadcast_to(jnp.exp(glast), (8, hw)))

    def local_store(c, vals):
        u_ref[c], wq_ref[c, 0], wq_ref[c, 1], qkd_ref[c], kdt_ref[c, 0], kdt_ref[c, 1], egl_ref[c] = vals

    def local_chain(c):
        vals = yield from local_compute(*local_load(c))
        local_store(c, vals)

    def state_chain(chunks):
        def unpair(r):
            return [r[:, :A_DK], r[:, A_DK:]]

        def state_update(s_old, kdt, vn_pair, egl_pair):
            upd = _dot(kdt, jnp.concatenate([vn_pair, zeros], axis=0).astype(BF16))
            return s_old * egl_pair + jnp.where(pair_mask, upd, 0.0)

        s01 = s01_ref[...]
        s23 = s23_ref[...]
        for c in chunks:
            r0 = c * CHUNK if isinstance(c, int) else pl.multiple_of(c * CHUNK, CHUNK)
            r01 = _dot(wq_ref[c, 0], s01.astype(BF16))
            r23 = _dot(wq_ref[c, 1], s23.astype(BF16))
            yield
            ws = jnp.concatenate(unpair(r01[:CHUNK]) + unpair(r23[:CHUNK]), axis=0)
            qs = jnp.concatenate(unpair(r01[CHUNK:]) + unpair(r23[CHUNK:]), axis=0)
            vn = u_ref[c] - ws
            o_rs = qs + _dot(qkd_ref[c], vn.astype(BF16))
            egl = egl_ref[c][0:1, :]
            s01 = state_update(s01, kdt_ref[c, 0], pair_lanes(vn, 0, 1), egl[:, :2 * A_DK])
            s23 = state_update(s23, kdt_ref[c, 1], pair_lanes(vn, 2, 3), egl[:, 2 * A_DK:])
            yield
            zz = z_ref[0, pl.ds(r0, CHUNK), :]
            parts = [_rms(o_rs[hd * CHUNK:(hd + 1) * CHUNK], ng) for hd in range(A_HEADS)]
            out_ref[0, pl.ds(r0, CHUNK), :] = jnp.concatenate(parts, axis=1) * _silu(zz)
        s01_ref[...] = s01
        s23_ref[...] = s23

    def run_lockstep(chains):
        live = list(chains)
        while live:
            for gen in list(live):
                try:
                    next(gen)
                except StopIteration:
                    live.remove(gen)

    nc = rb // CHUNK
    npairs = nc // 2
    if npairs == 0:
        run_lockstep([local_chain(0)])
        run_lockstep([state_chain([0])])
    else:
        run_lockstep([local_chain(0), local_chain(1)])

        def steady(i, carry):
            run_lockstep([local_chain(2 * i), local_chain(2 * i + 1), state_chain([2 * i - 2, 2 * i - 1])])
            return carry

        lax.fori_loop(1, npairs, steady, 0)
        last = [2 * npairs - 2, 2 * npairs - 1]
        if nc % 2:
            run_lockstep([local_chain(nc - 1), state_chain(last)])
            run_lockstep([state_chain([nc - 1])])
        else:
            run_lockstep([state_chain(last)])
    xe_ref[0:8, :] = xe_ref[rb:rb + 8, :]


def _gdn_call(qkv, z, sm, cw, alog, dtb, ng, lt3, id3, egb, rb):
    bsz, tp, _ = qkv.shape
    nt = tp // rb
    nc = rb // CHUNK
    hw = A_HEADS * A_DK
    pw = A_HEADS * CHUNK
    row = lambda w_: pl.BlockSpec((1, rb, w_), lambda b, t: (b, t, 0))
    return pl.pallas_call(
        functools.partial(_gdn_body, rb=rb),
        grid=(bsz, nt),
        in_specs=[row(3 * hw), row(hw), row(LANES), _const_spec(cw.shape), _const_spec((1, LANES)),
                  _const_spec((1, LANES)), _const_spec((1, LANES)), _const_spec(lt3.shape),
                  _const_spec(id3.shape), _const_spec(egb.shape)],
        out_specs=row(hw),
        out_shape=jax.ShapeDtypeStruct((bsz, tp, hw), F32),
        scratch_shapes=[pltpu.VMEM((rb + 8, 3 * hw), F32),
                        pltpu.VMEM((2 * A_DK, 2 * A_DK), F32), pltpu.VMEM((2 * A_DK, 2 * A_DK), F32),
                        pltpu.VMEM((nc, pw, A_DK), F32), pltpu.VMEM((nc, 2, 2 * CHUNK, 2 * A_DK), BF16),
                        pltpu.VMEM((nc, pw, pw), BF16), pltpu.VMEM((nc, 2, 2 * A_DK, 2 * CHUNK), BF16),
                        pltpu.VMEM((nc, 8, hw), F32)],
        compiler_params=_params(("parallel", "arbitrary")),
        name="gdn",
    )(qkv, z, sm, cw, alog, dtb, ng, lt3, id3, egb)


def _dsa_body(qr_ref, kr_ref, v_ref, iqr_ref, ike_ref, iko_ref, sm_ref, lstrict_ref, out_ref,
              vt_ref, vmt_ref, skey_ref, planes_ref, st_ref, ot_ref, *, seq, topk):
    g = pl.program_id(1)
    iw_scale = IDX_HEADS ** -0.5 * IDX_DIM ** -0.5
    qpair = 2 * CHUNK
    neg_inf = -jnp.inf

    @pl.when(g == 0)
    def _():
        planes_ref[...] = jnp.zeros_like(planes_ref)

        def vblk(i, carry):
            r = pl.multiple_of(CHUNK + i * LANES, CHUNK)
            vt_ref[i] = v_ref[0, pl.ds(r, LANES), :].astype(F32).T.astype(BF16)
            return carry

        lax.fori_loop(0, seq // LANES, vblk, 0)
        vmt_ref[...] = v_ref[0, 0:LANES, :].astype(F32).T.astype(BF16)

        qm = qr_ref[0, 0:CHUNK, :]
        km = kr_ref[0, 0:CHUNK, :]
        vm = v_ref[0, 0:CHUNK, :]
        colv = lax.broadcasted_iota(I32, (1, CHUNK), 1) >= PAD
        parts = []
        for hd in range(B_HEADS):
            s = _dot_nt(qm[:, hd * B_DH:(hd + 1) * B_DH], km)
            s = jnp.where(colv, s, neg_inf)
            e = jnp.exp(s - jnp.max(s, axis=-1, keepdims=True))
            pr = e / jnp.sum(e, axis=-1, keepdims=True)
            parts.append(_dot(pr.astype(BF16), vm))
        out_ref[0, 0:CHUNK, :] = jnp.concatenate(parts, axis=1)

    lane = lax.broadcasted_iota(I32, (1, LANES), 1)
    groups = []
    for j in range(QUERY_GROUPS):
        p = g * QUERY_GROUPS + j
        r0 = pl.multiple_of(CHUNK + p * qpair, CHUNK)
        qb = qr_ref[0, pl.ds(r0, qpair), :]
        ib = iqr_ref[0, pl.ds(r0, qpair), :]
        smt = sm_ref[0, pl.ds(r0, qpair), :].T
        groups.append(dict(
            j=j, r0=r0,
            q_hq=jnp.concatenate([qb[:, hd * B_DH:(hd + 1) * B_DH] for hd in range(B_HEADS)], axis=0),
            iq_hq=jnp.concatenate([ib[:, hp * LANES:(hp + 1) * LANES] for hp in range(IDX_HEADS // 2)], axis=0),
            wts=[smt[2 * A_HEADS + hd:2 * A_HEADS + hd + 1, :] * iw_scale for hd in range(IDX_HEADS)],
            limit=jnp.where(lane < CHUNK, (2 * p + 1) * CHUNK, (2 * p + 2) * CHUNK)))
    nkc = ((g + 1) * QUERY_GROUPS * qpair + KEY_CHUNK - 1) // KEY_CHUNK
    krows = lax.broadcasted_iota(I32, (KEY_CHUNK, 1), 0)

    def score_chunk(kc, carry):
        k0 = pl.multiple_of(CHUNK + kc * KEY_CHUNK, CHUNK)
        rows = pl.ds(pl.multiple_of(kc * KEY_CHUNK, KEY_CHUNK), KEY_CHUNK)
        ke, ko, kk = (ref[0, pl.ds(k0, KEY_CHUNK), :] for ref in (ike_ref, iko_ref, kr_ref))
        logits = []
        for grp_ in groups:
            logits.append((_dot_nt(ke, grp_["iq_hq"]), _dot_nt(ko, grp_["iq_hq"])))
            st_ref[grp_["j"], rows, :] = _dot_nt(kk, grp_["q_hq"])
        for grp_, (le, lo) in zip(groups, logits):
            score_tail(kc, rows, grp_, le, lo)
        return carry

    def score_tail(kc, rows, grp_, le, lo):
        j, wts, limit = grp_["j"], grp_["wts"], grp_["limit"]
        sc = jnp.zeros((KEY_CHUNK, LANES), F32)
        for hp in range(IDX_HEADS // 2):
            sc = sc + wts[2 * hp] * jnp.maximum(le[:, hp * LANES:(hp + 1) * LANES], 0.0)
            sc = sc + wts[2 * hp + 1] * jnp.maximum(lo[:, hp * LANES:(hp + 1) * LANES], 0.0)
        bits = lax.bitcast_convert_type(sc, I32)
        bits = jnp.where(sc == 0.0, 0, bits)
        key = bits ^ ((bits >> 31) & 0x7FFFFFFF)
        valid = kc * KEY_CHUNK + krows < limit
        key = jnp.where(valid, key, INT_MIN)
        skey_ref[j, rows, :] = key
        for gi in range(KEY_CHUNK // PLANE_ROWS):
            a = [key[gi * PLANE_ROWS + 8 * r:gi * PLANE_ROWS + 8 * r + 8, :] ^ INT_MIN for r in range(32)]
            sh, msk = 16, 0x0000FFFF
            while sh:
                k = 0
                while k < 32:
                    tt = (a[k] ^ lax.shift_right_logical(a[k + sh], jnp.int32(sh))) & msk
                    a[k] = a[k] ^ tt
                    a[k + sh] = a[k + sh] ^ (tt << sh)
                    k = (k + sh + 1) & ~sh
                sh >>= 1
                msk = (msk ^ (msk << sh)) & 0xFFFFFFFF
            g8 = pl.multiple_of((kc * (KEY_CHUNK // PLANE_ROWS) + gi) * 8, 8)
            for b in range(32):
                planes_ref[j, b, pl.ds(g8, 8), :] = a[b]

    lax.fori_loop(0, nkc, score_chunk, 0)

    grp = seq // 32
    alive0 = jnp.where(lax.broadcasted_iota(I32, (grp, 1), 0) < nkc * (KEY_CHUNK // 32),
                       jnp.full((grp, LANES), -1, I32), 0)

    def lane_count(words):
        c = lax.population_count(words)
        if grp > 8:
            c = jnp.sum(c.reshape(grp // 8, 8, LANES), axis=0)
        return jnp.sum(c, axis=0, keepdims=True)

    def bit_body(i, carry):
        bit = jnp.left_shift(jnp.int32(1), 31 - i)
        out = []
        for grp_, (tu, above, alive) in zip(groups, carry):
            hit = alive & planes_ref[grp_["j"], i]
            c1 = lane_count(hit)
            take = above + c1 >= topk
            out.append((jnp.where(take, tu | bit, tu), jnp.where(take, above, above + c1),
                        jnp.where(take, hit, alive ^ hit)))
        return tuple(out)

    zero_row = jnp.zeros((1, LANES), I32)
    found_keys = lax.fori_loop(0, 32, bit_body, ((zero_row, zero_row, alive0),) * QUERY_GROUPS)

    mrows = lax.broadcasted_iota(I32, (LANES, 1), 0)
    k_meta = kr_ref[0, 0:LANES, :]
    init = []
    for grp_, (tu, above, alive) in zip(groups, found_keys):
        j = grp_["j"]
        thr = tu ^ INT_MIN
        found = tu != 0
        tie_lane = found & (above + lane_count(alive) > topk)

        @pl.when(jnp.max(jnp.where(tie_lane, 1, 0)) > 0)
        def _(j=j, thr=thr, found=found, above=above):
            need = (topk - above).astype(F32)

            def body(kc, seen):
                rows = pl.ds(pl.multiple_of(kc * KEY_CHUNK, KEY_CHUNK), KEY_CHUNK)
                blk = skey_ref[j, rows, :]
                tied = (blk == thr) & found
                eq = jnp.where(tied, 1.0, 0.0)
                rank = _dot(lstrict_ref[...], eq.astype(BF16)) + seen
                skey_ref[j, rows, :] = jnp.where(tied & (rank >= need), INT_MIN, blk)
                return seen + jnp.sum(eq, axis=0, keepdims=True)

            lax.fori_loop(0, nkc, body, jnp.zeros((1, LANES), F32))

        grp_["thr"] = jnp.maximum(thr, INT_MIN + 1)
        s_meta = _dot_nt(k_meta, grp_["q_hq"])
        s_meta = jnp.where((mrows >= PAD) & (mrows < CHUNK), s_meta, neg_inf)
        m0 = jnp.max(s_meta, axis=0, keepdims=True)
        p_meta = jnp.exp(s_meta - m0)
        ot_ref[j] = _dot(vmt_ref[...], p_meta.astype(BF16))
        init.append((m0, jnp.sum(p_meta, axis=0, keepdims=True)))

    def pv_chunk(kc, carry):
        rows = pl.ds(pl.multiple_of(kc * KEY_CHUNK, KEY_CHUNK), KEY_CHUNK)
        vblk = jnp.concatenate([vt_ref[(KEY_CHUNK // LANES) * kc + i] for i in range(KEY_CHUNK // LANES)], axis=1)
        out, probs = [], []
        for grp_, (m, l) in zip(groups, carry):
            bias = jnp.where(skey_ref[grp_["j"], rows, :] >= grp_["thr"], 0.0, neg_inf)
            s = st_ref[grp_["j"], rows, :] + jnp.concatenate([bias] * B_HEADS, axis=1)
            m_new = jnp.maximum(m, jnp.max(jnp.max(s.reshape(KEY_CHUNK // 8, 8, B_HEADS * LANES), axis=0),
                                           axis=0, keepdims=True))
            pr = jnp.exp(s - m_new)
            alpha = jnp.exp(m - m_new)
            out.append((m_new, l * alpha + jnp.sum(pr, axis=0, keepdims=True)))
            probs.append((alpha, pr.astype(BF16)))
        for grp_, (alpha, prb) in zip(groups, probs):
            ot_ref[grp_["j"]] = ot_ref[grp_["j"]] * alpha + _dot(vblk, prb)
        return tuple(out)

    sums = lax.fori_loop(0, nkc, pv_chunk, tuple(init))
    for grp_, (_, l) in zip(groups, sums):
        o_hq = (ot_ref[grp_["j"]] / l).T
        for hd in range(B_HEADS):
            out_ref[0, pl.ds(grp_["r0"], qpair), hd * B_DH:(hd + 1) * B_DH] = o_hq[hd * LANES:(hd + 1) * LANES, :]


def _dsa_call(qr, kr, v, iqr, ike, iko, sm, lstrict, seq, topk):
    bsz, tp, _ = qr.shape
    nsteps, rem = divmod(seq, 2 * CHUNK * QUERY_GROUPS)
    assert rem == 0
    full = lambda w_: pl.BlockSpec((1, tp, w_), lambda b, p: (b, 0, 0))
    hw = B_HEADS * B_DH
    nq = QUERY_GROUPS
    return pl.pallas_call(
        functools.partial(_dsa_body, seq=seq, topk=topk),
        grid=(bsz, nsteps),
        in_specs=[full(hw), full(B_DH), full(B_DH), full(hw), full(LANES), full(LANES), full(LANES),
                  _const_spec(lstrict.shape)],
        out_specs=full(hw),
        out_shape=jax.ShapeDtypeStruct((bsz, tp, hw), F32),
        scratch_shapes=[pltpu.VMEM((seq // LANES, B_DH, LANES), BF16), pltpu.VMEM((B_DH, LANES), BF16),
                        pltpu.VMEM((nq, seq, LANES), I32), pltpu.VMEM((nq, 32, seq // 32, LANES), I32),
                        pltpu.VMEM((nq, seq, hw), F32), pltpu.VMEM((nq, B_DH, hw), F32)],
        compiler_params=_params(("parallel", "arbitrary")),
        name="dsa",
    )(qr, kr, v, iqr, ike, iko, sm, lstrict)


def _mix_body(h_ref, oa_ref, ob_ref, g_ref, wg_ref, bg_ref, wa_ref, wb_ref, wo_ref, out_ref, *, rb):
    x = h_ref[0]
    d = x.shape[-1]
    n = _rms(x, g_ref[...]).astype(BF16)
    gates = _sigmoid(_dot(n, wg_ref[...]) + bg_ref[...])
    y = gates[:, :d] * _dot(oa_ref[0].astype(BF16), wa_ref[...]) + gates[:, d:] * _dot(ob_ref[0].astype(BF16), wb_ref[...])
    out = x + _dot(y.astype(BF16), wo_ref[...])
    rows = pl.program_id(1) * rb + lax.broadcasted_iota(I32, (rb, 1), 0)
    out_ref[0] = jnp.where(rows >= PAD, out, 0.0)


def _mix_call(h, oa, ob, gain, wg, bg, wa, wb, wo, rb):
    bsz, tp, d = h.shape
    row = lambda w_: pl.BlockSpec((1, rb, w_), lambda b, t: (b, t, 0))
    return pl.pallas_call(
        functools.partial(_mix_body, rb=rb),
        grid=(bsz, tp // rb),
        in_specs=[row(d), row(oa.shape[-1]), row(ob.shape[-1]), _const_spec((1, d)), _const_spec(wg.shape),
                  _const_spec(bg.shape), _const_spec(wa.shape), _const_spec(wb.shape), _const_spec(wo.shape)],
        out_specs=row(d),
        out_shape=jax.ShapeDtypeStruct((bsz, tp, d), F32),
        compiler_params=_params(("parallel", "arbitrary")),
        name="mix",
    )(h, oa, ob, gain, wg, bg, wa, wb, wo)


def _ffn_body(h_ref, g_ref, wup_ref, cw_ref, wd_ref, out_ref, carry_ref, *, rb, dff):
    t = pl.program_id(1)

    @pl.when(t == 0)
    def _():
        carry_ref[...] = jnp.zeros_like(carry_ref)

    x = h_ref[0]
    n = _rms(x, g_ref[...]).astype(BF16)
    rows = lax.broadcasted_iota(I32, (rb, 1), 0)
    acc = jnp.zeros(x.shape, F32)
    nch = dff // FF_CHUNK

    def gate_up(c):
        return (_dot(n, wup_ref[:, c * FF_CHUNK:(c + 1) * FF_CHUNK]),
                _dot(n, wup_ref[:, dff + c * FF_CHUNK:dff + (c + 1) * FF_CHUNK]))

    ahead = gate_up(0)
    for c in range(nch):
        cols = slice(c * FF_CHUNK, (c + 1) * FF_CHUNK)
        gate, up = ahead
        if c + 1 < nch:
            ahead = gate_up(c + 1)
        prev = carry_ref[:, cols]
        g1 = jnp.where(rows == 0, prev[7:8], pltpu.roll(gate, 1, 0))
        g2 = jnp.where(rows == 0, prev[6:7], jnp.where(rows == 1, prev[7:8], pltpu.roll(gate, 2, 0)))
        cw = cw_ref[:, cols]
        conv = g2 * cw[0:1] + g1 * cw[1:2] + gate * cw[2:3]
        carry_ref[:, cols] = gate[rb - 8:rb]
        acc = acc + _dot((_silu(conv) * up).astype(BF16), wd_ref[cols, :])
    out = x + acc
    out_ref[0] = jnp.where(t * rb + rows >= PAD, out, 0.0)


def _ffn_call(h, gain, wup, cw, wd, rb):
    bsz, tp, d = h.shape
    dff = wd.shape[0]
    row = pl.BlockSpec((1, rb, d), lambda b, t: (b, t, 0))
    return pl.pallas_call(
        functools.partial(_ffn_body, rb=rb, dff=dff),
        grid=(bsz, tp // rb),
        in_specs=[row, _const_spec((1, d)), _const_spec(wup.shape), _const_spec(cw.shape), _const_spec(wd.shape)],
        out_specs=row,
        out_shape=jax.ShapeDtypeStruct((bsz, tp, d), F32),
        scratch_shapes=[pltpu.VMEM((8, dff), F32)],
        compiler_params=_params(("parallel", "arbitrary")),
        name="ffn",
    )(h, gain, wup, cw, wd)


def _rope_tables(tp):
    pos = (jnp.arange(tp, dtype=F32) - PAD)[:, None]

    def cs(dim):
        inv = 1.0 / (ROPE_THETA ** (jnp.arange(0, dim, 2, dtype=F32) / dim))
        ang = pos * inv[None, :]
        return jnp.cos(ang), jnp.sin(ang)

    ca, sa = cs(B_DH)
    ci, si = cs(IDX_DIM)
    zi = jnp.zeros_like(si)
    return (jnp.concatenate([ca, ca], axis=1), jnp.concatenate([-sa, sa], axis=1),
            jnp.concatenate([ci] * 4, axis=1), jnp.concatenate([-si, zi, -si, zi], axis=1),
            jnp.concatenate([zi, si, zi, si], axis=1))


def _pad_lanes(vec):
    return jnp.zeros((1, LANES), F32).at[0, :vec.shape[0]].set(vec.astype(F32))


def _head_expander():
    pw, hw = A_HEADS * CHUNK, A_HEADS * A_DK
    head = np.concatenate([np.arange(pw) // CHUNK, np.arange(hw) // A_DK])
    src = np.arange(LANES)[:, None]
    return jnp.asarray(np.concatenate([src == head[None, :], src == A_HEADS + head[None, :]], axis=1), BF16)


def kernel(x, meta_tokens, norm_mix, w_in, conv_a, a_log, dt_bias, a_out_norm, q_norm, k_norm, kidx_norm,
           w_branch_a, w_branch_b, w_gate, b_gate, w_out, norm_ffn, w_up, conv_ffn, w_down):
    bsz, seq, d = x.shape
    depth = w_in.shape[0]
    assert seq % KEY_CHUNK == 0
    tp = CHUNK + seq
    rb = _row_block(tp // CHUNK)
    topk = min(TOPK_MAX, seq // TOPK_DIV)

    meta = jnp.broadcast_to(meta_tokens.astype(x.dtype)[None], (bsz, N_META, d))
    h = jnp.concatenate([jnp.zeros((bsz, PAD, d), x.dtype), meta, x], axis=1)

    tables = _rope_tables(tp)
    egb = _head_expander()
    tri = np.arange(CHUNK)
    lt3 = jnp.asarray(np.tile(tri[:, None] >= tri[None, :], (1, 3)), BF16)
    id3 = jnp.asarray(np.tile(np.eye(CHUNK), (1, 3)), BF16)
    trk = np.arange(KEY_CHUNK)
    lstrict = jnp.asarray(trk[:, None] > trk[None, :], BF16)

    hw = A_HEADS * A_DK
    o = np.cumsum((0, hw, hw, hw, hw, A_HEADS, A_HEADS, B_HEADS * B_DH, B_DH, B_DH,
                   IDX_HEADS * IDX_DIM, IDX_DIM, IDX_HEADS))
    for l in range(depth):
        w = w_in[l]
        small = jnp.concatenate([w[:, o[4]:o[6]], w[:, o[11]:o[12]],
                                 jnp.zeros((d, LANES - 2 * A_HEADS - IDX_HEADS), w.dtype)], axis=1)
        w_p = jnp.concatenate([w[:, o[0]:o[3]], w[:, o[3]:o[4]], small, w[:, o[6]:o[7]], w[:, o[7]:o[9]],
                               w[:, o[9]:o[10]], w[:, o[10]:o[11]], w[:, o[10]:o[11]]], axis=1).astype(BF16)
        qkv, z, sm, qr, kr, v, iqr, ike, iko = _proj_call(
            h, norm_mix[l][None], w_p, tables, q_norm[l][None], k_norm[l][None],
            jnp.concatenate([kidx_norm[l], kidx_norm[l]])[None], rb)
        o_a = _gdn_call(qkv, z, sm, conv_a[l], _pad_lanes(a_log[l]), _pad_lanes(dt_bias[l]),
                        a_out_norm[l][None], lt3, id3, egb, rb)
        o_b = _dsa_call(qr, kr, v, iqr, ike, iko, sm, lstrict, seq, topk)
        h = _mix_call(h, o_a, o_b, norm_mix[l][None], w_gate[l].astype(BF16), b_gate[l][None],
                      w_branch_a[l].astype(BF16), w_branch_b[l].astype(BF16), w_out[l].astype(BF16), rb)
        h = _ffn_call(h, norm_ffn[l][None], w_up[l].astype(BF16), conv_ffn[l], w_down[l].astype(BF16), rb)
    return h[:, CHUNK:]
```

```python
import functools
import math

import jax
import jax.numpy as jnp
import numpy as np
from jax import lax
from jax.experimental import pallas as pl
from jax.experimental.pallas import tpu as pltpu

F32 = jnp.float32
BF16 = jnp.bfloat16
I32 = jnp.int32

CHUNK = 64
N_META = 16
PAD = CHUNK - N_META
ROPE_THETA = 10000.0
EPS = 1e-6
A_HEADS = 4
A_DK = 128
A_CONV = 4
B_HEADS = 4
B_DH = 128
IDX_HEADS = 8
IDX_DIM = 64
TOPK_MAX = 256
TOPK_DIV = 4
FFN_CONV = 3
LANES = 128
KEY_CHUNK = 512
PLANE_ROWS = 256
QUERY_GROUPS = 4
FF_CHUNK = 256
INT_MIN = -(2 ** 31)
VMEM_LIMIT_BYTES = 56 * 1024 * 1024

NT_DIMS = (((1,), (1,)), ((), ()))


def _dot(a, b):
    return jnp.dot(a, b, preferred_element_type=F32)


def _dot_nt(a, b):
    return lax.dot_general(a, b, NT_DIMS, preferred_element_type=F32)


def _dot_exact(a, b):
    return jnp.dot(a, b, preferred_element_type=F32, precision=lax.Precision.HIGHEST)


def _rms(x, gain):
    return x * lax.rsqrt(jnp.mean(x * x, axis=-1, keepdims=True) + EPS) * gain


def _sigmoid(x):
    return 1.0 / (1.0 + jnp.exp(-x))


def _silu(x):
    return x * _sigmoid(x)


def _softplus(x):
    return jnp.maximum(x, 0.0) + jnp.log(1.0 + jnp.exp(-jnp.abs(x)))


def _row_block(n_chunks):
    g = max(d for d in range(1, 12) if n_chunks % d == 0)
    return g * CHUNK


def _params(sem):
    return pltpu.CompilerParams(dimension_semantics=sem, vmem_limit_bytes=VMEM_LIMIT_BYTES)


def _const_spec(shape):
    nd = len(shape)
    return pl.BlockSpec(shape, lambda *_: (0,) * nd)


PROJ_WIDTHS = (1536, 512, 128, 512, 256, 512, 128)


def _proj_body(h_ref, g_ref, w_ref, cosa_ref, sina_ref, cosi_ref, sinlo_ref, sinhi_ref,
               qn_ref, kn_ref, kin_ref,
               qkv_ref, z_ref, sm_ref, qr_ref, kr_ref, v_ref, iqr_ref, ike_ref, iko_ref):
    n = _rms(h_ref[0], g_ref[...]).astype(BF16)
    offs = np.cumsum((0,) + PROJ_WIDTHS)

    def mm(i):
        return _dot(n, w_ref[:, offs[i]:offs[i + 1]])

    ca, sa = cosa_ref[...], sina_ref[...]
    ci, slo, shi = cosi_ref[...], sinlo_ref[...], sinhi_ref[...]
    q = mm(3)
    iq = mm(5)
    parts = []
    for hd in range(B_HEADS):
        qh = _rms(q[:, hd * B_DH:(hd + 1) * B_DH], qn_ref[...])
        parts.append(qh * ca + pltpu.roll(qh, B_DH // 2, 1) * sa)
    qr_ref[0] = (jnp.concatenate(parts, axis=1) * (B_DH ** -0.5)).astype(BF16)

    kv = mm(4)
    ik = mm(6)
    parts = []
    for hp in range(IDX_HEADS // 2):
        x = iq[:, hp * LANES:(hp + 1) * LANES]
        parts.append(x * ci + pltpu.roll(x, LANES - IDX_DIM // 2, 1) * slo + pltpu.roll(x, IDX_DIM // 2, 1) * shi)
    iqr_ref[0] = jnp.concatenate(parts, axis=1).astype(BF16)

    qkv_ref[0] = mm(0)
    k = _rms(kv[:, :B_DH], kn_ref[...])
    kr_ref[0] = (k * ca + pltpu.roll(k, B_DH // 2, 1) * sa).astype(BF16)
    v_ref[0] = kv[:, B_DH:].astype(BF16)
    ik = _rms(ik, kin_ref[...])
    ik = ik * ci + pltpu.roll(ik, IDX_DIM // 2, 1) * (slo + shi)
    lane = lax.broadcasted_iota(I32, (1, LANES), 1)
    ike_ref[0] = jnp.where(lane < IDX_DIM, ik, 0.0).astype(BF16)
    iko_ref[0] = jnp.where(lane >= IDX_DIM, ik, 0.0).astype(BF16)
    z_ref[0] = mm(1)
    sm_ref[0] = mm(2)


def _proj_call(h, gain, w, tables, qn, kn, kin, rb):
    bsz, tp, d = h.shape
    nt = tp // rb
    row = lambda w_, dt: (jax.ShapeDtypeStruct((bsz, tp, w_), dt),
                          pl.BlockSpec((1, rb, w_), lambda b, t: (b, t, 0)))
    outs = [row(1536, F32), row(512, F32), row(128, F32), row(512, BF16), row(128, BF16),
            row(128, BF16), row(512, BF16), row(128, BF16), row(128, BF16)]
    tab_spec = pl.BlockSpec((rb, LANES), lambda b, t: (t, 0))
    return pl.pallas_call(
        _proj_body,
        grid=(bsz, nt),
        in_specs=[pl.BlockSpec((1, rb, d), lambda b, t: (b, t, 0)), _const_spec((1, d)), _const_spec(w.shape)]
                 + [tab_spec] * 5 + [_const_spec((1, LANES))] * 3,
        out_specs=[o[1] for o in outs],
        out_shape=[o[0] for o in outs],
        compiler_params=_params(("parallel", "arbitrary")),
        name="proj",
    )(h, gain, w, *tables, qn, kn, kin)


def _split3(x):
    x1 = x.astype(BF16)
    r1 = x - x1.astype(F32)
    x2 = r1.astype(BF16)
    x3 = (r1 - x2.astype(F32)).astype(BF16)
    return jnp.concatenate([x1, x2, x3], axis=0)


def _gdn_body(qkv_ref, z_ref, sm_ref, cw_ref, alog_ref, dtb_ref, ng_ref, lt3_ref, id3_ref, egb_ref,
              out_ref, xe_ref, s01_ref, s23_ref, u_ref, wq_ref, qkd_ref, kdt_ref, egl_ref, *, rb):
    t = pl.program_id(1)
    nwide = 3 * A_HEADS * A_DK

    @pl.when(t == 0)
    def _():
        xe_ref[0:8, :] = jnp.zeros((8, nwide), F32)
        s01_ref[...] = jnp.zeros_like(s01_ref)
        s23_ref[...] = jnp.zeros_like(s23_ref)

    xe_ref[8:8 + rb, :] = qkv_ref[0]
    cw = cw_ref[...]

    hw = A_HEADS * A_DK
    pw = A_HEADS * CHUNK
    ii = lax.broadcasted_iota(I32, (CHUNK, pw), 0)
    jj = lax.broadcasted_iota(I32, (CHUNK, pw), 1) & (CHUNK - 1)
    colhead = lax.broadcasted_iota(I32, (CHUNK, pw), 1) >> 6
    eye_p = ii == jj
    bd_mask = (lax.broadcasted_iota(I32, (pw, pw), 0) >> 6) == (lax.broadcasted_iota(I32, (pw, pw), 1) >> 6)
    bdk_mask = (lax.broadcasted_iota(I32, (pw, hw), 0) >> 6) == (lax.broadcasted_iota(I32, (pw, hw), 1) >> 7)
    pair_mask = (lax.broadcasted_iota(I32, (pw, pw), 0) >> 7) == (lax.broadcasted_iota(I32, (pw, pw), 1) >> 7)
    lane = lax.broadcasted_iota(I32, (1, LANES), 1)
    rows64 = lax.broadcasted_iota(I32, (CHUNK, 1), 0)
    neg_a = -jnp.exp(alog_ref[...])
    dtb = dtb_ref[...]
    ng = ng_ref[...]

    def block_diag(xp):
        return jnp.where(bd_mask, jnp.concatenate([xp.astype(BF16)] * A_HEADS, axis=0), 0)

    def stack_heads(xp):
        return jnp.concatenate([jnp.where(colhead == hd, xp, 0.0) for hd in range(A_HEADS)], axis=0)

    def l2n(x):
        parts = []
        for hd in range(A_HEADS):
            xh = x[:, hd * A_DK:(hd + 1) * A_DK]
            parts.append(xh * lax.rsqrt(jnp.sum(xh * xh, axis=-1, keepdims=True) + EPS))
        return jnp.concatenate(parts, axis=1)

    zeros = jnp.zeros((CHUNK, 2 * A_DK), F32)

    def pair_lanes(x, a_, b_):
        return jnp.concatenate([x[a_ * CHUNK:(a_ + 1) * CHUNK], x[b_ * CHUNK:(b_ + 1) * CHUNK]], axis=1)

    def local_load(c):
        r0 = c * CHUNK if isinstance(c, int) else pl.multiple_of(c * CHUNK, CHUNK)
        return xe_ref[pl.ds(r0, CHUNK + 8), :], sm_ref[0, pl.ds(r0, CHUNK), :], r0

    def local_compute(xw, sm, r0):
        acc = pltpu.roll(xw, 3, 0)[8:] * cw[0:1]
        acc = acc + pltpu.roll(xw, 2, 0)[8:] * cw[1:2]
        acc = acc + pltpu.roll(xw, 1, 0)[8:] * cw[2:3]
        acc = acc + xw[8:] * cw[3:4]
        xq = _silu(acc)
        q = l2n(xq[:, :hw]) * (A_DK ** -0.5)
        k = l2n(xq[:, hw:2 * hw])
        v = xq[:, 2 * hw:]

        g = jnp.where(lane < A_HEADS, neg_a * _softplus(sm + dtb), 0.0)
        g = jnp.where(t * rb + r0 + rows64 >= PAD, g, 0.0)
        x3 = _split3(jnp.where(lane < A_HEADS, g, _sigmoid(sm)))
        copies = _dot(x3, egb_ref[...]).astype(BF16)
        yield
        gx = _dot(lt3_ref[...], copies[:, :pw + hw])
        bx = _dot(id3_ref[...], copies[:, pw + hw:])
        yield
        gp, gq = gx[:, :pw], gx[:, pw:]
        bp, bq = bx[:, :pw], bx[:, pw:]
        grow = jnp.sum(jnp.where(eye_p, gp, 0.0), axis=0, keepdims=True)
        decay = jnp.where(ii >= jj, jnp.exp(gp - grow), 0.0)

        kb = k.astype(BF16)
        bdk = jnp.where(bdk_mask, jnp.concatenate([kb] * A_HEADS, axis=0), 0)
        kq = _dot_nt(jnp.concatenate([kb, q.astype(BF16)], axis=0), bdk)
        yield
        kk_p, qk_p = kq[:CHUNK], kq[CHUNK:]

        a = -jnp.where(ii > jj, bp * kk_p * decay, 0.0)
        tinv = jnp.where(eye_p, 1.0, 0.0) + a
        pk = _dot(a.astype(BF16), block_diag(a))
        yield
        for _ in range(4):
            pt = _dot(jnp.concatenate([pk, tinv], axis=0).astype(BF16), block_diag(pk))
            yield
            tinv = tinv + pt[CHUNK:]
            pk = pt[:CHUNK]
        tinv = tinv + _dot(tinv.astype(BF16), block_diag(pk))
        yield

        eg = jnp.exp(gq)
        vb = v * bq
        kbg = k * bq * eg
        rv = jnp.concatenate(
            [jnp.concatenate([vb[:, hd * A_DK:(hd + 1) * A_DK], kbg[:, hd * A_DK:(hd + 1) * A_DK]], axis=1)
             for hd in range(A_HEADS)], axis=0)
        uw = _dot(stack_heads(tinv).astype(BF16), rv.astype(BF16))
        yield
        u, w = uw[:, :A_DK], uw[:, A_DK:]

        qg = q * eg
        glast = gq[CHUNK - 1:CHUNK, :]
        kd = k * jnp.exp(glast - gq)
        return (u,
                jnp.concatenate([pair_lanes(w, 0, 1), qg[:, :2 * A_DK]], axis=0).astype(BF16),
                jnp.concatenate([pair_lanes(w, 2, 3), qg[:, 2 * A_DK:]], axis=0).astype(BF16),
                stack_heads(qk_p * decay).astype(BF16),
                jnp.concatenate([kd[:, :2 * A_DK], zeros], axis=0).T.astype(BF16),
                jnp.concatenate([kd[:, 2 * A_DK:], zeros], axis=0).T.astype(BF16),
                jnp.broadcast_to(jnp.exp(glast), (8, hw)))

    def local_store(c, vals):
        u_ref[c], wq_ref[c, 0], wq_ref[c, 1], qkd_ref[c], kdt_ref[c, 0], kdt_ref[c, 1], egl_ref[c] = vals

    def local_chain(c):
        vals = yield from local_compute(*local_load(c))
        local_store(c, vals)

    def state_chain(chunks):
        def unpair(r):
            return [r[:, :A_DK], r[:, A_DK:]]

        def state_update(s_old, kdt, vn_pair, egl_pair):
            upd = _dot(kdt, jnp.concatenate([vn_pair, zeros], axis=0).astype(BF16))
            return s_old * egl_pair + jnp.where(pair_mask, upd, 0.0)

        s01 = s01_ref[...]
        s23 = s23_ref[...]
        for c in chunks:
            r0 = c * CHUNK if isinstance(c, int) else pl.multiple_of(c * CHUNK, CHUNK)
            r01 = _dot(wq_ref[c, 0], s01.astype(BF16))
            r23 = _dot(wq_ref[c, 1], s23.astype(BF16))
            yield
            ws = jnp.concatenate(unpair(r01[:CHUNK]) + unpair(r23[:CHUNK]), axis=0)
            qs = jnp.concatenate(unpair(r01[CHUNK:]) + unpair(r23[CHUNK:]), axis=0)
            vn = u_ref[c] - ws
            o_rs = qs + _dot(qkd_ref[c], vn.astype(BF16))
            egl = egl_ref[c][0:1, :]
            s01 = state_update(s01, kdt_ref[c, 0], pair_lanes(vn, 0, 1), egl[:, :2 * A_DK])
            s23 = state_update(s23, kdt_ref[c, 1], pair_lanes(vn, 2, 3), egl[:, 2 * A_DK:])
            yield
            zz = z_ref[0, pl.ds(r0, CHUNK), :]
            parts = [_rms(o_rs[hd * CHUNK:(hd + 1) * CHUNK], ng) for hd in range(A_HEADS)]
            out_ref[0, pl.ds(r0, CHUNK), :] = jnp.concatenate(parts, axis=1) * _silu(zz)
        s01_ref[...] = s01
        s23_ref[...] = s23

    def run_lockstep(chains):
        live = list(chains)
        while live:
            for gen in list(live):
                try:
                    next(gen)
                except StopIteration:
                    live.remove(gen)

    nc = rb // CHUNK
    npairs = nc // 2
    if npairs == 0:
        run_lockstep([local_chain(0)])
        run_lockstep([state_chain([0])])
    else:
        run_lockstep([local_chain(0), local_chain(1)])

        def steady(i, carry):
            run_lockstep([local_chain(2 * i), local_chain(2 * i + 1), state_chain([2 * i - 2, 2 * i - 1])])
            return carry

        lax.fori_loop(1, npairs, steady, 0)
        last = [2 * npairs - 2, 2 * npairs - 1]
        if nc % 2:
            run_lockstep([local_chain(nc - 1), state_chain(last)])
            run_lockstep([state_chain([nc - 1])])
        else:
            run_lockstep([state_chain(last)])
    xe_ref[0:8, :] = xe_ref[rb:rb + 8, :]


def _gdn_call(qkv, z, sm, cw, alog, dtb, ng, lt3, id3, egb, rb):
    bsz, tp, _ = qkv.shape
    nt = tp // rb
    nc = rb // CHUNK
    hw = A_HEADS * A_DK
    pw = A_HEADS * CHUNK
    row = lambda w_: pl.BlockSpec((1, rb, w_), lambda b, t: (b, t, 0))
    return pl.pallas_call(
        functools.partial(_gdn_body, rb=rb),
        grid=(bsz, nt),
        in_specs=[row(3 * hw), row(hw), row(LANES), _const_spec(cw.shape), _const_spec((1, LANES)),
                  _const_spec((1, LANES)), _const_spec((1, LANES)), _const_spec(lt3.shape),
                  _const_spec(id3.shape), _const_spec(egb.shape)],
        out_specs=row(hw),
        out_shape=jax.ShapeDtypeStruct((bsz, tp, hw), F32),
        scratch_shapes=[pltpu.VMEM((rb + 8, 3 * hw), F32),
                        pltpu.VMEM((2 * A_DK, 2 * A_DK), F32), pltpu.VMEM((2 * A_DK, 2 * A_DK), F32),
                        pltpu.VMEM((nc, pw, A_DK), F32), pltpu.VMEM((nc, 2, 2 * CHUNK, 2 * A_DK), BF16),
                        pltpu.VMEM((nc, pw, pw), BF16), pltpu.VMEM((nc, 2, 2 * A_DK, 2 * CHUNK), BF16),
                        pltpu.VMEM((nc, 8, hw), F32)],
        compiler_params=_params(("parallel", "arbitrary")),
        name="gdn",
    )(qkv, z, sm, cw, alog, dtb, ng, lt3, id3, egb)


def _dsa_body(qr_ref, kr_ref, v_ref, iqr_ref, ike_ref, iko_ref, sm_ref, lstrict_ref, out_ref,
              vt_ref, vmt_ref, skey_ref, planes_ref, st_ref, ot_ref, *, seq, topk):
    g = pl.program_id(1)
    iw_scale = IDX_HEADS ** -0.5 * IDX_DIM ** -0.5
    qpair = 2 * CHUNK
    neg_inf = -jnp.inf

    @pl.when(g == 0)
    def _():
        planes_ref[...] = jnp.zeros_like(planes_ref)

        def vblk(i, carry):
            r = pl.multiple_of(CHUNK + i * LANES, CHUNK)
            vt_ref[i] = v_ref[0, pl.ds(r, LANES), :].astype(F32).T.astype(BF16)
            return carry

        lax.fori_loop(0, seq // LANES, vblk, 0)
        vmt_ref[...] = v_ref[0, 0:LANES, :].astype(F32).T.astype(BF16)

        qm = qr_ref[0, 0:CHUNK, :]
        km = kr_ref[0, 0:CHUNK, :]
        vm = v_ref[0, 0:CHUNK, :]
        colv = lax.broadcasted_iota(I32, (1, CHUNK), 1) >= PAD
        parts = []
        for hd in range(B_HEADS):
            s = _dot_nt(qm[:, hd * B_DH:(hd + 1) * B_DH], km)
            s = jnp.where(colv, s, neg_inf)
            e = jnp.exp(s - jnp.max(s, axis=-1, keepdims=True))
            pr = e / jnp.sum(e, axis=-1, keepdims=True)
            parts.append(_dot(pr.astype(BF16), vm))
        out_ref[0, 0:CHUNK, :] = jnp.concatenate(parts, axis=1)

    lane = lax.broadcasted_iota(I32, (1, LANES), 1)
    groups = []
    for j in range(QUERY_GROUPS):
        p = g * QUERY_GROUPS + j
        r0 = pl.multiple_of(CHUNK + p * qpair, CHUNK)
        qb = qr_ref[0, pl.ds(r0, qpair), :]
        ib = iqr_ref[0, pl.ds(r0, qpair), :]
        smt = sm_ref[0, pl.ds(r0, qpair), :].T
        groups.append(dict(
            j=j, r0=r0,
            q_hq=jnp.concatenate([qb[:, hd * B_DH:(hd + 1) * B_DH] for hd in range(B_HEADS)], axis=0),
            iq_hq=jnp.concatenate([ib[:, hp * LANES:(hp + 1) * LANES] for hp in range(IDX_HEADS // 2)], axis=0),
            wts=[smt[2 * A_HEADS + hd:2 * A_HEADS + hd + 1, :] * iw_scale for hd in range(IDX_HEADS)],
            limit=jnp.where(lane < CHUNK, (2 * p + 1) * CHUNK, (2 * p + 2) * CHUNK)))
    nkc = ((g + 1) * QUERY_GROUPS * qpair + KEY_CHUNK - 1) // KEY_CHUNK
    krows = lax.broadcasted_iota(I32, (KEY_CHUNK, 1), 0)

    def score_chunk(kc, carry):
        k0 = pl.multiple_of(CHUNK + kc * KEY_CHUNK, CHUNK)
        rows = pl.ds(pl.multiple_of(kc * KEY_CHUNK, KEY_CHUNK), KEY_CHUNK)
        ke, ko, kk = (ref[0, pl.ds(k0, KEY_CHUNK), :] for ref in (ike_ref, iko_ref, kr_ref))
        logits = []
        for grp_ in groups:
            logits.append((_dot_nt(ke, grp_["iq_hq"]), _dot_nt(ko, grp_["iq_hq"])))
            st_ref[grp_["j"], rows, :] = _dot_nt(kk, grp_["q_hq"])
        for grp_, (le, lo) in zip(groups, logits):
            score_tail(kc, rows, grp_, le, lo)
        return carry

    def score_tail(kc, rows, grp_, le, lo):
        j, wts, limit = grp_["j"], grp_["wts"], grp_["limit"]
        sc = jnp.zeros((KEY_CHUNK, LANES), F32)
        for hp in range(IDX_HEADS // 2):
            sc = sc + wts[2 * hp] * jnp.maximum(le[:, hp * LANES:(hp + 1) * LANES], 0.0)
            sc = sc + wts[2 * hp + 1] * jnp.maximum(lo[:, hp * LANES:(hp + 1) * LANES], 0.0)
        bits = lax.bitcast_convert_type(sc, I32)
        bits = jnp.where(sc == 0.0, 0, bits)
        key = bits ^ ((bits >> 31) & 0x7FFFFFFF)
        valid = kc * KEY_CHUNK + krows < limit
        key = jnp.where(valid, key, INT_MIN)
        skey_ref[j, rows, :] = key
        for gi in range(KEY_CHUNK // PLANE_ROWS):
            a = [key[gi * PLANE_ROWS + 8 * r:gi * PLANE_ROWS + 8 * r + 8, :] ^ INT_MIN for r in range(32)]
            sh, msk = 16, 0x0000FFFF
            while sh:
                k = 0
                while k < 32:
                    tt = (a[k] ^ lax.shift_right_logical(a[k + sh], jnp.int32(sh))) & msk
                    a[k] = a[k] ^ tt
                    a[k + sh] = a[k + sh] ^ (tt << sh)
                    k = (k + sh + 1) & ~sh
                sh >>= 1
                msk = (msk ^ (msk << sh)) & 0xFFFFFFFF
            g8 = pl.multiple_of((kc * (KEY_CHUNK // PLANE_ROWS) + gi) * 8, 8)
            for b in range(32):
                planes_ref[j, b, pl.ds(g8, 8), :] = a[b]

    lax.fori_loop(0, nkc, score_chunk, 0)

    grp = seq // 32
    alive0 = jnp.where(lax.broadcasted_iota(I32, (grp, 1), 0) < nkc * (KEY_CHUNK // 32),
                       jnp.full((grp, LANES), -1, I32), 0)

    def lane_count(words):
        c = lax.population_count(words)
        if grp > 8:
            c = jnp.sum(c.reshape(grp // 8, 8, LANES), axis=0)
        return jnp.sum(c, axis=0, keepdims=True)

    def bit_body(i, carry):
        bit = jnp.left_shift(jnp.int32(1), 31 - i)
        out = []
        for grp_, (tu, above, alive) in zip(groups, carry):
            hit = alive & planes_ref[grp_["j"], i]
            c1 = lane_count(hit)
            take = above + c1 >= topk
            out.append((jnp.where(take, tu | bit, tu), jnp.where(take, above, above + c1),
                        jnp.where(take, hit, alive ^ hit)))
        return tuple(out)

    zero_row = jnp.zeros((1, LANES), I32)
    found_keys = lax.fori_loop(0, 32, bit_body, ((zero_row, zero_row, alive0),) * QUERY_GROUPS)

    mrows = lax.broadcasted_iota(I32, (LANES, 1), 0)
    k_meta = kr_ref[0, 0:LANES, :]
    init = []
    for grp_, (tu, above, alive) in zip(groups, found_keys):
        j = grp_["j"]
        thr = tu ^ INT_MIN
        found = tu != 0
        tie_lane = found & (above + lane_count(alive) > topk)

        @pl.when(jnp.max(jnp.where(tie_lane, 1, 0)) > 0)
        def _(j=j, thr=thr, found=found, above=above):
            need = (topk - above).astype(F32)

            def body(kc, seen):
                rows = pl.ds(pl.multiple_of(kc * KEY_CHUNK, KEY_CHUNK), KEY_CHUNK)
                blk = skey_ref[j, rows, :]
                tied = (blk == thr) & found
                eq = jnp.where(tied, 1.0, 0.0)
                rank = _dot(lstrict_ref[...], eq.astype(BF16)) + seen
                skey_ref[j, rows, :] = jnp.where(tied & (rank >= need), INT_MIN, blk)
                return seen + jnp.sum(eq, axis=0, keepdims=True)

            lax.fori_loop(0, nkc, body, jnp.zeros((1, LANES), F32))

        grp_["thr"] = jnp.maximum(thr, INT_MIN + 1)
        s_meta = _dot_nt(k_meta, grp_["q_hq"])
        s_meta = jnp.where((mrows >= PAD) & (mrows < CHUNK), s_meta, neg_inf)
        m0 = jnp.max(s_meta, axis=0, keepdims=True)
        p_meta = jnp.exp(s_meta - m0)
        ot_ref[j] = _dot(vmt_ref[...], p_meta.astype(BF16))
        init.append((m0, jnp.sum(p_meta, axis=0, keepdims=True)))

    def pv_chunk(kc, carry):
        rows = pl.ds(pl.multiple_of(kc * KEY_CHUNK, KEY_CHUNK), KEY_CHUNK)
        vblk = jnp.concatenate([vt_ref[(KEY_CHUNK // LANES) * kc + i] for i in range(KEY_CHUNK // LANES)], axis=1)
        out, probs = [], []
        for grp_, (m, l) in zip(groups, carry):
            bias = jnp.where(skey_ref[grp_["j"], rows, :] >= grp_["thr"], 0.0, neg_inf)
            s = st_ref[grp_["j"], rows, :] + jnp.concatenate([bias] * B_HEADS, axis=1)
            m_new = jnp.maximum(m, jnp.max(jnp.max(s.reshape(KEY_CHUNK // 8, 8, B_HEADS * LANES), axis=0),
                                           axis=0, keepdims=True))
            pr = jnp.exp(s - m_new)
            alpha = jnp.exp(m - m_new)
            out.append((m_new, l * alpha + jnp.sum(pr, axis=0, keepdims=True)))
            probs.append((alpha, pr.astype(BF16)))
        for grp_, (alpha, prb) in zip(groups, probs):
            ot_ref[grp_["j"]] = ot_ref[grp_["j"]] * alpha + _dot(vblk, prb)
        return tuple(out)

    sums = lax.fori_loop(0, nkc, pv_chunk, tuple(init))
    for grp_, (_, l) in zip(groups, sums):
        o_hq = (ot_ref[grp_["j"]] / l).T
        for hd in range(B_HEADS):
            out_ref[0, pl.ds(grp_["r0"], qpair), hd * B_DH:(hd + 1) * B_DH] = o_hq[hd * LANES:(hd + 1) * LANES, :]


def _dsa_call(qr, kr, v, iqr, ike, iko, sm, lstrict, seq, topk):
    bsz, tp, _ = qr.shape
    nsteps, rem = divmod(seq, 2 * CHUNK * QUERY_GROUPS)
    assert rem == 0
    full = lambda w_: pl.BlockSpec((1, tp, w_), lambda b, p: (b, 0, 0))
    hw = B_HEADS * B_DH
    nq = QUERY_GROUPS
    return pl.pallas_call(
        functools.partial(_dsa_body, seq=seq, topk=topk),
        grid=(bsz, nsteps),
        in_specs=[full(hw), full(B_DH), full(B_DH), full(hw), full(LANES), full(LANES), full(LANES),
                  _const_spec(lstrict.shape)],
        out_specs=full(hw),
        out_shape=jax.ShapeDtypeStruct((bsz, tp, hw), F32),
        scratch_shapes=[pltpu.VMEM((seq // LANES, B_DH, LANES), BF16), pltpu.VMEM((B_DH, LANES), BF16),
                        pltpu.VMEM((nq, seq, LANES), I32), pltpu.VMEM((nq, 32, seq // 32, LANES), I32),
                        pltpu.VMEM((nq, seq, hw), F32), pltpu.VMEM((nq, B_DH, hw), F32)],
        compiler_params=_params(("parallel", "arbitrary")),
        name="dsa",
    )(qr, kr, v, iqr, ike, iko, sm, lstrict)


def _mix_body(h_ref, oa_ref, ob_ref, g_ref, wg_ref, bg_ref, wa_ref, wb_ref, wo_ref, out_ref, *, rb):
    x = h_ref[0]
    d = x.shape[-1]
    ya = _dot(oa_ref[0].astype(BF16), wa_ref[...])
    yb = _dot(ob_ref[0].astype(BF16), wb_ref[...])
    n = _rms(x, g_ref[...]).astype(BF16)
    gates = _sigmoid(_dot(n, wg_ref[...]) + bg_ref[...])
    y = gates[:, :d] * ya + gates[:, d:] * yb
    out = x + _dot(y.astype(BF16), wo_ref[...])
    rows = pl.program_id(1) * rb + lax.broadcasted_iota(I32, (rb, 1), 0)
    out_ref[0] = jnp.where(rows >= PAD, out, 0.0)


def _mix_call(h, oa, ob, gain, wg, bg, wa, wb, wo, rb):
    bsz, tp, d = h.shape
    row = lambda w_: pl.BlockSpec((1, rb, w_), lambda b, t: (b, t, 0))
    return pl.pallas_call(
        functools.partial(_mix_body, rb=rb),
        grid=(bsz, tp // rb),
        in_specs=[row(d), row(oa.shape[-1]), row(ob.shape[-1]), _const_spec((1, d)), _const_spec(wg.shape),
                  _const_spec(bg.shape), _const_spec(wa.shape), _const_spec(wb.shape), _const_spec(wo.shape)],
        out_specs=row(d),
        out_shape=jax.ShapeDtypeStruct((bsz, tp, d), F32),
        compiler_params=_params(("parallel", "arbitrary")),
        name="mix",
    )(h, oa, ob, gain, wg, bg, wa, wb, wo)


def _ffn_body(h_ref, g_ref, wup_ref, cw_ref, wd_ref, out_ref, carry_ref, act_ref, *, rb, dff):
    t = pl.program_id(1)

    @pl.when(t == 0)
    def _():
        carry_ref[...] = jnp.zeros_like(carry_ref)

    x = h_ref[0]
    n = _rms(x, g_ref[...]).astype(BF16)
    rows = lax.broadcasted_iota(I32, (rb, 1), 0)
    nch = dff // FF_CHUNK

    def gate_up(c):
        return (_dot(n, wup_ref[:, c * FF_CHUNK:(c + 1) * FF_CHUNK]),
                _dot(n, wup_ref[:, dff + c * FF_CHUNK:dff + (c + 1) * FF_CHUNK]))

    ahead = gate_up(0)
    for c in range(nch):
        cols = slice(c * FF_CHUNK, (c + 1) * FF_CHUNK)
        gate, up = ahead
        if c + 1 < nch:
            ahead = gate_up(c + 1)
        prev = carry_ref[:, cols]
        g1 = jnp.where(rows == 0, prev[7:8], pltpu.roll(gate, 1, 0))
        g2 = jnp.where(rows == 0, prev[6:7], jnp.where(rows == 1, prev[7:8], pltpu.roll(gate, 2, 0)))
        cw = cw_ref[:, cols]
        conv = g2 * cw[0:1] + g1 * cw[1:2] + gate * cw[2:3]
        carry_ref[:, cols] = gate[rb - 8:rb]
        act_ref[:, cols] = (_silu(conv) * up).astype(BF16)
    out = x + _dot(act_ref[...], wd_ref[...])
    out_ref[0] = jnp.where(t * rb + rows >= PAD, out, 0.0)


def _ffn_call(h, gain, wup, cw, wd, rb):
    bsz, tp, d = h.shape
    dff = wd.shape[0]
    row = pl.BlockSpec((1, rb, d), lambda b, t: (b, t, 0))
    return pl.pallas_call(
        functools.partial(_ffn_body, rb=rb, dff=dff),
        grid=(bsz, tp // rb),
        in_specs=[row, _const_spec((1, d)), _const_spec(wup.shape), _const_spec(cw.shape), _const_spec(wd.shape)],
        out_specs=row,
        out_shape=jax.ShapeDtypeStruct((bsz, tp, d), F32),
        scratch_shapes=[pltpu.VMEM((8, dff), F32), pltpu.VMEM((rb, dff), BF16)],
        compiler_params=_params(("parallel", "arbitrary")),
        name="ffn",
    )(h, gain, wup, cw, wd)


def _rope_tables(tp):
    pos = (jnp.arange(tp, dtype=F32) - PAD)[:, None]

    def cs(dim):
        inv = 1.0 / (ROPE_THETA ** (jnp.arange(0, dim, 2, dtype=F32) / dim))
        ang = pos * inv[None, :]
        return jnp.cos(ang), jnp.sin(ang)

    ca, sa = cs(B_DH)
    ci, si = cs(IDX_DIM)
    zi = jnp.zeros_like(si)
    return (jnp.concatenate([ca, ca], axis=1), jnp.concatenate([-sa, sa], axis=1),
            jnp.concatenate([ci] * 4, axis=1), jnp.concatenate([-si, zi, -si, zi], axis=1),
            jnp.concatenate([zi, si, zi, si], axis=1))


def _pad_lanes(vec):
    return jnp.zeros((1, LANES), F32).at[0, :vec.shape[0]].set(vec.astype(F32))


def _head_expander():
    pw, hw = A_HEADS * CHUNK, A_HEADS * A_DK
    head = np.concatenate([np.arange(pw) // CHUNK, np.arange(hw) // A_DK])
    src = np.arange(LANES)[:, None]
    return jnp.asarray(np.concatenate([src == head[None, :], src == A_HEADS + head[None, :]], axis=1), BF16)


def kernel(x, meta_tokens, norm_mix, w_in, conv_a, a_log, dt_bias, a_out_norm, q_norm, k_norm, kidx_norm,
           w_branch_a, w_branch_b, w_gate, b_gate, w_out, norm_ffn, w_up, conv_ffn, w_down):
    bsz, seq, d = x.shape
    depth = w_in.shape[0]
    assert seq % KEY_CHUNK == 0
    tp = CHUNK + seq
    rb = _row_block(tp // CHUNK)
    topk = min(TOPK_MAX, seq // TOPK_DIV)

    meta = jnp.broadcast_to(meta_tokens.astype(x.dtype)[None], (bsz, N_META, d))
    h = jnp.concatenate([jnp.zeros((bsz, PAD, d), x.dtype), meta, x], axis=1)

    tables = _rope_tables(tp)
    egb = _head_expander()
    tri = np.arange(CHUNK)
    lt3 = jnp.asarray(np.tile(tri[:, None] >= tri[None, :], (1, 3)), BF16)
    id3 = jnp.asarray(np.tile(np.eye(CHUNK), (1, 3)), BF16)
    trk = np.arange(KEY_CHUNK)
    lstrict = jnp.asarray(trk[:, None] > trk[None, :], BF16)

    hw = A_HEADS * A_DK
    o = np.cumsum((0, hw, hw, hw, hw, A_HEADS, A_HEADS, B_HEADS * B_DH, B_DH, B_DH,
                   IDX_HEADS * IDX_DIM, IDX_DIM, IDX_HEADS))
    for l in range(depth):
        w = w_in[l]
        small = jnp.concatenate([w[:, o[4]:o[6]], w[:, o[11]:o[12]],
                                 jnp.zeros((d, LANES - 2 * A_HEADS - IDX_HEADS), w.dtype)], axis=1)
        w_p = jnp.concatenate([w[:, o[0]:o[3]], w[:, o[3]:o[4]], small, w[:, o[6]:o[7]], w[:, o[7]:o[9]],
                               w[:, o[9]:o[10]], w[:, o[10]:o[11]], w[:, o[10]:o[11]]], axis=1).astype(BF16)
        qkv, z, sm, qr, kr, v, iqr, ike, iko = _proj_call(
            h, norm_mix[l][None], w_p, tables, q_norm[l][None], k_norm[l][None],
            jnp.concatenate([kidx_norm[l], kidx_norm[l]])[None], rb)
        o_a = _gdn_call(qkv, z, sm, conv_a[l], _pad_lanes(a_log[l]), _pad_lanes(dt_bias[l]),
                        a_out_norm[l][None], lt3, id3, egb, rb)
        o_b = _dsa_call(qr, kr, v, iqr, ike, iko, sm, lstrict, seq, topk)
        h = _mix_call(h, o_a, o_b, norm_mix[l][None], w_gate[l].astype(BF16), b_gate[l][None],
                      w_branch_a[l].astype(BF16), w_branch_b[l].astype(BF16), w_out[l].astype(BF16), rb)
        h = _ffn_call(h, norm_ffn[l][None], w_up[l].astype(BF16), conv_ffn[l], w_down[l].astype(BF16), rb)
    return h[:, CHUNK:]
```

```python
import functools
import math

import jax
import jax.numpy as jnp
import numpy as np
from jax import lax
from jax.experimental import pallas as pl
from jax.experimental.pallas import tpu as pltpu

F32 = jnp.float32
BF16 = jnp.bfloat16
I32 = jnp.int32

CHUNK = 64
N_META = 16
PAD = CHUNK - N_META
ROPE_THETA = 10000.0
EPS = 1e-6
A_HEADS = 4
A_DK = 128
A_CONV = 4
B_HEADS = 4
B_DH = 128
IDX_HEADS = 8
IDX_DIM = 64
TOPK_MAX = 256
TOPK_DIV = 4
FFN_CONV = 3
LANES = 128
KEY_CHUNK = 512
PLANE_ROWS = 256
QUERY_GROUPS = 4
ONES_ROWS = 16
FF_CHUNK = 256
INT_MIN = -(2 ** 31)
VMEM_LIMIT_BYTES = 56 * 1024 * 1024

NT_DIMS = (((1,), (1,)), ((), ()))


def _dot(a, b):
    return jnp.dot(a, b, preferred_element_type=F32)


def _dot_nt(a, b):
    return lax.dot_general(a, b, NT_DIMS, preferred_element_type=F32)


def _dot_exact(a, b):
    return jnp.dot(a, b, preferred_element_type=F32, precision=lax.Precision.HIGHEST)


def _rms(x, gain):
    return x * lax.rsqrt(jnp.mean(x * x, axis=-1, keepdims=True) + EPS) * gain


def _sigmoid(x):
    return 1.0 / (1.0 + jnp.exp(-x))


def _silu(x):
    return x * _sigmoid(x)


def _softplus(x):
    return jnp.maximum(x, 0.0) + jnp.log(1.0 + jnp.exp(-jnp.abs(x)))


def _row_block(n_chunks):
    g = max(d for d in range(1, 12) if n_chunks % d == 0)
    return g * CHUNK


def _params(sem):
    return pltpu.CompilerParams(dimension_semantics=sem, vmem_limit_bytes=VMEM_LIMIT_BYTES)


def _const_spec(shape):
    nd = len(shape)
    return pl.BlockSpec(shape, lambda *_: (0,) * nd)


PROJ_WIDTHS = (1536, 512, 128, 512, 256, 512, 128)


def _proj_body(h_ref, g_ref, w_ref, cosa_ref, sina_ref, cosi_ref, sinlo_ref, sinhi_ref,
               qn_ref, kn_ref, kin_ref,
               qkv_ref, z_ref, sm_ref, qr_ref, kr_ref, v_ref, iqr_ref, ike_ref, iko_ref):
    n = _rms(h_ref[0], g_ref[...]).astype(BF16)
    offs = np.cumsum((0,) + PROJ_WIDTHS)

    def mm(i):
        return _dot(n, w_ref[:, offs[i]:offs[i + 1]])

    ca, sa = cosa_ref[...], sina_ref[...]
    ci, slo, shi = cosi_ref[...], sinlo_ref[...], sinhi_ref[...]
    q = mm(3)
    iq = mm(5)
    parts = []
    for hd in range(B_HEADS):
        qh = _rms(q[:, hd * B_DH:(hd + 1) * B_DH], qn_ref[...])
        parts.append(qh * ca + pltpu.roll(qh, B_DH // 2, 1) * sa)
    qr_ref[0] = (jnp.concatenate(parts, axis=1) * (B_DH ** -0.5)).astype(BF16)

    kv = mm(4)
    ik = mm(6)
    parts = []
    for hp in range(IDX_HEADS // 2):
        x = iq[:, hp * LANES:(hp + 1) * LANES]
        parts.append(x * ci + pltpu.roll(x, LANES - IDX_DIM // 2, 1) * slo + pltpu.roll(x, IDX_DIM // 2, 1) * shi)
    iqr_ref[0] = jnp.concatenate(parts, axis=1).astype(BF16)

    qkv_ref[0] = mm(0)
    k = _rms(kv[:, :B_DH], kn_ref[...])
    kr_ref[0] = (k * ca + pltpu.roll(k, B_DH // 2, 1) * sa).astype(BF16)
    v_ref[0] = kv[:, B_DH:].astype(BF16)
    ik = _rms(ik, kin_ref[...])
    ik = ik * ci + pltpu.roll(ik, IDX_DIM // 2, 1) * (slo + shi)
    lane = lax.broadcasted_iota(I32, (1, LANES), 1)
    ike_ref[0] = jnp.where(lane < IDX_DIM, ik, 0.0).astype(BF16)
    iko_ref[0] = jnp.where(lane >= IDX_DIM, ik, 0.0).astype(BF16)
    z_ref[0] = mm(1)
    sm_ref[0] = mm(2)


def _proj_call(h, gain, w, tables, qn, kn, kin, rb):
    bsz, tp, d = h.shape
    nt = tp // rb
    row = lambda w_, dt: (jax.ShapeDtypeStruct((bsz, tp, w_), dt),
                          pl.BlockSpec((1, rb, w_), lambda b, t: (b, t, 0)))
    outs = [row(1536, F32), row(512, F32), row(128, F32), row(512, BF16), row(128, BF16),
            row(128, BF16), row(512, BF16), row(128, BF16), row(128, BF16)]
    tab_spec = pl.BlockSpec((rb, LANES), lambda b, t: (t, 0))
    return pl.pallas_call(
        _proj_body,
        grid=(bsz, nt),
        in_specs=[pl.BlockSpec((1, rb, d), lambda b, t: (b, t, 0)), _const_spec((1, d)), _const_spec(w.shape)]
                 + [tab_spec] * 5 + [_const_spec((1, LANES))] * 3,
        out_specs=[o[1] for o in outs],
        out_shape=[o[0] for o in outs],
        compiler_params=_params(("parallel", "arbitrary")),
        name="proj",
    )(h, gain, w, *tables, qn, kn, kin)


def _split3(x):
    x1 = x.astype(BF16)
    r1 = x - x1.astype(F32)
    x2 = r1.astype(BF16)
    x3 = (r1 - x2.astype(F32)).astype(BF16)
    return jnp.concatenate([x1, x2, x3], axis=0)


def _gdn_body(qkv_ref, z_ref, sm_ref, cw_ref, alog_ref, dtb_ref, ng_ref, lt3_ref, id3_ref, egb_ref,
              out_ref, xe_ref, s01_ref, s23_ref, u_ref, wq_ref, qkd_ref, kdt_ref, egl_ref, *, rb):
    t = pl.program_id(1)
    nwide = 3 * A_HEADS * A_DK

    @pl.when(t == 0)
    def _():
        xe_ref[0:8, :] = jnp.zeros((8, nwide), F32)
        s01_ref[...] = jnp.zeros_like(s01_ref)
        s23_ref[...] = jnp.zeros_like(s23_ref)

    xe_ref[8:8 + rb, :] = qkv_ref[0]
    cw = cw_ref[...]

    hw = A_HEADS * A_DK
    pw = A_HEADS * CHUNK
    ii = lax.broadcasted_iota(I32, (CHUNK, pw), 0)
    jj = lax.broadcasted_iota(I32, (CHUNK, pw), 1) & (CHUNK - 1)
    colhead = lax.broadcasted_iota(I32, (CHUNK, pw), 1) >> 6
    eye_p = ii == jj
    bd_mask = (lax.broadcasted_iota(I32, (pw, pw), 0) >> 6) == (lax.broadcasted_iota(I32, (pw, pw), 1) >> 6)
    bdk_mask = (lax.broadcasted_iota(I32, (pw, hw), 0) >> 6) == (lax.broadcasted_iota(I32, (pw, hw), 1) >> 7)
    pair_mask = (lax.broadcasted_iota(I32, (pw, pw), 0) >> 7) == (lax.broadcasted_iota(I32, (pw, pw), 1) >> 7)
    lane = lax.broadcasted_iota(I32, (1, LANES), 1)
    rows64 = lax.broadcasted_iota(I32, (CHUNK, 1), 0)
    neg_a = -jnp.exp(alog_ref[...])
    dtb = dtb_ref[...]
    ng = ng_ref[...]

    def block_diag(xp):
        return jnp.where(bd_mask, jnp.concatenate([xp.astype(BF16)] * A_HEADS, axis=0), 0)

    def stack_heads(xp):
        return jnp.concatenate([jnp.where(colhead == hd, xp, 0.0) for hd in range(A_HEADS)], axis=0)

    def l2n(x):
        parts = []
        for hd in range(A_HEADS):
            xh = x[:, hd * A_DK:(hd + 1) * A_DK]
            parts.append(xh * lax.rsqrt(jnp.sum(xh * xh, axis=-1, keepdims=True) + EPS))
        return jnp.concatenate(parts, axis=1)

    zeros = jnp.zeros((CHUNK, 2 * A_DK), F32)

    def pair_lanes(x, a_, b_):
        return jnp.concatenate([x[a_ * CHUNK:(a_ + 1) * CHUNK], x[b_ * CHUNK:(b_ + 1) * CHUNK]], axis=1)

    def local_load(c):
        r0 = c * CHUNK if isinstance(c, int) else pl.multiple_of(c * CHUNK, CHUNK)
        return xe_ref[pl.ds(r0, CHUNK + 8), :], sm_ref[0, pl.ds(r0, CHUNK), :], r0

    def local_compute(xw, sm, r0):
        acc = pltpu.roll(xw, 3, 0)[8:] * cw[0:1]
        acc = acc + pltpu.roll(xw, 2, 0)[8:] * cw[1:2]
        acc = acc + pltpu.roll(xw, 1, 0)[8:] * cw[2:3]
        acc = acc + xw[8:] * cw[3:4]
        xq = _silu(acc)
        q = l2n(xq[:, :hw]) * (A_DK ** -0.5)
        k = l2n(xq[:, hw:2 * hw])
        v = xq[:, 2 * hw:]

        g = jnp.where(lane < A_HEADS, neg_a * _softplus(sm + dtb), 0.0)
        g = jnp.where(t * rb + r0 + rows64 >= PAD, g, 0.0)
        x3 = _split3(jnp.where(lane < A_HEADS, g, _sigmoid(sm)))
        copies = _dot(x3, egb_ref[...]).astype(BF16)
        yield
        gx = _dot(lt3_ref[...], copies[:, :pw + hw])
        bx = _dot(id3_ref[...], copies[:, pw + hw:])
        yield
        gp, gq = gx[:, :pw], gx[:, pw:]
        bp, bq = bx[:, :pw], bx[:, pw:]
        grow = jnp.sum(jnp.where(eye_p, gp, 0.0), axis=0, keepdims=True)
        decay = jnp.where(ii >= jj, jnp.exp(gp - grow), 0.0)

        kb = k.astype(BF16)
        bdk = jnp.where(bdk_mask, jnp.concatenate([kb] * A_HEADS, axis=0), 0)
        kq = _dot_nt(jnp.concatenate([kb, q.astype(BF16)], axis=0), bdk)
        yield
        kk_p, qk_p = kq[:CHUNK], kq[CHUNK:]

        a = -jnp.where(ii > jj, bp * kk_p * decay, 0.0)
        tinv = jnp.where(eye_p, 1.0, 0.0) + a
        pk = _dot(a.astype(BF16), block_diag(a))
        yield
        for _ in range(4):
            pt = _dot(jnp.concatenate([pk, tinv], axis=0).astype(BF16), block_diag(pk))
            yield
            tinv = tinv + pt[CHUNK:]
            pk = pt[:CHUNK]
        tinv = tinv + _dot(tinv.astype(BF16), block_diag(pk))
        yield

        eg = jnp.exp(gq)
        vb = v * bq
        kbg = k * bq * eg
        rv = jnp.concatenate(
            [jnp.concatenate([vb[:, hd * A_DK:(hd + 1) * A_DK], kbg[:, hd * A_DK:(hd + 1) * A_DK]], axis=1)
             for hd in range(A_HEADS)], axis=0)
        uw = _dot(stack_heads(tinv).astype(BF16), rv.astype(BF16))
        yield
        u, w = uw[:, :A_DK], uw[:, A_DK:]

        qg = q * eg
        glast = gq[CHUNK - 1:CHUNK, :]
        kd = k * jnp.exp(glast - gq)
        return (u,
                jnp.concatenate([pair_lanes(w, 0, 1), qg[:, :2 * A_DK]], axis=0).astype(BF16),
                jnp.concatenate([pair_lanes(w, 2, 3), qg[:, 2 * A_DK:]], axis=0).astype(BF16),
                stack_heads(qk_p * decay).astype(BF16),
                jnp.concatenate([kd[:, :2 * A_DK], zeros], axis=0).T.astype(BF16),
                jnp.concatenate([kd[:, 2 * A_DK:], zeros], axis=0).T.astype(BF16),
                jnp.broadcast_to(jnp.exp(glast), (8, hw)))

    def local_store(c, vals):
        u_ref[c], wq_ref[c, 0], wq_ref[c, 1], qkd_ref[c], kdt_ref[c, 0], kdt_ref[c, 1], egl_ref[c] = vals

    def local_chain(c):
        vals = yield from local_compute(*local_load(c))
        local_store(c, vals)

    def state_chain(chunks):
        def unpair(r):
            return [r[:, :A_DK], r[:, A_DK:]]

        def state_update(s_old, kdt, vn_pair, egl_pair):
            upd = _dot(kdt, jnp.concatenate([vn_pair, zeros], axis=0).astype(BF16))
            return s_old * egl_pair + jnp.where(pair_mask, upd, 0.0)

        s01 = s01_ref[...]
        s23 = s23_ref[...]
        for c in chunks:
            r0 = c * CHUNK if isinstance(c, int) else pl.multiple_of(c * CHUNK, CHUNK)
            r01 = _dot(wq_ref[c, 0], s01.astype(BF16))
            r23 = _dot(wq_ref[c, 1], s23.astype(BF16))
            yield
            ws = jnp.concatenate(unpair(r01[:CHUNK]) + unpair(r23[:CHUNK]), axis=0)
            qs = jnp.concatenate(unpair(r01[CHUNK:]) + unpair(r23[CHUNK:]), axis=0)
            vn = u_ref[c] - ws
            o_rs = qs + _dot(qkd_ref[c], vn.astype(BF16))
            egl = egl_ref[c][0:1, :]
            s01 = state_update(s01, kdt_ref[c, 0], pair_lanes(vn, 0, 1), egl[:, :2 * A_DK])
            s23 = state_update(s23, kdt_ref[c, 1], pair_lanes(vn, 2, 3), egl[:, 2 * A_DK:])
            yield
            zz = z_ref[0, pl.ds(r0, CHUNK), :]
            parts = [_rms(o_rs[hd * CHUNK:(hd + 1) * CHUNK], ng) for hd in range(A_HEADS)]
            out_ref[0, pl.ds(r0, CHUNK), :] = jnp.concatenate(parts, axis=1) * _silu(zz)
        s01_ref[...] = s01
        s23_ref[...] = s23

    def run_lockstep(chains):
        live = list(chains)
        while live:
            for gen in list(live):
                try:
                    next(gen)
                except StopIteration:
                    live.remove(gen)

    nc = rb // CHUNK
    npairs = nc // 2
    if npairs == 0:
        run_lockstep([local_chain(0)])
        run_lockstep([state_chain([0])])
    else:
        run_lockstep([local_chain(0), local_chain(1)])

        def steady(i, carry):
            run_lockstep([local_chain(2 * i), local_chain(2 * i + 1), state_chain([2 * i - 2, 2 * i - 1])])
            return carry

        lax.fori_loop(1, npairs, steady, 0)
        last = [2 * npairs - 2, 2 * npairs - 1]
        if nc % 2:
            run_lockstep([local_chain(nc - 1), state_chain(last)])
            run_lockstep([state_chain([nc - 1])])
        else:
            run_lockstep([state_chain(last)])
    xe_ref[0:8, :] = xe_ref[rb:rb + 8, :]


def _gdn_call(qkv, z, sm, cw, alog, dtb, ng, lt3, id3, egb, rb):
    bsz, tp, _ = qkv.shape
    nt = tp // rb
    nc = rb // CHUNK
    hw = A_HEADS * A_DK
    pw = A_HEADS * CHUNK
    row = lambda w_: pl.BlockSpec((1, rb, w_), lambda b, t: (b, t, 0))
    return pl.pallas_call(
        functools.partial(_gdn_body, rb=rb),
        grid=(bsz, nt),
        in_specs=[row(3 * hw), row(hw), row(LANES), _const_spec(cw.shape), _const_spec((1, LANES)),
                  _const_spec((1, LANES)), _const_spec((1, LANES)), _const_spec(lt3.shape),
                  _const_spec(id3.shape), _const_spec(egb.shape)],
        out_specs=row(hw),
        out_shape=jax.ShapeDtypeStruct((bsz, tp, hw), F32),
        scratch_shapes=[pltpu.VMEM((rb + 8, 3 * hw), F32),
                        pltpu.VMEM((2 * A_DK, 2 * A_DK), F32), pltpu.VMEM((2 * A_DK, 2 * A_DK), F32),
                        pltpu.VMEM((nc, pw, A_DK), F32), pltpu.VMEM((nc, 2, 2 * CHUNK, 2 * A_DK), BF16),
                        pltpu.VMEM((nc, pw, pw), BF16), pltpu.VMEM((nc, 2, 2 * A_DK, 2 * CHUNK), BF16),
                        pltpu.VMEM((nc, 8, hw), F32)],
        compiler_params=_params(("parallel", "arbitrary")),
        name="gdn",
    )(qkv, z, sm, cw, alog, dtb, ng, lt3, id3, egb)


def _dsa_body(qr_ref, kr_ref, v_ref, iqr_ref, ike_ref, iko_ref, sm_ref, lstrict_ref, out_ref,
              vt_ref, vmt_ref, skey_ref, planes_ref, st_ref, ot_ref, *, seq, topk):
    g = pl.program_id(1)
    iw_scale = IDX_HEADS ** -0.5 * IDX_DIM ** -0.5
    qpair = 2 * CHUNK
    neg_inf = -jnp.inf

    @pl.when(g == 0)
    def _():
        planes_ref[...] = jnp.zeros_like(planes_ref)

        def vblk(i, carry):
            r = pl.multiple_of(CHUNK + i * LANES, CHUNK)
            vt_ref[i] = v_ref[0, pl.ds(r, LANES), :].astype(F32).T.astype(BF16)
            return carry

        lax.fori_loop(0, seq // LANES, vblk, 0)
        vmt_ref[...] = v_ref[0, 0:LANES, :].astype(F32).T.astype(BF16)

        qm = qr_ref[0, 0:CHUNK, :]
        km = kr_ref[0, 0:CHUNK, :]
        vm = v_ref[0, 0:CHUNK, :]
        colv = lax.broadcasted_iota(I32, (1, CHUNK), 1) >= PAD
        parts = []
        for hd in range(B_HEADS):
            s = _dot_nt(qm[:, hd * B_DH:(hd + 1) * B_DH], km)
            s = jnp.where(colv, s, neg_inf)
            e = jnp.exp(s - jnp.max(s, axis=-1, keepdims=True))
            pr = e / jnp.sum(e, axis=-1, keepdims=True)
            parts.append(_dot(pr.astype(BF16), vm))
        out_ref[0, 0:CHUNK, :] = jnp.concatenate(parts, axis=1)

    lane = lax.broadcasted_iota(I32, (1, LANES), 1)
    groups = []
    for j in range(QUERY_GROUPS):
        p = g * QUERY_GROUPS + j
        r0 = pl.multiple_of(CHUNK + p * qpair, CHUNK)
        qb = qr_ref[0, pl.ds(r0, qpair), :]
        ib = iqr_ref[0, pl.ds(r0, qpair), :]
        smt = sm_ref[0, pl.ds(r0, qpair), :].T
        groups.append(dict(
            j=j, r0=r0,
            q_hq=jnp.concatenate([qb[:, hd * B_DH:(hd + 1) * B_DH] for hd in range(B_HEADS)], axis=0),
            iq_hq=jnp.concatenate([ib[:, hp * LANES:(hp + 1) * LANES] for hp in range(IDX_HEADS // 2)], axis=0),
            wts=[smt[2 * A_HEADS + hd:2 * A_HEADS + hd + 1, :] * iw_scale for hd in range(IDX_HEADS)],
            limit=jnp.where(lane < CHUNK, (2 * p + 1) * CHUNK, (2 * p + 2) * CHUNK)))
    nkc = ((g + 1) * QUERY_GROUPS * qpair + KEY_CHUNK - 1) // KEY_CHUNK
    krows = lax.broadcasted_iota(I32, (KEY_CHUNK, 1), 0)

    def score_chunk(kc, carry):
        k0 = pl.multiple_of(CHUNK + kc * KEY_CHUNK, CHUNK)
        rows = pl.ds(pl.multiple_of(kc * KEY_CHUNK, KEY_CHUNK), KEY_CHUNK)
        ke, ko, kk = (ref[0, pl.ds(k0, KEY_CHUNK), :] for ref in (ike_ref, iko_ref, kr_ref))
        logits = []
        for grp_ in groups:
            logits.append((_dot_nt(ke, grp_["iq_hq"]), _dot_nt(ko, grp_["iq_hq"])))
            st_ref[grp_["j"], rows, :] = _dot_nt(kk, grp_["q_hq"])
        for grp_, (le, lo) in zip(groups, logits):
            score_tail(kc, rows, grp_, le, lo)
        return carry

    def score_tail(kc, rows, grp_, le, lo):
        j, wts, limit = grp_["j"], grp_["wts"], grp_["limit"]
        sc = jnp.zeros((KEY_CHUNK, LANES), F32)
        for hp in range(IDX_HEADS // 2):
            sc = sc + wts[2 * hp] * jnp.maximum(le[:, hp * LANES:(hp + 1) * LANES], 0.0)
            sc = sc + wts[2 * hp + 1] * jnp.maximum(lo[:, hp * LANES:(hp + 1) * LANES], 0.0)
        bits = lax.bitcast_convert_type(sc, I32)
        bits = jnp.where(sc == 0.0, 0, bits)
        key = bits ^ ((bits >> 31) & 0x7FFFFFFF)
        valid = kc * KEY_CHUNK + krows < limit
        key = jnp.where(valid, key, INT_MIN)
        skey_ref[j, rows, :] = key
        for gi in range(KEY_CHUNK // PLANE_ROWS):
            a = [key[gi * PLANE_ROWS + 8 * r:gi * PLANE_ROWS + 8 * r + 8, :] ^ INT_MIN for r in range(32)]
            sh, msk = 16, 0x0000FFFF
            while sh:
                k = 0
                while k < 32:
                    tt = (a[k] ^ lax.shift_right_logical(a[k + sh], jnp.int32(sh))) & msk
                    a[k] = a[k] ^ tt
                    a[k + sh] = a[k + sh] ^ (tt << sh)
                    k = (k + sh + 1) & ~sh
                sh >>= 1
                msk = (msk ^ (msk << sh)) & 0xFFFFFFFF
            g8 = pl.multiple_of((kc * (KEY_CHUNK // PLANE_ROWS) + gi) * 8, 8)
            for b in range(32):
                planes_ref[j, b, pl.ds(g8, 8), :] = a[b]

    lax.fori_loop(0, nkc, score_chunk, 0)

    grp = seq // 32
    alive0 = jnp.where(lax.broadcasted_iota(I32, (grp, 1), 0) < nkc * (KEY_CHUNK // 32),
                       jnp.full((grp, LANES), -1, I32), 0)

    def lane_count(words):
        c = lax.population_count(words)
        if grp > 8:
            c = jnp.sum(c.reshape(grp // 8, 8, LANES), axis=0)
        return jnp.sum(c, axis=0, keepdims=True)

    def bit_body(i, carry):
        bit = jnp.left_shift(jnp.int32(1), 31 - i)
        out = []
        for grp_, (tu, above, alive) in zip(groups, carry):
            hit = alive & planes_ref[grp_["j"], i]
            c1 = lane_count(hit)
            take = above + c1 >= topk
            out.append((jnp.where(take, tu | bit, tu), jnp.where(take, above, above + c1),
                        jnp.where(take, hit, alive ^ hit)))
        return tuple(out)

    zero_row = jnp.zeros((1, LANES), I32)
    found_keys = lax.fori_loop(0, 32, bit_body, ((zero_row, zero_row, alive0),) * QUERY_GROUPS)

    any_tie = jnp.zeros((1, LANES), I32)
    for grp_, (tu, above, alive) in zip(groups, found_keys):
        grp_["thr"], grp_["found"], grp_["above"] = tu ^ INT_MIN, tu != 0, above
        any_tie = any_tie | jnp.where(grp_["found"] & (above + lane_count(alive) > topk), 1, 0)

    @pl.when(jnp.max(any_tie) > 0)
    def _():
        for grp_ in groups:
            j, thr, found = grp_["j"], grp_["thr"], grp_["found"]
            need = (topk - grp_["above"]).astype(F32)

            def body(kc, seen, j=j, thr=thr, found=found, need=need):
                rows = pl.ds(pl.multiple_of(kc * KEY_CHUNK, KEY_CHUNK), KEY_CHUNK)
                blk = skey_ref[j, rows, :]
                tied = (blk == thr) & found
                eq = jnp.where(tied, 1.0, 0.0)
                rank = _dot(lstrict_ref[...], eq.astype(BF16)) + seen
                skey_ref[j, rows, :] = jnp.where(tied & (rank >= need), INT_MIN, blk)
                return seen + jnp.sum(eq, axis=0, keepdims=True)

            lax.fori_loop(0, nkc, body, jnp.zeros((1, LANES), F32))

    mrows = lax.broadcasted_iota(I32, (LANES, 1), 0)
    k_meta = kr_ref[0, 0:LANES, :]
    ones = jnp.ones((ONES_ROWS, LANES), BF16)
    v_meta = jnp.concatenate([vmt_ref[...], ones], axis=0)
    init = []
    for grp_ in groups:
        s_meta = _dot_nt(k_meta, grp_["q_hq"])
        s_meta = jnp.where((mrows >= PAD) & (mrows < CHUNK), s_meta, neg_inf)
        m0 = jnp.max(s_meta, axis=0, keepdims=True)
        ot_ref[grp_["j"]] = _dot(v_meta, jnp.exp(s_meta - m0).astype(BF16))
        init.append(m0)
        grp_["thr"] = jnp.maximum(grp_["thr"], INT_MIN + 1)

    def pv_chunk(kc, carry):
        rows = pl.ds(pl.multiple_of(kc * KEY_CHUNK, KEY_CHUNK), KEY_CHUNK)
        nblk = KEY_CHUNK // LANES
        vblk = jnp.concatenate([jnp.concatenate([vt_ref[nblk * kc + i], ones], axis=0) for i in range(nblk)], axis=1)
        out, probs = [], []
        for grp_, m in zip(groups, carry):
            bias = jnp.where(skey_ref[grp_["j"], rows, :] >= grp_["thr"], 0.0, neg_inf)
            s = st_ref[grp_["j"], rows, :] + jnp.concatenate([bias] * B_HEADS, axis=1)
            m_new = jnp.maximum(m, jnp.max(jnp.max(s.reshape(KEY_CHUNK // 8, 8, B_HEADS * LANES), axis=0),
                                           axis=0, keepdims=True))
            out.append(m_new)
            probs.append((jnp.exp(m - m_new), jnp.exp(s - m_new).astype(BF16)))
        for grp_, (alpha, prb) in zip(groups, probs):
            ot_ref[grp_["j"]] = ot_ref[grp_["j"]] * alpha + _dot(vblk, prb)
        return tuple(out)

    lax.fori_loop(0, nkc, pv_chunk, tuple(init))
    for grp_ in groups:
        acc = ot_ref[grp_["j"]]
        o_hq = (acc[:B_DH] / acc[B_DH:B_DH + 1]).T
        for hd in range(B_HEADS):
            out_ref[0, pl.ds(grp_["r0"], qpair), hd * B_DH:(hd + 1) * B_DH] = o_hq[hd * LANES:(hd + 1) * LANES, :]


def _dsa_call(qr, kr, v, iqr, ike, iko, sm, lstrict, seq, topk):
    bsz, tp, _ = qr.shape
    nsteps, rem = divmod(seq, 2 * CHUNK * QUERY_GROUPS)
    assert rem == 0
    full = lambda w_: pl.BlockSpec((1, tp, w_), lambda b, p: (b, 0, 0))
    hw = B_HEADS * B_DH
    nq = QUERY_GROUPS
    return pl.pallas_call(
        functools.partial(_dsa_body, seq=seq, topk=topk),
        grid=(bsz, nsteps),
        in_specs=[full(hw), full(B_DH), full(B_DH), full(hw), full(LANES), full(LANES), full(LANES),
                  _const_spec(lstrict.shape)],
        out_specs=full(hw),
        out_shape=jax.ShapeDtypeStruct((bsz, tp, hw), F32),
        scratch_shapes=[pltpu.VMEM((seq // LANES, B_DH, LANES), BF16), pltpu.VMEM((B_DH, LANES), BF16),
                        pltpu.VMEM((nq, seq, LANES), I32), pltpu.VMEM((nq, 32, seq // 32, LANES), I32),
                        pltpu.VMEM((nq, seq, hw), F32), pltpu.VMEM((nq, B_DH + ONES_ROWS, hw), F32)],
        compiler_params=_params(("parallel", "arbitrary")),
        name="dsa",
    )(qr, kr, v, iqr, ike, iko, sm, lstrict)


def _mix_body(h_ref, oa_ref, ob_ref, g_ref, wg_ref, bg_ref, wa_ref, wb_ref, wo_ref, out_ref, *, rb):
    x = h_ref[0]
    d = x.shape[-1]
    n = _rms(x, g_ref[...]).astype(BF16)
    gates = _sigmoid(_dot(n, wg_ref[...]) + bg_ref[...])
    y = gates[:, :d] * _dot(oa_ref[0].astype(BF16), wa_ref[...]) + gates[:, d:] * _dot(ob_ref[0].astype(BF16), wb_ref[...])
    out = x + _dot(y.astype(BF16), wo_ref[...])
    rows = pl.program_id(1) * rb + lax.broadcasted_iota(I32, (rb, 1), 0)
    out_ref[0] = jnp.where(rows >= PAD, out, 0.0)


def _mix_call(h, oa, ob, gain, wg, bg, wa, wb, wo, rb):
    bsz, tp, d = h.shape
    row = lambda w_: pl.BlockSpec((1, rb, w_), lambda b, t: (b, t, 0))
    return pl.pallas_call(
        functools.partial(_mix_body, rb=rb),
        grid=(bsz, tp // rb),
        in_specs=[row(d), row(oa.shape[-1]), row(ob.shape[-1]), _const_spec((1, d)), _const_spec(wg.shape),
                  _const_spec(bg.shape), _const_spec(wa.shape), _const_spec(wb.shape), _const_spec(wo.shape)],
        out_specs=row(d),
        out_shape=jax.ShapeDtypeStruct((bsz, tp, d), F32),
        compiler_params=_params(("parallel", "arbitrary")),
        name="mix",
    )(h, oa, ob, gain, wg, bg, wa, wb, wo)


def _ffn_body(h_ref, g_ref, wup_ref, cw_ref, wd_ref, out_ref, carry_ref, act_ref, *, rb, dff):
    t = pl.program_id(1)

    @pl.when(t == 0)
    def _():
        carry_ref[...] = jnp.zeros_like(carry_ref)

    x = h_ref[0]
    n = _rms(x, g_ref[...]).astype(BF16)
    rows = lax.broadcasted_iota(I32, (rb, 1), 0)
    nch = dff // FF_CHUNK

    def gate_up(c):
        return (_dot(n, wup_ref[:, c * FF_CHUNK:(c + 1) * FF_CHUNK]),
                _dot(n, wup_ref[:, dff + c * FF_CHUNK:dff + (c + 1) * FF_CHUNK]))

    ahead = gate_up(0)
    for c in range(nch):
        cols = slice(c * FF_CHUNK, (c + 1) * FF_CHUNK)
        gate, up = ahead
        if c + 1 < nch:
            ahead = gate_up(c + 1)
        prev = carry_ref[:, cols]
        g1 = jnp.where(rows == 0, prev[7:8], pltpu.roll(gate, 1, 0))
        g2 = jnp.where(rows == 0, prev[6:7], jnp.where(rows == 1, prev[7:8], pltpu.roll(gate, 2, 0)))
        cw = cw_ref[:, cols]
        conv = g2 * cw[0:1] + g1 * cw[1:2] + gate * cw[2:3]
        carry_ref[:, cols] = gate[rb - 8:rb]
        act_ref[:, cols] = (_silu(conv) * up).astype(BF16)
    out = x + _dot(act_ref[...], wd_ref[...])
    out_ref[0] = jnp.where(t * rb + rows >= PAD, out, 0.0)


def _ffn_call(h, gain, wup, cw, wd, rb):
    bsz, tp, d = h.shape
    dff = wd.shape[0]
    row = pl.BlockSpec((1, rb, d), lambda b, t: (b, t, 0))
    return pl.pallas_call(
        functools.partial(_ffn_body, rb=rb, dff=dff),
        grid=(bsz, tp // rb),
        in_specs=[row, _const_spec((1, d)), _const_spec(wup.shape), _const_spec(cw.shape), _const_spec(wd.shape)],
        out_specs=row,
        out_shape=jax.ShapeDtypeStruct((bsz, tp, d), F32),
        scratch_shapes=[pltpu.VMEM((8, dff), F32), pltpu.VMEM((rb, dff), BF16)],
        compiler_params=_params(("parallel", "arbitrary")),
        name="ffn",
    )(h, gain, wup, cw, wd)


def _rope_tables(tp):
    pos = (jnp.arange(tp, dtype=F32) - PAD)[:, None]

    def cs(dim):
        inv = 1.0 / (ROPE_THETA ** (jnp.arange(0, dim, 2, dtype=F32) / dim))
        ang = pos * inv[None, :]
        return jnp.cos(ang), jnp.sin(ang)

    ca, sa = cs(B_DH)
    ci, si = cs(IDX_DIM)
    zi = jnp.zeros_like(si)
    return (jnp.concatenate([ca, ca], axis=1), jnp.concatenate([-sa, sa], axis=1),
            jnp.concatenate([ci] * 4, axis=1), jnp.concatenate([-si, zi, -si, zi], axis=1),
            jnp.concatenate([zi, si, zi, si], axis=1))


def _pad_lanes(vec):
    return jnp.zeros((1, LANES), F32).at[0, :vec.shape[0]].set(vec.astype(F32))


def _head_expander():
    pw, hw = A_HEADS * CHUNK, A_HEADS * A_DK
    head = np.concatenate([np.arange(pw) // CHUNK, np.arange(hw) // A_DK])
    src = np.arange(LANES)[:, None]
    return jnp.asarray(np.concatenate([src == head[None, :], src == A_HEADS + head[None, :]], axis=1), BF16)


def kernel(x, meta_tokens, norm_mix, w_in, conv_a, a_log, dt_bias, a_out_norm, q_norm, k_norm, kidx_norm,
           w_branch_a, w_branch_b, w_gate, b_gate, w_out, norm_ffn, w_up, conv_ffn, w_down):
    bsz, seq, d = x.shape
    depth = w_in.shape[0]
    assert seq % KEY_CHUNK == 0
    tp = CHUNK + seq
    rb = _row_block(tp // CHUNK)
    topk = min(TOPK_MAX, seq // TOPK_DIV)

    meta = jnp.broadcast_to(meta_tokens.astype(x.dtype)[None], (bsz, N_META, d))
    h = jnp.concatenate([jnp.zeros((bsz, PAD, d), x.dtype), meta, x], axis=1)

    tables = _rope_tables(tp)
    egb = _head_expander()
    tri = np.arange(CHUNK)
    lt3 = jnp.asarray(np.tile(tri[:, None] >= tri[None, :], (1, 3)), BF16)
    id3 = jnp.asarray(np.tile(np.eye(CHUNK), (1, 3)), BF16)
    trk = np.arange(KEY_CHUNK)
    lstrict = jnp.asarray(trk[:, None] > trk[None, :], BF16)

    hw = A_HEADS * A_DK
    o = np.cumsum((0, hw, hw, hw, hw, A_HEADS, A_HEADS, B_HEADS * B_DH, B_DH, B_DH,
                   IDX_HEADS * IDX_DIM, IDX_DIM, IDX_HEADS))
    for l in range(depth):
        w = w_in[l]
        small = jnp.concatenate([w[:, o[4]:o[6]], w[:, o[11]:o[12]],
                                 jnp.zeros((d, LANES - 2 * A_HEADS - IDX_HEADS), w.dtype)], axis=1)
        w_p = jnp.concatenate([w[:, o[0]:o[3]], w[:, o[3]:o[4]], small, w[:, o[6]:o[7]], w[:, o[7]:o[9]],
                               w[:, o[9]:o[10]], w[:, o[10]:o[11]], w[:, o[10]:o[11]]], axis=1).astype(BF16)
        qkv, z, sm, qr, kr, v, iqr, ike, iko = _proj_call(
            h, norm_mix[l][None], w_p, tables, q_norm[l][None], k_norm[l][None],
            jnp.concatenate([kidx_norm[l], kidx_norm[l]])[None], rb)
        o_a = _gdn_call(qkv, z, sm, conv_a[l], _pad_lanes(a_log[l]), _pad_lanes(dt_bias[l]),
                        a_out_norm[l][None], lt3, id3, egb, rb)
        o_b = _dsa_call(qr, kr, v, iqr, ike, iko, sm, lstrict, seq, topk)
        h = _mix_call(h, o_a, o_b, norm_mix[l][None], w_gate[l].astype(BF16), b_gate[l][None],
                      w_branch_a[l].astype(BF16), w_branch_b[l].astype(BF16), w_out[l].astype(BF16), rb)
        h = _ffn_call(h, norm_ffn[l][None], w_up[l].astype(BF16), conv_ffn[l], w_down[l].astype(BF16), rb)
    return h[:, CHUNK:]
```

```python
import functools
import math

import jax
import jax.numpy as jnp
import numpy as np
from jax import lax
from jax.experimental import pallas as pl
from jax.experimental.pallas import tpu as pltpu

F32 = jnp.float32
BF16 = jnp.bfloat16
I32 = jnp.int32

CHUNK = 64
N_META = 16
PAD = CHUNK - N_META
ROPE_THETA = 10000.0
EPS = 1e-6
A_HEADS = 4
A_DK = 128
A_CONV = 4
B_HEADS = 4
B_DH = 128
IDX_HEADS = 8
IDX_DIM = 64
TOPK_MAX = 256
TOPK_DIV = 4
FFN_CONV = 3
LANES = 128
KEY_CHUNK = 512
PLANE_ROWS = 256
QUERY_GROUPS = 4
ONES_ROWS = 16
FF_CHUNK = 256
INT_MIN = -(2 ** 31)
VMEM_LIMIT_BYTES = 56 * 1024 * 1024

NT_DIMS = (((1,), (1,)), ((), ()))


def _dot(a, b):
    return jnp.dot(a, b, preferred_element_type=F32)


def _dot_nt(a, b):
    return lax.dot_general(a, b, NT_DIMS, preferred_element_type=F32)


def _dot_exact(a, b):
    return jnp.dot(a, b, preferred_element_type=F32, precision=lax.Precision.HIGHEST)


def _rms(x, gain):
    return x * lax.rsqrt(jnp.mean(x * x, axis=-1, keepdims=True) + EPS) * gain


def _sigmoid(x):
    return 1.0 / (1.0 + jnp.exp(-x))


def _silu(x):
    return x * _sigmoid(x)


def _softplus(x):
    return jnp.maximum(x, 0.0) + jnp.log(1.0 + jnp.exp(-jnp.abs(x)))


def _row_block(n_chunks):
    g = max(d for d in range(1, 12) if n_chunks % d == 0)
    return g * CHUNK


def _params(sem):
    return pltpu.CompilerParams(dimension_semantics=sem, vmem_limit_bytes=VMEM_LIMIT_BYTES)


def _const_spec(shape):
    nd = len(shape)
    return pl.BlockSpec(shape, lambda *_: (0,) * nd)


PROJ_WIDTHS = (1536, 512, 128, 512, 256, 512, 128)


def _proj_body(h_ref, g_ref, w_ref, cosa_ref, sina_ref, cosi_ref, sinlo_ref, sinhi_ref,
               qn_ref, kn_ref, kin_ref,
               qkv_ref, z_ref, sm_ref, qr_ref, kr_ref, v_ref, iqr_ref, ike_ref, iko_ref):
    n = _rms(h_ref[0], g_ref[...]).astype(BF16)
    offs = np.cumsum((0,) + PROJ_WIDTHS)

    def mm(i):
        return _dot(n, w_ref[:, offs[i]:offs[i + 1]])

    ca, sa = cosa_ref[...], sina_ref[...]
    ci, slo, shi = cosi_ref[...], sinlo_ref[...], sinhi_ref[...]
    q = mm(3)
    iq = mm(5)
    parts = []
    for hd in range(B_HEADS):
        qh = _rms(q[:, hd * B_DH:(hd + 1) * B_DH], qn_ref[...])
        parts.append(qh * ca + pltpu.roll(qh, B_DH // 2, 1) * sa)
    qr_ref[0] = (jnp.concatenate(parts, axis=1) * (B_DH ** -0.5)).astype(BF16)

    kv = mm(4)
    ik = mm(6)
    parts = []
    for hp in range(IDX_HEADS // 2):
        x = iq[:, hp * LANES:(hp + 1) * LANES]
        parts.append(x * ci + pltpu.roll(x, LANES - IDX_DIM // 2, 1) * slo + pltpu.roll(x, IDX_DIM // 2, 1) * shi)
    iqr_ref[0] = jnp.concatenate(parts, axis=1).astype(BF16)

    qkv_ref[0] = mm(0)
    k = _rms(kv[:, :B_DH], kn_ref[...])
    kr_ref[0] = (k * ca + pltpu.roll(k, B_DH // 2, 1) * sa).astype(BF16)
    v_ref[0] = kv[:, B_DH:].astype(BF16)
    ik = _rms(ik, kin_ref[...])
    ik = ik * ci + pltpu.roll(ik, IDX_DIM // 2, 1) * (slo + shi)
    lane = lax.broadcasted_iota(I32, (1, LANES), 1)
    ike_ref[0] = jnp.where(lane < IDX_DIM, ik, 0.0).astype(BF16)
    iko_ref[0] = jnp.where(lane >= IDX_DIM, ik, 0.0).astype(BF16)
    z_ref[0] = mm(1)
    sm_ref[0] = mm(2)


def _proj_call(h, gain, w, tables, qn, kn, kin, rb):
    bsz, tp, d = h.shape
    nt = tp // rb
    row = lambda w_, dt: (jax.ShapeDtypeStruct((bsz, tp, w_), dt),
                          pl.BlockSpec((1, rb, w_), lambda b, t: (b, t, 0)))
    outs = [row(1536, F32), row(512, F32), row(128, F32), row(512, BF16), row(128, BF16),
            row(128, BF16), row(512, BF16), row(128, BF16), row(128, BF16)]
    tab_spec = pl.BlockSpec((rb, LANES), lambda b, t: (t, 0))
    return pl.pallas_call(
        _proj_body,
        grid=(bsz, nt),
        in_specs=[pl.BlockSpec((1, rb, d), lambda b, t: (b, t, 0)), _const_spec((1, d)), _const_spec(w.shape)]
                 + [tab_spec] * 5 + [_const_spec((1, LANES))] * 3,
        out_specs=[o[1] for o in outs],
        out_shape=[o[0] for o in outs],
        compiler_params=_params(("parallel", "arbitrary")),
        name="proj",
    )(h, gain, w, *tables, qn, kn, kin)


def _split3(x):
    x1 = x.astype(BF16)
    r1 = x - x1.astype(F32)
    x2 = r1.astype(BF16)
    x3 = (r1 - x2.astype(F32)).astype(BF16)
    return jnp.concatenate([x1, x2, x3], axis=0)


def _gdn_body(qkv_ref, z_ref, sm_ref, cw_ref, alog_ref, dtb_ref, ng_ref, lt3_ref, id3_ref, egb_ref,
              out_ref, xe_ref, s01_ref, s23_ref, u_ref, wq_ref, qkd_ref, kdt_ref, egl_ref, *, rb):
    t = pl.program_id(1)
    nwide = 3 * A_HEADS * A_DK

    @pl.when(t == 0)
    def _():
        xe_ref[0:8, :] = jnp.zeros((8, nwide), F32)
        s01_ref[...] = jnp.zeros_like(s01_ref)
        s23_ref[...] = jnp.zeros_like(s23_ref)

    xe_ref[8:8 + rb, :] = qkv_ref[0]
    cw = cw_ref[...]

    hw = A_HEADS * A_DK
    pw = A_HEADS * CHUNK
    ii = lax.broadcasted_iota(I32, (CHUNK, pw), 0)
    jj = lax.broadcasted_iota(I32, (CHUNK, pw), 1) & (CHUNK - 1)
    colhead = lax.broadcasted_iota(I32, (CHUNK, pw), 1) >> 6
    eye_p = ii == jj
    bd_mask = (lax.broadcasted_iota(I32, (pw, pw), 0) >> 6) == (lax.broadcasted_iota(I32, (pw, pw), 1) >> 6)
    bdk_mask = (lax.broadcasted_iota(I32, (pw, hw), 0) >> 6) == (lax.broadcasted_iota(I32, (pw, hw), 1) >> 7)
    pair_mask = (lax.broadcasted_iota(I32, (pw, pw), 0) >> 7) == (lax.broadcasted_iota(I32, (pw, pw), 1) >> 7)
    lane = lax.broadcasted_iota(I32, (1, LANES), 1)
    rows64 = lax.broadcasted_iota(I32, (CHUNK, 1), 0)
    neg_a = -jnp.exp(alog_ref[...])
    dtb = dtb_ref[...]
    ng = ng_ref[...]

    def block_diag(xp):
        return jnp.where(bd_mask, jnp.concatenate([xp.astype(BF16)] * A_HEADS, axis=0), 0)

    def stack_heads(xp):
        return jnp.concatenate([jnp.where(colhead == hd, xp, 0.0) for hd in range(A_HEADS)], axis=0)

    def l2n(x):
        parts = []
        for hd in range(A_HEADS):
            xh = x[:, hd * A_DK:(hd + 1) * A_DK]
            parts.append(xh * lax.rsqrt(jnp.sum(xh * xh, axis=-1, keepdims=True) + EPS))
        return jnp.concatenate(parts, axis=1)

    zeros = jnp.zeros((CHUNK, 2 * A_DK), F32)

    def pair_lanes(x, a_, b_):
        return jnp.concatenate([x[a_ * CHUNK:(a_ + 1) * CHUNK], x[b_ * CHUNK:(b_ + 1) * CHUNK]], axis=1)

    def local_load(c):
        r0 = c * CHUNK if isinstance(c, int) else pl.multiple_of(c * CHUNK, CHUNK)
        return xe_ref[pl.ds(r0, CHUNK + 8), :], sm_ref[0, pl.ds(r0, CHUNK), :], r0

    def local_compute(xw, sm, r0):
        acc = pltpu.roll(xw, 3, 0)[8:] * cw[0:1]
        acc = acc + pltpu.roll(xw, 2, 0)[8:] * cw[1:2]
        acc = acc + pltpu.roll(xw, 1, 0)[8:] * cw[2:3]
        acc = acc + xw[8:] * cw[3:4]
        xq = _silu(acc)
        q = l2n(xq[:, :hw]) * (A_DK ** -0.5)
        k = l2n(xq[:, hw:2 * hw])
        v = xq[:, 2 * hw:]

        g = jnp.where(lane < A_HEADS, neg_a * _softplus(sm + dtb), 0.0)
        g = jnp.where(t * rb + r0 + rows64 >= PAD, g, 0.0)
        x3 = _split3(jnp.where(lane < A_HEADS, g, _sigmoid(sm)))
        copies = _dot(x3, egb_ref[...]).astype(BF16)
        yield
        gx = _dot(lt3_ref[...], copies[:, :pw + hw])
        bx = _dot(id3_ref[...], copies[:, pw + hw:])
        yield
        gp, gq = gx[:, :pw], gx[:, pw:]
        bp, bq = bx[:, :pw], bx[:, pw:]
        grow = jnp.sum(jnp.where(eye_p, gp, 0.0), axis=0, keepdims=True)
        decay = jnp.where(ii >= jj, jnp.exp(gp - grow), 0.0)

        kb = k.astype(BF16)
        bdk = jnp.where(bdk_mask, jnp.concatenate([kb] * A_HEADS, axis=0), 0)
        kq = _dot_nt(jnp.concatenate([kb, q.astype(BF16)], axis=0), bdk)
        yield
        kk_p, qk_p = kq[:CHUNK], kq[CHUNK:]

        a = -jnp.where(ii > jj, bp * kk_p * decay, 0.0)
        tinv = jnp.where(eye_p, 1.0, 0.0) + a
        pk = _dot(a.astype(BF16), block_diag(a))
        yield
        for _ in range(4):
            pt = _dot(jnp.concatenate([pk, tinv], axis=0).astype(BF16), block_diag(pk))
            yield
            tinv = tinv + pt[CHUNK:]
            pk = pt[:CHUNK]
        tinv = tinv + _dot(tinv.astype(BF16), block_diag(pk))
        yield

        eg = jnp.exp(gq)
        vb = v * bq
        kbg = k * bq * eg
        rv = jnp.concatenate(
            [jnp.concatenate([vb[:, hd * A_DK:(hd + 1) * A_DK], kbg[:, hd * A_DK:(hd + 1) * A_DK]], axis=1)
             for hd in range(A_HEADS)], axis=0)
        uw = _dot(stack_heads(tinv).astype(BF16), rv.astype(BF16))
        yield
        u, w = uw[:, :A_DK], uw[:, A_DK:]

        qg = q * eg
        glast = gq[CHUNK - 1:CHUNK, :]
        kd = k * jnp.exp(glast - gq)
        return (u,
                jnp.concatenate([pair_lanes(w, 0, 1), qg[:, :2 * A_DK]], axis=0).astype(BF16),
                jnp.concatenate([pair_lanes(w, 2, 3), qg[:, 2 * A_DK:]], axis=0).astype(BF16),
                stack_heads(qk_p * decay).astype(BF16),
                jnp.concatenate([kd[:, :2 * A_DK], zeros], axis=0).T.astype(BF16),
                jnp.concatenate([kd[:, 2 * A_DK:], zeros], axis=0).T.astype(BF16),
                jnp.broadcast_to(jnp.exp(glast), (8, hw)))

    def local_store(c, vals):
        u_ref[c], wq_ref[c, 0], wq_ref[c, 1], qkd_ref[c], kdt_ref[c, 0], kdt_ref[c, 1], egl_ref[c] = vals

    def local_chain(c):
        vals = yield from local_compute(*local_load(c))
        local_store(c, vals)

    def state_chain(chunks):
        def unpair(r):
            return [r[:, :A_DK], r[:, A_DK:]]

        def state_update(s_old, kdt, vn_pair, egl_pair):
            upd = _dot(kdt, jnp.concatenate([vn_pair, zeros], axis=0).astype(BF16))
            return s_old * egl_pair + jnp.where(pair_mask, upd, 0.0)

        s01 = s01_ref[...]
        s23 = s23_ref[...]
        for c in chunks:
            r0 = c * CHUNK if isinstance(c, int) else pl.multiple_of(c * CHUNK, CHUNK)
            r01 = _dot(wq_ref[c, 0], s01.astype(BF16))
            r23 = _dot(wq_ref[c, 1], s23.astype(BF16))
            yield
            ws = jnp.concatenate(unpair(r01[:CHUNK]) + unpair(r23[:CHUNK]), axis=0)
            qs = jnp.concatenate(unpair(r01[CHUNK:]) + unpair(r23[CHUNK:]), axis=0)
            vn = u_ref[c] - ws
            o_rs = qs + _dot(qkd_ref[c], vn.astype(BF16))
            egl = egl_ref[c][0:1, :]
            s01 = state_update(s01, kdt_ref[c, 0], pair_lanes(vn, 0, 1), egl[:, :2 * A_DK])
            s23 = state_update(s23, kdt_ref[c, 1], pair_lanes(vn, 2, 3), egl[:, 2 * A_DK:])
            yield
            zz = z_ref[0, pl.ds(r0, CHUNK), :]
            parts = [_rms(o_rs[hd * CHUNK:(hd + 1) * CHUNK], ng) for hd in range(A_HEADS)]
            out_ref[0, pl.ds(r0, CHUNK), :] = jnp.concatenate(parts, axis=1) * _silu(zz)
        s01_ref[...] = s01
        s23_ref[...] = s23

    def run_lockstep(chains):
        live = list(chains)
        while live:
            for gen in list(live):
                try:
                    next(gen)
                except StopIteration:
                    live.remove(gen)

    nc = rb // CHUNK
    npairs = nc // 2
    if npairs == 0:
        run_lockstep([local_chain(0)])
        run_lockstep([state_chain([0])])
    else:
        run_lockstep([local_chain(0), local_chain(1)])

        def steady(i, carry):
            run_lockstep([local_chain(2 * i), local_chain(2 * i + 1), state_chain([2 * i - 2, 2 * i - 1])])
            return carry

        lax.fori_loop(1, npairs, steady, 0)
        last = [2 * npairs - 2, 2 * npairs - 1]
        if nc % 2:
            run_lockstep([local_chain(nc - 1), state_chain(last)])
            run_lockstep([state_chain([nc - 1])])
        else:
            run_lockstep([state_chain(last)])
    xe_ref[0:8, :] = xe_ref[rb:rb + 8, :]


def _gdn_call(qkv, z, sm, cw, alog, dtb, ng, lt3, id3, egb, rb):
    bsz, tp, _ = qkv.shape
    nt = tp // rb
    nc = rb // CHUNK
    hw = A_HEADS * A_DK
    pw = A_HEADS * CHUNK
    row = lambda w_: pl.BlockSpec((1, rb, w_), lambda b, t: (b, t, 0))
    return pl.pallas_call(
        functools.partial(_gdn_body, rb=rb),
        grid=(bsz, nt),
        in_specs=[row(3 * hw), row(hw), row(LANES), _const_spec(cw.shape), _const_spec((1, LANES)),
                  _const_spec((1, LANES)), _const_spec((1, LANES)), _const_spec(lt3.shape),
                  _const_spec(id3.shape), _const_spec(egb.shape)],
        out_specs=row(hw),
        out_shape=jax.ShapeDtypeStruct((bsz, tp, hw), F32),
        scratch_shapes=[pltpu.VMEM((rb + 8, 3 * hw), F32),
                        pltpu.VMEM((2 * A_DK, 2 * A_DK), F32), pltpu.VMEM((2 * A_DK, 2 * A_DK), F32),
                        pltpu.VMEM((nc, pw, A_DK), F32), pltpu.VMEM((nc, 2, 2 * CHUNK, 2 * A_DK), BF16),
                        pltpu.VMEM((nc, pw, pw), BF16), pltpu.VMEM((nc, 2, 2 * A_DK, 2 * CHUNK), BF16),
                        pltpu.VMEM((nc, 8, hw), F32)],
        compiler_params=_params(("parallel", "arbitrary")),
        name="gdn",
    )(qkv, z, sm, cw, alog, dtb, ng, lt3, id3, egb)


def _dsa_body(qr_ref, kr_ref, v_ref, iqr_ref, ike_ref, iko_ref, sm_ref, lstrict_ref, out_ref,
              vt_ref, vmt_ref, skey_ref, planes_ref, st_ref, ot_ref, *, seq, topk):
    g = pl.program_id(1)
    iw_scale = IDX_HEADS ** -0.5 * IDX_DIM ** -0.5
    qpair = 2 * CHUNK
    neg_inf = -jnp.inf

    @pl.when(g == 0)
    def _():
        planes_ref[...] = jnp.zeros_like(planes_ref)

        def vblk(i, carry):
            r = pl.multiple_of(CHUNK + i * LANES, CHUNK)
            vt_ref[i] = v_ref[0, pl.ds(r, LANES), :].astype(F32).T.astype(BF16)
            return carry

        lax.fori_loop(0, seq // LANES, vblk, 0)
        vmt_ref[...] = v_ref[0, 0:LANES, :].astype(F32).T.astype(BF16)

        qm = qr_ref[0, 0:CHUNK, :]
        km = kr_ref[0, 0:CHUNK, :]
        vm = v_ref[0, 0:CHUNK, :]
        colv = lax.broadcasted_iota(I32, (1, CHUNK), 1) >= PAD
        parts = []
        for hd in range(B_HEADS):
            s = _dot_nt(qm[:, hd * B_DH:(hd + 1) * B_DH], km)
            s = jnp.where(colv, s, neg_inf)
            e = jnp.exp(s - jnp.max(s, axis=-1, keepdims=True))
            pr = e / jnp.sum(e, axis=-1, keepdims=True)
            parts.append(_dot(pr.astype(BF16), vm))
        out_ref[0, 0:CHUNK, :] = jnp.concatenate(parts, axis=1)

    lane = lax.broadcasted_iota(I32, (1, LANES), 1)
    groups = []
    for j in range(QUERY_GROUPS):
        p = g * QUERY_GROUPS + j
        r0 = pl.multiple_of(CHUNK + p * qpair, CHUNK)
        qb = qr_ref[0, pl.ds(r0, qpair), :]
        ib = iqr_ref[0, pl.ds(r0, qpair), :]
        smt = sm_ref[0, pl.ds(r0, qpair), :].T
        groups.append(dict(
            j=j, r0=r0,
            q_hq=jnp.concatenate([qb[:, hd * B_DH:(hd + 1) * B_DH] for hd in range(B_HEADS)], axis=0),
            iq_hq=jnp.concatenate([ib[:, hp * LANES:(hp + 1) * LANES] for hp in range(IDX_HEADS // 2)], axis=0),
            wts=[smt[2 * A_HEADS + hd:2 * A_HEADS + hd + 1, :] * iw_scale for hd in range(IDX_HEADS)],
            limit=jnp.where(lane < CHUNK, (2 * p + 1) * CHUNK, (2 * p + 2) * CHUNK)))
    nkc = ((g + 1) * QUERY_GROUPS * qpair + KEY_CHUNK - 1) // KEY_CHUNK
    krows = lax.broadcasted_iota(I32, (KEY_CHUNK, 1), 0)

    def score_chunk(kc, carry):
        k0 = pl.multiple_of(CHUNK + kc * KEY_CHUNK, CHUNK)
        rows = pl.ds(pl.multiple_of(kc * KEY_CHUNK, KEY_CHUNK), KEY_CHUNK)
        ke, ko, kk = (ref[0, pl.ds(k0, KEY_CHUNK), :] for ref in (ike_ref, iko_ref, kr_ref))
        logits = []
        for grp_ in groups:
            logits.append((_dot_nt(ke, grp_["iq_hq"]), _dot_nt(ko, grp_["iq_hq"])))
            st_ref[grp_["j"], rows, :] = _dot_nt(kk, grp_["q_hq"])
        for grp_, (le, lo) in zip(groups, logits):
            score_tail(kc, rows, grp_, le, lo)
        return carry

    def score_tail(kc, rows, grp_, le, lo):
        j, wts, limit = grp_["j"], grp_["wts"], grp_["limit"]
        sc = jnp.zeros((KEY_CHUNK, LANES), F32)
        for hp in range(IDX_HEADS // 2):
            sc = sc + wts[2 * hp] * jnp.maximum(le[:, hp * LANES:(hp + 1) * LANES], 0.0)
            sc = sc + wts[2 * hp + 1] * jnp.maximum(lo[:, hp * LANES:(hp + 1) * LANES], 0.0)
        bits = lax.bitcast_convert_type(sc, I32)
        bits = jnp.where(sc == 0.0, 0, bits)
        key = bits ^ ((bits >> 31) & 0x7FFFFFFF)
        valid = kc * KEY_CHUNK + krows < limit
        key = jnp.where(valid, key, INT_MIN)
        skey_ref[j, rows, :] = key
        for gi in range(KEY_CHUNK // PLANE_ROWS):
            a = [key[gi * PLANE_ROWS + 8 * r:gi * PLANE_ROWS + 8 * r + 8, :] ^ INT_MIN for r in range(32)]
            sh, msk = 16, 0x0000FFFF
            while sh:
                k = 0
                while k < 32:
                    tt = (a[k] ^ lax.shift_right_logical(a[k + sh], jnp.int32(sh))) & msk
                    a[k] = a[k] ^ tt
                    a[k + sh] = a[k + sh] ^ (tt << sh)
                    k = (k + sh + 1) & ~sh
                sh >>= 1
                msk = (msk ^ (msk << sh)) & 0xFFFFFFFF
            g8 = pl.multiple_of((kc * (KEY_CHUNK // PLANE_ROWS) + gi) * 8, 8)
            for b in range(32):
                planes_ref[j, b, pl.ds(g8, 8), :] = a[b]

    lax.fori_loop(0, nkc, score_chunk, 0)

    grp = seq // 32
    alive0 = jnp.where(lax.broadcasted_iota(I32, (grp, 1), 0) < nkc * (KEY_CHUNK // 32),
                       jnp.full((grp, LANES), -1, I32), 0)

    def lane_count(words):
        c = lax.population_count(words)
        if grp > 8:
            c = jnp.sum(c.reshape(grp // 8, 8, LANES), axis=0)
        return jnp.sum(c, axis=0, keepdims=True)

    def bit_body(i, carry):
        bit = jnp.left_shift(jnp.int32(1), 31 - i)
        out = []
        for grp_, (tu, above, alive) in zip(groups, carry):
            hit = alive & planes_ref[grp_["j"], i]
            c1 = lane_count(hit)
            take = above + c1 >= topk
            out.append((jnp.where(take, tu | bit, tu), jnp.where(take, above, above + c1),
                        jnp.where(take, hit, alive ^ hit)))
        return tuple(out)

    zero_row = jnp.zeros((1, LANES), I32)
    found_keys = lax.fori_loop(0, 32, bit_body, ((zero_row, zero_row, alive0),) * QUERY_GROUPS)

    for grp_, (tu, above, alive) in zip(groups, found_keys):
        j, thr, found = grp_["j"], tu ^ INT_MIN, tu != 0
        grp_["thr"] = thr
        tie_lane = found & (above + lane_count(alive) > topk)

        @pl.when(jnp.max(jnp.where(tie_lane, 1, 0)) > 0)
        def _(j=j, thr=thr, found=found, above=above):
            need = (topk - above).astype(F32)

            def body(kc, seen):
                rows = pl.ds(pl.multiple_of(kc * KEY_CHUNK, KEY_CHUNK), KEY_CHUNK)
                blk = skey_ref[j, rows, :]
                tied = (blk == thr) & found
                eq = jnp.where(tied, 1.0, 0.0)
                rank = _dot(lstrict_ref[...], eq.astype(BF16)) + seen
                skey_ref[j, rows, :] = jnp.where(tied & (rank >= need), INT_MIN, blk)
                return seen + jnp.sum(eq, axis=0, keepdims=True)

            lax.fori_loop(0, nkc, body, jnp.zeros((1, LANES), F32))

    mrows = lax.broadcasted_iota(I32, (LANES, 1), 0)
    k_meta = kr_ref[0, 0:LANES, :]
    ones = jnp.ones((ONES_ROWS, LANES), BF16)
    v_meta = jnp.concatenate([vmt_ref[...], ones], axis=0)
    init = []
    for grp_ in groups:
        s_meta = _dot_nt(k_meta, grp_["q_hq"])
        s_meta = jnp.where((mrows >= PAD) & (mrows < CHUNK), s_meta, neg_inf)
        m0 = jnp.max(s_meta, axis=0, keepdims=True)
        ot_ref[grp_["j"]] = _dot(v_meta, jnp.exp(s_meta - m0).astype(BF16))
        init.append(m0)
        grp_["thr"] = jnp.maximum(grp_["thr"], INT_MIN + 1)

    def pv_chunk(kc, carry):
        rows = pl.ds(pl.multiple_of(kc * KEY_CHUNK, KEY_CHUNK), KEY_CHUNK)
        nblk = KEY_CHUNK // LANES
        vblk = jnp.concatenate([jnp.concatenate([vt_ref[nblk * kc + i], ones], axis=0) for i in range(nblk)], axis=1)
        out, probs = [], []
        for grp_, m in zip(groups, carry):
            bias = jnp.where(skey_ref[grp_["j"], rows, :] >= grp_["thr"], 0.0, neg_inf)
            s = st_ref[grp_["j"], rows, :] + jnp.concatenate([bias] * B_HEADS, axis=1)
            m_new = jnp.maximum(m, jnp.max(jnp.max(s.reshape(KEY_CHUNK // 8, 8, B_HEADS * LANES), axis=0),
                                           axis=0, keepdims=True))
            out.append(m_new)
            probs.append((jnp.exp(m - m_new), jnp.exp(s - m_new).astype(BF16)))
        for grp_, (alpha, prb) in zip(groups, probs):
            ot_ref[grp_["j"]] = ot_ref[grp_["j"]] * alpha + _dot(vblk, prb)
        return tuple(out)

    lax.fori_loop(0, nkc, pv_chunk, tuple(init))
    for grp_ in groups:
        acc = ot_ref[grp_["j"]]
        o_hq = (acc[:B_DH] / acc[B_DH:B_DH + 1]).T
        for hd in range(B_HEADS):
            out_ref[0, pl.ds(grp_["r0"], qpair), hd * B_DH:(hd + 1) * B_DH] = o_hq[hd * LANES:(hd + 1) * LANES, :]


def _dsa_call(qr, kr, v, iqr, ike, iko, sm, lstrict, seq, topk):
    bsz, tp, _ = qr.shape
    nsteps, rem = divmod(seq, 2 * CHUNK * QUERY_GROUPS)
    assert rem == 0
    full = lambda w_: pl.BlockSpec((1, tp, w_), lambda b, p: (b, 0, 0))
    hw = B_HEADS * B_DH
    nq = QUERY_GROUPS
    return pl.pallas_call(
        functools.partial(_dsa_body, seq=seq, topk=topk),
        grid=(bsz, nsteps),
        in_specs=[full(hw), full(B_DH), full(B_DH), full(hw), full(LANES), full(LANES), full(LANES),
                  _const_spec(lstrict.shape)],
        out_specs=full(hw),
        out_shape=jax.ShapeDtypeStruct((bsz, tp, hw), F32),
        scratch_shapes=[pltpu.VMEM((seq // LANES, B_DH, LANES), BF16), pltpu.VMEM((B_DH, LANES), BF16),
                        pltpu.VMEM((nq, seq, LANES), I32), pltpu.VMEM((nq, 32, seq // 32, LANES), I32),
                        pltpu.VMEM((nq, seq, hw), F32), pltpu.VMEM((nq, B_DH + ONES_ROWS, hw), F32)],
        compiler_params=_params(("parallel", "arbitrary")),
        name="dsa",
    )(qr, kr, v, iqr, ike, iko, sm, lstrict)


def _mix_body(h_ref, oa_ref, ob_ref, g_ref, wg_ref, bg_ref, wa_ref, wb_ref, wo_ref, out_ref, *, rb):
    x = h_ref[0]
    d = x.shape[-1]
    n = _rms(x, g_ref[...]).astype(BF16)
    gates = _sigmoid(_dot(n, wg_ref[...]) + bg_ref[...])
    y = gates[:, :d] * _dot(oa_ref[0].astype(BF16), wa_ref[...]) + gates[:, d:] * _dot(ob_ref[0].astype(BF16), wb_ref[...])
    out = x + _dot(y.astype(BF16), wo_ref[...])
    rows = pl.program_id(1) * rb + lax.broadcasted_iota(I32, (rb, 1), 0)
    out_ref[0] = jnp.where(rows >= PAD, out, 0.0)


def _mix_call(h, oa, ob, gain, wg, bg, wa, wb, wo, rb):
    bsz, tp, d = h.shape
    row = lambda w_: pl.BlockSpec((1, rb, w_), lambda b, t: (b, t, 0))
    return pl.pallas_call(
        functools.partial(_mix_body, rb=rb),
        grid=(bsz, tp // rb),
        in_specs=[row(d), row(oa.shape[-1]), row(ob.shape[-1]), _const_spec((1, d)), _const_spec(wg.shape),
                  _const_spec(bg.shape), _const_spec(wa.shape), _const_spec(wb.shape), _const_spec(wo.shape)],
        out_specs=row(d),
        out_shape=jax.ShapeDtypeStruct((bsz, tp, d), F32),
        compiler_params=_params(("parallel", "arbitrary")),
        name="mix",
    )(h, oa, ob, gain, wg, bg, wa, wb, wo)


def _ffn_body(h_ref, g_ref, wup_ref, cw_ref, wd_ref, out_ref, carry_ref, act_ref, *, rb, dff):
    t = pl.program_id(1)

    @pl.when(t == 0)
    def _():
        carry_ref[...] = jnp.zeros_like(carry_ref)

    x = h_ref[0]
    n = _rms(x, g_ref[...]).astype(BF16)
    rows = lax.broadcasted_iota(I32, (rb, 1), 0)
    nch = dff // FF_CHUNK

    def gate_up(c):
        return (_dot(n, wup_ref[:, c * FF_CHUNK:(c + 1) * FF_CHUNK]),
                _dot(n, wup_ref[:, dff + c * FF_CHUNK:dff + (c + 1) * FF_CHUNK]))

    ahead = gate_up(0)
    for c in range(nch):
        cols = slice(c * FF_CHUNK, (c + 1) * FF_CHUNK)
        gate, up = ahead
        if c + 1 < nch:
            ahead = gate_up(c + 1)
        prev = carry_ref[:, cols]
        g1 = jnp.where(rows == 0, prev[7:8], pltpu.roll(gate, 1, 0))
        g2 = jnp.where(rows == 0, prev[6:7], jnp.where(rows == 1, prev[7:8], pltpu.roll(gate, 2, 0)))
        cw = cw_ref[:, cols]
        conv = g2 * cw[0:1] + g1 * cw[1:2] + gate * cw[2:3]
        carry_ref[:, cols] = gate[rb - 8:rb]
        act_ref[:, cols] = (_silu(conv) * up).astype(BF16)
    out = x + _dot(act_ref[...], wd_ref[...])
    out_ref[0] = jnp.where(t * rb + rows >= PAD, out, 0.0)


def _ffn_call(h, gain, wup, cw, wd, rb):
    bsz, tp, d = h.shape
    dff = wd.shape[0]
    row = pl.BlockSpec((1, rb, d), lambda b, t: (b, t, 0))
    return pl.pallas_call(
        functools.partial(_ffn_body, rb=rb, dff=dff),
        grid=(bsz, tp // rb),
        in_specs=[row, _const_spec((1, d)), _const_spec(wup.shape), _const_spec(cw.shape), _const_spec(wd.shape)],
        out_specs=row,
        out_shape=jax.ShapeDtypeStruct((bsz, tp, d), F32),
        scratch_shapes=[pltpu.VMEM((8, dff), F32), pltpu.VMEM((rb, dff), BF16)],
        compiler_params=_params(("parallel", "arbitrary")),
        name="ffn",
    )(h, gain, wup, cw, wd)


def _rope_tables(tp):
    pos = (jnp.arange(tp, dtype=F32) - PAD)[:, None]

    def cs(dim):
        inv = 1.0 / (ROPE_THETA ** (jnp.arange(0, dim, 2, dtype=F32) / dim))
        ang = pos * inv[None, :]
        return jnp.cos(ang), jnp.sin(ang)

    ca, sa = cs(B_DH)
    ci, si = cs(IDX_DIM)
    zi = jnp.zeros_like(si)
    return (jnp.concatenate([ca, ca], axis=1), jnp.concatenate([-sa, sa], axis=1),
            jnp.concatenate([ci] * 4, axis=1), jnp.concatenate([-si, zi, -si, zi], axis=1),
            jnp.concatenate([zi, si, zi, si], axis=1))


def _pad_lanes(vec):
    return jnp.zeros((1, LANES), F32).at[0, :vec.shape[0]].set(vec.astype(F32))


def _head_expander():
    pw, hw = A_HEADS * CHUNK, A_HEADS * A_DK
    head = np.concatenate([np.arange(pw) // CHUNK, np.arange(hw) // A_DK])
    src = np.arange(LANES)[:, None]
    return jnp.asarray(np.concatenate([src == head[None, :], src == A_HEADS + head[None, :]], axis=1), BF16)


def kernel(x, meta_tokens, norm_mix, w_in, conv_a, a_log, dt_bias, a_out_norm, q_norm, k_norm, kidx_norm,
           w_branch_a, w_branch_b, w_gate, b_gate, w_out, norm_ffn, w_up, conv_ffn, w_down):
    bsz, seq, d = x.shape
    depth = w_in.shape[0]
    assert seq % KEY_CHUNK == 0
    tp = CHUNK + seq
    rb = _row_block(tp // CHUNK)
    topk = min(TOPK_MAX, seq // TOPK_DIV)

    meta = jnp.broadcast_to(meta_tokens.astype(x.dtype)[None], (bsz, N_META, d))
    h = jnp.concatenate([jnp.zeros((bsz, PAD, d), x.dtype), meta, x], axis=1)

    tables = _rope_tables(tp)
    egb = _head_expander()
    tri = np.arange(CHUNK)
    lt3 = jnp.asarray(np.tile(tri[:, None] >= tri[None, :], (1, 3)), BF16)
    id3 = jnp.asarray(np.tile(np.eye(CHUNK), (1, 3)), BF16)
    trk = np.arange(KEY_CHUNK)
    lstrict = jnp.asarray(trk[:, None] > trk[None, :], BF16)

    hw = A_HEADS * A_DK
    o = np.cumsum((0, hw, hw, hw, hw, A_HEADS, A_HEADS, B_HEADS * B_DH, B_DH, B_DH,
                   IDX_HEADS * IDX_DIM, IDX_DIM, IDX_HEADS))
    for l in range(depth):
        w = w_in[l]
        small = jnp.concatenate([w[:, o[4]:o[6]], w[:, o[11]:o[12]],
                                 jnp.zeros((d, LANES - 2 * A_HEADS - IDX_HEADS), w.dtype)], axis=1)
        w_p = jnp.concatenate([w[:, o[0]:o[3]], w[:, o[3]:o[4]], small, w[:, o[6]:o[7]], w[:, o[7]:o[9]],
                               w[:, o[9]:o[10]], w[:, o[10]:o[11]], w[:, o[10]:o[11]]], axis=1).astype(BF16)
        qkv, z, sm, qr, kr, v, iqr, ike, iko = _proj_call(
            h, norm_mix[l][None], w_p, tables, q_norm[l][None], k_norm[l][None],
            jnp.concatenate([kidx_norm[l], kidx_norm[l]])[None], rb)
        o_a = _gdn_call(qkv, z, sm, conv_a[l], _pad_lanes(a_log[l]), _pad_lanes(dt_bias[l]),
                        a_out_norm[l][None], lt3, id3, egb, rb)
        o_b = _dsa_call(qr, kr, v, iqr, ike, iko, sm, lstrict, seq, topk)
        h = _mix_call(h, o_a, o_b, norm_mix[l][None], w_gate[l].astype(BF16), b_gate[l][None],
                      w_branch_a[l].astype(BF16), w_branch_b[l].astype(BF16), w_out[l].astype(BF16), rb)
        h = _ffn_call(h, norm_ffn[l][None], w_up[l].astype(BF16), conv_ffn[l], w_down[l].astype(BF16), rb)
    return h[:, CHUNK:]
```

```python
import functools
import math

import jax
import jax.numpy as jnp
import numpy as np
from jax import lax
from jax.experimental import pallas as pl
from jax.experimental.pallas import tpu as pltpu

F32 = jnp.float32
BF16 = jnp.bfloat16
I32 = jnp.int32

CHUNK = 64
N_META = 16
PAD = CHUNK - N_META
ROPE_THETA = 10000.0
EPS = 1e-6
A_HEADS = 4
A_DK = 128
A_CONV = 4
B_HEADS = 4
B_DH = 128
IDX_HEADS = 8
IDX_DIM = 64
TOPK_MAX = 256
TOPK_DIV = 4
FFN_CONV = 3
LANES = 128
KEY_CHUNK = 512
PLANE_ROWS = 256
QUERY_GROUPS = 4
ONES_ROWS = 16
FF_CHUNK = 256
INT_MIN = -(2 ** 31)
VMEM_LIMIT_BYTES = 56 * 1024 * 1024

NT_DIMS = (((1,), (1,)), ((), ()))


def _dot(a, b):
    return jnp.dot(a, b, preferred_element_type=F32)


def _dot_nt(a, b):
    return lax.dot_general(a, b, NT_DIMS, preferred_element_type=F32)


def _dot_exact(a, b):
    return jnp.dot(a, b, preferred_element_type=F32, precision=lax.Precision.HIGHEST)


def _rms(x, gain):
    return x * lax.rsqrt(jnp.mean(x * x, axis=-1, keepdims=True) + EPS) * gain


def _sigmoid(x):
    return 1.0 / (1.0 + jnp.exp(-x))


def _silu(x):
    return x * _sigmoid(x)


def _softplus(x):
    return jnp.maximum(x, 0.0) + jnp.log(1.0 + jnp.exp(-jnp.abs(x)))


def _row_block(n_chunks):
    g = max(d for d in range(1, 12) if n_chunks % d == 0)
    return g * CHUNK


def _params(sem):
    return pltpu.CompilerParams(dimension_semantics=sem, vmem_limit_bytes=VMEM_LIMIT_BYTES)


def _const_spec(shape):
    nd = len(shape)
    return pl.BlockSpec(shape, lambda *_: (0,) * nd)


PROJ_WIDTHS = (1536, 512, 128, 512, 256, 512, 128)


def _proj_body(h_ref, g_ref, w_ref, cosa_ref, sina_ref, cosi_ref, sinlo_ref, sinhi_ref,
               qn_ref, kn_ref, kin_ref,
               qkv_ref, z_ref, sm_ref, qr_ref, kr_ref, v_ref, iqr_ref, ike_ref, iko_ref):
    n = _rms(h_ref[0], g_ref[...]).astype(BF16)
    offs = np.cumsum((0,) + PROJ_WIDTHS)

    def mm(i):
        return _dot(n, w_ref[:, offs[i]:offs[i + 1]])

    ca, sa = cosa_ref[...], sina_ref[...]
    ci, slo, shi = cosi_ref[...], sinlo_ref[...], sinhi_ref[...]
    q = mm(3)
    iq = mm(5)
    parts = []
    for hd in range(B_HEADS):
        qh = _rms(q[:, hd * B_DH:(hd + 1) * B_DH], qn_ref[...])
        parts.append(qh * ca + pltpu.roll(qh, B_DH // 2, 1) * sa)
    qr_ref[0] = (jnp.concatenate(parts, axis=1) * (B_DH ** -0.5)).astype(BF16)

    kv = mm(4)
    ik = mm(6)
    parts = []
    for hp in range(IDX_HEADS // 2):
        x = iq[:, hp * LANES:(hp + 1) * LANES]
        parts.append(x * ci + pltpu.roll(x, LANES - IDX_DIM // 2, 1) * slo + pltpu.roll(x, IDX_DIM // 2, 1) * shi)
    iqr_ref[0] = jnp.concatenate(parts, axis=1).astype(BF16)

    qkv_ref[0] = mm(0)
    k = _rms(kv[:, :B_DH], kn_ref[...])
    kr_ref[0] = (k * ca + pltpu.roll(k, B_DH // 2, 1) * sa).astype(BF16)
    v_ref[0] = kv[:, B_DH:].astype(BF16)
    ik = _rms(ik, kin_ref[...])
    ik = ik * ci + pltpu.roll(ik, IDX_DIM // 2, 1) * (slo + shi)
    lane = lax.broadcasted_iota(I32, (1, LANES), 1)
    ike_ref[0] = jnp.where(lane < IDX_DIM, ik, 0.0).astype(BF16)
    iko_ref[0] = jnp.where(lane >= IDX_DIM, ik, 0.0).astype(BF16)
    z_ref[0] = mm(1)
    sm_ref[0] = mm(2)


def _proj_call(h, gain, w, tables, qn, kn, kin, rb):
    bsz, tp, d = h.shape
    nt = tp // rb
    row = lambda w_, dt: (jax.ShapeDtypeStruct((bsz, tp, w_), dt),
                          pl.BlockSpec((1, rb, w_), lambda b, t: (b, t, 0)))
    outs = [row(1536, F32), row(512, F32), row(128, F32), row(512, BF16), row(128, BF16),
            row(128, BF16), row(512, BF16), row(128, BF16), row(128, BF16)]
    tab_spec = pl.BlockSpec((rb, LANES), lambda b, t: (t, 0))
    return pl.pallas_call(
        _proj_body,
        grid=(bsz, nt),
        in_specs=[pl.BlockSpec((1, rb, d), lambda b, t: (b, t, 0)), _const_spec((1, d)), _const_spec(w.shape)]
                 + [tab_spec] * 5 + [_const_spec((1, LANES))] * 3,
        out_specs=[o[1] for o in outs],
        out_shape=[o[0] for o in outs],
        compiler_params=_params(("parallel", "arbitrary")),
        name="proj",
    )(h, gain, w, *tables, qn, kn, kin)


def _split3(x):
    x1 = x.astype(BF16)
    r1 = x - x1.astype(F32)
    x2 = r1.astype(BF16)
    x3 = (r1 - x2.astype(F32)).astype(BF16)
    return jnp.concatenate([x1, x2, x3], axis=0)


def _gdn_body(qkv_ref, z_ref, sm_ref, cw_ref, alog_ref, dtb_ref, ng_ref, lt3_ref, id3_ref, egb_ref,
              out_ref, xe_ref, s01_ref, s23_ref, u_ref, wq_ref, qkd_ref, kdt_ref, egl_ref, *, rb):
    t = pl.program_id(1)
    nwide = 3 * A_HEADS * A_DK

    @pl.when(t == 0)
    def _():
        xe_ref[0:8, :] = jnp.zeros((8, nwide), F32)
        s01_ref[...] = jnp.zeros_like(s01_ref)
        s23_ref[...] = jnp.zeros_like(s23_ref)

    xe_ref[8:8 + rb, :] = qkv_ref[0]
    cw = cw_ref[...]

    hw = A_HEADS * A_DK
    pw = A_HEADS * CHUNK
    ii = lax.broadcasted_iota(I32, (CHUNK, pw), 0)
    jj = lax.broadcasted_iota(I32, (CHUNK, pw), 1) & (CHUNK - 1)
    colhead = lax.broadcasted_iota(I32, (CHUNK, pw), 1) >> 6
    eye_p = ii == jj
    bd_mask = (lax.broadcasted_iota(I32, (pw, pw), 0) >> 6) == (lax.broadcasted_iota(I32, (pw, pw), 1) >> 6)
    bdk_mask = (lax.broadcasted_iota(I32, (pw, hw), 0) >> 6) == (lax.broadcasted_iota(I32, (pw, hw), 1) >> 7)
    pair_mask = (lax.broadcasted_iota(I32, (pw, pw), 0) >> 7) == (lax.broadcasted_iota(I32, (pw, pw), 1) >> 7)
    lane = lax.broadcasted_iota(I32, (1, LANES), 1)
    rows64 = lax.broadcasted_iota(I32, (CHUNK, 1), 0)
    neg_a = -jnp.exp(alog_ref[...])
    dtb = dtb_ref[...]
    ng = ng_ref[...]

    def block_diag(xp):
        return jnp.where(bd_mask, jnp.concatenate([xp.astype(BF16)] * A_HEADS, axis=0), 0)

    def stack_heads(xp):
        return jnp.concatenate([jnp.where(colhead == hd, xp, 0.0) for hd in range(A_HEADS)], axis=0)

    def l2n(x):
        parts = []
        for hd in range(A_HEADS):
            xh = x[:, hd * A_DK:(hd + 1) * A_DK]
            parts.append(xh * lax.rsqrt(jnp.sum(xh * xh, axis=-1, keepdims=True) + EPS))
        return jnp.concatenate(parts, axis=1)

    zeros = jnp.zeros((CHUNK, 2 * A_DK), F32)

    def pair_lanes(x, a_, b_):
        return jnp.concatenate([x[a_ * CHUNK:(a_ + 1) * CHUNK], x[b_ * CHUNK:(b_ + 1) * CHUNK]], axis=1)

    def local_load(c):
        r0 = c * CHUNK if isinstance(c, int) else pl.multiple_of(c * CHUNK, CHUNK)
        return xe_ref[pl.ds(r0, CHUNK + 8), :], sm_ref[0, pl.ds(r0, CHUNK), :], r0

    def local_compute(xw, sm, r0):
        acc = pltpu.roll(xw, 3, 0)[8:] * cw[0:1]
        acc = acc + pltpu.roll(xw, 2, 0)[8:] * cw[1:2]
        acc = acc + pltpu.roll(xw, 1, 0)[8:] * cw[2:3]
        acc = acc + xw[8:] * cw[3:4]
        xq = _silu(acc)
        q = l2n(xq[:, :hw]) * (A_DK ** -0.5)
        k = l2n(xq[:, hw:2 * hw])
        v = xq[:, 2 * hw:]

        g = jnp.where(lane < A_HEADS, neg_a * _softplus(sm + dtb), 0.0)
        g = jnp.where(t * rb + r0 + rows64 >= PAD, g, 0.0)
        x3 = _split3(jnp.where(lane < A_HEADS, g, _sigmoid(sm)))
        copies = _dot(x3, egb_ref[...]).astype(BF16)
        yield
        gx = _dot(lt3_ref[...], copies[:, :pw + hw])
        bx = _dot(id3_ref[...], copies[:, pw + hw:])
        yield
        gp, gq = gx[:, :pw], gx[:, pw:]
        bp, bq = bx[:, :pw], bx[:, pw:]
        grow = jnp.sum(jnp.where(eye_p, gp, 0.0), axis=0, keepdims=True)
        decay = jnp.where(ii >= jj, jnp.exp(gp - grow), 0.0)

        kb = k.astype(BF16)
        bdk = jnp.where(bdk_mask, jnp.concatenate([kb] * A_HEADS, axis=0), 0)
        kq = _dot_nt(jnp.concatenate([kb, q.astype(BF16)], axis=0), bdk)
        yield
        kk_p, qk_p = kq[:CHUNK], kq[CHUNK:]

        a = -jnp.where(ii > jj, bp * kk_p * decay, 0.0)
        tinv = jnp.where(eye_p, 1.0, 0.0) + a
        pk = _dot(a.astype(BF16), block_diag(a))
        yield
        for _ in range(4):
            pt = _dot(jnp.concatenate([pk, tinv], axis=0).astype(BF16), block_diag(pk))
            yield
            tinv = tinv + pt[CHUNK:]
            pk = pt[:CHUNK]
        tinv = tinv + _dot(tinv.astype(BF16), block_diag(pk))
        yield

        eg = jnp.exp(gq)
        vb = v * bq
        kbg = k * bq * eg
        rv = jnp.concatenate(
            [jnp.concatenate([vb[:, hd * A_DK:(hd + 1) * A_DK], kbg[:, hd * A_DK:(hd + 1) * A_DK]], axis=1)
             for hd in range(A_HEADS)], axis=0)
        uw = _dot(stack_heads(tinv).astype(BF16), rv.astype(BF16))
        yield
        u, w = uw[:, :A_DK], uw[:, A_DK:]

        qg = q * eg
        glast = gq[CHUNK - 1:CHUNK, :]
        kd = k * jnp.exp(glast - gq)
        return (u,
                jnp.concatenate([pair_lanes(w, 0, 1), qg[:, :2 * A_DK]], axis=0).astype(BF16),
                jnp.concatenate([pair_lanes(w, 2, 3), qg[:, 2 * A_DK:]], axis=0).astype(BF16),
                stack_heads(qk_p * decay).astype(BF16),
                jnp.concatenate([kd[:, :2 * A_DK], zeros], axis=0).T.astype(BF16),
                jnp.concatenate([kd[:, 2 * A_DK:], zeros], axis=0).T.astype(BF16),
                jnp.broadcast_to(jnp.exp(glast), (8, hw)))

    def local_store(c, vals):
        u_ref[c], wq_ref[c, 0], wq_ref[c, 1], qkd_ref[c], kdt_ref[c, 0], kdt_ref[c, 1], egl_ref[c] = vals

    def local_chain(c):
        vals = yield from local_compute(*local_load(c))
        local_store(c, vals)

    def state_chain(chunks):
        def unpair(r):
            return [r[:, :A_DK], r[:, A_DK:]]

        def state_update(s_old, kdt, vn_pair, egl_pair):
            upd = _dot(kdt, jnp.concatenate([vn_pair, zeros], axis=0).astype(BF16))
            return s_old * egl_pair + jnp.where(pair_mask, upd, 0.0)

        s01 = s01_ref[...]
        s23 = s23_ref[...]
        for c in chunks:
            r0 = c * CHUNK if isinstance(c, int) else pl.multiple_of(c * CHUNK, CHUNK)
            r01 = _dot(wq_ref[c, 0], s01.astype(BF16))
            r23 = _dot(wq_ref[c, 1], s23.astype(BF16))
            yield
            ws = jnp.concatenate(unpair(r01[:CHUNK]) + unpair(r23[:CHUNK]), axis=0)
            qs = jnp.concatenate(unpair(r01[CHUNK:]) + unpair(r23[CHUNK:]), axis=0)
            vn = u_ref[c] - ws
            o_rs = qs + _dot(qkd_ref[c], vn.astype(BF16))
            egl = egl_ref[c][0:1, :]
            s01 = state_update(s01, kdt_ref[c, 0], pair_lanes(vn, 0, 1), egl[:, :2 * A_DK])
            s23 = state_update(s23, kdt_ref[c, 1], pair_lanes(vn, 2, 3), egl[:, 2 * A_DK:])
            yield
            zz = z_ref[0, pl.ds(r0, CHUNK), :]
            parts = [_rms(o_rs[hd * CHUNK:(hd + 1) * CHUNK], ng) for hd in range(A_HEADS)]
            out_ref[0, pl.ds(r0, CHUNK), :] = jnp.concatenate(parts, axis=1) * _silu(zz)
        s01_ref[...] = s01
        s23_ref[...] = s23

    def run_lockstep(chains):
        live = list(chains)
        while live:
            for gen in list(live):
                try:
                    next(gen)
                except StopIteration:
                    live.remove(gen)

    nc = rb // CHUNK
    npairs = nc // 2
    if npairs == 0:
        run_lockstep([local_chain(0)])
        run_lockstep([state_chain([0])])
    else:
        run_lockstep([local_chain(0), local_chain(1)])

        def steady(i, carry):
            run_lockstep([local_chain(2 * i), local_chain(2 * i + 1), state_chain([2 * i - 2, 2 * i - 1])])
            return carry

        lax.fori_loop(1, npairs, steady, 0)
        last = [2 * npairs - 2, 2 * npairs - 1]
        if nc % 2:
            run_lockstep([local_chain(nc - 1), state_chain(last)])
            run_lockstep([state_chain([nc - 1])])
        else:
            run_lockstep([state_chain(last)])
    xe_ref[0:8, :] = xe_ref[rb:rb + 8, :]


def _gdn_call(qkv, z, sm, cw, alog, dtb, ng, lt3, id3, egb, rb):
    bsz, tp, _ = qkv.shape
    nt = tp // rb
    nc = rb // CHUNK
    hw = A_HEADS * A_DK
    pw = A_HEADS * CHUNK
    row = lambda w_: pl.BlockSpec((1, rb, w_), lambda b, t: (b, t, 0))
    return pl.pallas_call(
        functools.partial(_gdn_body, rb=rb),
        grid=(bsz, nt),
        in_specs=[row(3 * hw), row(hw), row(LANES), _const_spec(cw.shape), _const_spec((1, LANES)),
                  _const_spec((1, LANES)), _const_spec((1, LANES)), _const_spec(lt3.shape),
                  _const_spec(id3.shape), _const_spec(egb.shape)],
        out_specs=row(hw),
        out_shape=jax.ShapeDtypeStruct((bsz, tp, hw), F32),
        scratch_shapes=[pltpu.VMEM((rb + 8, 3 * hw), F32),
                        pltpu.VMEM((2 * A_DK, 2 * A_DK), F32), pltpu.VMEM((2 * A_DK, 2 * A_DK), F32),
                        pltpu.VMEM((nc, pw, A_DK), F32), pltpu.VMEM((nc, 2, 2 * CHUNK, 2 * A_DK), BF16),
                        pltpu.VMEM((nc, pw, pw), BF16), pltpu.VMEM((nc, 2, 2 * A_DK, 2 * CHUNK), BF16),
                        pltpu.VMEM((nc, 8, hw), F32)],
        compiler_params=_params(("parallel", "arbitrary")),
        name="gdn",
    )(qkv, z, sm, cw, alog, dtb, ng, lt3, id3, egb)


def _dsa_body(qr_ref, kr_ref, v_ref, iqr_ref, ike_ref, iko_ref, sm_ref, lstrict_ref, out_ref,
              vt_ref, vmt_ref, skey_ref, planes_ref, st_ref, ot_ref, *, seq, topk):
    g = pl.program_id(1)
    iw_scale = IDX_HEADS ** -0.5 * IDX_DIM ** -0.5
    qpair = 2 * CHUNK
    neg_inf = -jnp.inf

    @pl.when(g == 0)
    def _():
        planes_ref[...] = jnp.zeros_like(planes_ref)

        def vblk(i, carry):
            r = pl.multiple_of(CHUNK + i * LANES, CHUNK)
            vt_ref[i] = v_ref[0, pl.ds(r, LANES), :].astype(F32).T.astype(BF16)
            return carry

        lax.fori_loop(0, seq // LANES, vblk, 0)
        vmt_ref[...] = v_ref[0, 0:LANES, :].astype(F32).T.astype(BF16)

        qm = qr_ref[0, 0:CHUNK, :]
        km = kr_ref[0, 0:CHUNK, :]
        vm = v_ref[0, 0:CHUNK, :]
        colv = lax.broadcasted_iota(I32, (1, CHUNK), 1) >= PAD
        parts = []
        for hd in range(B_HEADS):
            s = _dot_nt(qm[:, hd * B_DH:(hd + 1) * B_DH], km)
            s = jnp.where(colv, s, neg_inf)
            e = jnp.exp(s - jnp.max(s, axis=-1, keepdims=True))
            pr = e / jnp.sum(e, axis=-1, keepdims=True)
            parts.append(_dot(pr.astype(BF16), vm))
        out_ref[0, 0:CHUNK, :] = jnp.concatenate(parts, axis=1)

    lane = lax.broadcasted_iota(I32, (1, LANES), 1)
    groups = []
    for j in range(QUERY_GROUPS):
        p = g * QUERY_GROUPS + j
        r0 = pl.multiple_of(CHUNK + p * qpair, CHUNK)
        qb = qr_ref[0, pl.ds(r0, qpair), :]
        ib = iqr_ref[0, pl.ds(r0, qpair), :]
        smt = sm_ref[0, pl.ds(r0, qpair), :].T
        groups.append(dict(
            j=j, r0=r0,
            q_hq=jnp.concatenate([qb[:, hd * B_DH:(hd + 1) * B_DH] for hd in range(B_HEADS)], axis=0),
            iq_hq=jnp.concatenate([ib[:, hp * LANES:(hp + 1) * LANES] for hp in range(IDX_HEADS // 2)], axis=0),
            wts=[smt[2 * A_HEADS + hd:2 * A_HEADS + hd + 1, :] * iw_scale for hd in range(IDX_HEADS)],
            limit=jnp.where(lane < CHUNK, (2 * p + 1) * CHUNK, (2 * p + 2) * CHUNK)))
    assert QUERY_GROUPS * qpair == KEY_CHUNK
    full_rows = [KEY_CHUNK] * QUERY_GROUPS
    diag_rows = [(j + 1) * qpair for j in range(QUERY_GROUPS)]
    diag_plane_rows = [-(-r // PLANE_ROWS) * PLANE_ROWS for r in diag_rows]
    krows = lax.broadcasted_iota(I32, (KEY_CHUNK, 1), 0)

    def score_chunk(kc, carry, nrows=full_rows):
        k0 = pl.multiple_of(CHUNK + kc * KEY_CHUNK, CHUNK)
        base = pl.multiple_of(kc * KEY_CHUNK, KEY_CHUNK)
        ke, ko, kk = (ref[0, pl.ds(k0, KEY_CHUNK), :] for ref in (ike_ref, iko_ref, kr_ref))
        logits = []
        for grp_, nr in zip(groups, nrows):
            logits.append((_dot_nt(ke[:nr], grp_["iq_hq"]), _dot_nt(ko[:nr], grp_["iq_hq"])))
            st_ref[grp_["j"], pl.ds(base, nr), :] = _dot_nt(kk[:nr], grp_["q_hq"])
        for grp_, nr, (le, lo) in zip(groups, nrows, logits):
            score_tail(kc, pl.ds(base, nr), nr, grp_, le, lo)
        return carry

    def score_tail(kc, rows, nr, grp_, le, lo):
        j, wts, limit = grp_["j"], grp_["wts"], grp_["limit"]
        sc = jnp.zeros((nr, LANES), F32)
        for hp in range(IDX_HEADS // 2):
            sc = sc + wts[2 * hp] * jnp.maximum(le[:, hp * LANES:(hp + 1) * LANES], 0.0)
            sc = sc + wts[2 * hp + 1] * jnp.maximum(lo[:, hp * LANES:(hp + 1) * LANES], 0.0)
        bits = lax.bitcast_convert_type(sc, I32)
        bits = jnp.where(sc == 0.0, 0, bits)
        key = bits ^ ((bits >> 31) & 0x7FFFFFFF)
        valid = kc * KEY_CHUNK + krows[:nr] < limit
        key = jnp.where(valid, key, INT_MIN)
        skey_ref[j, rows, :] = key
        for gi in range(nr // PLANE_ROWS):
            a = [key[gi * PLANE_ROWS + 8 * r:gi * PLANE_ROWS + 8 * r + 8, :] ^ INT_MIN for r in range(32)]
            sh, msk = 16, 0x0000FFFF
            while sh:
                k = 0
                while k < 32:
                    tt = (a[k] ^ lax.shift_right_logical(a[k + sh], jnp.int32(sh))) & msk
                    a[k] = a[k] ^ tt
                    a[k + sh] = a[k + sh] ^ (tt << sh)
                    k = (k + sh + 1) & ~sh
                sh >>= 1
                msk = (msk ^ (msk << sh)) & 0xFFFFFFFF
            g8 = pl.multiple_of((kc * (KEY_CHUNK // PLANE_ROWS) + gi) * 8, 8)
            for b in range(32):
                planes_ref[j, b, pl.ds(g8, 8), :] = a[b]

    lax.fori_loop(0, g, score_chunk, 0)
    score_chunk(g, 0, diag_plane_rows)

    grp = seq // 32
    prow = lax.broadcasted_iota(I32, (grp, 1), 0)
    alive0 = [jnp.where(prow < (g * KEY_CHUNK + nr) // 32, jnp.full((grp, LANES), -1, I32), 0)
              for nr in diag_plane_rows]

    def lane_count(words):
        c = lax.population_count(words)
        if grp > 8:
            c = jnp.sum(c.reshape(grp // 8, 8, LANES), axis=0)
        return jnp.sum(c, axis=0, keepdims=True)

    def bit_body(i, carry):
        bit = jnp.left_shift(jnp.int32(1), 31 - i)
        out = []
        for grp_, (tu, above, alive) in zip(groups, carry):
            hit = alive & planes_ref[grp_["j"], i]
            c1 = lane_count(hit)
            take = above + c1 >= topk
            out.append((jnp.where(take, tu | bit, tu), jnp.where(take, above, above + c1),
                        jnp.where(take, hit, alive ^ hit)))
        return tuple(out)

    zero_row = jnp.zeros((1, LANES), I32)
    found_keys = lax.fori_loop(0, 32, bit_body, tuple((zero_row, zero_row, a0) for a0 in alive0))

    for grp_, nr_diag, (tu, above, alive) in zip(groups, diag_rows, found_keys):
        j, thr, found = grp_["j"], tu ^ INT_MIN, tu != 0
        grp_["thr"] = thr
        tie_lane = found & (above + lane_count(alive) > topk)

        @pl.when(jnp.max(jnp.where(tie_lane, 1, 0)) > 0)
        def _(j=j, thr=thr, found=found, above=above, nr_diag=nr_diag):
            need = (topk - above).astype(F32)

            def body(kc, seen, nr=KEY_CHUNK):
                rows = pl.ds(pl.multiple_of(kc * KEY_CHUNK, KEY_CHUNK), nr)
                blk = skey_ref[j, rows, :]
                tied = (blk == thr) & found
                eq = jnp.where(tied, 1.0, 0.0)
                rank = _dot(lstrict_ref[0:nr, 0:nr], eq.astype(BF16)) + seen
                skey_ref[j, rows, :] = jnp.where(tied & (rank >= need), INT_MIN, blk)
                return seen + jnp.sum(eq, axis=0, keepdims=True)

            body(g, lax.fori_loop(0, g, body, jnp.zeros((1, LANES), F32)), nr_diag)

    mrows = lax.broadcasted_iota(I32, (LANES, 1), 0)
    k_meta = kr_ref[0, 0:LANES, :]
    ones = jnp.ones((ONES_ROWS, LANES), BF16)
    v_meta = jnp.concatenate([vmt_ref[...], ones], axis=0)
    init = []
    for grp_ in groups:
        s_meta = _dot_nt(k_meta, grp_["q_hq"])
        s_meta = jnp.where((mrows >= PAD) & (mrows < CHUNK), s_meta, neg_inf)
        m0 = jnp.max(s_meta, axis=0, keepdims=True)
        ot_ref[grp_["j"]] = _dot(v_meta, jnp.exp(s_meta - m0).astype(BF16))
        init.append(m0)
        grp_["thr"] = jnp.maximum(grp_["thr"], INT_MIN + 1)

    def pv_chunk(kc, carry, nrows=full_rows):
        base = pl.multiple_of(kc * KEY_CHUNK, KEY_CHUNK)
        nblk = KEY_CHUNK // LANES
        vblk = jnp.concatenate([jnp.concatenate([vt_ref[nblk * kc + i], ones], axis=0) for i in range(nblk)], axis=1)
        out, probs = [], []
        for grp_, nr, m in zip(groups, nrows, carry):
            rows = pl.ds(base, nr)
            bias = jnp.where(skey_ref[grp_["j"], rows, :] >= grp_["thr"], 0.0, neg_inf)
            s = st_ref[grp_["j"], rows, :] + jnp.concatenate([bias] * B_HEADS, axis=1)
            m_new = jnp.maximum(m, jnp.max(jnp.max(s.reshape(nr // 8, 8, B_HEADS * LANES), axis=0),
                                           axis=0, keepdims=True))
            out.append(m_new)
            probs.append((jnp.exp(m - m_new), jnp.exp(s - m_new).astype(BF16)))
        for grp_, nr, (alpha, prb) in zip(groups, nrows, probs):
            ot_ref[grp_["j"]] = ot_ref[grp_["j"]] * alpha + _dot(vblk[:, :nr], prb)
        return tuple(out)

    pv_chunk(g, lax.fori_loop(0, g, pv_chunk, tuple(init)), diag_rows)
    for grp_ in groups:
        acc = ot_ref[grp_["j"]]
        o_hq = (acc[:B_DH] / acc[B_DH:B_DH + 1]).T
        for hd in range(B_HEADS):
            out_ref[0, pl.ds(grp_["r0"], qpair), hd * B_DH:(hd + 1) * B_DH] = o_hq[hd * LANES:(hd + 1) * LANES, :]


def _dsa_call(qr, kr, v, iqr, ike, iko, sm, lstrict, seq, topk):
    bsz, tp, _ = qr.shape
    nsteps, rem = divmod(seq, 2 * CHUNK * QUERY_GROUPS)
    assert rem == 0
    full = lambda w_: pl.BlockSpec((1, tp, w_), lambda b, p: (b, 0, 0))
    hw = B_HEADS * B_DH
    nq = QUERY_GROUPS
    return pl.pallas_call(
        functools.partial(_dsa_body, seq=seq, topk=topk),
        grid=(bsz, nsteps),
        in_specs=[full(hw), full(B_DH), full(B_DH), full(hw), full(LANES), full(LANES), full(LANES),
                  _const_spec(lstrict.shape)],
        out_specs=full(hw),
        out_shape=jax.ShapeDtypeStruct((bsz, tp, hw), F32),
        scratch_shapes=[pltpu.VMEM((seq // LANES, B_DH, LANES), BF16), pltpu.VMEM((B_DH, LANES), BF16),
                        pltpu.VMEM((nq, seq, LANES), I32), pltpu.VMEM((nq, 32, seq // 32, LANES), I32),
                        pltpu.VMEM((nq, seq, hw), F32), pltpu.VMEM((nq, B_DH + ONES_ROWS, hw), F32)],
        compiler_params=_params(("parallel", "arbitrary")),
        name="dsa",
    )(qr, kr, v, iqr, ike, iko, sm, lstrict)


def _mix_body(h_ref, oa_ref, ob_ref, g_ref, wg_ref, bg_ref, wa_ref, wb_ref, wo_ref, out_ref, *, rb):
    x = h_ref[0]
    d = x.shape[-1]
    n = _rms(x, g_ref[...]).astype(BF16)
    gates = _sigmoid(_dot(n, wg_ref[...]) + bg_ref[...])
    y = gates[:, :d] * _dot(oa_ref[0].astype(BF16), wa_ref[...]) + gates[:, d:] * _dot(ob_ref[0].astype(BF16), wb_ref[...])
    out = x + _dot(y.astype(BF16), wo_ref[...])
    rows = pl.program_id(1) * rb + lax.broadcasted_iota(I32, (rb, 1), 0)
    out_ref[0] = jnp.where(rows >= PAD, out, 0.0)


def _mix_call(h, oa, ob, gain, wg, bg, wa, wb, wo, rb):
    bsz, tp, d = h.shape
    row = lambda w_: pl.BlockSpec((1, rb, w_), lambda b, t: (b, t, 0))
    return pl.pallas_call(
        functools.partial(_mix_body, rb=rb),
        grid=(bsz, tp // rb),
        in_specs=[row(d), row(oa.shape[-1]), row(ob.shape[-1]), _const_spec((1, d)), _const_spec(wg.shape),
                  _const_spec(bg.shape), _const_spec(wa.shape), _const_spec(wb.shape), _const_spec(wo.shape)],
        out_specs=row(d),
        out_shape=jax.ShapeDtypeStruct((bsz, tp, d), F32),
        compiler_params=_params(("parallel", "arbitrary")),
        name="mix",
    )(h, oa, ob, gain, wg, bg, wa, wb, wo)


def _ffn_body(h_ref, g_ref, wup_ref, cw_ref, wd_ref, out_ref, carry_ref, act_ref, *, rb, dff):
    t = pl.program_id(1)

    @pl.when(t == 0)
    def _():
        carry_ref[...] = jnp.zeros_like(carry_ref)

    x = h_ref[0]
    n = _rms(x, g_ref[...]).astype(BF16)
    rows = lax.broadcasted_iota(I32, (rb, 1), 0)
    nch = dff // FF_CHUNK

    def gate_up(c):
        return (_dot(n, wup_ref[:, c * FF_CHUNK:(c + 1) * FF_CHUNK]),
                _dot(n, wup_ref[:, dff + c * FF_CHUNK:dff + (c + 1) * FF_CHUNK]))

    ahead = gate_up(0)
    for c in range(nch):
        cols = slice(c * FF_CHUNK, (c + 1) * FF_CHUNK)
        gate, up = ahead
        if c + 1 < nch:
            ahead = gate_up(c + 1)
        prev = carry_ref[:, cols]
        g1 = jnp.where(rows == 0, prev[7:8], pltpu.roll(gate, 1, 0))
        g2 = jnp.where(rows == 0, prev[6:7], jnp.where(rows == 1, prev[7:8], pltpu.roll(gate, 2, 0)))
        cw = cw_ref[:, cols]
        conv = g2 * cw[0:1] + g1 * cw[1:2] + gate * cw[2:3]
        carry_ref[:, cols] = gate[rb - 8:rb]
        act_ref[:, cols] = (_silu(conv) * up).astype(BF16)
    out = x + _dot(act_ref[...], wd_ref[...])
    out_ref[0] = jnp.where(t * rb + rows >= PAD, out, 0.0)


def _ffn_call(h, gain, wup, cw, wd, rb):
    bsz, tp, d = h.shape
    dff = wd.shape[0]
    row = pl.BlockSpec((1, rb, d), lambda b, t: (b, t, 0))
    return pl.pallas_call(
        functools.partial(_ffn_body, rb=rb, dff=dff),
        grid=(bsz, tp // rb),
        in_specs=[row, _const_spec((1, d)), _const_spec(wup.shape), _const_spec(cw.shape), _const_spec(wd.shape)],
        out_specs=row,
        out_shape=jax.ShapeDtypeStruct((bsz, tp, d), F32),
        scratch_shapes=[pltpu.VMEM((8, dff), F32), pltpu.VMEM((rb, dff), BF16)],
        compiler_params=_params(("parallel", "arbitrary")),
        name="ffn",
    )(h, gain, wup, cw, wd)


def _rope_tables(tp):
    pos = (jnp.arange(tp, dtype=F32) - PAD)[:, None]

    def cs(dim):
        inv = 1.0 / (ROPE_THETA ** (jnp.arange(0, dim, 2, dtype=F32) / dim))
        ang = pos * inv[None, :]
        return jnp.cos(ang), jnp.sin(ang)

    ca, sa = cs(B_DH)
    ci, si = cs(IDX_DIM)
    zi = jnp.zeros_like(si)
    return (jnp.concatenate([ca, ca], axis=1), jnp.concatenate([-sa, sa], axis=1),
            jnp.concatenate([ci] * 4, axis=1), jnp.concatenate([-si, zi, -si, zi], axis=1),
            jnp.concatenate([zi, si, zi, si], axis=1))


def _pad_lanes(vec):
    return jnp.zeros((1, LANES), F32).at[0, :vec.shape[0]].set(vec.astype(F32))


def _head_expander():
    pw, hw = A_HEADS * CHUNK, A_HEADS * A_DK
    head = np.concatenate([np.arange(pw) // CHUNK, np.arange(hw) // A_DK])
    src = np.arange(LANES)[:, None]
    return jnp.asarray(np.concatenate([src == head[None, :], src == A_HEADS + head[None, :]], axis=1), BF16)


def kernel(x, meta_tokens, norm_mix, w_in, conv_a, a_log, dt_bias, a_out_norm, q_norm, k_norm, kidx_norm,
           w_branch_a, w_branch_b, w_gate, b_gate, w_out, norm_ffn, w_up, conv_ffn, w_down):
    bsz, seq, d = x.shape
    depth = w_in.shape[0]
    assert seq % KEY_CHUNK == 0
    tp = CHUNK + seq
    rb = _row_block(tp // CHUNK)
    topk = min(TOPK_MAX, seq // TOPK_DIV)

    meta = jnp.broadcast_to(meta_tokens.astype(x.dtype)[None], (bsz, N_META, d))
    h = jnp.concatenate([jnp.zeros((bsz, PAD, d), x.dtype), meta, x], axis=1)

    tables = _rope_tables(tp)
    egb = _head_expander()
    tri = np.arange(CHUNK)
    lt3 = jnp.asarray(np.tile(tri[:, None] >= tri[None, :], (1, 3)), BF16)
    id3 = jnp.asarray(np.tile(np.eye(CHUNK), (1, 3)), BF16)
    trk = np.arange(KEY_CHUNK)
    lstrict = jnp.asarray(trk[:, None] > trk[None, :], BF16)

    hw = A_HEADS * A_DK
    o = np.cumsum((0, hw, hw, hw, hw, A_HEADS, A_HEADS, B_HEADS * B_DH, B_DH, B_DH,
                   IDX_HEADS * IDX_DIM, IDX_DIM, IDX_HEADS))
    for l in range(depth):
        w = w_in[l]
        small = jnp.concatenate([w[:, o[4]:o[6]], w[:, o[11]:o[12]],
                                 jnp.zeros((d, LANES - 2 * A_HEADS - IDX_HEADS), w.dtype)], axis=1)
        w_p = jnp.concatenate([w[:, o[0]:o[3]], w[:, o[3]:o[4]], small, w[:, o[6]:o[7]], w[:, o[7]:o[9]],
                               w[:, o[9]:o[10]], w[:, o[10]:o[11]], w[:, o[10]:o[11]]], axis=1).astype(BF16)
        qkv, z, sm, qr, kr, v, iqr, ike, iko = _proj_call(
            h, norm_mix[l][None], w_p, tables, q_norm[l][None], k_norm[l][None],
            jnp.concatenate([kidx_norm[l], kidx_norm[l]])[None], rb)
        o_a = _gdn_call(qkv, z, sm, conv_a[l], _pad_lanes(a_log[l]), _pad_lanes(dt_bias[l]),
                        a_out_norm[l][None], lt3, id3, egb, rb)
        o_b = _dsa_call(qr, kr, v, iqr, ike, iko, sm, lstrict, seq, topk)
        h = _mix_call(h, o_a, o_b, norm_mix[l][None], w_gate[l].astype(BF16), b_gate[l][None],
                      w_branch_a[l].astype(BF16), w_branch_b[l].astype(BF16), w_out[l].astype(BF16), rb)
        h = _ffn_call(h, norm_ffn[l][None], w_up[l].astype(BF16), conv_ffn[l], w_down[l].astype(BF16), rb)
    return h[:, CHUNK:]
```

```python
import functools
import math

import jax
import jax.numpy as jnp
import numpy as np
from jax import lax
from jax.experimental import pallas as pl
from jax.experimental.pallas import tpu as pltpu

F32 = jnp.float32
BF16 = jnp.bfloat16
I32 = jnp.int32

CHUNK = 64
N_META = 16
PAD = CHUNK - N_META
ROPE_THETA = 10000.0
EPS = 1e-6
A_HEADS = 4
A_DK = 128
A_CONV = 4
B_HEADS = 4
B_DH = 128
IDX_HEADS = 8
IDX_DIM = 64
TOPK_MAX = 256
TOPK_DIV = 4
FFN_CONV = 3
LANES = 128
KEY_CHUNK = 512
PLANE_ROWS = 256
QUERY_GROUPS = 4
ONES_ROWS = 16
FF_CHUNK = 256
INT_MIN = -(2 ** 31)
VMEM_LIMIT_BYTES = 56 * 1024 * 1024

NT_DIMS = (((1,), (1,)), ((), ()))


def _dot(a, b):
    return jnp.dot(a, b, preferred_element_type=F32)


def _dot_nt(a, b):
    return lax.dot_general(a, b, NT_DIMS, preferred_element_type=F32)


def _dot_exact(a, b):
    return jnp.dot(a, b, preferred_element_type=F32, precision=lax.Precision.HIGHEST)


def _rms(x, gain):
    return x * lax.rsqrt(jnp.mean(x * x, axis=-1, keepdims=True) + EPS) * gain


def _sigmoid(x):
    return 1.0 / (1.0 + jnp.exp(-x))


def _silu(x):
    return x * _sigmoid(x)


def _softplus(x):
    return jnp.maximum(x, 0.0) + jnp.log(1.0 + jnp.exp(-jnp.abs(x)))


def _row_block(n_chunks):
    g = max(d for d in range(1, 12) if n_chunks % d == 0)
    return g * CHUNK


def _params(sem):
    return pltpu.CompilerParams(dimension_semantics=sem, vmem_limit_bytes=VMEM_LIMIT_BYTES)


def _const_spec(shape):
    nd = len(shape)
    return pl.BlockSpec(shape, lambda *_: (0,) * nd)


PROJ_WIDTHS = (1536, 512, 128, 512, 256, 512, 128)


def _proj_body(h_ref, g_ref, w_ref, cosa_ref, sina_ref, cosi_ref, sinlo_ref, sinhi_ref,
               qn_ref, kn_ref, kin_ref,
               qkv_ref, z_ref, sm_ref, qr_ref, kr_ref, v_ref, iqr_ref, ike_ref, iko_ref):
    n = _rms(h_ref[0], g_ref[...]).astype(BF16)
    offs = np.cumsum((0,) + PROJ_WIDTHS)

    def mm(i):
        return _dot(n, w_ref[:, offs[i]:offs[i + 1]])

    ca, sa = cosa_ref[...], sina_ref[...]
    ci, slo, shi = cosi_ref[...], sinlo_ref[...], sinhi_ref[...]
    q = mm(3)
    iq = mm(5)
    parts = []
    for hd in range(B_HEADS):
        qh = _rms(q[:, hd * B_DH:(hd + 1) * B_DH], qn_ref[...])
        parts.append(qh * ca + pltpu.roll(qh, B_DH // 2, 1) * sa)
    qr_ref[0] = (jnp.concatenate(parts, axis=1) * (B_DH ** -0.5)).astype(BF16)

    kv = mm(4)
    ik = mm(6)
    parts = []
    for hp in range(IDX_HEADS // 2):
        x = iq[:, hp * LANES:(hp + 1) * LANES]
        parts.append(x * ci + pltpu.roll(x, LANES - IDX_DIM // 2, 1) * slo + pltpu.roll(x, IDX_DIM // 2, 1) * shi)
    iqr_ref[0] = jnp.concatenate(parts, axis=1).astype(BF16)

    qkv_ref[0] = mm(0)
    k = _rms(kv[:, :B_DH], kn_ref[...])
    kr_ref[0] = (k * ca + pltpu.roll(k, B_DH // 2, 1) * sa).astype(BF16)
    v_ref[0] = kv[:, B_DH:].astype(BF16)
    ik = _rms(ik, kin_ref[...])
    ik = ik * ci + pltpu.roll(ik, IDX_DIM // 2, 1) * (slo + shi)
    lane = lax.broadcasted_iota(I32, (1, LANES), 1)
    ike_ref[0] = jnp.where(lane < IDX_DIM, ik, 0.0).astype(BF16)
    iko_ref[0] = jnp.where(lane >= IDX_DIM, ik, 0.0).astype(BF16)
    z_ref[0] = mm(1)
    sm_ref[0] = mm(2)


def _proj_call(h, gain, w, tables, qn, kn, kin, rb):
    bsz, tp, d = h.shape
    nt = tp // rb
    row = lambda w_, dt: (jax.ShapeDtypeStruct((bsz, tp, w_), dt),
                          pl.BlockSpec((1, rb, w_), lambda b, t: (b, t, 0)))
    outs = [row(1536, F32), row(512, F32), row(128, F32), row(512, BF16), row(128, BF16),
            row(128, BF16), row(512, BF16), row(128, BF16), row(128, BF16)]
    tab_spec = pl.BlockSpec((rb, LANES), lambda b, t: (t, 0))
    return pl.pallas_call(
        _proj_body,
        grid=(bsz, nt),
        in_specs=[pl.BlockSpec((1, rb, d), lambda b, t: (b, t, 0)), _const_spec((1, d)), _const_spec(w.shape)]
                 + [tab_spec] * 5 + [_const_spec((1, LANES))] * 3,
        out_specs=[o[1] for o in outs],
        out_shape=[o[0] for o in outs],
        compiler_params=_params(("parallel", "arbitrary")),
        name="proj",
    )(h, gain, w, *tables, qn, kn, kin)


def _split3(x):
    x1 = x.astype(BF16)
    r1 = x - x1.astype(F32)
    x2 = r1.astype(BF16)
    x3 = (r1 - x2.astype(F32)).astype(BF16)
    return jnp.concatenate([x1, x2, x3], axis=0)


def _gdn_body(qkv_ref, z_ref, sm_ref, cw_ref, alog_ref, dtb_ref, ng_ref, lt3_ref, id3_ref, egb_ref,
              out_ref, xe_ref, s01_ref, s23_ref, u_ref, wq_ref, qkd_ref, kdt_ref, egl_ref, *, rb):
    t = pl.program_id(1)
    nwide = 3 * A_HEADS * A_DK

    @pl.when(t == 0)
    def _():
        xe_ref[0:8, :] = jnp.zeros((8, nwide), F32)
        s01_ref[...] = jnp.zeros_like(s01_ref)
        s23_ref[...] = jnp.zeros_like(s23_ref)

    xe_ref[8:8 + rb, :] = qkv_ref[0]
    cw = cw_ref[...]

    hw = A_HEADS * A_DK
    pw = A_HEADS * CHUNK
    ii = lax.broadcasted_iota(I32, (CHUNK, pw), 0)
    jj = lax.broadcasted_iota(I32, (CHUNK, pw), 1) & (CHUNK - 1)
    colhead = lax.broadcasted_iota(I32, (CHUNK, pw), 1) >> 6
    eye_p = ii == jj
    bd_mask = (lax.broadcasted_iota(I32, (pw, pw), 0) >> 6) == (lax.broadcasted_iota(I32, (pw, pw), 1) >> 6)
    bdk_mask = (lax.broadcasted_iota(I32, (pw, hw), 0) >> 6) == (lax.broadcasted_iota(I32, (pw, hw), 1) >> 7)
    pair_mask = (lax.broadcasted_iota(I32, (pw, pw), 0) >> 7) == (lax.broadcasted_iota(I32, (pw, pw), 1) >> 7)
    lane = lax.broadcasted_iota(I32, (1, LANES), 1)
    rows64 = lax.broadcasted_iota(I32, (CHUNK, 1), 0)
    neg_a = -jnp.exp(alog_ref[...])
    dtb = dtb_ref[...]
    ng = ng_ref[...]

    def block_diag(xp):
        return jnp.where(bd_mask, jnp.concatenate([xp.astype(BF16)] * A_HEADS, axis=0), 0)

    def stack_heads(xp):
        return jnp.concatenate([jnp.where(colhead == hd, xp, 0.0) for hd in range(A_HEADS)], axis=0)

    def l2n(x):
        parts = []
        for hd in range(A_HEADS):
            xh = x[:, hd * A_DK:(hd + 1) * A_DK]
            parts.append(xh * lax.rsqrt(jnp.sum(xh * xh, axis=-1, keepdims=True) + EPS))
        return jnp.concatenate(parts, axis=1)

    zeros = jnp.zeros((CHUNK, 2 * A_DK), F32)

    def pair_lanes(x, a_, b_):
        return jnp.concatenate([x[a_ * CHUNK:(a_ + 1) * CHUNK], x[b_ * CHUNK:(b_ + 1) * CHUNK]], axis=1)

    def local_load(c):
        r0 = c * CHUNK if isinstance(c, int) else pl.multiple_of(c * CHUNK, CHUNK)
        return xe_ref[pl.ds(r0, CHUNK + 8), :], sm_ref[0, pl.ds(r0, CHUNK), :], r0

    def local_compute(xw, sm, r0):
        acc = pltpu.roll(xw, 3, 0)[8:] * cw[0:1]
        acc = acc + pltpu.roll(xw, 2, 0)[8:] * cw[1:2]
        acc = acc + pltpu.roll(xw, 1, 0)[8:] * cw[2:3]
        acc = acc + xw[8:] * cw[3:4]
        xq = _silu(acc)
        q = l2n(xq[:, :hw]) * (A_DK ** -0.5)
        k = l2n(xq[:, hw:2 * hw])
        v = xq[:, 2 * hw:]

        g = jnp.where(lane < A_HEADS, neg_a * _softplus(sm + dtb), 0.0)
        g = jnp.where(t * rb + r0 + rows64 >= PAD, g, 0.0)
        x3 = _split3(jnp.where(lane < A_HEADS, g, _sigmoid(sm)))
        copies = _dot(x3, egb_ref[...]).astype(BF16)
        yield
        gx = _dot(lt3_ref[...], copies[:, :pw + hw])
        bx = _dot(id3_ref[...], copies[:, pw + hw:])
        yield
        gp, gq = gx[:, :pw], gx[:, pw:]
        bp, bq = bx[:, :pw], bx[:, pw:]
        grow = jnp.sum(jnp.where(eye_p, gp, 0.0), axis=0, keepdims=True)
        decay = jnp.where(ii >= jj, jnp.exp(gp - grow), 0.0)

        kb = k.astype(BF16)
        bdk = jnp.where(bdk_mask, jnp.concatenate([kb] * A_HEADS, axis=0), 0)
        kq = _dot_nt(jnp.concatenate([kb, q.astype(BF16)], axis=0), bdk)
        yield
        kk_p, qk_p = kq[:CHUNK], kq[CHUNK:]

        a = -jnp.where(ii > jj, bp * kk_p * decay, 0.0)
        tinv = jnp.where(eye_p, 1.0, 0.0) + a
        pk = _dot(a.astype(BF16), block_diag(a))
        yield
        for _ in range(4):
            pt = _dot(jnp.concatenate([pk, tinv], axis=0).astype(BF16), block_diag(pk))
            yield
            tinv = tinv + pt[CHUNK:]
            pk = pt[:CHUNK]
        tinv = tinv + _dot(tinv.astype(BF16), block_diag(pk))
        yield

        eg = jnp.exp(gq)
        vb = v * bq
        kbg = k * bq * eg
        rv = jnp.concatenate(
            [jnp.concatenate([vb[:, hd * A_DK:(hd + 1) * A_DK], kbg[:, hd * A_DK:(hd + 1) * A_DK]], axis=1)
             for hd in range(A_HEADS)], axis=0)
        uw = _dot(stack_heads(tinv).astype(BF16), rv.astype(BF16))
        yield
        u, w = uw[:, :A_DK], uw[:, A_DK:]

        qg = q * eg
        glast = gq[CHUNK - 1:CHUNK, :]
        kd = k * jnp.exp(glast - gq)
        return (u,
                jnp.concatenate([pair_lanes(w, 0, 1), qg[:, :2 * A_DK]], axis=0).astype(BF16),
                jnp.concatenate([pair_lanes(w, 2, 3), qg[:, 2 * A_DK:]], axis=0).astype(BF16),
                stack_heads(qk_p * decay).astype(BF16),
                jnp.concatenate([kd[:, :2 * A_DK], zeros], axis=0).T.astype(BF16),
                jnp.concatenate([kd[:, 2 * A_DK:], zeros], axis=0).T.astype(BF16),
                jnp.broadcast_to(jnp.exp(glast), (8, hw)))

    def local_store(c, vals):
        u_ref[c], wq_ref[c, 0], wq_ref[c, 1], qkd_ref[c], kdt_ref[c, 0], kdt_ref[c, 1], egl_ref[c] = vals

    def local_chain(c):
        vals = yield from local_compute(*local_load(c))
        local_store(c, vals)

    def state_chain(chunks):
        def unpair(r):
            return [r[:, :A_DK], r[:, A_DK:]]

        def state_update(s_old, kdt, vn_pair, egl_pair):
            upd = _dot(kdt, jnp.concatenate([vn_pair, zeros], axis=0).astype(BF16))
            return s_old * egl_pair + jnp.where(pair_mask, upd, 0.0)

        s01 = s01_ref[...]
        s23 = s23_ref[...]
        for c in chunks:
            r0 = c * CHUNK if isinstance(c, int) else pl.multiple_of(c * CHUNK, CHUNK)
            r01 = _dot(wq_ref[c, 0], s01.astype(BF16))
            r23 = _dot(wq_ref[c, 1], s23.astype(BF16))
            yield
            ws = jnp.concatenate(unpair(r01[:CHUNK]) + unpair(r23[:CHUNK]), axis=0)
            qs = jnp.concatenate(unpair(r01[CHUNK:]) + unpair(r23[CHUNK:]), axis=0)
            vn = u_ref[c] - ws
            o_rs = qs + _dot(qkd_ref[c], vn.astype(BF16))
            egl = egl_ref[c][0:1, :]
            s01 = state_update(s01, kdt_ref[c, 0], pair_lanes(vn, 0, 1), egl[:, :2 * A_DK])
            s23 = state_update(s23, kdt_ref[c, 1], pair_lanes(vn, 2, 3), egl[:, 2 * A_DK:])
            yield
            zz = z_ref[0, pl.ds(r0, CHUNK), :]
            parts = [_rms(o_rs[hd * CHUNK:(hd + 1) * CHUNK], ng) for hd in range(A_HEADS)]
            out_ref[0, pl.ds(r0, CHUNK), :] = jnp.concatenate(parts, axis=1) * _silu(zz)
        s01_ref[...] = s01
        s23_ref[...] = s23

    def run_lockstep(chains):
        live = list(chains)
        while live:
            for gen in list(live):
                try:
                    next(gen)
                except StopIteration:
                    live.remove(gen)

    nc = rb // CHUNK
    npairs = nc // 2
    if npairs == 0:
        run_lockstep([local_chain(0)])
        run_lockstep([state_chain([0])])
    else:
        run_lockstep([local_chain(0), local_chain(1)])

        def steady(i, carry):
            run_lockstep([local_chain(2 * i), local_chain(2 * i + 1), state_chain([2 * i - 2, 2 * i - 1])])
            return carry

        lax.fori_loop(1, npairs, steady, 0)
        last = [2 * npairs - 2, 2 * npairs - 1]
        if nc % 2:
            run_lockstep([local_chain(nc - 1), state_chain(last)])
            run_lockstep([state_chain([nc - 1])])
        else:
            run_lockstep([state_chain(last)])
    xe_ref[0:8, :] = xe_ref[rb:rb + 8, :]


def _gdn_call(qkv, z, sm, cw, alog, dtb, ng, lt3, id3, egb, rb):
    bsz, tp, _ = qkv.shape
    nt = tp // rb
    nc = rb // CHUNK
    hw = A_HEADS * A_DK
    pw = A_HEADS * CHUNK
    row = lambda w_: pl.BlockSpec((1, rb, w_), lambda b, t: (b, t, 0))
    return pl.pallas_call(
        functools.partial(_gdn_body, rb=rb),
        grid=(bsz, nt),
        in_specs=[row(3 * hw), row(hw), row(LANES), _const_spec(cw.shape), _const_spec((1, LANES)),
                  _const_spec((1, LANES)), _const_spec((1, LANES)), _const_spec(lt3.shape),
                  _const_spec(id3.shape), _const_spec(egb.shape)],
        out_specs=row(hw),
        out_shape=jax.ShapeDtypeStruct((bsz, tp, hw), F32),
        scratch_shapes=[pltpu.VMEM((rb + 8, 3 * hw), F32),
                        pltpu.VMEM((2 * A_DK, 2 * A_DK), F32), pltpu.VMEM((2 * A_DK, 2 * A_DK), F32),
                        pltpu.VMEM((nc, pw, A_DK), F32), pltpu.VMEM((nc, 2, 2 * CHUNK, 2 * A_DK), BF16),
                        pltpu.VMEM((nc, pw, pw), BF16), pltpu.VMEM((nc, 2, 2 * A_DK, 2 * CHUNK), BF16),
                        pltpu.VMEM((nc, 8, hw), F32)],
        compiler_params=_params(("parallel", "arbitrary")),
        name="gdn",
    )(qkv, z, sm, cw, alog, dtb, ng, lt3, id3, egb)


def _dsa_body(qr_ref, kr_ref, v_ref, iqr_ref, ike_ref, iko_ref, sm_ref, lstrict_ref, out_ref,
              vt_ref, vmt_ref, skey_ref, planes_ref, sel_ref, st_ref, ot_ref, *, seq, topk):
    g = pl.program_id(1)
    iw_scale = IDX_HEADS ** -0.5 * IDX_DIM ** -0.5
    qpair = 2 * CHUNK
    neg_inf = -jnp.inf

    @pl.when(g == 0)
    def _():
        planes_ref[...] = jnp.zeros_like(planes_ref)

        def vblk(i, carry):
            r = pl.multiple_of(CHUNK + i * LANES, CHUNK)
            vt_ref[i] = v_ref[0, pl.ds(r, LANES), :].astype(F32).T.astype(BF16)
            return carry

        lax.fori_loop(0, seq // LANES, vblk, 0)
        vmt_ref[...] = v_ref[0, 0:LANES, :].astype(F32).T.astype(BF16)

        qm = qr_ref[0, 0:CHUNK, :]
        km = kr_ref[0, 0:CHUNK, :]
        vm = v_ref[0, 0:CHUNK, :]
        colv = lax.broadcasted_iota(I32, (1, CHUNK), 1) >= PAD
        parts = []
        for hd in range(B_HEADS):
            s = _dot_nt(qm[:, hd * B_DH:(hd + 1) * B_DH], km)
            s = jnp.where(colv, s, neg_inf)
            e = jnp.exp(s - jnp.max(s, axis=-1, keepdims=True))
            pr = e / jnp.sum(e, axis=-1, keepdims=True)
            parts.append(_dot(pr.astype(BF16), vm))
        out_ref[0, 0:CHUNK, :] = jnp.concatenate(parts, axis=1)

    lane = lax.broadcasted_iota(I32, (1, LANES), 1)
    groups = []
    for j in range(QUERY_GROUPS):
        p = g * QUERY_GROUPS + j
        r0 = pl.multiple_of(CHUNK + p * qpair, CHUNK)
        qb = qr_ref[0, pl.ds(r0, qpair), :]
        ib = iqr_ref[0, pl.ds(r0, qpair), :]
        smt = sm_ref[0, pl.ds(r0, qpair), :].T
        groups.append(dict(
            j=j, r0=r0,
            q_hq=jnp.concatenate([qb[:, hd * B_DH:(hd + 1) * B_DH] for hd in range(B_HEADS)], axis=0),
            iq_hq=jnp.concatenate([ib[:, hp * LANES:(hp + 1) * LANES] for hp in range(IDX_HEADS // 2)], axis=0),
            wts=[smt[2 * A_HEADS + hd:2 * A_HEADS + hd + 1, :] * iw_scale for hd in range(IDX_HEADS)],
            limit=jnp.where(lane < CHUNK, (2 * p + 1) * CHUNK, (2 * p + 2) * CHUNK)))
    assert QUERY_GROUPS * qpair == KEY_CHUNK
    full_rows = [KEY_CHUNK] * QUERY_GROUPS
    diag_rows = [(j + 1) * qpair for j in range(QUERY_GROUPS)]
    diag_plane_rows = [-(-r // PLANE_ROWS) * PLANE_ROWS for r in diag_rows]
    krows = lax.broadcasted_iota(I32, (KEY_CHUNK, 1), 0)

    def score_chunk(kc, carry, nrows=full_rows):
        k0 = pl.multiple_of(CHUNK + kc * KEY_CHUNK, CHUNK)
        base = pl.multiple_of(kc * KEY_CHUNK, KEY_CHUNK)
        ke, ko, kk = (ref[0, pl.ds(k0, KEY_CHUNK), :] for ref in (ike_ref, iko_ref, kr_ref))
        logits = []
        for grp_, nr in zip(groups, nrows):
            logits.append((_dot_nt(ke[:nr], grp_["iq_hq"]), _dot_nt(ko[:nr], grp_["iq_hq"])))
            st_ref[grp_["j"], pl.ds(base, nr), :] = _dot_nt(kk[:nr], grp_["q_hq"])
        for grp_, nr, (le, lo) in zip(groups, nrows, logits):
            score_tail(kc, pl.ds(base, nr), nr, grp_, le, lo)
        return carry

    def score_tail(kc, rows, nr, grp_, le, lo):
        j, wts, limit = grp_["j"], grp_["wts"], grp_["limit"]
        sc = jnp.zeros((nr, LANES), F32)
        for hp in range(IDX_HEADS // 2):
            sc = sc + wts[2 * hp] * jnp.maximum(le[:, hp * LANES:(hp + 1) * LANES], 0.0)
            sc = sc + wts[2 * hp + 1] * jnp.maximum(lo[:, hp * LANES:(hp + 1) * LANES], 0.0)
        bits = lax.bitcast_convert_type(sc, I32)
        bits = jnp.where(sc == 0.0, 0, bits)
        key = bits ^ ((bits >> 31) & 0x7FFFFFFF)
        valid = kc * KEY_CHUNK + krows[:nr] < limit
        key = jnp.where(valid, key, INT_MIN)
        skey_ref[j, rows, :] = key
        for gi in range(nr // PLANE_ROWS):
            a = [key[gi * PLANE_ROWS + 8 * r:gi * PLANE_ROWS + 8 * r + 8, :] ^ INT_MIN for r in range(32)]
            sh, msk = 16, 0x0000FFFF
            while sh:
                k = 0
                while k < 32:
                    tt = (a[k] ^ lax.shift_right_logical(a[k + sh], jnp.int32(sh))) & msk
                    a[k] = a[k] ^ tt
                    a[k + sh] = a[k + sh] ^ (tt << sh)
                    k = (k + sh + 1) & ~sh
                sh >>= 1
                msk = (msk ^ (msk << sh)) & 0xFFFFFFFF
            g8 = pl.multiple_of((kc * (KEY_CHUNK // PLANE_ROWS) + gi) * 8, 8)
            for b in range(32):
                planes_ref[j, b, pl.ds(g8, 8), :] = a[b]

    lax.fori_loop(0, g, score_chunk, 0)
    score_chunk(g, 0, diag_plane_rows)

    words_per_chunk = KEY_CHUNK // 32

    def lane_count(words):
        c = lax.population_count(words)
        if words.shape[0] > 8:
            c = jnp.sum(c.reshape(words.shape[0] // 8, 8, LANES), axis=0)
        return jnp.sum(c, axis=0, keepdims=True)

    def select_threshold(step):
        nrow = (step + 1) * words_per_chunk
        prow = lax.broadcasted_iota(I32, (nrow, 1), 0)
        alive0 = [jnp.where(prow < step * words_per_chunk + nr // 32, jnp.full((nrow, LANES), -1, I32), 0)
                  for nr in diag_plane_rows]

        def bit_body(i, carry):
            bit = jnp.left_shift(jnp.int32(1), 31 - i)
            out = []
            for grp_, (tu, above, alive) in zip(groups, carry):
                hit = alive & planes_ref[grp_["j"], i, 0:nrow, :]
                c1 = lane_count(hit)
                take = above + c1 >= topk
                out.append((jnp.where(take, tu | bit, tu), jnp.where(take, above, above + c1),
                            jnp.where(take, hit, alive ^ hit)))
            return tuple(out)

        zero_row = jnp.zeros((1, LANES), I32)
        result = lax.fori_loop(0, 32, bit_body, tuple((zero_row, zero_row, a0) for a0 in alive0))
        for grp_, (tu, above, alive) in zip(groups, result):
            for n, row in enumerate((tu, above, lane_count(alive))):
                sel_ref[grp_["j"], n] = jnp.broadcast_to(row, (8, LANES))

    for step in range(seq // KEY_CHUNK):
        pl.when(g == step)(functools.partial(select_threshold, step))

    for grp_, nr_diag in zip(groups, diag_rows):
        tu, above, n_eq = (sel_ref[grp_["j"], n][0:1] for n in range(3))
        j, thr, found = grp_["j"], tu ^ INT_MIN, tu != 0
        grp_["thr"] = thr
        tie_lane = found & (above + n_eq > topk)

        @pl.when(jnp.max(jnp.where(tie_lane, 1, 0)) > 0)
        def _(j=j, thr=thr, found=found, above=above, nr_diag=nr_diag):
            need = (topk - above).astype(F32)

            def body(kc, seen, nr=KEY_CHUNK):
                base = pl.multiple_of(kc * KEY_CHUNK, KEY_CHUNK)
                for sub in range(nr // LANES):
                    rows = pl.ds(base + sub * LANES, LANES)
                    blk = skey_ref[j, rows, :]
                    tied = (blk == thr) & found
                    eq = jnp.where(tied, 1.0, 0.0)
                    rank = _dot(lstrict_ref[...], eq.astype(BF16)) + seen
                    skey_ref[j, rows, :] = jnp.where(tied & (rank >= need), INT_MIN, blk)
                    seen = seen + jnp.sum(eq, axis=0, keepdims=True)
                return seen

            body(g, lax.fori_loop(0, g, body, jnp.zeros((1, LANES), F32)), nr_diag)

    mrows = lax.broadcasted_iota(I32, (LANES, 1), 0)
    k_meta = kr_ref[0, 0:LANES, :]
    ones = jnp.ones((ONES_ROWS, LANES), BF16)
    v_meta = jnp.concatenate([vmt_ref[...], ones], axis=0)
    init = []
    for grp_ in groups:
        s_meta = _dot_nt(k_meta, grp_["q_hq"])
        s_meta = jnp.where((mrows >= PAD) & (mrows < CHUNK), s_meta, neg_inf)
        m0 = jnp.max(s_meta, axis=0, keepdims=True)
        ot_ref[grp_["j"]] = _dot(v_meta, jnp.exp(s_meta - m0).astype(BF16))
        init.append(m0)
        grp_["thr"] = jnp.maximum(grp_["thr"], INT_MIN + 1)

    def pv_chunk(kc, carry, nrows=full_rows):
        base = pl.multiple_of(kc * KEY_CHUNK, KEY_CHUNK)
        nblk = KEY_CHUNK // LANES
        vblk = jnp.concatenate([jnp.concatenate([vt_ref[nblk * kc + i], ones], axis=0) for i in range(nblk)], axis=1)
        out, probs = [], []
        for grp_, nr, m in zip(groups, nrows, carry):
            rows = pl.ds(base, nr)
            bias = jnp.where(skey_ref[grp_["j"], rows, :] >= grp_["thr"], 0.0, neg_inf)
            s = st_ref[grp_["j"], rows, :] + jnp.concatenate([bias] * B_HEADS, axis=1)
            m_new = jnp.maximum(m, jnp.max(jnp.max(s.reshape(nr // 8, 8, B_HEADS * LANES), axis=0),
                                           axis=0, keepdims=True))
            out.append(m_new)
            probs.append((jnp.exp(m - m_new), jnp.exp(s - m_new).astype(BF16)))
        for grp_, nr, (alpha, prb) in zip(groups, nrows, probs):
            ot_ref[grp_["j"]] = ot_ref[grp_["j"]] * alpha + _dot(vblk[:, :nr], prb)
        return tuple(out)

    pv_chunk(g, lax.fori_loop(0, g, pv_chunk, tuple(init)), diag_rows)
    for grp_ in groups:
        acc = ot_ref[grp_["j"]]
        o_hq = (acc[:B_DH] / acc[B_DH:B_DH + 1]).T
        for hd in range(B_HEADS):
            out_ref[0, pl.ds(grp_["r0"], qpair), hd * B_DH:(hd + 1) * B_DH] = o_hq[hd * LANES:(hd + 1) * LANES, :]


def _dsa_call(qr, kr, v, iqr, ike, iko, sm, lstrict, seq, topk):
    bsz, tp, _ = qr.shape
    nsteps, rem = divmod(seq, 2 * CHUNK * QUERY_GROUPS)
    assert rem == 0
    full = lambda w_: pl.BlockSpec((1, tp, w_), lambda b, p: (b, 0, 0))
    hw = B_HEADS * B_DH
    nq = QUERY_GROUPS
    return pl.pallas_call(
        functools.partial(_dsa_body, seq=seq, topk=topk),
        grid=(bsz, nsteps),
        in_specs=[full(hw), full(B_DH), full(B_DH), full(hw), full(LANES), full(LANES), full(LANES),
                  _const_spec(lstrict.shape)],
        out_specs=full(hw),
        out_shape=jax.ShapeDtypeStruct((bsz, tp, hw), F32),
        scratch_shapes=[pltpu.VMEM((seq // LANES, B_DH, LANES), BF16), pltpu.VMEM((B_DH, LANES), BF16),
                        pltpu.VMEM((nq, seq, LANES), I32), pltpu.VMEM((nq, 32, seq // 32, LANES), I32),
                        pltpu.VMEM((nq, 3, 8, LANES), I32),
                        pltpu.VMEM((nq, seq, hw), F32), pltpu.VMEM((nq, B_DH + ONES_ROWS, hw), F32)],
        compiler_params=_params(("parallel", "arbitrary")),
        name="dsa",
    )(qr, kr, v, iqr, ike, iko, sm, lstrict)


def _mix_body(h_ref, oa_ref, ob_ref, g_ref, wg_ref, bg_ref, wa_ref, wb_ref, wo_ref, out_ref, *, rb):
    x = h_ref[0]
    d = x.shape[-1]
    n = _rms(x, g_ref[...]).astype(BF16)
    gates = _sigmoid(_dot(n, wg_ref[...]) + bg_ref[...])
    y = gates[:, :d] * _dot(oa_ref[0].astype(BF16), wa_ref[...]) + gates[:, d:] * _dot(ob_ref[0].astype(BF16), wb_ref[...])
    out = x + _dot(y.astype(BF16), wo_ref[...])
    rows = pl.program_id(1) * rb + lax.broadcasted_iota(I32, (rb, 1), 0)
    out_ref[0] = jnp.where(rows >= PAD, out, 0.0)


def _mix_call(h, oa, ob, gain, wg, bg, wa, wb, wo, rb):
    bsz, tp, d = h.shape
    row = lambda w_: pl.BlockSpec((1, rb, w_), lambda b, t: (b, t, 0))
    return pl.pallas_call(
        functools.partial(_mix_body, rb=rb),
        grid=(bsz, tp // rb),
        in_specs=[row(d), row(oa.shape[-1]), row(ob.shape[-1]), _const_spec((1, d)), _const_spec(wg.shape),
                  _const_spec(bg.shape), _const_spec(wa.shape), _const_spec(wb.shape), _const_spec(wo.shape)],
        out_specs=row(d),
        out_shape=jax.ShapeDtypeStruct((bsz, tp, d), F32),
        compiler_params=_params(("parallel", "arbitrary")),
        name="mix",
    )(h, oa, ob, gain, wg, bg, wa, wb, wo)


def _ffn_body(h_ref, g_ref, wup_ref, cw_ref, wd_ref, out_ref, carry_ref, act_ref, *, rb, dff):
    t = pl.program_id(1)

    @pl.when(t == 0)
    def _():
        carry_ref[...] = jnp.zeros_like(carry_ref)

    x = h_ref[0]
    n = _rms(x, g_ref[...]).astype(BF16)
    rows = lax.broadcasted_iota(I32, (rb, 1), 0)
    nch = dff // FF_CHUNK

    def gate_up(c):
        return (_dot(n, wup_ref[:, c * FF_CHUNK:(c + 1) * FF_CHUNK]),
                _dot(n, wup_ref[:, dff + c * FF_CHUNK:dff + (c + 1) * FF_CHUNK]))

    ahead = gate_up(0)
    for c in range(nch):
        cols = slice(c * FF_CHUNK, (c + 1) * FF_CHUNK)
        gate, up = ahead
        if c + 1 < nch:
            ahead = gate_up(c + 1)
        prev = carry_ref[:, cols]
        g1 = jnp.where(rows == 0, prev[7:8], pltpu.roll(gate, 1, 0))
        g2 = jnp.where(rows == 0, prev[6:7], jnp.where(rows == 1, prev[7:8], pltpu.roll(gate, 2, 0)))
        cw = cw_ref[:, cols]
        conv = g2 * cw[0:1] + g1 * cw[1:2] + gate * cw[2:3]
        carry_ref[:, cols] = gate[rb - 8:rb]
        act_ref[:, cols] = (_silu(conv) * up).astype(BF16)
    out = x + _dot(act_ref[...], wd_ref[...])
    out_ref[0] = jnp.where(t * rb + rows >= PAD, out, 0.0)


def _ffn_call(h, gain, wup, cw, wd, rb):
    bsz, tp, d = h.shape
    dff = wd.shape[0]
    row = pl.BlockSpec((1, rb, d), lambda b, t: (b, t, 0))
    return pl.pallas_call(
        functools.partial(_ffn_body, rb=rb, dff=dff),
        grid=(bsz, tp // rb),
        in_specs=[row, _const_spec((1, d)), _const_spec(wup.shape), _const_spec(cw.shape), _const_spec(wd.shape)],
        out_specs=row,
        out_shape=jax.ShapeDtypeStruct((bsz, tp, d), F32),
        scratch_shapes=[pltpu.VMEM((8, dff), F32), pltpu.VMEM((rb, dff), BF16)],
        compiler_params=_params(("parallel", "arbitrary")),
        name="ffn",
    )(h, gain, wup, cw, wd)


def _rope_tables(tp):
    pos = (jnp.arange(tp, dtype=F32) - PAD)[:, None]

    def cs(dim):
        inv = 1.0 / (ROPE_THETA ** (jnp.arange(0, dim, 2, dtype=F32) / dim))
        ang = pos * inv[None, :]
        return jnp.cos(ang), jnp.sin(ang)

    ca, sa = cs(B_DH)
    ci, si = cs(IDX_DIM)
    zi = jnp.zeros_like(si)
    return (jnp.concatenate([ca, ca], axis=1), jnp.concatenate([-sa, sa], axis=1),
            jnp.concatenate([ci] * 4, axis=1), jnp.concatenate([-si, zi, -si, zi], axis=1),
            jnp.concatenate([zi, si, zi, si], axis=1))


def _pad_lanes(vec):
    return jnp.zeros((1, LANES), F32).at[0, :vec.shape[0]].set(vec.astype(F32))


def _head_expander():
    pw, hw = A_HEADS * CHUNK, A_HEADS * A_DK
    head = np.concatenate([np.arange(pw) // CHUNK, np.arange(hw) // A_DK])
    src = np.arange(LANES)[:, None]
    return jnp.asarray(np.concatenate([src == head[None, :], src == A_HEADS + head[None, :]], axis=1), BF16)


def kernel(x, meta_tokens, norm_mix, w_in, conv_a, a_log, dt_bias, a_out_norm, q_norm, k_norm, kidx_norm,
           w_branch_a, w_branch_b, w_gate, b_gate, w_out, norm_ffn, w_up, conv_ffn, w_down):
    bsz, seq, d = x.shape
    depth = w_in.shape[0]
    assert seq % KEY_CHUNK == 0
    tp = CHUNK + seq
    rb = _row_block(tp // CHUNK)
    topk = min(TOPK_MAX, seq // TOPK_DIV)

    meta = jnp.broadcast_to(meta_tokens.astype(x.dtype)[None], (bsz, N_META, d))
    h = jnp.concatenate([jnp.zeros((bsz, PAD, d), x.dtype), meta, x], axis=1)

    tables = _rope_tables(tp)
    egb = _head_expander()
    tri = np.arange(CHUNK)
    lt3 = jnp.asarray(np.tile(tri[:, None] >= tri[None, :], (1, 3)), BF16)
    id3 = jnp.asarray(np.tile(np.eye(CHUNK), (1, 3)), BF16)
    trk = np.arange(LANES)
    lstrict = jnp.asarray(trk[:, None] > trk[None, :], BF16)

    hw = A_HEADS * A_DK
    o = np.cumsum((0, hw, hw, hw, hw, A_HEADS, A_HEADS, B_HEADS * B_DH, B_DH, B_DH,
                   IDX_HEADS * IDX_DIM, IDX_DIM, IDX_HEADS))
    for l in range(depth):
        w = w_in[l]
        small = jnp.concatenate([w[:, o[4]:o[6]], w[:, o[11]:o[12]],
                                 jnp.zeros((d, LANES - 2 * A_HEADS - IDX_HEADS), w.dtype)], axis=1)
        w_p = jnp.concatenate([w[:, o[0]:o[3]], w[:, o[3]:o[4]], small, w[:, o[6]:o[7]], w[:, o[7]:o[9]],
                               w[:, o[9]:o[10]], w[:, o[10]:o[11]], w[:, o[10]:o[11]]], axis=1).astype(BF16)
        qkv, z, sm, qr, kr, v, iqr, ike, iko = _proj_call(
            h, norm_mix[l][None], w_p, tables, q_norm[l][None], k_norm[l][None],
            jnp.concatenate([kidx_norm[l], kidx_norm[l]])[None], rb)
        o_a = _gdn_call(qkv, z, sm, conv_a[l], _pad_lanes(a_log[l]), _pad_lanes(dt_bias[l]),
                        a_out_norm[l][None], lt3, id3, egb, rb)
        o_b = _dsa_call(qr, kr, v, iqr, ike, iko, sm, lstrict, seq, topk)
        h = _mix_call(h, o_a, o_b, norm_mix[l][None], w_gate[l].astype(BF16), b_gate[l][None],
                      w_branch_a[l].astype(BF16), w_branch_b[l].astype(BF16), w_out[l].astype(BF16), rb)
        h = _ffn_call(h, norm_ffn[l][None], w_up[l].astype(BF16), conv_ffn[l], w_down[l].astype(BF16), rb)
    return h[:, CHUNK:]
```

```python
import functools
import math

import jax
import jax.numpy as jnp
import numpy as np
from jax import lax
from jax.experimental import pallas as pl
from jax.experimental.pallas import tpu as pltpu

F32 = jnp.float32
BF16 = jnp.bfloat16
I32 = jnp.int32

CHUNK = 64
N_META = 16
PAD = CHUNK - N_META
ROPE_THETA = 10000.0
EPS = 1e-6
A_HEADS = 4
A_DK = 128
A_CONV = 4
B_HEADS = 4
B_DH = 128
IDX_HEADS = 8
IDX_DIM = 64
TOPK_MAX = 256
TOPK_DIV = 4
FFN_CONV = 3
LANES = 128
KEY_CHUNK = 512
PLANE_ROWS = 256
QUERY_GROUPS = 4
ONES_ROWS = 16
FF_CHUNK = 256
INT_MIN = -(2 ** 31)
VMEM_LIMIT_BYTES = 56 * 1024 * 1024

NT_DIMS = (((1,), (1,)), ((), ()))


def _dot(a, b):
    return jnp.dot(a, b, preferred_element_type=F32)


def _dot_nt(a, b):
    return lax.dot_general(a, b, NT_DIMS, preferred_element_type=F32)


def _dot_exact(a, b):
    return jnp.dot(a, b, preferred_element_type=F32, precision=lax.Precision.HIGHEST)


def _rms(x, gain):
    return x * lax.rsqrt(jnp.mean(x * x, axis=-1, keepdims=True) + EPS) * gain


def _sigmoid(x):
    return 1.0 / (1.0 + jnp.exp(-x))


def _silu(x):
    return x * _sigmoid(x)


def _softplus(x):
    return jnp.maximum(x, 0.0) + jnp.log(1.0 + jnp.exp(-jnp.abs(x)))


def _row_block(n_chunks):
    g = max(d for d in range(1, 12) if n_chunks % d == 0)
    return g * CHUNK


def _params(sem):
    return pltpu.CompilerParams(dimension_semantics=sem, vmem_limit_bytes=VMEM_LIMIT_BYTES)


def _const_spec(shape):
    nd = len(shape)
    return pl.BlockSpec(shape, lambda *_: (0,) * nd)


PROJ_WIDTHS = (1536, 512, 128, 512, 256, 512, 128)


def _proj_body(h_ref, g_ref, w_ref, cosa_ref, sina_ref, cosi_ref, sinlo_ref, sinhi_ref,
               qn_ref, kn_ref, kin_ref,
               qkv_ref, z_ref, sm_ref, qr_ref, kr_ref, v_ref, iqr_ref, ike_ref, iko_ref):
    n = _rms(h_ref[0], g_ref[...]).astype(BF16)
    offs = np.cumsum((0,) + PROJ_WIDTHS)

    def mm(i):
        return _dot(n, w_ref[:, offs[i]:offs[i + 1]])

    ca, sa = cosa_ref[...], sina_ref[...]
    ci, slo, shi = cosi_ref[...], sinlo_ref[...], sinhi_ref[...]
    q = mm(3)
    iq = mm(5)
    parts = []
    for hd in range(B_HEADS):
        qh = _rms(q[:, hd * B_DH:(hd + 1) * B_DH], qn_ref[...])
        parts.append(qh * ca + pltpu.roll(qh, B_DH // 2, 1) * sa)
    qr_ref[0] = (jnp.concatenate(parts, axis=1) * (B_DH ** -0.5)).astype(BF16)

    kv = mm(4)
    ik = mm(6)
    parts = []
    for hp in range(IDX_HEADS // 2):
        x = iq[:, hp * LANES:(hp + 1) * LANES]
        parts.append(x * ci + pltpu.roll(x, LANES - IDX_DIM // 2, 1) * slo + pltpu.roll(x, IDX_DIM // 2, 1) * shi)
    iqr_ref[0] = jnp.concatenate(parts, axis=1).astype(BF16)

    qkv_ref[0] = mm(0)
    k = _rms(kv[:, :B_DH], kn_ref[...])
    kr_ref[0] = (k * ca + pltpu.roll(k, B_DH // 2, 1) * sa).astype(BF16)
    v_ref[0] = kv[:, B_DH:].astype(BF16)
    ik = _rms(ik, kin_ref[...])
    ik = ik * ci + pltpu.roll(ik, IDX_DIM // 2, 1) * (slo + shi)
    lane = lax.broadcasted_iota(I32, (1, LANES), 1)
    ike_ref[0] = jnp.where(lane < IDX_DIM, ik, 0.0).astype(BF16)
    iko_ref[0] = jnp.where(lane >= IDX_DIM, ik, 0.0).astype(BF16)
    z_ref[0] = mm(1)
    sm_ref[0] = mm(2)


def _proj_call(h, gain, w, tables, qn, kn, kin, rb):
    bsz, tp, d = h.shape
    nt = tp // rb
    row = lambda w_, dt: (jax.ShapeDtypeStruct((bsz, tp, w_), dt),
                          pl.BlockSpec((1, rb, w_), lambda b, t: (b, t, 0)))
    outs = [row(1536, F32), row(512, F32), row(128, F32), row(512, BF16), row(128, BF16),
            row(128, BF16), row(512, BF16), row(128, BF16), row(128, BF16)]
    tab_spec = pl.BlockSpec((rb, LANES), lambda b, t: (t, 0))
    return pl.pallas_call(
        _proj_body,
        grid=(bsz, nt),
        in_specs=[pl.BlockSpec((1, rb, d), lambda b, t: (b, t, 0)), _const_spec((1, d)), _const_spec(w.shape)]
                 + [tab_spec] * 5 + [_const_spec((1, LANES))] * 3,
        out_specs=[o[1] for o in outs],
        out_shape=[o[0] for o in outs],
        compiler_params=_params(("parallel", "arbitrary")),
        name="proj",
    )(h, gain, w, *tables, qn, kn, kin)


def _split3(x):
    x1 = x.astype(BF16)
    r1 = x - x1.astype(F32)
    x2 = r1.astype(BF16)
    x3 = (r1 - x2.astype(F32)).astype(BF16)
    return jnp.concatenate([x1, x2, x3], axis=0)


def _gdn_body(qkv_ref, z_ref, sm_ref, cw_ref, alog_ref, dtb_ref, ng_ref, lt3_ref, expand_ref,
              out_ref, xe_ref, s01_ref, s23_ref, u_ref, wq_ref, qkd_ref, kdt_ref, egl_ref, *, rb):
    t = pl.program_id(1)
    nwide = 3 * A_HEADS * A_DK

    @pl.when(t == 0)
    def _():
        xe_ref[0:8, :] = jnp.zeros((8, nwide), F32)
        s01_ref[...] = jnp.zeros_like(s01_ref)
        s23_ref[...] = jnp.zeros_like(s23_ref)

    xe_ref[8:8 + rb, :] = qkv_ref[0]
    cw = cw_ref[...]

    hw = A_HEADS * A_DK
    pw = A_HEADS * CHUNK
    ii = lax.broadcasted_iota(I32, (CHUNK, pw), 0)
    jj = lax.broadcasted_iota(I32, (CHUNK, pw), 1) & (CHUNK - 1)
    colhead = lax.broadcasted_iota(I32, (CHUNK, pw), 1) >> 6
    eye_p = ii == jj
    bd_mask = (lax.broadcasted_iota(I32, (pw, pw), 0) >> 6) == (lax.broadcasted_iota(I32, (pw, pw), 1) >> 6)
    bdk_mask = (lax.broadcasted_iota(I32, (pw, hw), 0) >> 6) == (lax.broadcasted_iota(I32, (pw, hw), 1) >> 7)
    pair_mask = (lax.broadcasted_iota(I32, (pw, pw), 0) >> 7) == (lax.broadcasted_iota(I32, (pw, pw), 1) >> 7)
    lane = lax.broadcasted_iota(I32, (1, LANES), 1)
    rows64 = lax.broadcasted_iota(I32, (CHUNK, 1), 0)
    neg_a = -jnp.exp(alog_ref[...])
    dtb = dtb_ref[...]
    ng = ng_ref[...]

    def block_diag(xp):
        return jnp.where(bd_mask, jnp.concatenate([xp.astype(BF16)] * A_HEADS, axis=0), 0)

    def stack_heads(xp):
        return jnp.concatenate([jnp.where(colhead == hd, xp, 0.0) for hd in range(A_HEADS)], axis=0)

    def l2n(x):
        parts = []
        for hd in range(A_HEADS):
            xh = x[:, hd * A_DK:(hd + 1) * A_DK]
            parts.append(xh * lax.rsqrt(jnp.sum(xh * xh, axis=-1, keepdims=True) + EPS))
        return jnp.concatenate(parts, axis=1)

    zeros = jnp.zeros((CHUNK, 2 * A_DK), F32)

    def pair_lanes(x, a_, b_):
        return jnp.concatenate([x[a_ * CHUNK:(a_ + 1) * CHUNK], x[b_ * CHUNK:(b_ + 1) * CHUNK]], axis=1)

    def local_load(c):
        r0 = c * CHUNK if isinstance(c, int) else pl.multiple_of(c * CHUNK, CHUNK)
        return xe_ref[pl.ds(r0, CHUNK + 8), :], sm_ref[0, pl.ds(r0, CHUNK), :], r0

    def local_compute(xw, sm, r0):
        acc = pltpu.roll(xw, 3, 0)[8:] * cw[0:1]
        acc = acc + pltpu.roll(xw, 2, 0)[8:] * cw[1:2]
        acc = acc + pltpu.roll(xw, 1, 0)[8:] * cw[2:3]
        acc = acc + xw[8:] * cw[3:4]
        xq = _silu(acc)
        q = l2n(xq[:, :hw]) * (A_DK ** -0.5)
        k = l2n(xq[:, hw:2 * hw])
        v = xq[:, 2 * hw:]

        g = jnp.where(lane < A_HEADS, neg_a * _softplus(sm + dtb), 0.0)
        g = jnp.where(t * rb + r0 + rows64 >= PAD, g, 0.0)
        to_packed, to_wide = expand_ref[:, :pw], expand_ref[:, pw:]
        g3 = _split3(g)
        cg = _dot(g3, to_packed).astype(BF16)
        gcs = _dot(lt3_ref[...], g3)
        yield
        gp = _dot(lt3_ref[...], cg)
        in_head = lane < A_HEADS
        beta = jnp.where(in_head, pltpu.roll(_sigmoid(sm), LANES - A_HEADS, 1), 0.0)
        eg_s = jnp.where(in_head, jnp.exp(gcs), 0.0)
        to_last = jnp.where(in_head, jnp.exp(gcs[CHUNK - 1:CHUNK] - gcs), 0.0)
        wide = _dot(jnp.concatenate([beta, beta * eg_s, eg_s, to_last], axis=0).astype(BF16), to_wide)
        bp = _dot(beta.astype(BF16), to_packed)
        glast3 = _dot(_split3(gcs[CHUNK - 16:]), to_wide)
        yield
        bq, bq_eg, eg, kd_scale = (wide[i * CHUNK:(i + 1) * CHUNK] for i in range(4))
        glast = glast3[15:16] + glast3[31:32] + glast3[47:48]
        grow = jnp.sum(jnp.where(eye_p, gp, 0.0), axis=0, keepdims=True)
        decay = jnp.where(ii >= jj, jnp.exp(gp - grow), 0.0)

        kb = k.astype(BF16)
        bdk = jnp.where(bdk_mask, jnp.concatenate([kb] * A_HEADS, axis=0), 0)
        kq = _dot_nt(jnp.concatenate([kb, q.astype(BF16)], axis=0), bdk)
        yield
        kk_p, qk_p = kq[:CHUNK], kq[CHUNK:]

        a = -jnp.where(ii > jj, bp * kk_p * decay, 0.0)
        tinv = jnp.where(eye_p, 1.0, 0.0) + a
        pk = _dot(a.astype(BF16), block_diag(a))
        yield
        for _ in range(4):
            pt = _dot(jnp.concatenate([pk, tinv], axis=0).astype(BF16), block_diag(pk))
            yield
            tinv = tinv + pt[CHUNK:]
            pk = pt[:CHUNK]
        tinv = tinv + _dot(tinv.astype(BF16), block_diag(pk))
        yield

        vb = v * bq
        kbg = k * bq_eg
        rv = jnp.concatenate(
            [jnp.concatenate([vb[:, hd * A_DK:(hd + 1) * A_DK], kbg[:, hd * A_DK:(hd + 1) * A_DK]], axis=1)
             for hd in range(A_HEADS)], axis=0)
        uw = _dot(stack_heads(tinv).astype(BF16), rv.astype(BF16))
        yield
        u, w = uw[:, :A_DK], uw[:, A_DK:]

        qg = q * eg
        kd = k * kd_scale
        return (u,
                jnp.concatenate([pair_lanes(w, 0, 1), qg[:, :2 * A_DK]], axis=0).astype(BF16),
                jnp.concatenate([pair_lanes(w, 2, 3), qg[:, 2 * A_DK:]], axis=0).astype(BF16),
                stack_heads(qk_p * decay).astype(BF16),
                jnp.concatenate([kd[:, :2 * A_DK], zeros], axis=0).T.astype(BF16),
                jnp.concatenate([kd[:, 2 * A_DK:], zeros], axis=0).T.astype(BF16),
                jnp.broadcast_to(jnp.exp(glast), (8, hw)))

    def local_store(c, vals):
        u_ref[c], wq_ref[c, 0], wq_ref[c, 1], qkd_ref[c], kdt_ref[c, 0], kdt_ref[c, 1], egl_ref[c] = vals

    def local_chain(c):
        vals = yield from local_compute(*local_load(c))
        local_store(c, vals)

    def state_chain(chunks):
        def unpair(r):
            return [r[:, :A_DK], r[:, A_DK:]]

        def state_update(s_old, kdt, vn_pair, egl_pair):
            upd = _dot(kdt, jnp.concatenate([vn_pair, zeros], axis=0).astype(BF16))
            return s_old * egl_pair + jnp.where(pair_mask, upd, 0.0)

        s01 = s01_ref[...]
        s23 = s23_ref[...]
        for c in chunks:
            r0 = c * CHUNK if isinstance(c, int) else pl.multiple_of(c * CHUNK, CHUNK)
            r01 = _dot(wq_ref[c, 0], s01.astype(BF16))
            r23 = _dot(wq_ref[c, 1], s23.astype(BF16))
            yield
            ws = jnp.concatenate(unpair(r01[:CHUNK]) + unpair(r23[:CHUNK]), axis=0)
            qs = jnp.concatenate(unpair(r01[CHUNK:]) + unpair(r23[CHUNK:]), axis=0)
            vn = u_ref[c] - ws
            o_rs = qs + _dot(qkd_ref[c], vn.astype(BF16))
            egl = egl_ref[c][0:1, :]
            s01 = state_update(s01, kdt_ref[c, 0], pair_lanes(vn, 0, 1), egl[:, :2 * A_DK])
            s23 = state_update(s23, kdt_ref[c, 1], pair_lanes(vn, 2, 3), egl[:, 2 * A_DK:])
            yield
            zz = z_ref[0, pl.ds(r0, CHUNK), :]
            parts = [_rms(o_rs[hd * CHUNK:(hd + 1) * CHUNK], ng) for hd in range(A_HEADS)]
            out_ref[0, pl.ds(r0, CHUNK), :] = jnp.concatenate(parts, axis=1) * _silu(zz)
        s01_ref[...] = s01
        s23_ref[...] = s23

    def run_lockstep(chains):
        live = list(chains)
        while live:
            for gen in list(live):
                try:
                    next(gen)
                except StopIteration:
                    live.remove(gen)

    nc = rb // CHUNK
    npairs = nc // 2
    if npairs == 0:
        run_lockstep([local_chain(0)])
        run_lockstep([state_chain([0])])
    else:
        run_lockstep([local_chain(0), local_chain(1)])

        def steady(i, carry):
            run_lockstep([local_chain(2 * i), local_chain(2 * i + 1), state_chain([2 * i - 2, 2 * i - 1])])
            return carry

        lax.fori_loop(1, npairs, steady, 0)
        last = [2 * npairs - 2, 2 * npairs - 1]
        if nc % 2:
            run_lockstep([local_chain(nc - 1), state_chain(last)])
            run_lockstep([state_chain([nc - 1])])
        else:
            run_lockstep([state_chain(last)])
    xe_ref[0:8, :] = xe_ref[rb:rb + 8, :]


def _gdn_call(qkv, z, sm, cw, alog, dtb, ng, lt3, expand, rb):
    bsz, tp, _ = qkv.shape
    nt = tp // rb
    nc = rb // CHUNK
    hw = A_HEADS * A_DK
    pw = A_HEADS * CHUNK
    row = lambda w_: pl.BlockSpec((1, rb, w_), lambda b, t: (b, t, 0))
    return pl.pallas_call(
        functools.partial(_gdn_body, rb=rb),
        grid=(bsz, nt),
        in_specs=[row(3 * hw), row(hw), row(LANES), _const_spec(cw.shape), _const_spec((1, LANES)),
                  _const_spec((1, LANES)), _const_spec((1, LANES)), _const_spec(lt3.shape),
                  _const_spec(expand.shape)],
        out_specs=row(hw),
        out_shape=jax.ShapeDtypeStruct((bsz, tp, hw), F32),
        scratch_shapes=[pltpu.VMEM((rb + 8, 3 * hw), F32),
                        pltpu.VMEM((2 * A_DK, 2 * A_DK), F32), pltpu.VMEM((2 * A_DK, 2 * A_DK), F32),
                        pltpu.VMEM((nc, pw, A_DK), F32), pltpu.VMEM((nc, 2, 2 * CHUNK, 2 * A_DK), BF16),
                        pltpu.VMEM((nc, pw, pw), BF16), pltpu.VMEM((nc, 2, 2 * A_DK, 2 * CHUNK), BF16),
                        pltpu.VMEM((nc, 8, hw), F32)],
        compiler_params=_params(("parallel", "arbitrary")),
        name="gdn",
    )(qkv, z, sm, cw, alog, dtb, ng, lt3, expand)


def _dsa_body(qr_ref, kr_ref, v_ref, iqr_ref, ike_ref, iko_ref, sm_ref, lstrict_ref, out_ref,
              vt_ref, vmt_ref, skey_ref, planes_ref, sel_ref, st_ref, ot_ref, *, seq, topk):
    g = pl.program_id(1)
    iw_scale = IDX_HEADS ** -0.5 * IDX_DIM ** -0.5
    qpair = 2 * CHUNK
    neg_inf = -jnp.inf

    @pl.when(g == 0)
    def _():
        planes_ref[...] = jnp.zeros_like(planes_ref)

        def vblk(i, carry):
            r = pl.multiple_of(CHUNK + i * LANES, CHUNK)
            vt_ref[i] = v_ref[0, pl.ds(r, LANES), :].astype(F32).T.astype(BF16)
            return carry

        lax.fori_loop(0, seq // LANES, vblk, 0)
        vmt_ref[...] = v_ref[0, 0:LANES, :].astype(F32).T.astype(BF16)

        qm = qr_ref[0, 0:CHUNK, :]
        km = kr_ref[0, 0:CHUNK, :]
        vm = v_ref[0, 0:CHUNK, :]
        colv = lax.broadcasted_iota(I32, (1, CHUNK), 1) >= PAD
        parts = []
        for hd in range(B_HEADS):
            s = _dot_nt(qm[:, hd * B_DH:(hd + 1) * B_DH], km)
            s = jnp.where(colv, s, neg_inf)
            e = jnp.exp(s - jnp.max(s, axis=-1, keepdims=True))
            pr = e / jnp.sum(e, axis=-1, keepdims=True)
            parts.append(_dot(pr.astype(BF16), vm))
        out_ref[0, 0:CHUNK, :] = jnp.concatenate(parts, axis=1)

    lane = lax.broadcasted_iota(I32, (1, LANES), 1)
    groups = []
    for j in range(QUERY_GROUPS):
        p = g * QUERY_GROUPS + j
        r0 = pl.multiple_of(CHUNK + p * qpair, CHUNK)
        qb = qr_ref[0, pl.ds(r0, qpair), :]
        ib = iqr_ref[0, pl.ds(r0, qpair), :]
        smt = sm_ref[0, pl.ds(r0, qpair), :].T
        groups.append(dict(
            j=j, r0=r0,
            q_hq=jnp.concatenate([qb[:, hd * B_DH:(hd + 1) * B_DH] for hd in range(B_HEADS)], axis=0),
            iq_hq=jnp.concatenate([ib[:, hp * LANES:(hp + 1) * LANES] for hp in range(IDX_HEADS // 2)], axis=0),
            wts=[smt[2 * A_HEADS + hd:2 * A_HEADS + hd + 1, :] * iw_scale for hd in range(IDX_HEADS)],
            limit=jnp.where(lane < CHUNK, (2 * p + 1) * CHUNK, (2 * p + 2) * CHUNK)))
    assert QUERY_GROUPS * qpair == KEY_CHUNK
    full_rows = [KEY_CHUNK] * QUERY_GROUPS
    diag_rows = [(j + 1) * qpair for j in range(QUERY_GROUPS)]
    diag_plane_rows = [-(-r // PLANE_ROWS) * PLANE_ROWS for r in diag_rows]
    krows = lax.broadcasted_iota(I32, (KEY_CHUNK, 1), 0)

    def score_chunk(kc, carry, nrows=full_rows):
        k0 = pl.multiple_of(CHUNK + kc * KEY_CHUNK, CHUNK)
        base = pl.multiple_of(kc * KEY_CHUNK, KEY_CHUNK)
        ke, ko, kk = (ref[0, pl.ds(k0, KEY_CHUNK), :] for ref in (ike_ref, iko_ref, kr_ref))
        logits = []
        for grp_, nr in zip(groups, nrows):
            logits.append((_dot_nt(ke[:nr], grp_["iq_hq"]), _dot_nt(ko[:nr], grp_["iq_hq"])))
            st_ref[grp_["j"], pl.ds(base, nr), :] = _dot_nt(kk[:nr], grp_["q_hq"])
        for grp_, nr, (le, lo) in zip(groups, nrows, logits):
            score_tail(kc, pl.ds(base, nr), nr, grp_, le, lo)
        return carry

    def score_tail(kc, rows, nr, grp_, le, lo):
        j, wts, limit = grp_["j"], grp_["wts"], grp_["limit"]
        sc = jnp.zeros((nr, LANES), F32)
        for hp in range(IDX_HEADS // 2):
            sc = sc + wts[2 * hp] * jnp.maximum(le[:, hp * LANES:(hp + 1) * LANES], 0.0)
            sc = sc + wts[2 * hp + 1] * jnp.maximum(lo[:, hp * LANES:(hp + 1) * LANES], 0.0)
        bits = lax.bitcast_convert_type(sc, I32)
        bits = jnp.where(sc == 0.0, 0, bits)
        key = bits ^ ((bits >> 31) & 0x7FFFFFFF)
        valid = kc * KEY_CHUNK + krows[:nr] < limit
        key = jnp.where(valid, key, INT_MIN)
        skey_ref[j, rows, :] = key
        for gi in range(nr // PLANE_ROWS):
            a = [key[gi * PLANE_ROWS + 8 * r:gi * PLANE_ROWS + 8 * r + 8, :] ^ INT_MIN for r in range(32)]
            sh, msk = 16, 0x0000FFFF
            while sh:
                k = 0
                while k < 32:
                    tt = (a[k] ^ lax.shift_right_logical(a[k + sh], jnp.int32(sh))) & msk
                    a[k] = a[k] ^ tt
                    a[k + sh] = a[k + sh] ^ (tt << sh)
                    k = (k + sh + 1) & ~sh
                sh >>= 1
                msk = (msk ^ (msk << sh)) & 0xFFFFFFFF
            g8 = pl.multiple_of((kc * (KEY_CHUNK // PLANE_ROWS) + gi) * 8, 8)
            for b in range(32):
                planes_ref[j, b, pl.ds(g8, 8), :] = a[b]

    lax.fori_loop(0, g, score_chunk, 0)
    score_chunk(g, 0, diag_plane_rows)

    words_per_chunk = KEY_CHUNK // 32

    def lane_count(words):
        c = lax.population_count(words)
        if words.shape[0] > 8:
            c = jnp.sum(c.reshape(words.shape[0] // 8, 8, LANES), axis=0)
        return jnp.sum(c, axis=0, keepdims=True)

    def select_threshold(step):
        nrow = (step + 1) * words_per_chunk
        prow = lax.broadcasted_iota(I32, (nrow, 1), 0)
        alive0 = [jnp.where(prow < step * words_per_chunk + nr // 32, jnp.full((nrow, LANES), -1, I32), 0)
                  for nr in diag_plane_rows]

        def bit_body(i, carry):
            bit = jnp.left_shift(jnp.int32(1), 31 - i)
            out = []
            for grp_, (tu, above, alive) in zip(groups, carry):
                hit = alive & planes_ref[grp_["j"], i, 0:nrow, :]
                c1 = lane_count(hit)
                take = above + c1 >= topk
                out.append((jnp.where(take, tu | bit, tu), jnp.where(take, above, above + c1),
                            jnp.where(take, hit, alive ^ hit)))
            return tuple(out)

        zero_row = jnp.zeros((1, LANES), I32)
        result = lax.fori_loop(0, 32, bit_body, tuple((zero_row, zero_row, a0) for a0 in alive0))
        for grp_, (tu, above, alive) in zip(groups, result):
            for n, row in enumerate((tu, above, lane_count(alive))):
                sel_ref[grp_["j"], n] = jnp.broadcast_to(row, (8, LANES))

    for step in range(seq // KEY_CHUNK):
        pl.when(g == step)(functools.partial(select_threshold, step))

    for grp_, nr_diag in zip(groups, diag_rows):
        tu, above, n_eq = (sel_ref[grp_["j"], n][0:1] for n in range(3))
        j, thr, found = grp_["j"], tu ^ INT_MIN, tu != 0
        grp_["thr"] = thr
        tie_lane = found & (above + n_eq > topk)

        @pl.when(jnp.max(jnp.where(tie_lane, 1, 0)) > 0)
        def _(j=j, thr=thr, found=found, above=above, nr_diag=nr_diag):
            need = (topk - above).astype(F32)

            def body(kc, seen, nr=KEY_CHUNK):
                base = pl.multiple_of(kc * KEY_CHUNK, KEY_CHUNK)
                for sub in range(nr // LANES):
                    rows = pl.ds(base + sub * LANES, LANES)
                    blk = skey_ref[j, rows, :]
                    tied = (blk == thr) & found
                    eq = jnp.where(tied, 1.0, 0.0)
                    rank = _dot(lstrict_ref[...], eq.astype(BF16)) + seen
                    skey_ref[j, rows, :] = jnp.where(tied & (rank >= need), INT_MIN, blk)
                    seen = seen + jnp.sum(eq, axis=0, keepdims=True)
                return seen

            body(g, lax.fori_loop(0, g, body, jnp.zeros((1, LANES), F32)), nr_diag)

    mrows = lax.broadcasted_iota(I32, (LANES, 1), 0)
    k_meta = kr_ref[0, 0:LANES, :]
    ones = jnp.ones((ONES_ROWS, LANES), BF16)
    v_meta = jnp.concatenate([vmt_ref[...], ones], axis=0)
    init = []
    for grp_ in groups:
        s_meta = _dot_nt(k_meta, grp_["q_hq"])
        s_meta = jnp.where((mrows >= PAD) & (mrows < CHUNK), s_meta, neg_inf)
        m0 = jnp.max(s_meta, axis=0, keepdims=True)
        ot_ref[grp_["j"]] = _dot(v_meta, jnp.exp(s_meta - m0).astype(BF16))
        init.append(m0)
        grp_["thr"] = jnp.maximum(grp_["thr"], INT_MIN + 1)

    def pv_chunk(kc, carry, nrows=full_rows):
        base = pl.multiple_of(kc * KEY_CHUNK, KEY_CHUNK)
        nblk = KEY_CHUNK // LANES
        vblk = jnp.concatenate([jnp.concatenate([vt_ref[nblk * kc + i], ones], axis=0) for i in range(nblk)], axis=1)
        out, probs = [], []
        for grp_, nr, m in zip(groups, nrows, carry):
            rows = pl.ds(base, nr)
            bias = jnp.where(skey_ref[grp_["j"], rows, :] >= grp_["thr"], 0.0, neg_inf)
            s = st_ref[grp_["j"], rows, :] + jnp.concatenate([bias] * B_HEADS, axis=1)
            m_new = jnp.maximum(m, jnp.max(jnp.max(s.reshape(nr // 8, 8, B_HEADS * LANES), axis=0),
                                           axis=0, keepdims=True))
            out.append(m_new)
            probs.append((jnp.exp(m - m_new), jnp.exp(s - m_new).astype(BF16)))
        for grp_, nr, (alpha, prb) in zip(groups, nrows, probs):
            ot_ref[grp_["j"]] = ot_ref[grp_["j"]] * alpha + _dot(vblk[:, :nr], prb)
        return tuple(out)

    pv_chunk(g, lax.fori_loop(0, g, pv_chunk, tuple(init)), diag_rows)
    for grp_ in groups:
        acc = ot_ref[grp_["j"]]
        o_hq = (acc[:B_DH] / acc[B_DH:B_DH + 1]).T
        for hd in range(B_HEADS):
            out_ref[0, pl.ds(grp_["r0"], qpair), hd * B_DH:(hd + 1) * B_DH] = o_hq[hd * LANES:(hd + 1) * LANES, :]


def _dsa_call(qr, kr, v, iqr, ike, iko, sm, lstrict, seq, topk):
    bsz, tp, _ = qr.shape
    nsteps, rem = divmod(seq, 2 * CHUNK * QUERY_GROUPS)
    assert rem == 0
    full = lambda w_: pl.BlockSpec((1, tp, w_), lambda b, p: (b, 0, 0))
    hw = B_HEADS * B_DH
    nq = QUERY_GROUPS
    return pl.pallas_call(
        functools.partial(_dsa_body, seq=seq, topk=topk),
        grid=(bsz, nsteps),
        in_specs=[full(hw), full(B_DH), full(B_DH), full(hw), full(LANES), full(LANES), full(LANES),
                  _const_spec(lstrict.shape)],
        out_specs=full(hw),
        out_shape=jax.ShapeDtypeStruct((bsz, tp, hw), F32),
        scratch_shapes=[pltpu.VMEM((seq // LANES, B_DH, LANES), BF16), pltpu.VMEM((B_DH, LANES), BF16),
                        pltpu.VMEM((nq, seq, LANES), I32), pltpu.VMEM((nq, 32, seq // 32, LANES), I32),
                        pltpu.VMEM((nq, 3, 8, LANES), I32),
                        pltpu.VMEM((nq, seq, hw), F32), pltpu.VMEM((nq, B_DH + ONES_ROWS, hw), F32)],
        compiler_params=_params(("parallel", "arbitrary")),
        name="dsa",
    )(qr, kr, v, iqr, ike, iko, sm, lstrict)


def _mix_body(h_ref, oa_ref, ob_ref, g_ref, wg_ref, bg_ref, wa_ref, wb_ref, wo_ref, out_ref, *, rb):
    x = h_ref[0]
    d = x.shape[-1]
    n = _rms(x, g_ref[...]).astype(BF16)
    gates = _sigmoid(_dot(n, wg_ref[...]) + bg_ref[...])
    y = gates[:, :d] * _dot(oa_ref[0].astype(BF16), wa_ref[...]) + gates[:, d:] * _dot(ob_ref[0].astype(BF16), wb_ref[...])
    out = x + _dot(y.astype(BF16), wo_ref[...])
    rows = pl.program_id(1) * rb + lax.broadcasted_iota(I32, (rb, 1), 0)
    out_ref[0] = jnp.where(rows >= PAD, out, 0.0)


def _mix_call(h, oa, ob, gain, wg, bg, wa, wb, wo, rb):
    bsz, tp, d = h.shape
    row = lambda w_: pl.BlockSpec((1, rb, w_), lambda b, t: (b, t, 0))
    return pl.pallas_call(
        functools.partial(_mix_body, rb=rb),
        grid=(bsz, tp // rb),
        in_specs=[row(d), row(oa.shape[-1]), row(ob.shape[-1]), _const_spec((1, d)), _const_spec(wg.shape),
                  _const_spec(bg.shape), _const_spec(wa.shape), _const_spec(wb.shape), _const_spec(wo.shape)],
        out_specs=row(d),
        out_shape=jax.ShapeDtypeStruct((bsz, tp, d), F32),
        compiler_params=_params(("parallel", "arbitrary")),
        name="mix",
    )(h, oa, ob, gain, wg, bg, wa, wb, wo)


def _ffn_body(h_ref, g_ref, wup_ref, cw_ref, wd_ref, out_ref, carry_ref, act_ref, *, rb, dff):
    t = pl.program_id(1)

    @pl.when(t == 0)
    def _():
        carry_ref[...] = jnp.zeros_like(carry_ref)

    x = h_ref[0]
    n = _rms(x, g_ref[...]).astype(BF16)
    rows = lax.broadcasted_iota(I32, (rb, 1), 0)
    nch = dff // FF_CHUNK

    def gate_up(c):
        return (_dot(n, wup_ref[:, c * FF_CHUNK:(c + 1) * FF_CHUNK]),
                _dot(n, wup_ref[:, dff + c * FF_CHUNK:dff + (c + 1) * FF_CHUNK]))

    ahead = gate_up(0)
    for c in range(nch):
        cols = slice(c * FF_CHUNK, (c + 1) * FF_CHUNK)
        gate, up = ahead
        if c + 1 < nch:
            ahead = gate_up(c + 1)
        prev = carry_ref[:, cols]
        g1 = jnp.where(rows == 0, prev[7:8], pltpu.roll(gate, 1, 0))
        g2 = jnp.where(rows == 0, prev[6:7], jnp.where(rows == 1, prev[7:8], pltpu.roll(gate, 2, 0)))
        cw = cw_ref[:, cols]
        conv = g2 * cw[0:1] + g1 * cw[1:2] + gate * cw[2:3]
        carry_ref[:, cols] = gate[rb - 8:rb]
        act_ref[:, cols] = (_silu(conv) * up).astype(BF16)
    out = x + _dot(act_ref[...], wd_ref[...])
    out_ref[0] = jnp.where(t * rb + rows >= PAD, out, 0.0)


def _ffn_call(h, gain, wup, cw, wd, rb):
    bsz, tp, d = h.shape
    dff = wd.shape[0]
    row = pl.BlockSpec((1, rb, d), lambda b, t: (b, t, 0))
    return pl.pallas_call(
        functools.partial(_ffn_body, rb=rb, dff=dff),
        grid=(bsz, tp // rb),
        in_specs=[row, _const_spec((1, d)), _const_spec(wup.shape), _const_spec(cw.shape), _const_spec(wd.shape)],
        out_specs=row,
        out_shape=jax.ShapeDtypeStruct((bsz, tp, d), F32),
        scratch_shapes=[pltpu.VMEM((8, dff), F32), pltpu.VMEM((rb, dff), BF16)],
        compiler_params=_params(("parallel", "arbitrary")),
        name="ffn",
    )(h, gain, wup, cw, wd)


def _rope_tables(tp):
    pos = (jnp.arange(tp, dtype=F32) - PAD)[:, None]

    def cs(dim):
        inv = 1.0 / (ROPE_THETA ** (jnp.arange(0, dim, 2, dtype=F32) / dim))
        ang = pos * inv[None, :]
        return jnp.cos(ang), jnp.sin(ang)

    ca, sa = cs(B_DH)
    ci, si = cs(IDX_DIM)
    zi = jnp.zeros_like(si)
    return (jnp.concatenate([ca, ca], axis=1), jnp.concatenate([-sa, sa], axis=1),
            jnp.concatenate([ci] * 4, axis=1), jnp.concatenate([-si, zi, -si, zi], axis=1),
            jnp.concatenate([zi, si, zi, si], axis=1))


def _pad_lanes(vec):
    return jnp.zeros((1, LANES), F32).at[0, :vec.shape[0]].set(vec.astype(F32))


def _head_expander():
    pw, hw = A_HEADS * CHUNK, A_HEADS * A_DK
    head = np.concatenate([np.arange(pw) // CHUNK, np.arange(hw) // A_DK])
    return jnp.asarray(np.arange(LANES)[:, None] == head[None, :], BF16)


def kernel(x, meta_tokens, norm_mix, w_in, conv_a, a_log, dt_bias, a_out_norm, q_norm, k_norm, kidx_norm,
           w_branch_a, w_branch_b, w_gate, b_gate, w_out, norm_ffn, w_up, conv_ffn, w_down):
    bsz, seq, d = x.shape
    depth = w_in.shape[0]
    assert seq % KEY_CHUNK == 0
    tp = CHUNK + seq
    rb = _row_block(tp // CHUNK)
    topk = min(TOPK_MAX, seq // TOPK_DIV)

    meta = jnp.broadcast_to(meta_tokens.astype(x.dtype)[None], (bsz, N_META, d))
    h = jnp.concatenate([jnp.zeros((bsz, PAD, d), x.dtype), meta, x], axis=1)

    tables = _rope_tables(tp)
    expand = _head_expander()
    tri = np.arange(CHUNK)
    lt3 = jnp.asarray(np.tile(tri[:, None] >= tri[None, :], (1, 3)), BF16)
    trk = np.arange(LANES)
    lstrict = jnp.asarray(trk[:, None] > trk[None, :], BF16)

    hw = A_HEADS * A_DK
    o = np.cumsum((0, hw, hw, hw, hw, A_HEADS, A_HEADS, B_HEADS * B_DH, B_DH, B_DH,
                   IDX_HEADS * IDX_DIM, IDX_DIM, IDX_HEADS))
    for l in range(depth):
        w = w_in[l]
        small = jnp.concatenate([w[:, o[4]:o[6]], w[:, o[11]:o[12]],
                                 jnp.zeros((d, LANES - 2 * A_HEADS - IDX_HEADS), w.dtype)], axis=1)
        w_p = jnp.concatenate([w[:, o[0]:o[3]], w[:, o[3]:o[4]], small, w[:, o[6]:o[7]], w[:, o[7]:o[9]],
                               w[:, o[9]:o[10]], w[:, o[10]:o[11]], w[:, o[10]:o[11]]], axis=1).astype(BF16)
        qkv, z, sm, qr, kr, v, iqr, ike, iko = _proj_call(
            h, norm_mix[l][None], w_p, tables, q_norm[l][None], k_norm[l][None],
            jnp.concatenate([kidx_norm[l], kidx_norm[l]])[None], rb)
        o_a = _gdn_call(qkv, z, sm, conv_a[l], _pad_lanes(a_log[l]), _pad_lanes(dt_bias[l]),
                        a_out_norm[l][None], lt3, expand, rb)
        o_b = _dsa_call(qr, kr, v, iqr, ike, iko, sm, lstrict, seq, topk)
        h = _mix_call(h, o_a, o_b, norm_mix[l][None], w_gate[l].astype(BF16), b_gate[l][None],
                      w_branch_a[l].astype(BF16), w_branch_b[l].astype(BF16), w_out[l].astype(BF16), rb)
        h = _ffn_call(h, norm_ffn[l][None], w_up[l].astype(BF16), conv_ffn[l], w_down[l].astype(BF16), rb)
    return h[:, CHUNK:]
```

```python
import functools
import math

import jax
import jax.numpy as jnp
import numpy as np
from jax import lax
from jax.experimental import pallas as pl
from jax.experimental.pallas import tpu as pltpu

F32 = jnp.float32
BF16 = jnp.bfloat16
I32 = jnp.int32

CHUNK = 64
N_META = 16
PAD = CHUNK - N_META
ROPE_THETA = 10000.0
EPS = 1e-6
A_HEADS = 4
A_DK = 128
A_CONV = 4
B_HEADS = 4
B_DH = 128
IDX_HEADS = 8
IDX_DIM = 64
TOPK_MAX = 256
TOPK_DIV = 4
FFN_CONV = 3
LANES = 128
KEY_CHUNK = 512
PLANE_ROWS = 256
QUERY_GROUPS = 4
ONES_ROWS = 16
FF_CHUNK = 256
INT_MIN = -(2 ** 31)
VMEM_LIMIT_BYTES = 56 * 1024 * 1024

NT_DIMS = (((1,), (1,)), ((), ()))


def _dot(a, b):
    return jnp.dot(a, b, preferred_element_type=F32)


def _dot_nt(a, b):
    return lax.dot_general(a, b, NT_DIMS, preferred_element_type=F32)


def _dot_exact(a, b):
    return jnp.dot(a, b, preferred_element_type=F32, precision=lax.Precision.HIGHEST)


def _rms(x, gain):
    return x * lax.rsqrt(jnp.mean(x * x, axis=-1, keepdims=True) + EPS) * gain


def _sigmoid(x):
    return 1.0 / (1.0 + jnp.exp(-x))


def _silu(x):
    return x * _sigmoid(x)


def _softplus(x):
    return jnp.maximum(x, 0.0) + jnp.log(1.0 + jnp.exp(-jnp.abs(x)))


def _row_block(n_chunks):
    g = max(d for d in range(1, 12) if n_chunks % d == 0)
    return g * CHUNK


def _params(sem):
    return pltpu.CompilerParams(dimension_semantics=sem, vmem_limit_bytes=VMEM_LIMIT_BYTES)


def _const_spec(shape):
    nd = len(shape)
    return pl.BlockSpec(shape, lambda *_: (0,) * nd)


PROJ_WIDTHS = (1536, 512, 512, 256, 512, 256)


def _proj_body(h_ref, g_ref, w_ref, cosa_ref, sina_ref, cosi_ref, sinlo_ref, sinhi_ref,
               qn_ref, kn_ref, kin_ref, cw_ref,
               qkv_ref, z_ref, sm_ref, qr_ref, kr_ref, v_ref, iqr_ref, ike_ref, iko_ref, tail_ref, *, rb):
    @pl.when(pl.program_id(1) == 0)
    def _():
        tail_ref[...] = jnp.zeros_like(tail_ref)

    n = _rms(h_ref[0], g_ref[...]).astype(BF16)
    offs = dict(zip(("qkv", "z", "q", "kv", "iq", "sm_ik"), np.cumsum((0,) + PROJ_WIDTHS)[:-1]))
    rows8 = lax.broadcasted_iota(I32, (8, 1), 0)
    lane = lax.broadcasted_iota(I32, (1, LANES), 1)
    ca, sa = cosa_ref[...], sina_ref[...]
    ci, slo, shi = cosi_ref[...], sinlo_ref[...], sinhi_ref[...]

    def conv_silu(x, cols, r0, nrows):
        prev, cw = (tail_ref[:, cols] if r0 == 0 else x[r0 - 8:r0]), cw_ref[:, cols]
        xb = x[r0:r0 + nrows]
        acc = None
        for back in range(A_CONV - 1, 0, -1):
            shifted = pltpu.roll(xb, back, 0)
            top = shifted[:8]
            for r in range(back):
                top = jnp.where(rows8 == r, prev[8 - back + r:8 - back + r + 1], top)
            term = jnp.concatenate([top, shifted[8:]], axis=0) * cw[A_CONV - 1 - back:A_CONV - back]
            acc = term if acc is None else acc + term
        return _silu(acc + xb * cw[A_CONV - 1:A_CONV])

    half = rb // 2
    assert half % 8 == 0

    def gdn_piece(i):
        cols = slice(i * 256, (i + 1) * 256)

        def finish(x):
            def first():
                qkv_ref[0, 0:half, cols] = conv_silu(x, cols, 0, half)

            def second():
                qkv_ref[0, half:rb, cols] = conv_silu(x, cols, half, half)
                tail_ref[:, cols] = x[rb - 8:rb]
            return [first, second]
        return offs["qkv"] + i * 256, finish

    def plain_piece(i):
        def finish(x):
            def store():
                z_ref[0, :, i * 256:(i + 1) * 256] = x
            return [store]
        return offs["z"] + i * 256, finish

    def single(fn):
        return lambda x: [functools.partial(fn, x)]

    def q_piece(i):
        @single
        def finish(x):
            parts = []
            for hd in range(2):
                qh = _rms(x[:, hd * B_DH:(hd + 1) * B_DH], qn_ref[...])
                parts.append(qh * ca + pltpu.roll(qh, B_DH // 2, 1) * sa)
            qr_ref[0, :, i * 256:(i + 1) * 256] = (jnp.concatenate(parts, axis=1) * (B_DH ** -0.5)).astype(BF16)
        return offs["q"] + i * 256, finish

    def kv_piece():
        @single
        def finish(x):
            k = _rms(x[:, :B_DH], kn_ref[...])
            kr_ref[0] = (k * ca + pltpu.roll(k, B_DH // 2, 1) * sa).astype(BF16)
            v_ref[0] = x[:, B_DH:].astype(BF16)
        return offs["kv"], finish

    def iq_piece(i):
        @single
        def finish(x):
            parts = []
            for hp in range(2):
                xh = x[:, hp * LANES:(hp + 1) * LANES]
                parts.append(xh * ci + pltpu.roll(xh, LANES - IDX_DIM // 2, 1) * slo
                             + pltpu.roll(xh, IDX_DIM // 2, 1) * shi)
            iqr_ref[0, :, i * 256:(i + 1) * 256] = jnp.concatenate(parts, axis=1).astype(BF16)
        return offs["iq"] + i * 256, finish

    def small_ik_piece():
        @single
        def finish(x):
            sm_ref[0] = x[:, :LANES]
            ik = _rms(x[:, LANES:], kin_ref[...])
            ik = ik * ci + pltpu.roll(ik, IDX_DIM // 2, 1) * (slo + shi)
            ike_ref[0] = jnp.where(lane < IDX_DIM, ik, 0.0).astype(BF16)
            iko_ref[0] = jnp.where(lane >= IDX_DIM, ik, 0.0).astype(BF16)
        return offs["sm_ik"], finish

    pieces = [q_piece(0), gdn_piece(0), q_piece(1), gdn_piece(1), iq_piece(0), gdn_piece(2), iq_piece(1),
              gdn_piece(3), kv_piece(), gdn_piece(4), small_ik_piece(), gdn_piece(5), plain_piece(0), plain_piece(1)]
    ready, staged = [], []
    for col, finish in pieces:
        x = _dot(n, w_ref[:, col:col + 256])
        ready += staged
        staged = finish(x)
        for task in [ready.pop(0) for _ in range(min(2, len(ready)))]:
            task()
    for task in ready + staged:
        task()


def _proj_call(h, gain, w, tables, qn, kn, kin, cw, rb):
    bsz, tp, d = h.shape
    nt = tp // rb
    row = lambda w_, dt: (jax.ShapeDtypeStruct((bsz, tp, w_), dt),
                          pl.BlockSpec((1, rb, w_), lambda b, t: (b, t, 0)))
    outs = [row(1536, F32), row(512, F32), row(128, F32), row(512, BF16), row(128, BF16),
            row(128, BF16), row(512, BF16), row(128, BF16), row(128, BF16)]
    tab_spec = pl.BlockSpec((rb, LANES), lambda b, t: (t, 0))
    return pl.pallas_call(
        functools.partial(_proj_body, rb=rb),
        grid=(bsz, nt),
        in_specs=[pl.BlockSpec((1, rb, d), lambda b, t: (b, t, 0)), _const_spec((1, d)), _const_spec(w.shape)]
                 + [tab_spec] * 5 + [_const_spec((1, LANES))] * 3 + [_const_spec(cw.shape)],
        out_specs=[o[1] for o in outs],
        out_shape=[o[0] for o in outs],
        scratch_shapes=[pltpu.VMEM((8, PROJ_WIDTHS[0]), F32)],
        compiler_params=_params(("parallel", "arbitrary")),
        name="proj",
    )(h, gain, w, *tables, qn, kn, kin, cw)


def _split3(x):
    x1 = x.astype(BF16)
    r1 = x - x1.astype(F32)
    x2 = r1.astype(BF16)
    x3 = (r1 - x2.astype(F32)).astype(BF16)
    return jnp.concatenate([x1, x2, x3], axis=0)


def _gdn_body(qkv_ref, z_ref, sm_ref, alog_ref, dtb_ref, ng_ref, lt3_ref, expand_ref,
              out_ref, s01_ref, s23_ref, u_ref, wq_ref, qkd_ref, kdt_ref, egl_ref, *, rb):
    t = pl.program_id(1)

    @pl.when(t == 0)
    def _():
        s01_ref[...] = jnp.zeros_like(s01_ref)
        s23_ref[...] = jnp.zeros_like(s23_ref)

    hw = A_HEADS * A_DK
    pw = A_HEADS * CHUNK
    ii = lax.broadcasted_iota(I32, (CHUNK, pw), 0)
    jj = lax.broadcasted_iota(I32, (CHUNK, pw), 1) & (CHUNK - 1)
    colhead = lax.broadcasted_iota(I32, (CHUNK, pw), 1) >> 6
    eye_p = ii == jj
    bd_mask = (lax.broadcasted_iota(I32, (pw, pw), 0) >> 6) == (lax.broadcasted_iota(I32, (pw, pw), 1) >> 6)
    bdk_mask = (lax.broadcasted_iota(I32, (pw, hw), 0) >> 6) == (lax.broadcasted_iota(I32, (pw, hw), 1) >> 7)
    pair_mask = (lax.broadcasted_iota(I32, (pw, pw), 0) >> 7) == (lax.broadcasted_iota(I32, (pw, pw), 1) >> 7)
    lane = lax.broadcasted_iota(I32, (1, LANES), 1)
    rows64 = lax.broadcasted_iota(I32, (CHUNK, 1), 0)
    neg_a = -jnp.exp(alog_ref[...])
    dtb = dtb_ref[...]
    ng = ng_ref[...]

    def block_diag(xp):
        return jnp.where(bd_mask, jnp.concatenate([xp.astype(BF16)] * A_HEADS, axis=0), 0)

    def stack_heads(xp):
        return jnp.concatenate([jnp.where(colhead == hd, xp, 0.0) for hd in range(A_HEADS)], axis=0)

    def l2n(x):
        parts = []
        for hd in range(A_HEADS):
            xh = x[:, hd * A_DK:(hd + 1) * A_DK]
            parts.append(xh * lax.rsqrt(jnp.sum(xh * xh, axis=-1, keepdims=True) + EPS))
        return jnp.concatenate(parts, axis=1)

    zeros = jnp.zeros((CHUNK, 2 * A_DK), F32)

    def pair_lanes(x, a_, b_):
        return jnp.concatenate([x[a_ * CHUNK:(a_ + 1) * CHUNK], x[b_ * CHUNK:(b_ + 1) * CHUNK]], axis=1)

    def local_load(c):
        r0 = c * CHUNK if isinstance(c, int) else pl.multiple_of(c * CHUNK, CHUNK)
        return qkv_ref[0, pl.ds(r0, CHUNK), :], sm_ref[0, pl.ds(r0, CHUNK), :], r0

    def local_compute(xq, sm, r0):
        q = l2n(xq[:, :hw]) * (A_DK ** -0.5)
        k = l2n(xq[:, hw:2 * hw])
        v = xq[:, 2 * hw:]

        g = jnp.where(lane < A_HEADS, neg_a * _softplus(sm + dtb), 0.0)
        g = jnp.where(t * rb + r0 + rows64 >= PAD, g, 0.0)
        to_packed, to_wide = expand_ref[:, :pw], expand_ref[:, pw:]
        g3 = _split3(g)
        cg = _dot(g3, to_packed).astype(BF16)
        gcs = _dot(lt3_ref[...], g3)
        yield
        gp = _dot(lt3_ref[...], cg)
        in_head = lane < A_HEADS
        beta = jnp.where(in_head, pltpu.roll(_sigmoid(sm), LANES - A_HEADS, 1), 0.0)
        eg_s = jnp.where(in_head, jnp.exp(gcs), 0.0)
        to_last = jnp.where(in_head, jnp.exp(gcs[CHUNK - 1:CHUNK] - gcs), 0.0)
        wide = _dot(jnp.concatenate([beta, beta * eg_s, eg_s, to_last], axis=0).astype(BF16), to_wide)
        bp = _dot(beta.astype(BF16), to_packed)
        glast3 = _dot(_split3(gcs[CHUNK - 16:]), to_wide)
        yield
        bq, bq_eg, eg, kd_scale = (wide[i * CHUNK:(i + 1) * CHUNK] for i in range(4))
        glast = glast3[15:16] + glast3[31:32] + glast3[47:48]
        grow = jnp.sum(jnp.where(eye_p, gp, 0.0), axis=0, keepdims=True)
        decay = jnp.where(ii >= jj, jnp.exp(gp - grow), 0.0)

        kb = k.astype(BF16)
        bdk = jnp.where(bdk_mask, jnp.concatenate([kb] * A_HEADS, axis=0), 0)
        kq = _dot_nt(jnp.concatenate([kb, q.astype(BF16)], axis=0), bdk)
        yield
        kk_p, qk_p = kq[:CHUNK], kq[CHUNK:]

        a = -jnp.where(ii > jj, bp * kk_p * decay, 0.0)
        tinv = jnp.where(eye_p, 1.0, 0.0) + a
        pk = _dot(a.astype(BF16), block_diag(a))
        yield
        for _ in range(4):
            pt = _dot(jnp.concatenate([pk, tinv], axis=0).astype(BF16), block_diag(pk))
            yield
            tinv = tinv + pt[CHUNK:]
            pk = pt[:CHUNK]
        tinv = tinv + _dot(tinv.astype(BF16), block_diag(pk))
        yield

        vb = v * bq
        kbg = k * bq_eg
        rv = jnp.concatenate(
            [jnp.concatenate([vb[:, hd * A_DK:(hd + 1) * A_DK], kbg[:, hd * A_DK:(hd + 1) * A_DK]], axis=1)
             for hd in range(A_HEADS)], axis=0)
        uw = _dot(stack_heads(tinv).astype(BF16), rv.astype(BF16))
        yield
        u, w = uw[:, :A_DK], uw[:, A_DK:]

        qg = q * eg
        kd = k * kd_scale
        return (u,
                jnp.concatenate([pair_lanes(w, 0, 1), qg[:, :2 * A_DK]], axis=0).astype(BF16),
                jnp.concatenate([pair_lanes(w, 2, 3), qg[:, 2 * A_DK:]], axis=0).astype(BF16),
                stack_heads(qk_p * decay).astype(BF16),
                jnp.concatenate([kd[:, :2 * A_DK], zeros], axis=0).T.astype(BF16),
                jnp.concatenate([kd[:, 2 * A_DK:], zeros], axis=0).T.astype(BF16),
                jnp.broadcast_to(jnp.exp(glast), (8, hw)))

    def local_store(c, vals):
        u_ref[c], wq_ref[c, 0], wq_ref[c, 1], qkd_ref[c], kdt_ref[c, 0], kdt_ref[c, 1], egl_ref[c] = vals

    def local_chain(c):
        vals = yield from local_compute(*local_load(c))
        local_store(c, vals)

    def state_chain(chunks):
        def unpair(r):
            return [r[:, :A_DK], r[:, A_DK:]]

        def state_update(s_old, kdt, vn_pair, egl_pair):
            upd = _dot(kdt, jnp.concatenate([vn_pair, zeros], axis=0).astype(BF16))
            return s_old * egl_pair + jnp.where(pair_mask, upd, 0.0)

        s01 = s01_ref[...]
        s23 = s23_ref[...]
        for c in chunks:
            r0 = c * CHUNK if isinstance(c, int) else pl.multiple_of(c * CHUNK, CHUNK)
            r01 = _dot(wq_ref[c, 0], s01.astype(BF16))
            r23 = _dot(wq_ref[c, 1], s23.astype(BF16))
            yield
            ws = jnp.concatenate(unpair(r01[:CHUNK]) + unpair(r23[:CHUNK]), axis=0)
            qs = jnp.concatenate(unpair(r01[CHUNK:]) + unpair(r23[CHUNK:]), axis=0)
            vn = u_ref[c] - ws
            o_rs = qs + _dot(qkd_ref[c], vn.astype(BF16))
            egl = egl_ref[c][0:1, :]
            s01 = state_update(s01, kdt_ref[c, 0], pair_lanes(vn, 0, 1), egl[:, :2 * A_DK])
            s23 = state_update(s23, kdt_ref[c, 1], pair_lanes(vn, 2, 3), egl[:, 2 * A_DK:])
            yield
            zz = z_ref[0, pl.ds(r0, CHUNK), :]
            parts = [_rms(o_rs[hd * CHUNK:(hd + 1) * CHUNK], ng) for hd in range(A_HEADS)]
            out_ref[0, pl.ds(r0, CHUNK), :] = jnp.concatenate(parts, axis=1) * _silu(zz)
        s01_ref[...] = s01
        s23_ref[...] = s23

    def run_lockstep(chains):
        live = list(chains)
        while live:
            for gen in list(live):
                try:
                    next(gen)
                except StopIteration:
                    live.remove(gen)

    nc = rb // CHUNK
    npairs = nc // 2
    if npairs == 0:
        run_lockstep([local_chain(0)])
        run_lockstep([state_chain([0])])
    else:
        run_lockstep([local_chain(0), local_chain(1)])

        def steady(i, carry):
            run_lockstep([local_chain(2 * i), local_chain(2 * i + 1), state_chain([2 * i - 2, 2 * i - 1])])
            return carry

        lax.fori_loop(1, npairs, steady, 0)
        last = [2 * npairs - 2, 2 * npairs - 1]
        if nc % 2:
            run_lockstep([local_chain(nc - 1), state_chain(last)])
            run_lockstep([state_chain([nc - 1])])
        else:
            run_lockstep([state_chain(last)])


def _gdn_call(qkv, z, sm, alog, dtb, ng, lt3, expand, rb):
    bsz, tp, _ = qkv.shape
    nt = tp // rb
    nc = rb // CHUNK
    hw = A_HEADS * A_DK
    pw = A_HEADS * CHUNK
    row = lambda w_: pl.BlockSpec((1, rb, w_), lambda b, t: (b, t, 0))
    return pl.pallas_call(
        functools.partial(_gdn_body, rb=rb),
        grid=(bsz, nt),
        in_specs=[row(3 * hw), row(hw), row(LANES), _const_spec((1, LANES)),
                  _const_spec((1, LANES)), _const_spec((1, LANES)), _const_spec(lt3.shape),
                  _const_spec(expand.shape)],
        out_specs=row(hw),
        out_shape=jax.ShapeDtypeStruct((bsz, tp, hw), F32),
        scratch_shapes=[pltpu.VMEM((2 * A_DK, 2 * A_DK), F32), pltpu.VMEM((2 * A_DK, 2 * A_DK), F32),
                        pltpu.VMEM((nc, pw, A_DK), F32), pltpu.VMEM((nc, 2, 2 * CHUNK, 2 * A_DK), BF16),
                        pltpu.VMEM((nc, pw, pw), BF16), pltpu.VMEM((nc, 2, 2 * A_DK, 2 * CHUNK), BF16),
                        pltpu.VMEM((nc, 8, hw), F32)],
        compiler_params=_params(("parallel", "arbitrary")),
        name="gdn",
    )(qkv, z, sm, alog, dtb, ng, lt3, expand)


def _dsa_body(qr_ref, kr_ref, v_ref, iqr_ref, ike_ref, iko_ref, sm_ref, lstrict_ref, out_ref,
              vt_ref, vmt_ref, skey_ref, planes_ref, sel_ref, st_ref, ot_ref, *, seq, topk):
    g = pl.program_id(1)
    iw_scale = IDX_HEADS ** -0.5 * IDX_DIM ** -0.5
    qpair = 2 * CHUNK
    neg_inf = -jnp.inf

    @pl.when(g == 0)
    def _():
        planes_ref[...] = jnp.zeros_like(planes_ref)

        def vblk(i, carry):
            r = pl.multiple_of(CHUNK + i * LANES, CHUNK)
            vt_ref[i] = v_ref[0, pl.ds(r, LANES), :].astype(F32).T.astype(BF16)
            return carry

        lax.fori_loop(0, seq // LANES, vblk, 0)
        vmt_ref[...] = v_ref[0, 0:LANES, :].astype(F32).T.astype(BF16)

        qm = qr_ref[0, 0:CHUNK, :]
        km = kr_ref[0, 0:CHUNK, :]
        vm = v_ref[0, 0:CHUNK, :]
        colv = lax.broadcasted_iota(I32, (1, CHUNK), 1) >= PAD
        parts = []
        for hd in range(B_HEADS):
            s = _dot_nt(qm[:, hd * B_DH:(hd + 1) * B_DH], km)
            s = jnp.where(colv, s, neg_inf)
            e = jnp.exp(s - jnp.max(s, axis=-1, keepdims=True))
            pr = e / jnp.sum(e, axis=-1, keepdims=True)
            parts.append(_dot(pr.astype(BF16), vm))
        out_ref[0, 0:CHUNK, :] = jnp.concatenate(parts, axis=1)

    lane = lax.broadcasted_iota(I32, (1, LANES), 1)
    groups = []
    for j in range(QUERY_GROUPS):
        p = g * QUERY_GROUPS + j
        r0 = pl.multiple_of(CHUNK + p * qpair, CHUNK)
        qb = qr_ref[0, pl.ds(r0, qpair), :]
        ib = iqr_ref[0, pl.ds(r0, qpair), :]
        smt = sm_ref[0, pl.ds(r0, qpair), :].T
        groups.append(dict(
            j=j, r0=r0,
            q_hq=jnp.concatenate([qb[:, hd * B_DH:(hd + 1) * B_DH] for hd in range(B_HEADS)], axis=0),
            iq_hq=jnp.concatenate([ib[:, hp * LANES:(hp + 1) * LANES] for hp in range(IDX_HEADS // 2)], axis=0),
            wts=[smt[2 * A_HEADS + hd:2 * A_HEADS + hd + 1, :] * iw_scale for hd in range(IDX_HEADS)],
            limit=jnp.where(lane < CHUNK, (2 * p + 1) * CHUNK, (2 * p + 2) * CHUNK)))
    assert QUERY_GROUPS * qpair == KEY_CHUNK
    full_rows = [KEY_CHUNK] * QUERY_GROUPS
    diag_rows = [(j + 1) * qpair for j in range(QUERY_GROUPS)]
    diag_plane_rows = [-(-r // PLANE_ROWS) * PLANE_ROWS for r in diag_rows]
    krows = lax.broadcasted_iota(I32, (KEY_CHUNK, 1), 0)

    def score_chunk(kc, carry, nrows=full_rows):
        k0 = pl.multiple_of(CHUNK + kc * KEY_CHUNK, CHUNK)
        base = pl.multiple_of(kc * KEY_CHUNK, KEY_CHUNK)
        ke, ko, kk = (ref[0, pl.ds(k0, KEY_CHUNK), :] for ref in (ike_ref, iko_ref, kr_ref))
        logits = []
        for grp_, nr in zip(groups, nrows):
            logits.append((_dot_nt(ke[:nr], grp_["iq_hq"]), _dot_nt(ko[:nr], grp_["iq_hq"])))
            st_ref[grp_["j"], pl.ds(base, nr), :] = _dot_nt(kk[:nr], grp_["q_hq"])
        for grp_, nr, (le, lo) in zip(groups, nrows, logits):
            score_tail(kc, pl.ds(base, nr), nr, grp_, le, lo)
        return carry

    def score_tail(kc, rows, nr, grp_, le, lo):
        j, wts, limit = grp_["j"], grp_["wts"], grp_["limit"]
        sc = jnp.zeros((nr, LANES), F32)
        for hp in range(IDX_HEADS // 2):
            sc = sc + wts[2 * hp] * jnp.maximum(le[:, hp * LANES:(hp + 1) * LANES], 0.0)
            sc = sc + wts[2 * hp + 1] * jnp.maximum(lo[:, hp * LANES:(hp + 1) * LANES], 0.0)
        bits = lax.bitcast_convert_type(sc, I32)
        bits = jnp.where(sc == 0.0, 0, bits)
        key = bits ^ ((bits >> 31) & 0x7FFFFFFF)
        valid = kc * KEY_CHUNK + krows[:nr] < limit
        key = jnp.where(valid, key, INT_MIN)
        skey_ref[j, rows, :] = key
        for gi in range(nr // PLANE_ROWS):
            a = [key[gi * PLANE_ROWS + 8 * r:gi * PLANE_ROWS + 8 * r + 8, :] ^ INT_MIN for r in range(32)]
            sh, msk = 16, 0x0000FFFF
            while sh:
                k = 0
                while k < 32:
                    tt = (a[k] ^ lax.shift_right_logical(a[k + sh], jnp.int32(sh))) & msk
                    a[k] = a[k] ^ tt
                    a[k + sh] = a[k + sh] ^ (tt << sh)
                    k = (k + sh + 1) & ~sh
                sh >>= 1
                msk = (msk ^ (msk << sh)) & 0xFFFFFFFF
            g8 = pl.multiple_of((kc * (KEY_CHUNK // PLANE_ROWS) + gi) * 8, 8)
            for b in range(32):
                planes_ref[j, b, pl.ds(g8, 8), :] = a[b]

    lax.fori_loop(0, g, score_chunk, 0)
    score_chunk(g, 0, diag_plane_rows)

    words_per_chunk = KEY_CHUNK // 32

    def lane_count(words):
        c = lax.population_count(words)
        if words.shape[0] > 8:
            c = jnp.sum(c.reshape(words.shape[0] // 8, 8, LANES), axis=0)
        return jnp.sum(c, axis=0, keepdims=True)

    def select_threshold(step):
        nrow = (step + 1) * words_per_chunk
        prow = lax.broadcasted_iota(I32, (nrow, 1), 0)
        alive0 = [jnp.where(prow < step * words_per_chunk + nr // 32, jnp.full((nrow, LANES), -1, I32), 0)
                  for nr in diag_plane_rows]

        def bit_body(i, carry):
            bit = jnp.left_shift(jnp.int32(1), 31 - i)
            out = []
            for grp_, (tu, above, alive) in zip(groups, carry):
                hit = alive & planes_ref[grp_["j"], i, 0:nrow, :]
                c1 = lane_count(hit)
                take = above + c1 >= topk
                out.append((jnp.where(take, tu | bit, tu), jnp.where(take, above, above + c1),
                            jnp.where(take, hit, alive ^ hit)))
            return tuple(out)

        zero_row = jnp.zeros((1, LANES), I32)
        result = lax.fori_loop(0, 32, bit_body, tuple((zero_row, zero_row, a0) for a0 in alive0))
        for grp_, (tu, above, alive) in zip(groups, result):
            for n, row in enumerate((tu, above, lane_count(alive))):
                sel_ref[grp_["j"], n] = jnp.broadcast_to(row, (8, LANES))

    for step in range(seq // KEY_CHUNK):
        pl.when(g == step)(functools.partial(select_threshold, step))

    for grp_, nr_diag in zip(groups, diag_rows):
        tu, above, n_eq = (sel_ref[grp_["j"], n][0:1] for n in range(3))
        j, thr, found = grp_["j"], tu ^ INT_MIN, tu != 0
        grp_["thr"] = thr
        tie_lane = found & (above + n_eq > topk)

        @pl.when(jnp.max(jnp.where(tie_lane, 1, 0)) > 0)
        def _(j=j, thr=thr, found=found, above=above, nr_diag=nr_diag):
            need = (topk - above).astype(F32)

            def body(kc, seen, nr=KEY_CHUNK):
                base = pl.multiple_of(kc * KEY_CHUNK, KEY_CHUNK)
                for sub in range(nr // LANES):
                    rows = pl.ds(base + sub * LANES, LANES)
                    blk = skey_ref[j, rows, :]
                    tied = (blk == thr) & found
                    eq = jnp.where(tied, 1.0, 0.0)
                    rank = _dot(lstrict_ref[...], eq.astype(BF16)) + seen
                    skey_ref[j, rows, :] = jnp.where(tied & (rank >= need), INT_MIN, blk)
                    seen = seen + jnp.sum(eq, axis=0, keepdims=True)
                return seen

            body(g, lax.fori_loop(0, g, body, jnp.zeros((1, LANES), F32)), nr_diag)

    mrows = lax.broadcasted_iota(I32, (LANES, 1), 0)
    k_meta = kr_ref[0, 0:LANES, :]
    ones = jnp.ones((ONES_ROWS, LANES), BF16)
    v_meta = jnp.concatenate([vmt_ref[...], ones], axis=0)
    init = []
    for grp_ in groups:
        s_meta = _dot_nt(k_meta, grp_["q_hq"])
        s_meta = jnp.where((mrows >= PAD) & (mrows < CHUNK), s_meta, neg_inf)
        m0 = jnp.max(s_meta, axis=0, keepdims=True)
        ot_ref[grp_["j"]] = _dot(v_meta, jnp.exp(s_meta - m0).astype(BF16))
        init.append(m0)
        grp_["thr"] = jnp.maximum(grp_["thr"], INT_MIN + 1)

    def pv_chunk(kc, carry, nrows=full_rows):
        base = pl.multiple_of(kc * KEY_CHUNK, KEY_CHUNK)
        nblk = KEY_CHUNK // LANES
        vblk = jnp.concatenate([jnp.concatenate([vt_ref[nblk * kc + i], ones], axis=0) for i in range(nblk)], axis=1)
        out, probs = [], []
        for grp_, nr, m in zip(groups, nrows, carry):
            rows = pl.ds(base, nr)
            bias = jnp.where(skey_ref[grp_["j"], rows, :] >= grp_["thr"], 0.0, neg_inf)
            s = st_ref[grp_["j"], rows, :] + jnp.concatenate([bias] * B_HEADS, axis=1)
            m_new = jnp.maximum(m, jnp.max(jnp.max(s.reshape(nr // 8, 8, B_HEADS * LANES), axis=0),
                                           axis=0, keepdims=True))
            out.append(m_new)
            probs.append((jnp.exp(m - m_new), jnp.exp(s - m_new).astype(BF16)))
        for grp_, nr, (alpha, prb) in zip(groups, nrows, probs):
            ot_ref[grp_["j"]] = ot_ref[grp_["j"]] * alpha + _dot(vblk[:, :nr], prb)
        return tuple(out)

    pv_chunk(g, lax.fori_loop(0, g, pv_chunk, tuple(init)), diag_rows)
    for grp_ in groups:
        acc = ot_ref[grp_["j"]]
        o_hq = (acc[:B_DH] / acc[B_DH:B_DH + 1]).T
        for hd in range(B_HEADS):
            out_ref[0, pl.ds(grp_["r0"], qpair), hd * B_DH:(hd + 1) * B_DH] = o_hq[hd * LANES:(hd + 1) * LANES, :]


def _dsa_call(qr, kr, v, iqr, ike, iko, sm, lstrict, seq, topk):
    bsz, tp, _ = qr.shape
    nsteps, rem = divmod(seq, 2 * CHUNK * QUERY_GROUPS)
    assert rem == 0
    full = lambda w_: pl.BlockSpec((1, tp, w_), lambda b, p: (b, 0, 0))
    hw = B_HEADS * B_DH
    nq = QUERY_GROUPS
    return pl.pallas_call(
        functools.partial(_dsa_body, seq=seq, topk=topk),
        grid=(bsz, nsteps),
        in_specs=[full(hw), full(B_DH), full(B_DH), full(hw), full(LANES), full(LANES), full(LANES),
                  _const_spec(lstrict.shape)],
        out_specs=full(hw),
        out_shape=jax.ShapeDtypeStruct((bsz, tp, hw), F32),
        scratch_shapes=[pltpu.VMEM((seq // LANES, B_DH, LANES), BF16), pltpu.VMEM((B_DH, LANES), BF16),
                        pltpu.VMEM((nq, seq, LANES), I32), pltpu.VMEM((nq, 32, seq // 32, LANES), I32),
                        pltpu.VMEM((nq, 3, 8, LANES), I32),
                        pltpu.VMEM((nq, seq, hw), F32), pltpu.VMEM((nq, B_DH + ONES_ROWS, hw), F32)],
        compiler_params=_params(("parallel", "arbitrary")),
        name="dsa",
    )(qr, kr, v, iqr, ike, iko, sm, lstrict)


def _mix_body(h_ref, oa_ref, ob_ref, g_ref, wg_ref, bg_ref, wa_ref, wb_ref, wo_ref, out_ref, *, rb):
    x = h_ref[0]
    d = x.shape[-1]
    n = _rms(x, g_ref[...]).astype(BF16)
    gates = _sigmoid(_dot(n, wg_ref[...]) + bg_ref[...])
    y = gates[:, :d] * _dot(oa_ref[0].astype(BF16), wa_ref[...]) + gates[:, d:] * _dot(ob_ref[0].astype(BF16), wb_ref[...])
    out = x + _dot(y.astype(BF16), wo_ref[...])
    rows = pl.program_id(1) * rb + lax.broadcasted_iota(I32, (rb, 1), 0)
    out_ref[0] = jnp.where(rows >= PAD, out, 0.0)


def _mix_call(h, oa, ob, gain, wg, bg, wa, wb, wo, rb):
    bsz, tp, d = h.shape
    row = lambda w_: pl.BlockSpec((1, rb, w_), lambda b, t: (b, t, 0))
    return pl.pallas_call(
        functools.partial(_mix_body, rb=rb),
        grid=(bsz, tp // rb),
        in_specs=[row(d), row(oa.shape[-1]), row(ob.shape[-1]), _const_spec((1, d)), _const_spec(wg.shape),
                  _const_spec(bg.shape), _const_spec(wa.shape), _const_spec(wb.shape), _const_spec(wo.shape)],
        out_specs=row(d),
        out_shape=jax.ShapeDtypeStruct((bsz, tp, d), F32),
        compiler_params=_params(("parallel", "arbitrary")),
        name="mix",
    )(h, oa, ob, gain, wg, bg, wa, wb, wo)


def _ffn_body(h_ref, g_ref, wup_ref, cw_ref, wd_ref, out_ref, carry_ref, act_ref, *, rb, dff):
    t = pl.program_id(1)

    @pl.when(t == 0)
    def _():
        carry_ref[...] = jnp.zeros_like(carry_ref)

    x = h_ref[0]
    n = _rms(x, g_ref[...]).astype(BF16)
    rows = lax.broadcasted_iota(I32, (rb, 1), 0)
    nch = dff // FF_CHUNK

    def gate_up(c):
        return (_dot(n, wup_ref[:, c * FF_CHUNK:(c + 1) * FF_CHUNK]),
                _dot(n, wup_ref[:, dff + c * FF_CHUNK:dff + (c + 1) * FF_CHUNK]))

    ahead = gate_up(0)
    for c in range(nch):
        cols = slice(c * FF_CHUNK, (c + 1) * FF_CHUNK)
        gate, up = ahead
        if c + 1 < nch:
            ahead = gate_up(c + 1)
        prev = carry_ref[:, cols]
        g1 = jnp.where(rows == 0, prev[7:8], pltpu.roll(gate, 1, 0))
        g2 = jnp.where(rows == 0, prev[6:7], jnp.where(rows == 1, prev[7:8], pltpu.roll(gate, 2, 0)))
        cw = cw_ref[:, cols]
        conv = g2 * cw[0:1] + g1 * cw[1:2] + gate * cw[2:3]
        carry_ref[:, cols] = gate[rb - 8:rb]
        act_ref[:, cols] = (_silu(conv) * up).astype(BF16)
    out = x + _dot(act_ref[...], wd_ref[...])
    out_ref[0] = jnp.where(t * rb + rows >= PAD, out, 0.0)


def _ffn_call(h, gain, wup, cw, wd, rb):
    bsz, tp, d = h.shape
    dff = wd.shape[0]
    row = pl.BlockSpec((1, rb, d), lambda b, t: (b, t, 0))
    return pl.pallas_call(
        functools.partial(_ffn_body, rb=rb, dff=dff),
        grid=(bsz, tp // rb),
        in_specs=[row, _const_spec((1, d)), _const_spec(wup.shape), _const_spec(cw.shape), _const_spec(wd.shape)],
        out_specs=row,
        out_shape=jax.ShapeDtypeStruct((bsz, tp, d), F32),
        scratch_shapes=[pltpu.VMEM((8, dff), F32), pltpu.VMEM((rb, dff), BF16)],
        compiler_params=_params(("parallel", "arbitrary")),
        name="ffn",
    )(h, gain, wup, cw, wd)


def _rope_tables(tp):
    pos = (jnp.arange(tp, dtype=F32) - PAD)[:, None]

    def cs(dim):
        inv = 1.0 / (ROPE_THETA ** (jnp.arange(0, dim, 2, dtype=F32) / dim))
        ang = pos * inv[None, :]
        return jnp.cos(ang), jnp.sin(ang)

    ca, sa = cs(B_DH)
    ci, si = cs(IDX_DIM)
    zi = jnp.zeros_like(si)
    return (jnp.concatenate([ca, ca], axis=1), jnp.concatenate([-sa, sa], axis=1),
            jnp.concatenate([ci] * 4, axis=1), jnp.concatenate([-si, zi, -si, zi], axis=1),
            jnp.concatenate([zi, si, zi, si], axis=1))


def _pad_lanes(vec):
    return jnp.zeros((1, LANES), F32).at[0, :vec.shape[0]].set(vec.astype(F32))


def _head_expander():
    pw, hw = A_HEADS * CHUNK, A_HEADS * A_DK
    head = np.concatenate([np.arange(pw) // CHUNK, np.arange(hw) // A_DK])
    return jnp.asarray(np.arange(LANES)[:, None] == head[None, :], BF16)


def kernel(x, meta_tokens, norm_mix, w_in, conv_a, a_log, dt_bias, a_out_norm, q_norm, k_norm, kidx_norm,
           w_branch_a, w_branch_b, w_gate, b_gate, w_out, norm_ffn, w_up, conv_ffn, w_down):
    bsz, seq, d = x.shape
    depth = w_in.shape[0]
    assert seq % KEY_CHUNK == 0
    tp = CHUNK + seq
    rb = _row_block(tp // CHUNK)
    topk = min(TOPK_MAX, seq // TOPK_DIV)

    meta = jnp.broadcast_to(meta_tokens.astype(x.dtype)[None], (bsz, N_META, d))
    h = jnp.concatenate([jnp.zeros((bsz, PAD, d), x.dtype), meta, x], axis=1)

    tables = _rope_tables(tp)
    expand = _head_expander()
    tri = np.arange(CHUNK)
    lt3 = jnp.asarray(np.tile(tri[:, None] >= tri[None, :], (1, 3)), BF16)
    trk = np.arange(LANES)
    lstrict = jnp.asarray(trk[:, None] > trk[None, :], BF16)

    hw = A_HEADS * A_DK
    o = np.cumsum((0, hw, hw, hw, hw, A_HEADS, A_HEADS, B_HEADS * B_DH, B_DH, B_DH,
                   IDX_HEADS * IDX_DIM, IDX_DIM, IDX_HEADS))
    for l in range(depth):
        w = w_in[l]
        small = jnp.concatenate([w[:, o[4]:o[6]], w[:, o[11]:o[12]],
                                 jnp.zeros((d, LANES - 2 * A_HEADS - IDX_HEADS), w.dtype)], axis=1)
        w_p = jnp.concatenate([w[:, o[0]:o[3]], w[:, o[3]:o[4]], w[:, o[6]:o[7]], w[:, o[7]:o[9]],
                               w[:, o[9]:o[10]], small, w[:, o[10]:o[11]], w[:, o[10]:o[11]]], axis=1).astype(BF16)
        qkv, z, sm, qr, kr, v, iqr, ike, iko = _proj_call(
            h, norm_mix[l][None], w_p, tables, q_norm[l][None], k_norm[l][None],
            jnp.concatenate([kidx_norm[l], kidx_norm[l]])[None], conv_a[l], rb)
        o_a = _gdn_call(qkv, z, sm, _pad_lanes(a_log[l]), _pad_lanes(dt_bias[l]),
                        a_out_norm[l][None], lt3, expand, rb)
        o_b = _dsa_call(qr, kr, v, iqr, ike, iko, sm, lstrict, seq, topk)
        h = _mix_call(h, o_a, o_b, norm_mix[l][None], w_gate[l].astype(BF16), b_gate[l][None],
                      w_branch_a[l].astype(BF16), w_branch_b[l].astype(BF16), w_out[l].astype(BF16), rb)
        h = _ffn_call(h, norm_ffn[l][None], w_up[l].astype(BF16), conv_ffn[l], w_down[l].astype(BF16), rb)
    return h[:, CHUNK:]
```

```python
import functools
import math

import jax
import jax.numpy as jnp
import numpy as np
from jax import lax
from jax.experimental import pallas as pl
from jax.experimental.pallas import tpu as pltpu

F32 = jnp.float32
BF16 = jnp.bfloat16
I32 = jnp.int32

CHUNK = 64
N_META = 16
PAD = CHUNK - N_META
ROPE_THETA = 10000.0
EPS = 1e-6
A_HEADS = 4
A_DK = 128
A_CONV = 4
B_HEADS = 4
B_DH = 128
IDX_HEADS = 8
IDX_DIM = 64
TOPK_MAX = 256
TOPK_DIV = 4
FFN_CONV = 3
LANES = 128
KEY_CHUNK = 512
PLANE_ROWS = 256
QUERY_GROUPS = 4
ONES_ROWS = 16
FF_CHUNK = 256
FRAME_BLOCK = 512
INT_MIN = -(2 ** 31)
VMEM_LIMIT_BYTES = 56 * 1024 * 1024

NT_DIMS = (((1,), (1,)), ((), ()))


def _dot(a, b):
    return jnp.dot(a, b, preferred_element_type=F32)


def _dot_nt(a, b):
    return lax.dot_general(a, b, NT_DIMS, preferred_element_type=F32)


def _dot_exact(a, b):
    return jnp.dot(a, b, preferred_element_type=F32, precision=lax.Precision.HIGHEST)


def _rms(x, gain):
    return x * lax.rsqrt(jnp.mean(x * x, axis=-1, keepdims=True) + EPS) * gain


def _sigmoid(x):
    return 1.0 / (1.0 + jnp.exp(-x))


def _silu(x):
    return x * _sigmoid(x)


def _softplus(x):
    return jnp.maximum(x, 0.0) + jnp.log(1.0 + jnp.exp(-jnp.abs(x)))


def _row_block(n_chunks):
    g = max(d for d in range(1, 12) if n_chunks % d == 0)
    return g * CHUNK


def _params(sem):
    return pltpu.CompilerParams(dimension_semantics=sem, vmem_limit_bytes=VMEM_LIMIT_BYTES)


def _const_spec(shape):
    nd = len(shape)
    return pl.BlockSpec(shape, lambda *_: (0,) * nd)


PROJ_WIDTHS = (1536, 512, 512, 256, 512, 256)


def _proj_body(h_ref, g_ref, w_ref, cosa_ref, sina_ref, cosi_ref, sinlo_ref, sinhi_ref,
               qn_ref, kn_ref, kin_ref, cw_ref,
               qkv_ref, z_ref, sm_ref, qr_ref, kr_ref, v_ref, iqr_ref, ike_ref, iko_ref, tail_ref, *, rb):
    @pl.when(pl.program_id(1) == 0)
    def _():
        tail_ref[...] = jnp.zeros_like(tail_ref)

    n = _rms(h_ref[0], g_ref[...]).astype(BF16)
    offs = dict(zip(("qkv", "z", "q", "kv", "iq", "sm_ik"), np.cumsum((0,) + PROJ_WIDTHS)[:-1]))
    rows8 = lax.broadcasted_iota(I32, (8, 1), 0)
    lane = lax.broadcasted_iota(I32, (1, LANES), 1)
    ca, sa = cosa_ref[...], sina_ref[...]
    ci, slo, shi = cosi_ref[...], sinlo_ref[...], sinhi_ref[...]

    def conv_silu(x, cols, r0, nrows):
        prev, cw = (tail_ref[:, cols] if r0 == 0 else x[r0 - 8:r0]), cw_ref[:, cols]
        xb = x[r0:r0 + nrows]
        acc = None
        for back in range(A_CONV - 1, 0, -1):
            shifted = pltpu.roll(xb, back, 0)
            top = shifted[:8]
            for r in range(back):
                top = jnp.where(rows8 == r, prev[8 - back + r:8 - back + r + 1], top)
            term = jnp.concatenate([top, shifted[8:]], axis=0) * cw[A_CONV - 1 - back:A_CONV - back]
            acc = term if acc is None else acc + term
        return _silu(acc + xb * cw[A_CONV - 1:A_CONV])

    half = rb // 2
    assert half % 8 == 0

    def gdn_piece(i):
        cols = slice(i * 256, (i + 1) * 256)

        def finish(x):
            def first():
                qkv_ref[0, 0:half, cols] = conv_silu(x, cols, 0, half)

            def second():
                qkv_ref[0, half:rb, cols] = conv_silu(x, cols, half, half)
                tail_ref[:, cols] = x[rb - 8:rb]
            return [first, second]
        return offs["qkv"] + i * 256, finish

    def plain_piece(i):
        def finish(x):
            def store():
                z_ref[0, :, i * 256:(i + 1) * 256] = x
            return [store]
        return offs["z"] + i * 256, finish

    def single(fn):
        return lambda x: [functools.partial(fn, x)]

    def q_piece(i):
        @single
        def finish(x):
            parts = []
            for hd in range(2):
                qh = _rms(x[:, hd * B_DH:(hd + 1) * B_DH], qn_ref[...])
                parts.append(qh * ca + pltpu.roll(qh, B_DH // 2, 1) * sa)
            qr_ref[0, :, i * 256:(i + 1) * 256] = (jnp.concatenate(parts, axis=1) * (B_DH ** -0.5)).astype(BF16)
        return offs["q"] + i * 256, finish

    def kv_piece():
        @single
        def finish(x):
            k = _rms(x[:, :B_DH], kn_ref[...])
            kr_ref[0] = (k * ca + pltpu.roll(k, B_DH // 2, 1) * sa).astype(BF16)
            v_ref[0] = x[:, B_DH:].astype(BF16)
        return offs["kv"], finish

    def iq_piece(i):
        @single
        def finish(x):
            parts = []
            for hp in range(2):
                xh = x[:, hp * LANES:(hp + 1) * LANES]
                parts.append(xh * ci + pltpu.roll(xh, LANES - IDX_DIM // 2, 1) * slo
                             + pltpu.roll(xh, IDX_DIM // 2, 1) * shi)
            iqr_ref[0, :, i * 256:(i + 1) * 256] = jnp.concatenate(parts, axis=1).astype(BF16)
        return offs["iq"] + i * 256, finish

    def small_ik_piece():
        @single
        def finish(x):
            sm_ref[0] = x[:, :LANES]
            ik = _rms(x[:, LANES:], kin_ref[...])
            ik = ik * ci + pltpu.roll(ik, IDX_DIM // 2, 1) * (slo + shi)
            ike_ref[0] = jnp.where(lane < IDX_DIM, ik, 0.0).astype(BF16)
            iko_ref[0] = jnp.where(lane >= IDX_DIM, ik, 0.0).astype(BF16)
        return offs["sm_ik"], finish

    pieces = [q_piece(0), gdn_piece(0), q_piece(1), gdn_piece(1), iq_piece(0), gdn_piece(2), iq_piece(1),
              gdn_piece(3), kv_piece(), gdn_piece(4), small_ik_piece(), gdn_piece(5), plain_piece(0), plain_piece(1)]
    ready, staged = [], []
    for col, finish in pieces:
        x = _dot(n, w_ref[:, col:col + 256])
        ready += staged
        staged = finish(x)
        for task in [ready.pop(0) for _ in range(min(1, len(ready)))]:
            task()
    for task in ready + staged:
        task()


def _proj_call(h, gain, w, tables, qn, kn, kin, cw, rb):
    bsz, tp, d = h.shape
    nt = tp // rb
    row = lambda w_, dt: (jax.ShapeDtypeStruct((bsz, tp, w_), dt),
                          pl.BlockSpec((1, rb, w_), lambda b, t: (b, t, 0)))
    outs = [row(1536, F32), row(512, F32), row(128, F32), row(512, BF16), row(128, BF16),
            row(128, BF16), row(512, BF16), row(128, BF16), row(128, BF16)]
    tab_spec = pl.BlockSpec((rb, LANES), lambda b, t: (t, 0))
    return pl.pallas_call(
        functools.partial(_proj_body, rb=rb),
        grid=(bsz, nt),
        in_specs=[pl.BlockSpec((1, rb, d), lambda b, t: (b, t, 0)), _const_spec((1, d)), _const_spec(w.shape)]
                 + [tab_spec] * 5 + [_const_spec((1, LANES))] * 3 + [_const_spec(cw.shape)],
        out_specs=[o[1] for o in outs],
        out_shape=[o[0] for o in outs],
        scratch_shapes=[pltpu.VMEM((8, PROJ_WIDTHS[0]), F32)],
        compiler_params=_params(("parallel", "arbitrary")),
        name="proj",
    )(h, gain, w, *tables, qn, kn, kin, cw)


def _split3(x):
    x1 = x.astype(BF16)
    r1 = x - x1.astype(F32)
    x2 = r1.astype(BF16)
    x3 = (r1 - x2.astype(F32)).astype(BF16)
    return jnp.concatenate([x1, x2, x3], axis=0)


def _gdn_body(qkv_ref, z_ref, sm_ref, alog_ref, dtb_ref, ng_ref, lt3_ref, expand_ref,
              out_ref, s01_ref, s23_ref, u_ref, wq_ref, qkd_ref, kdt_ref, egl_ref, *, rb):
    t = pl.program_id(1)

    @pl.when(t == 0)
    def _():
        s01_ref[...] = jnp.zeros_like(s01_ref)
        s23_ref[...] = jnp.zeros_like(s23_ref)

    hw = A_HEADS * A_DK
    pw = A_HEADS * CHUNK
    ii = lax.broadcasted_iota(I32, (CHUNK, pw), 0)
    jj = lax.broadcasted_iota(I32, (CHUNK, pw), 1) & (CHUNK - 1)
    colhead = lax.broadcasted_iota(I32, (CHUNK, pw), 1) >> 6
    eye_p = ii == jj
    bd_mask = (lax.broadcasted_iota(I32, (pw, pw), 0) >> 6) == (lax.broadcasted_iota(I32, (pw, pw), 1) >> 6)
    bdk_mask = (lax.broadcasted_iota(I32, (pw, hw), 0) >> 6) == (lax.broadcasted_iota(I32, (pw, hw), 1) >> 7)
    pair_mask = (lax.broadcasted_iota(I32, (pw, pw), 0) >> 7) == (lax.broadcasted_iota(I32, (pw, pw), 1) >> 7)
    lane = lax.broadcasted_iota(I32, (1, LANES), 1)
    rows64 = lax.broadcasted_iota(I32, (CHUNK, 1), 0)
    neg_a = -jnp.exp(alog_ref[...])
    dtb = dtb_ref[...]
    ng = ng_ref[...]

    def block_diag(xp):
        return jnp.where(bd_mask, jnp.concatenate([xp.astype(BF16)] * A_HEADS, axis=0), 0)

    def stack_heads(xp):
        return jnp.concatenate([jnp.where(colhead == hd, xp, 0.0) for hd in range(A_HEADS)], axis=0)

    def l2n(x):
        parts = []
        for hd in range(A_HEADS):
            xh = x[:, hd * A_DK:(hd + 1) * A_DK]
            parts.append(xh * lax.rsqrt(jnp.sum(xh * xh, axis=-1, keepdims=True) + EPS))
        return jnp.concatenate(parts, axis=1)

    zeros = jnp.zeros((CHUNK, 2 * A_DK), F32)

    def pair_lanes(x, a_, b_):
        return jnp.concatenate([x[a_ * CHUNK:(a_ + 1) * CHUNK], x[b_ * CHUNK:(b_ + 1) * CHUNK]], axis=1)

    def local_load(c):
        r0 = c * CHUNK if isinstance(c, int) else pl.multiple_of(c * CHUNK, CHUNK)
        return qkv_ref[0, pl.ds(r0, CHUNK), :], sm_ref[0, pl.ds(r0, CHUNK), :], r0

    def local_compute(xq, sm, r0):
        q = l2n(xq[:, :hw]) * (A_DK ** -0.5)
        k = l2n(xq[:, hw:2 * hw])
        v = xq[:, 2 * hw:]

        g = jnp.where(lane < A_HEADS, neg_a * _softplus(sm + dtb), 0.0)
        g = jnp.where(t * rb + r0 + rows64 >= PAD, g, 0.0)
        to_packed, to_wide = expand_ref[:, :pw], expand_ref[:, pw:]
        g3 = _split3(g)
        cg = _dot(g3, to_packed).astype(BF16)
        gcs = _dot(lt3_ref[...], g3)
        yield
        gp = _dot(lt3_ref[...], cg)
        in_head = lane < A_HEADS
        beta = jnp.where(in_head, pltpu.roll(_sigmoid(sm), LANES - A_HEADS, 1), 0.0)
        eg_s = jnp.where(in_head, jnp.exp(gcs), 0.0)
        to_last = jnp.where(in_head, jnp.exp(gcs[CHUNK - 1:CHUNK] - gcs), 0.0)
        wide = _dot(jnp.concatenate([beta, beta * eg_s, eg_s, to_last], axis=0).astype(BF16), to_wide)
        bp = _dot(beta.astype(BF16), to_packed)
        glast3 = _dot(_split3(gcs[CHUNK - 16:]), to_wide)
        yield
        bq, bq_eg, eg, kd_scale = (wide[i * CHUNK:(i + 1) * CHUNK] for i in range(4))
        glast = glast3[15:16] + glast3[31:32] + glast3[47:48]
        grow = jnp.sum(jnp.where(eye_p, gp, 0.0), axis=0, keepdims=True)
        decay = jnp.where(ii >= jj, jnp.exp(gp - grow), 0.0)

        kb = k.astype(BF16)
        bdk = jnp.where(bdk_mask, jnp.concatenate([kb] * A_HEADS, axis=0), 0)
        kq = _dot_nt(jnp.concatenate([kb, q.astype(BF16)], axis=0), bdk)
        yield
        kk_p, qk_p = kq[:CHUNK], kq[CHUNK:]

        a = -jnp.where(ii > jj, bp * kk_p * decay, 0.0)
        tinv = jnp.where(eye_p, 1.0, 0.0) + a
        pk = _dot(a.astype(BF16), block_diag(a))
        yield
        for _ in range(4):
            pt = _dot(jnp.concatenate([pk, tinv], axis=0).astype(BF16), block_diag(pk))
            yield
            tinv = tinv + pt[CHUNK:]
            pk = pt[:CHUNK]
        tinv = tinv + _dot(tinv.astype(BF16), block_diag(pk))
        yield

        vb = v * bq
        kbg = k * bq_eg
        rv = jnp.concatenate(
            [jnp.concatenate([vb[:, hd * A_DK:(hd + 1) * A_DK], kbg[:, hd * A_DK:(hd + 1) * A_DK]], axis=1)
             for hd in range(A_HEADS)], axis=0)
        uw = _dot(stack_heads(tinv).astype(BF16), rv.astype(BF16))
        yield
        u, w = uw[:, :A_DK], uw[:, A_DK:]

        qg = q * eg
        kd = k * kd_scale
        return (u,
                jnp.concatenate([pair_lanes(w, 0, 1), qg[:, :2 * A_DK]], axis=0).astype(BF16),
                jnp.concatenate([pair_lanes(w, 2, 3), qg[:, 2 * A_DK:]], axis=0).astype(BF16),
                stack_heads(qk_p * decay).astype(BF16),
                jnp.concatenate([kd[:, :2 * A_DK], zeros], axis=0).T.astype(BF16),
                jnp.concatenate([kd[:, 2 * A_DK:], zeros], axis=0).T.astype(BF16),
                jnp.broadcast_to(jnp.exp(glast), (8, hw)))

    def local_store(c, vals):
        u_ref[c], wq_ref[c, 0], wq_ref[c, 1], qkd_ref[c], kdt_ref[c, 0], kdt_ref[c, 1], egl_ref[c] = vals

    def local_chain(c):
        vals = yield from local_compute(*local_load(c))
        local_store(c, vals)

    def state_chain(chunks):
        def unpair(r):
            return [r[:, :A_DK], r[:, A_DK:]]

        def state_update(s_old, kdt, vn_pair, egl_pair):
            upd = _dot(kdt, jnp.concatenate([vn_pair, zeros], axis=0).astype(BF16))
            return s_old * egl_pair + jnp.where(pair_mask, upd, 0.0)

        s01 = s01_ref[...]
        s23 = s23_ref[...]
        for c in chunks:
            r0 = c * CHUNK if isinstance(c, int) else pl.multiple_of(c * CHUNK, CHUNK)
            r01 = _dot(wq_ref[c, 0], s01.astype(BF16))
            r23 = _dot(wq_ref[c, 1], s23.astype(BF16))
            yield
            ws = jnp.concatenate(unpair(r01[:CHUNK]) + unpair(r23[:CHUNK]), axis=0)
            qs = jnp.concatenate(unpair(r01[CHUNK:]) + unpair(r23[CHUNK:]), axis=0)
            vn = u_ref[c] - ws
            o_rs = qs + _dot(qkd_ref[c], vn.astype(BF16))
            egl = egl_ref[c][0:1, :]
            s01 = state_update(s01, kdt_ref[c, 0], pair_lanes(vn, 0, 1), egl[:, :2 * A_DK])
            s23 = state_update(s23, kdt_ref[c, 1], pair_lanes(vn, 2, 3), egl[:, 2 * A_DK:])
            yield
            zz = z_ref[0, pl.ds(r0, CHUNK), :]
            parts = [_rms(o_rs[hd * CHUNK:(hd + 1) * CHUNK], ng) for hd in range(A_HEADS)]
            out_ref[0, pl.ds(r0, CHUNK), :] = jnp.concatenate(parts, axis=1) * _silu(zz)
        s01_ref[...] = s01
        s23_ref[...] = s23

    def run_lockstep(chains):
        live = list(chains)
        while live:
            for gen in list(live):
                try:
                    next(gen)
                except StopIteration:
                    live.remove(gen)

    nc = rb // CHUNK
    npairs = nc // 2
    if npairs == 0:
        run_lockstep([local_chain(0)])
        run_lockstep([state_chain([0])])
    else:
        run_lockstep([local_chain(0), local_chain(1)])

        def steady(i, carry):
            run_lockstep([local_chain(2 * i), local_chain(2 * i + 1), state_chain([2 * i - 2, 2 * i - 1])])
            return carry

        lax.fori_loop(1, npairs, steady, 0)
        last = [2 * npairs - 2, 2 * npairs - 1]
        if nc % 2:
            run_lockstep([local_chain(nc - 1), state_chain(last)])
            run_lockstep([state_chain([nc - 1])])
        else:
            run_lockstep([state_chain(last)])


def _gdn_call(qkv, z, sm, alog, dtb, ng, lt3, expand, rb):
    bsz, tp, _ = qkv.shape
    nt = tp // rb
    nc = rb // CHUNK
    hw = A_HEADS * A_DK
    pw = A_HEADS * CHUNK
    row = lambda w_: pl.BlockSpec((1, rb, w_), lambda b, t: (b, t, 0))
    return pl.pallas_call(
        functools.partial(_gdn_body, rb=rb),
        grid=(bsz, nt),
        in_specs=[row(3 * hw), row(hw), row(LANES), _const_spec((1, LANES)),
                  _const_spec((1, LANES)), _const_spec((1, LANES)), _const_spec(lt3.shape),
                  _const_spec(expand.shape)],
        out_specs=row(hw),
        out_shape=jax.ShapeDtypeStruct((bsz, tp, hw), F32),
        scratch_shapes=[pltpu.VMEM((2 * A_DK, 2 * A_DK), F32), pltpu.VMEM((2 * A_DK, 2 * A_DK), F32),
                        pltpu.VMEM((nc, pw, A_DK), F32), pltpu.VMEM((nc, 2, 2 * CHUNK, 2 * A_DK), BF16),
                        pltpu.VMEM((nc, pw, pw), BF16), pltpu.VMEM((nc, 2, 2 * A_DK, 2 * CHUNK), BF16),
                        pltpu.VMEM((nc, 8, hw), F32)],
        compiler_params=_params(("parallel", "arbitrary")),
        name="gdn",
    )(qkv, z, sm, alog, dtb, ng, lt3, expand)


def _dsa_body(qr_ref, kr_ref, v_ref, iqr_ref, ike_ref, iko_ref, sm_ref, lstrict_ref, out_ref,
              vt_ref, vmt_ref, skey_ref, planes_ref, sel_ref, st_ref, ot_ref, *, seq, topk):
    g = pl.program_id(1)
    iw_scale = IDX_HEADS ** -0.5 * IDX_DIM ** -0.5
    qpair = 2 * CHUNK
    neg_inf = -jnp.inf

    @pl.when(g == 0)
    def _():
        planes_ref[...] = jnp.zeros_like(planes_ref)

        def vblk(i, carry):
            r = pl.multiple_of(CHUNK + i * LANES, CHUNK)
            vt_ref[i] = v_ref[0, pl.ds(r, LANES), :].astype(F32).T.astype(BF16)
            return carry

        lax.fori_loop(0, seq // LANES, vblk, 0)
        vmt_ref[...] = v_ref[0, 0:LANES, :].astype(F32).T.astype(BF16)

        qm = qr_ref[0, 0:CHUNK, :]
        km = kr_ref[0, 0:CHUNK, :]
        vm = v_ref[0, 0:CHUNK, :]
        colv = lax.broadcasted_iota(I32, (1, CHUNK), 1) >= PAD
        parts = []
        for hd in range(B_HEADS):
            s = _dot_nt(qm[:, hd * B_DH:(hd + 1) * B_DH], km)
            s = jnp.where(colv, s, neg_inf)
            e = jnp.exp(s - jnp.max(s, axis=-1, keepdims=True))
            pr = e / jnp.sum(e, axis=-1, keepdims=True)
            parts.append(_dot(pr.astype(BF16), vm))
        out_ref[0, 0:CHUNK, :] = jnp.concatenate(parts, axis=1)

    lane = lax.broadcasted_iota(I32, (1, LANES), 1)
    groups = []
    for j in range(QUERY_GROUPS):
        p = g * QUERY_GROUPS + j
        r0 = pl.multiple_of(CHUNK + p * qpair, CHUNK)
        qb = qr_ref[0, pl.ds(r0, qpair), :]
        ib = iqr_ref[0, pl.ds(r0, qpair), :]
        smt = sm_ref[0, pl.ds(r0, qpair), :].T
        groups.append(dict(
            j=j, r0=r0,
            q_hq=jnp.concatenate([qb[:, hd * B_DH:(hd + 1) * B_DH] for hd in range(B_HEADS)], axis=0),
            iq_hq=jnp.concatenate([ib[:, hp * LANES:(hp + 1) * LANES] for hp in range(IDX_HEADS // 2)], axis=0),
            wts=[smt[2 * A_HEADS + hd:2 * A_HEADS + hd + 1, :] * iw_scale for hd in range(IDX_HEADS)],
            limit=jnp.where(lane < CHUNK, (2 * p + 1) * CHUNK, (2 * p + 2) * CHUNK)))
    assert QUERY_GROUPS * qpair == KEY_CHUNK
    full_rows = [KEY_CHUNK] * QUERY_GROUPS
    diag_rows = [(j + 1) * qpair for j in range(QUERY_GROUPS)]
    diag_plane_rows = [-(-r // PLANE_ROWS) * PLANE_ROWS for r in diag_rows]
    krows = lax.broadcasted_iota(I32, (KEY_CHUNK, 1), 0)

    def score_chunk(kc, carry, nrows=full_rows):
        k0 = pl.multiple_of(CHUNK + kc * KEY_CHUNK, CHUNK)
        base = pl.multiple_of(kc * KEY_CHUNK, KEY_CHUNK)
        ke, ko, kk = (ref[0, pl.ds(k0, KEY_CHUNK), :] for ref in (ike_ref, iko_ref, kr_ref))
        logits = []
        for grp_, nr in zip(groups, nrows):
            logits.append((_dot_nt(ke[:nr], grp_["iq_hq"]), _dot_nt(ko[:nr], grp_["iq_hq"])))
            st_ref[grp_["j"], pl.ds(base, nr), :] = _dot_nt(kk[:nr], grp_["q_hq"])
        for grp_, nr, (le, lo) in zip(groups, nrows, logits):
            score_tail(kc, pl.ds(base, nr), nr, grp_, le, lo)
        return carry

    def score_tail(kc, rows, nr, grp_, le, lo):
        j, wts, limit = grp_["j"], grp_["wts"], grp_["limit"]
        sc = jnp.zeros((nr, LANES), F32)
        for hp in range(IDX_HEADS // 2):
            sc = sc + wts[2 * hp] * jnp.maximum(le[:, hp * LANES:(hp + 1) * LANES], 0.0)
            sc = sc + wts[2 * hp + 1] * jnp.maximum(lo[:, hp * LANES:(hp + 1) * LANES], 0.0)
        bits = lax.bitcast_convert_type(sc, I32)
        bits = jnp.where(sc == 0.0, 0, bits)
        key = bits ^ ((bits >> 31) & 0x7FFFFFFF)
        valid = kc * KEY_CHUNK + krows[:nr] < limit
        key = jnp.where(valid, key, INT_MIN)
        skey_ref[j, rows, :] = key
        for gi in range(nr // PLANE_ROWS):
            a = [key[gi * PLANE_ROWS + 8 * r:gi * PLANE_ROWS + 8 * r + 8, :] ^ INT_MIN for r in range(32)]
            sh, msk = 16, 0x0000FFFF
            while sh:
                k = 0
                while k < 32:
                    tt = (a[k] ^ lax.shift_right_logical(a[k + sh], jnp.int32(sh))) & msk
                    a[k] = a[k] ^ tt
                    a[k + sh] = a[k + sh] ^ (tt << sh)
                    k = (k + sh + 1) & ~sh
                sh >>= 1
                msk = (msk ^ (msk << sh)) & 0xFFFFFFFF
            g8 = pl.multiple_of((kc * (KEY_CHUNK // PLANE_ROWS) + gi) * 8, 8)
            for b in range(32):
                planes_ref[j, b, pl.ds(g8, 8), :] = a[b]

    lax.fori_loop(0, g, score_chunk, 0)
    score_chunk(g, 0, diag_plane_rows)

    words_per_chunk = KEY_CHUNK // 32

    def lane_count(words):
        c = lax.population_count(words)
        if words.shape[0] > 8:
            c = jnp.sum(c.reshape(words.shape[0] // 8, 8, LANES), axis=0)
        return jnp.sum(c, axis=0, keepdims=True)

    def select_threshold(step):
        nrow = (step + 1) * words_per_chunk
        prow = lax.broadcasted_iota(I32, (nrow, 1), 0)
        alive0 = [jnp.where(prow < step * words_per_chunk + nr // 32, jnp.full((nrow, LANES), -1, I32), 0)
                  for nr in diag_plane_rows]

        def bit_body(i, carry):
            bit = jnp.left_shift(jnp.int32(1), 31 - i)
            out = []
            for grp_, (tu, above, alive) in zip(groups, carry):
                hit = alive & planes_ref[grp_["j"], i, 0:nrow, :]
                c1 = lane_count(hit)
                take = above + c1 >= topk
                out.append((jnp.where(take, tu | bit, tu), jnp.where(take, above, above + c1),
                            jnp.where(take, hit, alive ^ hit)))
            return tuple(out)

        zero_row = jnp.zeros((1, LANES), I32)
        result = lax.fori_loop(0, 32, bit_body, tuple((zero_row, zero_row, a0) for a0 in alive0))
        for grp_, (tu, above, alive) in zip(groups, result):
            for n, row in enumerate((tu, above, lane_count(alive))):
                sel_ref[grp_["j"], n] = jnp.broadcast_to(row, (8, LANES))

    for step in range(seq // KEY_CHUNK):
        pl.when(g == step)(functools.partial(select_threshold, step))

    for grp_, nr_diag in zip(groups, diag_rows):
        tu, above, n_eq = (sel_ref[grp_["j"], n][0:1] for n in range(3))
        j, thr, found = grp_["j"], tu ^ INT_MIN, tu != 0
        grp_["thr"] = thr
        tie_lane = found & (above + n_eq > topk)

        @pl.when(jnp.max(jnp.where(tie_lane, 1, 0)) > 0)
        def _(j=j, thr=thr, found=found, above=above, nr_diag=nr_diag):
            need = (topk - above).astype(F32)

            def body(kc, seen, nr=KEY_CHUNK):
                base = pl.multiple_of(kc * KEY_CHUNK, KEY_CHUNK)
                for sub in range(nr // LANES):
                    rows = pl.ds(base + sub * LANES, LANES)
                    blk = skey_ref[j, rows, :]
                    tied = (blk == thr) & found
                    eq = jnp.where(tied, 1.0, 0.0)
                    rank = _dot(lstrict_ref[...], eq.astype(BF16)) + seen
                    skey_ref[j, rows, :] = jnp.where(tied & (rank >= need), INT_MIN, blk)
                    seen = seen + jnp.sum(eq, axis=0, keepdims=True)
                return seen

            body(g, lax.fori_loop(0, g, body, jnp.zeros((1, LANES), F32)), nr_diag)

    mrows = lax.broadcasted_iota(I32, (LANES, 1), 0)
    k_meta = kr_ref[0, 0:LANES, :]
    ones = jnp.ones((ONES_ROWS, LANES), BF16)
    v_meta = jnp.concatenate([vmt_ref[...], ones], axis=0)
    init = []
    for grp_ in groups:
        s_meta = _dot_nt(k_meta, grp_["q_hq"])
        s_meta = jnp.where((mrows >= PAD) & (mrows < CHUNK), s_meta, neg_inf)
        m0 = jnp.max(s_meta, axis=0, keepdims=True)
        ot_ref[grp_["j"]] = _dot(v_meta, jnp.exp(s_meta - m0).astype(BF16))
        init.append(m0)
        grp_["thr"] = jnp.maximum(grp_["thr"], INT_MIN + 1)

    def pv_chunk(kc, carry, nrows=full_rows):
        base = pl.multiple_of(kc * KEY_CHUNK, KEY_CHUNK)
        nblk = KEY_CHUNK // LANES
        vblk = jnp.concatenate([jnp.concatenate([vt_ref[nblk * kc + i], ones], axis=0) for i in range(nblk)], axis=1)
        out, probs = [], []
        for grp_, nr, m in zip(groups, nrows, carry):
            rows = pl.ds(base, nr)
            bias = jnp.where(skey_ref[grp_["j"], rows, :] >= grp_["thr"], 0.0, neg_inf)
            s = st_ref[grp_["j"], rows, :] + jnp.concatenate([bias] * B_HEADS, axis=1)
            m_new = jnp.maximum(m, jnp.max(jnp.max(s.reshape(nr // 8, 8, B_HEADS * LANES), axis=0),
                                           axis=0, keepdims=True))
            out.append(m_new)
            probs.append((jnp.exp(m - m_new), jnp.exp(s - m_new).astype(BF16)))
        for grp_, nr, (alpha, prb) in zip(groups, nrows, probs):
            ot_ref[grp_["j"]] = ot_ref[grp_["j"]] * alpha + _dot(vblk[:, :nr], prb)
        return tuple(out)

    pv_chunk(g, lax.fori_loop(0, g, pv_chunk, tuple(init)), diag_rows)
    for grp_ in groups:
        acc = ot_ref[grp_["j"]]
        o_hq = (acc[:B_DH] / acc[B_DH:B_DH + 1]).T
        for hd in range(B_HEADS):
            out_ref[0, pl.ds(grp_["r0"], qpair), hd * B_DH:(hd + 1) * B_DH] = o_hq[hd * LANES:(hd + 1) * LANES, :]


def _dsa_call(qr, kr, v, iqr, ike, iko, sm, lstrict, seq, topk):
    bsz, tp, _ = qr.shape
    nsteps, rem = divmod(seq, 2 * CHUNK * QUERY_GROUPS)
    assert rem == 0
    full = lambda w_: pl.BlockSpec((1, tp, w_), lambda b, p: (b, 0, 0))
    hw = B_HEADS * B_DH
    nq = QUERY_GROUPS
    return pl.pallas_call(
        functools.partial(_dsa_body, seq=seq, topk=topk),
        grid=(bsz, nsteps),
        in_specs=[full(hw), full(B_DH), full(B_DH), full(hw), full(LANES), full(LANES), full(LANES),
                  _const_spec(lstrict.shape)],
        out_specs=full(hw),
        out_shape=jax.ShapeDtypeStruct((bsz, tp, hw), F32),
        scratch_shapes=[pltpu.VMEM((seq // LANES, B_DH, LANES), BF16), pltpu.VMEM((B_DH, LANES), BF16),
                        pltpu.VMEM((nq, seq, LANES), I32), pltpu.VMEM((nq, 32, seq // 32, LANES), I32),
                        pltpu.VMEM((nq, 3, 8, LANES), I32),
                        pltpu.VMEM((nq, seq, hw), F32), pltpu.VMEM((nq, B_DH + ONES_ROWS, hw), F32)],
        compiler_params=_params(("parallel", "arbitrary")),
        name="dsa",
    )(qr, kr, v, iqr, ike, iko, sm, lstrict)


def _mix_body(h_ref, oa_ref, ob_ref, g_ref, wg_ref, bg_ref, wa_ref, wb_ref, wo_ref, out_ref, *, rb):
    x = h_ref[0]
    d = x.shape[-1]
    n = _rms(x, g_ref[...]).astype(BF16)
    gates = _sigmoid(_dot(n, wg_ref[...]) + bg_ref[...])
    y = gates[:, :d] * _dot(oa_ref[0].astype(BF16), wa_ref[...]) + gates[:, d:] * _dot(ob_ref[0].astype(BF16), wb_ref[...])
    out = x + _dot(y.astype(BF16), wo_ref[...])
    rows = pl.program_id(1) * rb + lax.broadcasted_iota(I32, (rb, 1), 0)
    out_ref[0] = jnp.where(rows >= PAD, out, 0.0)


def _mix_call(h, oa, ob, gain, wg, bg, wa, wb, wo, rb):
    bsz, tp, d = h.shape
    row = lambda w_: pl.BlockSpec((1, rb, w_), lambda b, t: (b, t, 0))
    return pl.pallas_call(
        functools.partial(_mix_body, rb=rb),
        grid=(bsz, tp // rb),
        in_specs=[row(d), row(oa.shape[-1]), row(ob.shape[-1]), _const_spec((1, d)), _const_spec(wg.shape),
                  _const_spec(bg.shape), _const_spec(wa.shape), _const_spec(wb.shape), _const_spec(wo.shape)],
        out_specs=row(d),
        out_shape=jax.ShapeDtypeStruct((bsz, tp, d), F32),
        compiler_params=_params(("parallel", "arbitrary")),
        name="mix",
    )(h, oa, ob, gain, wg, bg, wa, wb, wo)


def _ffn_body(h_ref, g_ref, wup_ref, cw_ref, wd_ref, out_ref, carry_ref, act_ref, *, rb, dff, frames_only):
    t = pl.program_id(1)

    @pl.when(t == 0)
    def _():
        carry_ref[...] = jnp.zeros_like(carry_ref)

    if frames_only:
        rb = rb + 8
    x = h_ref[0]
    n = _rms(x, g_ref[...]).astype(BF16)
    rows = lax.broadcasted_iota(I32, (rb, 1), 0)
    nch = dff // FF_CHUNK

    def gate_up(c):
        return (_dot(n, wup_ref[:, c * FF_CHUNK:(c + 1) * FF_CHUNK]),
                _dot(n, wup_ref[:, dff + c * FF_CHUNK:dff + (c + 1) * FF_CHUNK]))

    ahead = gate_up(0)
    for c in range(nch):
        cols = slice(c * FF_CHUNK, (c + 1) * FF_CHUNK)
        gate, up = ahead
        if c + 1 < nch:
            ahead = gate_up(c + 1)
        g1, g2 = pltpu.roll(gate, 1, 0), pltpu.roll(gate, 2, 0)
        if not frames_only:
            prev = carry_ref[:, cols]
            g1 = jnp.where(rows == 0, prev[7:8], g1)
            g2 = jnp.where(rows == 0, prev[6:7], jnp.where(rows == 1, prev[7:8], g2))
            carry_ref[:, cols] = gate[rb - 8:rb]
        cw = cw_ref[:, cols]
        conv = g2 * cw[0:1] + g1 * cw[1:2] + gate * cw[2:3]
        act_ref[:, cols] = (_silu(conv) * up).astype(BF16)
    out = x + _dot(act_ref[...], wd_ref[...])
    if frames_only:
        out_ref[0] = out[8:]
    else:
        out_ref[0] = jnp.where(t * rb + rows >= PAD, out, 0.0)


def _ffn_call(h, gain, wup, cw, wd, rb, frames_only=False):
    bsz, tp, d = h.shape
    dff = wd.shape[0]
    if frames_only:
        rb = FRAME_BLOCK
        nrows = tp - CHUNK
        in_rows = pl.BlockSpec((pl.Element(1), pl.Element(rb + 8), pl.Element(d)),
                               lambda b, t: (b, pl.multiple_of(CHUNK - 8 + t * rb, 8), 0))
    else:
        nrows = tp
        in_rows = pl.BlockSpec((1, rb, d), lambda b, t: (b, t, 0))
    return pl.pallas_call(
        functools.partial(_ffn_body, rb=rb, dff=dff, frames_only=frames_only),
        grid=(bsz, nrows // rb),
        in_specs=[in_rows, _const_spec((1, d)), _const_spec(wup.shape), _const_spec(cw.shape),
                  _const_spec(wd.shape)],
        out_specs=pl.BlockSpec((1, rb, d), lambda b, t: (b, t, 0)),
        out_shape=jax.ShapeDtypeStruct((bsz, nrows, d), F32),
        scratch_shapes=[pltpu.VMEM((8, dff), F32), pltpu.VMEM((rb + 8 * frames_only, dff), BF16)],
        compiler_params=_params(("parallel", "arbitrary")),
        name="ffn",
    )(h, gain, wup, cw, wd)


def _rope_tables(tp):
    pos = (jnp.arange(tp, dtype=F32) - PAD)[:, None]

    def cs(dim):
        inv = 1.0 / (ROPE_THETA ** (jnp.arange(0, dim, 2, dtype=F32) / dim))
        ang = pos * inv[None, :]
        return jnp.cos(ang), jnp.sin(ang)

    ca, sa = cs(B_DH)
    ci, si = cs(IDX_DIM)
    zi = jnp.zeros_like(si)
    return (jnp.concatenate([ca, ca], axis=1), jnp.concatenate([-sa, sa], axis=1),
            jnp.concatenate([ci] * 4, axis=1), jnp.concatenate([-si, zi, -si, zi], axis=1),
            jnp.concatenate([zi, si, zi, si], axis=1))


def _pad_lanes(vec):
    return jnp.zeros((1, LANES), F32).at[0, :vec.shape[0]].set(vec.astype(F32))


def _head_expander():
    pw, hw = A_HEADS * CHUNK, A_HEADS * A_DK
    head = np.concatenate([np.arange(pw) // CHUNK, np.arange(hw) // A_DK])
    return jnp.asarray(np.arange(LANES)[:, None] == head[None, :], BF16)


def kernel(x, meta_tokens, norm_mix, w_in, conv_a, a_log, dt_bias, a_out_norm, q_norm, k_norm, kidx_norm,
           w_branch_a, w_branch_b, w_gate, b_gate, w_out, norm_ffn, w_up, conv_ffn, w_down):
    bsz, seq, d = x.shape
    depth = w_in.shape[0]
    assert seq % KEY_CHUNK == 0
    tp = CHUNK + seq
    rb = _row_block(tp // CHUNK)
    topk = min(TOPK_MAX, seq // TOPK_DIV)

    meta = jnp.broadcast_to(meta_tokens.astype(x.dtype)[None], (bsz, N_META, d))
    h = jnp.concatenate([jnp.zeros((bsz, PAD, d), x.dtype), meta, x], axis=1)

    tables = _rope_tables(tp)
    expand = _head_expander()
    tri = np.arange(CHUNK)
    lt3 = jnp.asarray(np.tile(tri[:, None] >= tri[None, :], (1, 3)), BF16)
    trk = np.arange(LANES)
    lstrict = jnp.asarray(trk[:, None] > trk[None, :], BF16)

    hw = A_HEADS * A_DK
    o = np.cumsum((0, hw, hw, hw, hw, A_HEADS, A_HEADS, B_HEADS * B_DH, B_DH, B_DH,
                   IDX_HEADS * IDX_DIM, IDX_DIM, IDX_HEADS))
    for l in range(depth):
        w = w_in[l]
        small = jnp.concatenate([w[:, o[4]:o[6]], w[:, o[11]:o[12]],
                                 jnp.zeros((d, LANES - 2 * A_HEADS - IDX_HEADS), w.dtype)], axis=1)
        w_p = jnp.concatenate([w[:, o[0]:o[3]], w[:, o[3]:o[4]], w[:, o[6]:o[7]], w[:, o[7]:o[9]],
                               w[:, o[9]:o[10]], small, w[:, o[10]:o[11]], w[:, o[10]:o[11]]], axis=1).astype(BF16)
        qkv, z, sm, qr, kr, v, iqr, ike, iko = _proj_call(
            h, norm_mix[l][None], w_p, tables, q_norm[l][None], k_norm[l][None],
            jnp.concatenate([kidx_norm[l], kidx_norm[l]])[None], conv_a[l], rb)
        o_a = _gdn_call(qkv, z, sm, _pad_lanes(a_log[l]), _pad_lanes(dt_bias[l]),
                        a_out_norm[l][None], lt3, expand, rb)
        o_b = _dsa_call(qr, kr, v, iqr, ike, iko, sm, lstrict, seq, topk)
        h = _mix_call(h, o_a, o_b, norm_mix[l][None], w_gate[l].astype(BF16), b_gate[l][None],
                      w_branch_a[l].astype(BF16), w_branch_b[l].astype(BF16), w_out[l].astype(BF16), rb)
        h = _ffn_call(h, norm_ffn[l][None], w_up[l].astype(BF16), conv_ffn[l], w_down[l].astype(BF16), rb,
                      frames_only=l == depth - 1)
    return h
```

```python
import functools
import math

import jax
import jax.numpy as jnp
import numpy as np
from jax import lax
from jax.experimental import pallas as pl
from jax.experimental.pallas import tpu as pltpu

F32 = jnp.float32
BF16 = jnp.bfloat16
I32 = jnp.int32

CHUNK = 64
N_META = 16
PAD = CHUNK - N_META
ROPE_THETA = 10000.0
EPS = 1e-6
A_HEADS = 4
A_DK = 128
A_CONV = 4
B_HEADS = 4
B_DH = 128
IDX_HEADS = 8
IDX_DIM = 64
TOPK_MAX = 256
TOPK_DIV = 4
FFN_CONV = 3
LANES = 128
KEY_CHUNK = 512
PLANE_ROWS = 256
QUERY_GROUPS = 4
ONES_ROWS = 16
FF_CHUNK = 256
FRAME_BLOCK = 512
GDN_SLOTS = 4
INT_MIN = -(2 ** 31)
VMEM_LIMIT_BYTES = 56 * 1024 * 1024

NT_DIMS = (((1,), (1,)), ((), ()))


def _dot(a, b):
    return jnp.dot(a, b, preferred_element_type=F32)


def _dot_nt(a, b):
    return lax.dot_general(a, b, NT_DIMS, preferred_element_type=F32)


def _dot_exact(a, b):
    return jnp.dot(a, b, preferred_element_type=F32, precision=lax.Precision.HIGHEST)


def _rms(x, gain):
    return x * lax.rsqrt(jnp.mean(x * x, axis=-1, keepdims=True) + EPS) * gain


def _sigmoid(x):
    return 1.0 / (1.0 + jnp.exp(-x))


def _silu(x):
    return x * _sigmoid(x)


def _softplus(x):
    return jnp.maximum(x, 0.0) + jnp.log(1.0 + jnp.exp(-jnp.abs(x)))


def _row_block(n_chunks):
    g = max(d for d in range(1, 12) if n_chunks % d == 0)
    return g * CHUNK


def _params(sem):
    return pltpu.CompilerParams(dimension_semantics=sem, vmem_limit_bytes=VMEM_LIMIT_BYTES)


def _const_spec(shape):
    nd = len(shape)
    return pl.BlockSpec(shape, lambda *_: (0,) * nd)


PROJ_WIDTHS = (1536, 512, 512, 256, 512, 256)


def _proj_body(h_ref, g_ref, w_ref, cosa_ref, sina_ref, cosi_ref, sinlo_ref, sinhi_ref,
               qn_ref, kn_ref, kin_ref, cw_ref,
               qkv_ref, z_ref, sm_ref, qr_ref, kr_ref, v_ref, iqr_ref, ike_ref, iko_ref, tail_ref, *, rb):
    @pl.when(pl.program_id(1) == 0)
    def _():
        tail_ref[...] = jnp.zeros_like(tail_ref)

    n = _rms(h_ref[0], g_ref[...]).astype(BF16)
    offs = dict(zip(("qkv", "z", "q", "kv", "iq", "sm_ik"), np.cumsum((0,) + PROJ_WIDTHS)[:-1]))
    rows8 = lax.broadcasted_iota(I32, (8, 1), 0)
    lane = lax.broadcasted_iota(I32, (1, LANES), 1)
    ca, sa = cosa_ref[...], sina_ref[...]
    ci, slo, shi = cosi_ref[...], sinlo_ref[...], sinhi_ref[...]

    def conv_silu(x, cols, r0, nrows):
        prev, cw = (tail_ref[:, cols] if r0 == 0 else x[r0 - 8:r0]), cw_ref[:, cols]
        xb = x[r0:r0 + nrows]
        acc = None
        for back in range(A_CONV - 1, 0, -1):
            shifted = pltpu.roll(xb, back, 0)
            top = shifted[:8]
            for r in range(back):
                top = jnp.where(rows8 == r, prev[8 - back + r:8 - back + r + 1], top)
            term = jnp.concatenate([top, shifted[8:]], axis=0) * cw[A_CONV - 1 - back:A_CONV - back]
            acc = term if acc is None else acc + term
        return _silu(acc + xb * cw[A_CONV - 1:A_CONV])

    half = rb // 2
    assert half % 8 == 0

    def gdn_piece(i):
        cols = slice(i * 256, (i + 1) * 256)

        def finish(x):
            def first():
                qkv_ref[0, 0:half, cols] = conv_silu(x, cols, 0, half)

            def second():
                qkv_ref[0, half:rb, cols] = conv_silu(x, cols, half, half)
                tail_ref[:, cols] = x[rb - 8:rb]
            return [first, second]
        return offs["qkv"] + i * 256, finish

    def plain_piece(i):
        def finish(x):
            def store():
                z_ref[0, :, i * 256:(i + 1) * 256] = x
            return [store]
        return offs["z"] + i * 256, finish

    def single(fn):
        return lambda x: [functools.partial(fn, x)]

    def q_piece(i):
        @single
        def finish(x):
            parts = []
            for hd in range(2):
                qh = _rms(x[:, hd * B_DH:(hd + 1) * B_DH], qn_ref[...])
                parts.append(qh * ca + pltpu.roll(qh, B_DH // 2, 1) * sa)
            qr_ref[0, :, i * 256:(i + 1) * 256] = (jnp.concatenate(parts, axis=1) * (B_DH ** -0.5)).astype(BF16)
        return offs["q"] + i * 256, finish

    def kv_piece():
        @single
        def finish(x):
            k = _rms(x[:, :B_DH], kn_ref[...])
            kr_ref[0] = (k * ca + pltpu.roll(k, B_DH // 2, 1) * sa).astype(BF16)
            v_ref[0] = x[:, B_DH:].astype(BF16)
        return offs["kv"], finish

    def iq_piece(i):
        @single
        def finish(x):
            parts = []
            for hp in range(2):
                xh = x[:, hp * LANES:(hp + 1) * LANES]
                parts.append(xh * ci + pltpu.roll(xh, LANES - IDX_DIM // 2, 1) * slo
                             + pltpu.roll(xh, IDX_DIM // 2, 1) * shi)
            iqr_ref[0, :, i * 256:(i + 1) * 256] = jnp.concatenate(parts, axis=1).astype(BF16)
        return offs["iq"] + i * 256, finish

    def small_ik_piece():
        @single
        def finish(x):
            sm_ref[0] = x[:, :LANES]
            ik = _rms(x[:, LANES:], kin_ref[...])
            ik = ik * ci + pltpu.roll(ik, IDX_DIM // 2, 1) * (slo + shi)
            ike_ref[0] = jnp.where(lane < IDX_DIM, ik, 0.0).astype(BF16)
            iko_ref[0] = jnp.where(lane >= IDX_DIM, ik, 0.0).astype(BF16)
        return offs["sm_ik"], finish

    pieces = [q_piece(0), gdn_piece(0), q_piece(1), gdn_piece(1), iq_piece(0), gdn_piece(2), iq_piece(1),
              gdn_piece(3), kv_piece(), gdn_piece(4), small_ik_piece(), gdn_piece(5), plain_piece(0), plain_piece(1)]
    ready, staged = [], []
    for col, finish in pieces:
        x = _dot(n, w_ref[:, col:col + 256])
        ready += staged
        staged = finish(x)
        for task in [ready.pop(0) for _ in range(min(1, len(ready)))]:
            task()
    for task in ready + staged:
        task()


def _proj_call(h, gain, w, tables, qn, kn, kin, cw, rb):
    bsz, tp, d = h.shape
    nt = tp // rb
    row = lambda w_, dt: (jax.ShapeDtypeStruct((bsz, tp, w_), dt),
                          pl.BlockSpec((1, rb, w_), lambda b, t: (b, t, 0)))
    outs = [row(1536, F32), row(512, F32), row(128, F32), row(512, BF16), row(128, BF16),
            row(128, BF16), row(512, BF16), row(128, BF16), row(128, BF16)]
    tab_spec = pl.BlockSpec((rb, LANES), lambda b, t: (t, 0))
    return pl.pallas_call(
        functools.partial(_proj_body, rb=rb),
        grid=(bsz, nt),
        in_specs=[pl.BlockSpec((1, rb, d), lambda b, t: (b, t, 0)), _const_spec((1, d)), _const_spec(w.shape)]
                 + [tab_spec] * 5 + [_const_spec((1, LANES))] * 3 + [_const_spec(cw.shape)],
        out_specs=[o[1] for o in outs],
        out_shape=[o[0] for o in outs],
        scratch_shapes=[pltpu.VMEM((8, PROJ_WIDTHS[0]), F32)],
        compiler_params=_params(("parallel", "arbitrary")),
        name="proj",
    )(h, gain, w, *tables, qn, kn, kin, cw)


def _split3(x):
    x1 = x.astype(BF16)
    r1 = x - x1.astype(F32)
    x2 = r1.astype(BF16)
    x3 = (r1 - x2.astype(F32)).astype(BF16)
    return jnp.concatenate([x1, x2, x3], axis=0)


def _gdn_body(qkv_ref, z_ref, sm_ref, alog_ref, dtb_ref, ng_ref, lt3_ref, expand_ref,
              out_ref, s01_ref, s23_ref, u_ref, wq_ref, qkd_ref, kdt_ref, egl_ref, *, rb):
    t = pl.program_id(1)

    @pl.when(t == 0)
    def _():
        s01_ref[...] = jnp.zeros_like(s01_ref)
        s23_ref[...] = jnp.zeros_like(s23_ref)

    hw = A_HEADS * A_DK
    pw = A_HEADS * CHUNK
    ii = lax.broadcasted_iota(I32, (CHUNK, pw), 0)
    jj = lax.broadcasted_iota(I32, (CHUNK, pw), 1) & (CHUNK - 1)
    colhead = lax.broadcasted_iota(I32, (CHUNK, pw), 1) >> 6
    eye_p = ii == jj
    bd_mask = (lax.broadcasted_iota(I32, (pw, pw), 0) >> 6) == (lax.broadcasted_iota(I32, (pw, pw), 1) >> 6)
    bdk_mask = (lax.broadcasted_iota(I32, (pw, hw), 0) >> 6) == (lax.broadcasted_iota(I32, (pw, hw), 1) >> 7)
    pair_mask = (lax.broadcasted_iota(I32, (pw, pw), 0) >> 7) == (lax.broadcasted_iota(I32, (pw, pw), 1) >> 7)
    lane = lax.broadcasted_iota(I32, (1, LANES), 1)
    rows64 = lax.broadcasted_iota(I32, (CHUNK, 1), 0)
    neg_a = -jnp.exp(alog_ref[...])
    dtb = dtb_ref[...]
    ng = ng_ref[...]

    def block_diag(xp):
        return jnp.where(bd_mask, jnp.concatenate([xp.astype(BF16)] * A_HEADS, axis=0), 0)

    def stack_heads(xp):
        return jnp.concatenate([jnp.where(colhead == hd, xp, 0.0) for hd in range(A_HEADS)], axis=0)

    def l2n(x):
        parts = []
        for hd in range(A_HEADS):
            xh = x[:, hd * A_DK:(hd + 1) * A_DK]
            parts.append(xh * lax.rsqrt(jnp.sum(xh * xh, axis=-1, keepdims=True) + EPS))
        return jnp.concatenate(parts, axis=1)

    zeros = jnp.zeros((CHUNK, 2 * A_DK), F32)

    def pair_lanes(x, a_, b_):
        return jnp.concatenate([x[a_ * CHUNK:(a_ + 1) * CHUNK], x[b_ * CHUNK:(b_ + 1) * CHUNK]], axis=1)

    def local_load(c):
        r0 = c * CHUNK if isinstance(c, int) else pl.multiple_of(c * CHUNK, CHUNK)
        return qkv_ref[0, pl.ds(r0, CHUNK), :], sm_ref[0, pl.ds(r0, CHUNK), :], r0

    def local_compute(xq, sm, r0):
        q = l2n(xq[:, :hw]) * (A_DK ** -0.5)
        k = l2n(xq[:, hw:2 * hw])
        v = xq[:, 2 * hw:]

        g = jnp.where(lane < A_HEADS, neg_a * _softplus(sm + dtb), 0.0)
        g = jnp.where(t * rb + r0 + rows64 >= PAD, g, 0.0)
        to_packed, to_wide = expand_ref[:, :pw], expand_ref[:, pw:]
        g3 = _split3(g)
        cg = _dot(g3, to_packed).astype(BF16)
        gcs = _dot(lt3_ref[...], g3)
        yield
        gp = _dot(lt3_ref[...], cg)
        in_head = lane < A_HEADS
        beta = jnp.where(in_head, pltpu.roll(_sigmoid(sm), LANES - A_HEADS, 1), 0.0)
        eg_s = jnp.where(in_head, jnp.exp(gcs), 0.0)
        to_last = jnp.where(in_head, jnp.exp(gcs[CHUNK - 1:CHUNK] - gcs), 0.0)
        wide = _dot(jnp.concatenate([beta, beta * eg_s, eg_s, to_last], axis=0).astype(BF16), to_wide)
        bp = _dot(beta.astype(BF16), to_packed)
        glast3 = _dot(_split3(gcs[CHUNK - 16:]), to_wide)
        yield
        bq, bq_eg, eg, kd_scale = (wide[i * CHUNK:(i + 1) * CHUNK] for i in range(4))
        glast = glast3[15:16] + glast3[31:32] + glast3[47:48]
        grow = jnp.sum(jnp.where(eye_p, gp, 0.0), axis=0, keepdims=True)
        decay = jnp.where(ii >= jj, jnp.exp(gp - grow), 0.0)

        kb = k.astype(BF16)
        bdk = jnp.where(bdk_mask, jnp.concatenate([kb] * A_HEADS, axis=0), 0)
        kq = _dot_nt(jnp.concatenate([kb, q.astype(BF16)], axis=0), bdk)
        yield
        kk_p, qk_p = kq[:CHUNK], kq[CHUNK:]

        a = -jnp.where(ii > jj, bp * kk_p * decay, 0.0)
        tinv = jnp.where(eye_p, 1.0, 0.0) + a
        pk = _dot(a.astype(BF16), block_diag(a))
        yield
        for _ in range(4):
            pt = _dot(jnp.concatenate([pk, tinv], axis=0).astype(BF16), block_diag(pk))
            yield
            tinv = tinv + pt[CHUNK:]
            pk = pt[:CHUNK]
        tinv = tinv + _dot(tinv.astype(BF16), block_diag(pk))
        yield

        vb = v * bq
        kbg = k * bq_eg
        rv = jnp.concatenate(
            [jnp.concatenate([vb[:, hd * A_DK:(hd + 1) * A_DK], kbg[:, hd * A_DK:(hd + 1) * A_DK]], axis=1)
             for hd in range(A_HEADS)], axis=0)
        uw = _dot(stack_heads(tinv).astype(BF16), rv.astype(BF16))
        yield
        u, w = uw[:, :A_DK], uw[:, A_DK:]

        qg = q * eg
        kd = k * kd_scale
        return (u,
                jnp.concatenate([pair_lanes(w, 0, 1), qg[:, :2 * A_DK]], axis=0).astype(BF16),
                jnp.concatenate([pair_lanes(w, 2, 3), qg[:, 2 * A_DK:]], axis=0).astype(BF16),
                stack_heads(qk_p * decay).astype(BF16),
                jnp.concatenate([kd[:, :2 * A_DK], zeros], axis=0).T.astype(BF16),
                jnp.concatenate([kd[:, 2 * A_DK:], zeros], axis=0).T.astype(BF16),
                jnp.broadcast_to(jnp.exp(glast), (8, hw)))

    def slot(c):
        return c % GDN_SLOTS if isinstance(c, int) else c & (GDN_SLOTS - 1)

    def local_store(c, vals):
        s = slot(c)
        u_ref[s], wq_ref[s, 0], wq_ref[s, 1], qkd_ref[s], kdt_ref[s, 0], kdt_ref[s, 1], egl_ref[s] = vals

    def local_chain(c):
        vals = yield from local_compute(*local_load(c))
        local_store(c, vals)

    def state_chain(chunks):
        def unpair(r):
            return [r[:, :A_DK], r[:, A_DK:]]

        def state_update(s_old, kdt, vn_pair, egl_pair):
            upd = _dot(kdt, jnp.concatenate([vn_pair, zeros], axis=0).astype(BF16))
            return s_old * egl_pair + jnp.where(pair_mask, upd, 0.0)

        s01 = s01_ref[...]
        s23 = s23_ref[...]
        for c in chunks:
            r0 = c * CHUNK if isinstance(c, int) else pl.multiple_of(c * CHUNK, CHUNK)
            s = slot(c)
            r01 = _dot(wq_ref[s, 0], s01.astype(BF16))
            r23 = _dot(wq_ref[s, 1], s23.astype(BF16))
            yield
            ws = jnp.concatenate(unpair(r01[:CHUNK]) + unpair(r23[:CHUNK]), axis=0)
            qs = jnp.concatenate(unpair(r01[CHUNK:]) + unpair(r23[CHUNK:]), axis=0)
            vn = u_ref[s] - ws
            o_rs = qs + _dot(qkd_ref[s], vn.astype(BF16))
            egl = egl_ref[s][0:1, :]
            s01 = state_update(s01, kdt_ref[s, 0], pair_lanes(vn, 0, 1), egl[:, :2 * A_DK])
            s23 = state_update(s23, kdt_ref[s, 1], pair_lanes(vn, 2, 3), egl[:, 2 * A_DK:])
            yield
            zz = z_ref[0, pl.ds(r0, CHUNK), :]
            parts = [_rms(o_rs[hd * CHUNK:(hd + 1) * CHUNK], ng) for hd in range(A_HEADS)]
            out_ref[0, pl.ds(r0, CHUNK), :] = jnp.concatenate(parts, axis=1) * _silu(zz)
        s01_ref[...] = s01
        s23_ref[...] = s23

    def run_lockstep(chains):
        live = list(chains)
        while live:
            for gen in list(live):
                try:
                    next(gen)
                except StopIteration:
                    live.remove(gen)

    nc = rb // CHUNK
    npairs = nc // 2
    if npairs == 0:
        run_lockstep([local_chain(0)])
        run_lockstep([state_chain([0])])
    else:
        run_lockstep([local_chain(0), local_chain(1)])

        def steady(i, carry):
            run_lockstep([local_chain(2 * i), local_chain(2 * i + 1), state_chain([2 * i - 2, 2 * i - 1])])
            return carry

        lax.fori_loop(1, npairs, steady, 0)
        last = [2 * npairs - 2, 2 * npairs - 1]
        if nc % 2:
            run_lockstep([local_chain(nc - 1), state_chain(last)])
            run_lockstep([state_chain([nc - 1])])
        else:
            run_lockstep([state_chain(last)])


def _gdn_call(qkv, z, sm, alog, dtb, ng, lt3, expand):
    bsz, tp, _ = qkv.shape
    rb = tp
    nt = tp // rb
    nc = GDN_SLOTS
    hw = A_HEADS * A_DK
    pw = A_HEADS * CHUNK
    row = lambda w_: pl.BlockSpec((1, rb, w_), lambda b, t: (b, t, 0))
    return pl.pallas_call(
        functools.partial(_gdn_body, rb=rb),
        grid=(bsz, nt),
        in_specs=[row(3 * hw), row(hw), row(LANES), _const_spec((1, LANES)),
                  _const_spec((1, LANES)), _const_spec((1, LANES)), _const_spec(lt3.shape),
                  _const_spec(expand.shape)],
        out_specs=row(hw),
        out_shape=jax.ShapeDtypeStruct((bsz, tp, hw), F32),
        scratch_shapes=[pltpu.VMEM((2 * A_DK, 2 * A_DK), F32), pltpu.VMEM((2 * A_DK, 2 * A_DK), F32),
                        pltpu.VMEM((nc, pw, A_DK), F32), pltpu.VMEM((nc, 2, 2 * CHUNK, 2 * A_DK), BF16),
                        pltpu.VMEM((nc, pw, pw), BF16), pltpu.VMEM((nc, 2, 2 * A_DK, 2 * CHUNK), BF16),
                        pltpu.VMEM((nc, 8, hw), F32)],
        compiler_params=_params(("parallel", "arbitrary")),
        name="gdn",
    )(qkv, z, sm, alog, dtb, ng, lt3, expand)


def _dsa_body(qr_ref, kr_ref, v_ref, iqr_ref, ike_ref, iko_ref, sm_ref, lstrict_ref, out_ref,
              vt_ref, vmt_ref, skey_ref, planes_ref, sel_ref, st_ref, ot_ref, *, seq, topk):
    g = pl.program_id(1)
    iw_scale = IDX_HEADS ** -0.5 * IDX_DIM ** -0.5
    qpair = 2 * CHUNK
    neg_inf = -jnp.inf

    @pl.when(g == 0)
    def _():
        planes_ref[...] = jnp.zeros_like(planes_ref)

        def vblk(i, carry):
            r = pl.multiple_of(CHUNK + i * LANES, CHUNK)
            vt_ref[i] = v_ref[0, pl.ds(r, LANES), :].astype(F32).T.astype(BF16)
            return carry

        lax.fori_loop(0, seq // LANES, vblk, 0)
        vmt_ref[...] = v_ref[0, 0:LANES, :].astype(F32).T.astype(BF16)

        qm = qr_ref[0, 0:CHUNK, :]
        km = kr_ref[0, 0:CHUNK, :]
        vm = v_ref[0, 0:CHUNK, :]
        colv = lax.broadcasted_iota(I32, (1, CHUNK), 1) >= PAD
        parts = []
        for hd in range(B_HEADS):
            s = _dot_nt(qm[:, hd * B_DH:(hd + 1) * B_DH], km)
            s = jnp.where(colv, s, neg_inf)
            e = jnp.exp(s - jnp.max(s, axis=-1, keepdims=True))
            pr = e / jnp.sum(e, axis=-1, keepdims=True)
            parts.append(_dot(pr.astype(BF16), vm))
        out_ref[0, 0:CHUNK, :] = jnp.concatenate(parts, axis=1)

    lane = lax.broadcasted_iota(I32, (1, LANES), 1)
    groups = []
    for j in range(QUERY_GROUPS):
        p = g * QUERY_GROUPS + j
        r0 = pl.multiple_of(CHUNK + p * qpair, CHUNK)
        qb = qr_ref[0, pl.ds(r0, qpair), :]
        ib = iqr_ref[0, pl.ds(r0, qpair), :]
        smt = sm_ref[0, pl.ds(r0, qpair), :].T
        groups.append(dict(
            j=j, r0=r0,
            q_hq=jnp.concatenate([qb[:, hd * B_DH:(hd + 1) * B_DH] for hd in range(B_HEADS)], axis=0),
            iq_hq=jnp.concatenate([ib[:, hp * LANES:(hp + 1) * LANES] for hp in range(IDX_HEADS // 2)], axis=0),
            wts=[smt[2 * A_HEADS + hd:2 * A_HEADS + hd + 1, :] * iw_scale for hd in range(IDX_HEADS)],
            limit=jnp.where(lane < CHUNK, (2 * p + 1) * CHUNK, (2 * p + 2) * CHUNK)))
    assert QUERY_GROUPS * qpair == KEY_CHUNK
    full_rows = [KEY_CHUNK] * QUERY_GROUPS
    diag_rows = [(j + 1) * qpair for j in range(QUERY_GROUPS)]
    diag_plane_rows = [-(-r // PLANE_ROWS) * PLANE_ROWS for r in diag_rows]
    krows = lax.broadcasted_iota(I32, (KEY_CHUNK, 1), 0)

    def score_chunk(kc, carry, nrows=full_rows):
        k0 = pl.multiple_of(CHUNK + kc * KEY_CHUNK, CHUNK)
        base = pl.multiple_of(kc * KEY_CHUNK, KEY_CHUNK)
        ke, ko, kk = (ref[0, pl.ds(k0, KEY_CHUNK), :] for ref in (ike_ref, iko_ref, kr_ref))
        logits = []
        for grp_, nr in zip(groups, nrows):
            logits.append((_dot_nt(ke[:nr], grp_["iq_hq"]), _dot_nt(ko[:nr], grp_["iq_hq"])))
            st_ref[grp_["j"], pl.ds(base, nr), :] = _dot_nt(kk[:nr], grp_["q_hq"])
        for grp_, nr, (le, lo) in zip(groups, nrows, logits):
            score_tail(kc, pl.ds(base, nr), nr, grp_, le, lo)
        return carry

    def score_tail(kc, rows, nr, grp_, le, lo):
        j, wts, limit = grp_["j"], grp_["wts"], grp_["limit"]
        sc = jnp.zeros((nr, LANES), F32)
        for hp in range(IDX_HEADS // 2):
            sc = sc + wts[2 * hp] * jnp.maximum(le[:, hp * LANES:(hp + 1) * LANES], 0.0)
            sc = sc + wts[2 * hp + 1] * jnp.maximum(lo[:, hp * LANES:(hp + 1) * LANES], 0.0)
        bits = lax.bitcast_convert_type(sc, I32)
        bits = jnp.where(sc == 0.0, 0, bits)
        key = bits ^ ((bits >> 31) & 0x7FFFFFFF)
        valid = kc * KEY_CHUNK + krows[:nr] < limit
        key = jnp.where(valid, key, INT_MIN)
        skey_ref[j, rows, :] = key
        for gi in range(nr // PLANE_ROWS):
            a = [key[gi * PLANE_ROWS + 8 * r:gi * PLANE_ROWS + 8 * r + 8, :] ^ INT_MIN for r in range(32)]
            sh, msk = 16, 0x0000FFFF
            while sh:
                k = 0
                while k < 32:
                    tt = (a[k] ^ lax.shift_right_logical(a[k + sh], jnp.int32(sh))) & msk
                    a[k] = a[k] ^ tt
                    a[k + sh] = a[k + sh] ^ (tt << sh)
                    k = (k + sh + 1) & ~sh
                sh >>= 1
                msk = (msk ^ (msk << sh)) & 0xFFFFFFFF
            g8 = pl.multiple_of((kc * (KEY_CHUNK // PLANE_ROWS) + gi) * 8, 8)
            for b in range(32):
                planes_ref[j, b, pl.ds(g8, 8), :] = a[b]

    lax.fori_loop(0, g, score_chunk, 0)
    score_chunk(g, 0, diag_plane_rows)

    words_per_chunk = KEY_CHUNK // 32

    def lane_count(words):
        c = lax.population_count(words)
        if words.shape[0] > 8:
            c = jnp.sum(c.reshape(words.shape[0] // 8, 8, LANES), axis=0)
        return jnp.sum(c, axis=0, keepdims=True)

    def select_threshold(step):
        nrow = (step + 1) * words_per_chunk
        prow = lax.broadcasted_iota(I32, (nrow, 1), 0)
        alive0 = [jnp.where(prow < step * words_per_chunk + nr // 32, jnp.full((nrow, LANES), -1, I32), 0)
                  for nr in diag_plane_rows]

        def bit_body(i, carry):
            bit = jnp.left_shift(jnp.int32(1), 31 - i)
            out = []
            for grp_, (tu, above, alive) in zip(groups, carry):
                hit = alive & planes_ref[grp_["j"], i, 0:nrow, :]
                c1 = lane_count(hit)
                take = above + c1 >= topk
                out.append((jnp.where(take, tu | bit, tu), jnp.where(take, above, above + c1),
                            jnp.where(take, hit, alive ^ hit)))
            return tuple(out)

        zero_row = jnp.zeros((1, LANES), I32)
        result = lax.fori_loop(0, 32, bit_body, tuple((zero_row, zero_row, a0) for a0 in alive0))
        for grp_, (tu, above, alive) in zip(groups, result):
            for n, row in enumerate((tu, above, lane_count(alive))):
                sel_ref[grp_["j"], n] = jnp.broadcast_to(row, (8, LANES))

    for step in range(seq // KEY_CHUNK):
        pl.when(g == step)(functools.partial(select_threshold, step))

    for grp_, nr_diag in zip(groups, diag_rows):
        tu, above, n_eq = (sel_ref[grp_["j"], n][0:1] for n in range(3))
        j, thr, found = grp_["j"], tu ^ INT_MIN, tu != 0
        grp_["thr"] = thr
        tie_lane = found & (above + n_eq > topk)

        @pl.when(jnp.max(jnp.where(tie_lane, 1, 0)) > 0)
        def _(j=j, thr=thr, found=found, above=above, nr_diag=nr_diag):
            need = (topk - above).astype(F32)

            def body(kc, seen, nr=KEY_CHUNK):
                base = pl.multiple_of(kc * KEY_CHUNK, KEY_CHUNK)
                for sub in range(nr // LANES):
                    rows = pl.ds(base + sub * LANES, LANES)
                    blk = skey_ref[j, rows, :]
                    tied = (blk == thr) & found
                    eq = jnp.where(tied, 1.0, 0.0)
                    rank = _dot(lstrict_ref[...], eq.astype(BF16)) + seen
                    skey_ref[j, rows, :] = jnp.where(tied & (rank >= need), INT_MIN, blk)
                    seen = seen + jnp.sum(eq, axis=0, keepdims=True)
                return seen

            body(g, lax.fori_loop(0, g, body, jnp.zeros((1, LANES), F32)), nr_diag)

    mrows = lax.broadcasted_iota(I32, (LANES, 1), 0)
    k_meta = kr_ref[0, 0:LANES, :]
    ones = jnp.ones((ONES_ROWS, LANES), BF16)
    v_meta = jnp.concatenate([vmt_ref[...], ones], axis=0)
    init = []
    for grp_ in groups:
        s_meta = _dot_nt(k_meta, grp_["q_hq"])
        s_meta = jnp.where((mrows >= PAD) & (mrows < CHUNK), s_meta, neg_inf)
        m0 = jnp.max(s_meta, axis=0, keepdims=True)
        ot_ref[grp_["j"]] = _dot(v_meta, jnp.exp(s_meta - m0).astype(BF16))
        init.append(m0)
        grp_["thr"] = jnp.maximum(grp_["thr"], INT_MIN + 1)

    def pv_chunk(kc, carry, nrows=full_rows):
        base = pl.multiple_of(kc * KEY_CHUNK, KEY_CHUNK)
        nblk = KEY_CHUNK // LANES
        vblk = jnp.concatenate([jnp.concatenate([vt_ref[nblk * kc + i], ones], axis=0) for i in range(nblk)], axis=1)
        out, probs = [], []
        for grp_, nr, m in zip(groups, nrows, carry):
            rows = pl.ds(base, nr)
            bias = jnp.where(skey_ref[grp_["j"], rows, :] >= grp_["thr"], 0.0, neg_inf)
            s = st_ref[grp_["j"], rows, :] + jnp.concatenate([bias] * B_HEADS, axis=1)
            m_new = jnp.maximum(m, jnp.max(jnp.max(s.reshape(nr // 8, 8, B_HEADS * LANES), axis=0),
                                           axis=0, keepdims=True))
            out.append(m_new)
            probs.append((jnp.exp(m - m_new), jnp.exp(s - m_new).astype(BF16)))
        for grp_, nr, (alpha, prb) in zip(groups, nrows, probs):
            ot_ref[grp_["j"]] = ot_ref[grp_["j"]] * alpha + _dot(vblk[:, :nr], prb)
        return tuple(out)

    pv_chunk(g, lax.fori_loop(0, g, pv_chunk, tuple(init)), diag_rows)
    for grp_ in groups:
        acc = ot_ref[grp_["j"]]
        o_hq = (acc[:B_DH] / acc[B_DH:B_DH + 1]).T
        for hd in range(B_HEADS):
            out_ref[0, pl.ds(grp_["r0"], qpair), hd * B_DH:(hd + 1) * B_DH] = o_hq[hd * LANES:(hd + 1) * LANES, :]


def _dsa_call(qr, kr, v, iqr, ike, iko, sm, lstrict, seq, topk):
    bsz, tp, _ = qr.shape
    nsteps, rem = divmod(seq, 2 * CHUNK * QUERY_GROUPS)
    assert rem == 0
    full = lambda w_: pl.BlockSpec((1, tp, w_), lambda b, p: (b, 0, 0))
    hw = B_HEADS * B_DH
    nq = QUERY_GROUPS
    return pl.pallas_call(
        functools.partial(_dsa_body, seq=seq, topk=topk),
        grid=(bsz, nsteps),
        in_specs=[full(hw), full(B_DH), full(B_DH), full(hw), full(LANES), full(LANES), full(LANES),
                  _const_spec(lstrict.shape)],
        out_specs=full(hw),
        out_shape=jax.ShapeDtypeStruct((bsz, tp, hw), F32),
        scratch_shapes=[pltpu.VMEM((seq // LANES, B_DH, LANES), BF16), pltpu.VMEM((B_DH, LANES), BF16),
                        pltpu.VMEM((nq, seq, LANES), I32), pltpu.VMEM((nq, 32, seq // 32, LANES), I32),
                        pltpu.VMEM((nq, 3, 8, LANES), I32),
                        pltpu.VMEM((nq, seq, hw), F32), pltpu.VMEM((nq, B_DH + ONES_ROWS, hw), F32)],
        compiler_params=_params(("parallel", "arbitrary")),
        name="dsa",
    )(qr, kr, v, iqr, ike, iko, sm, lstrict)


def _mix_body(h_ref, oa_ref, ob_ref, g_ref, wg_ref, bg_ref, wa_ref, wb_ref, wo_ref, out_ref, *, rb):
    x = h_ref[0]
    d = x.shape[-1]
    n = _rms(x, g_ref[...]).astype(BF16)
    gates = _sigmoid(_dot(n, wg_ref[...]) + bg_ref[...])
    y = gates[:, :d] * _dot(oa_ref[0].astype(BF16), wa_ref[...]) + gates[:, d:] * _dot(ob_ref[0].astype(BF16), wb_ref[...])
    out = x + _dot(y.astype(BF16), wo_ref[...])
    rows = pl.program_id(1) * rb + lax.broadcasted_iota(I32, (rb, 1), 0)
    out_ref[0] = jnp.where(rows >= PAD, out, 0.0)


def _mix_call(h, oa, ob, gain, wg, bg, wa, wb, wo, rb):
    bsz, tp, d = h.shape
    row = lambda w_: pl.BlockSpec((1, rb, w_), lambda b, t: (b, t, 0))
    return pl.pallas_call(
        functools.partial(_mix_body, rb=rb),
        grid=(bsz, tp // rb),
        in_specs=[row(d), row(oa.shape[-1]), row(ob.shape[-1]), _const_spec((1, d)), _const_spec(wg.shape),
                  _const_spec(bg.shape), _const_spec(wa.shape), _const_spec(wb.shape), _const_spec(wo.shape)],
        out_specs=row(d),
        out_shape=jax.ShapeDtypeStruct((bsz, tp, d), F32),
        compiler_params=_params(("parallel", "arbitrary")),
        name="mix",
    )(h, oa, ob, gain, wg, bg, wa, wb, wo)


def _ffn_body(h_ref, g_ref, wup_ref, cw_ref, wd_ref, out_ref, carry_ref, act_ref, *, rb, dff, frames_only):
    t = pl.program_id(1)

    @pl.when(t == 0)
    def _():
        carry_ref[...] = jnp.zeros_like(carry_ref)

    if frames_only:
        rb = rb + 8
    x = h_ref[0]
    n = _rms(x, g_ref[...]).astype(BF16)
    rows = lax.broadcasted_iota(I32, (rb, 1), 0)
    nch = dff // FF_CHUNK

    def gate_up(c):
        return (_dot(n, wup_ref[:, c * FF_CHUNK:(c + 1) * FF_CHUNK]),
                _dot(n, wup_ref[:, dff + c * FF_CHUNK:dff + (c + 1) * FF_CHUNK]))

    ahead = gate_up(0)
    for c in range(nch):
        cols = slice(c * FF_CHUNK, (c + 1) * FF_CHUNK)
        gate, up = ahead
        if c + 1 < nch:
            ahead = gate_up(c + 1)
        g1, g2 = pltpu.roll(gate, 1, 0), pltpu.roll(gate, 2, 0)
        if not frames_only:
            prev = carry_ref[:, cols]
            g1 = jnp.where(rows == 0, prev[7:8], g1)
            g2 = jnp.where(rows == 0, prev[6:7], jnp.where(rows == 1, prev[7:8], g2))
            carry_ref[:, cols] = gate[rb - 8:rb]
        cw = cw_ref[:, cols]
        conv = g2 * cw[0:1] + g1 * cw[1:2] + gate * cw[2:3]
        act_ref[:, cols] = (_silu(conv) * up).astype(BF16)
    out = x + _dot(act_ref[...], wd_ref[...])
    if frames_only:
        out_ref[0] = out[8:]
    else:
        out_ref[0] = jnp.where(t * rb + rows >= PAD, out, 0.0)


def _ffn_call(h, gain, wup, cw, wd, rb, frames_only=False):
    bsz, tp, d = h.shape
    dff = wd.shape[0]
    if frames_only:
        rb = FRAME_BLOCK
        nrows = tp - CHUNK
        in_rows = pl.BlockSpec((pl.Element(1), pl.Element(rb + 8), pl.Element(d)),
                               lambda b, t: (b, pl.multiple_of(CHUNK - 8 + t * rb, 8), 0))
    else:
        nrows = tp
        in_rows = pl.BlockSpec((1, rb, d), lambda b, t: (b, t, 0))
    return pl.pallas_call(
        functools.partial(_ffn_body, rb=rb, dff=dff, frames_only=frames_only),
        grid=(bsz, nrows // rb),
        in_specs=[in_rows, _const_spec((1, d)), _const_spec(wup.shape), _const_spec(cw.shape),
                  _const_spec(wd.shape)],
        out_specs=pl.BlockSpec((1, rb, d), lambda b, t: (b, t, 0)),
        out_shape=jax.ShapeDtypeStruct((bsz, nrows, d), F32),
        scratch_shapes=[pltpu.VMEM((8, dff), F32), pltpu.VMEM((rb + 8 * frames_only, dff), BF16)],
        compiler_params=_params(("parallel", "arbitrary")),
        name="ffn",
    )(h, gain, wup, cw, wd)


def _rope_tables(tp):
    pos = (jnp.arange(tp, dtype=F32) - PAD)[:, None]

    def cs(dim):
        inv = 1.0 / (ROPE_THETA ** (jnp.arange(0, dim, 2, dtype=F32) / dim))
        ang = pos * inv[None, :]
        return jnp.cos(ang), jnp.sin(ang)

    ca, sa = cs(B_DH)
    ci, si = cs(IDX_DIM)
    zi = jnp.zeros_like(si)
    return (jnp.concatenate([ca, ca], axis=1), jnp.concatenate([-sa, sa], axis=1),
            jnp.concatenate([ci] * 4, axis=1), jnp.concatenate([-si, zi, -si, zi], axis=1),
            jnp.concatenate([zi, si, zi, si], axis=1))


def _pad_lanes(vec):
    return jnp.zeros((1, LANES), F32).at[0, :vec.shape[0]].set(vec.astype(F32))


def _head_expander():
    pw, hw = A_HEADS * CHUNK, A_HEADS * A_DK
    head = np.concatenate([np.arange(pw) // CHUNK, np.arange(hw) // A_DK])
    return jnp.asarray(np.arange(LANES)[:, None] == head[None, :], BF16)


def kernel(x, meta_tokens, norm_mix, w_in, conv_a, a_log, dt_bias, a_out_norm, q_norm, k_norm, kidx_norm,
           w_branch_a, w_branch_b, w_gate, b_gate, w_out, norm_ffn, w_up, conv_ffn, w_down):
    bsz, seq, d = x.shape
    depth = w_in.shape[0]
    assert seq % KEY_CHUNK == 0
    tp = CHUNK + seq
    rb = _row_block(tp // CHUNK)
    topk = min(TOPK_MAX, seq // TOPK_DIV)

    meta = jnp.broadcast_to(meta_tokens.astype(x.dtype)[None], (bsz, N_META, d))
    h = jnp.concatenate([jnp.zeros((bsz, PAD, d), x.dtype), meta, x], axis=1)

    tables = _rope_tables(tp)
    expand = _head_expander()
    tri = np.arange(CHUNK)
    lt3 = jnp.asarray(np.tile(tri[:, None] >= tri[None, :], (1, 3)), BF16)
    trk = np.arange(LANES)
    lstrict = jnp.asarray(trk[:, None] > trk[None, :], BF16)

    hw = A_HEADS * A_DK
    o = np.cumsum((0, hw, hw, hw, hw, A_HEADS, A_HEADS, B_HEADS * B_DH, B_DH, B_DH,
                   IDX_HEADS * IDX_DIM, IDX_DIM, IDX_HEADS))
    for l in range(depth):
        w = w_in[l]
        small = jnp.concatenate([w[:, o[4]:o[6]], w[:, o[11]:o[12]],
                                 jnp.zeros((d, LANES - 2 * A_HEADS - IDX_HEADS), w.dtype)], axis=1)
        w_p = jnp.concatenate([w[:, o[0]:o[3]], w[:, o[3]:o[4]], w[:, o[6]:o[7]], w[:, o[7]:o[9]],
                               w[:, o[9]:o[10]], small, w[:, o[10]:o[11]], w[:, o[10]:o[11]]], axis=1).astype(BF16)
        qkv, z, sm, qr, kr, v, iqr, ike, iko = _proj_call(
            h, norm_mix[l][None], w_p, tables, q_norm[l][None], k_norm[l][None],
            jnp.concatenate([kidx_norm[l], kidx_norm[l]])[None], conv_a[l], rb)
        o_a = _gdn_call(qkv, z, sm, _pad_lanes(a_log[l]), _pad_lanes(dt_bias[l]),
                        a_out_norm[l][None], lt3, expand)
        o_b = _dsa_call(qr, kr, v, iqr, ike, iko, sm, lstrict, seq, topk)
        h = _mix_call(h, o_a, o_b, norm_mix[l][None], w_gate[l].astype(BF16), b_gate[l][None],
                      w_branch_a[l].astype(BF16), w_branch_b[l].astype(BF16), w_out[l].astype(BF16), rb)
        h = _ffn_call(h, norm_ffn[l][None], w_up[l].astype(BF16), conv_ffn[l], w_down[l].astype(BF16), rb,
                      frames_only=l == depth - 1)
    return h
```

```python
import functools
import math

import jax
import jax.numpy as jnp
import numpy as np
from jax import lax
from jax.experimental import pallas as pl
from jax.experimental.pallas import tpu as pltpu

F32 = jnp.float32
BF16 = jnp.bfloat16
I32 = jnp.int32

CHUNK = 64
N_META = 16
PAD = CHUNK - N_META
ROPE_THETA = 10000.0
EPS = 1e-6
A_HEADS = 4
A_DK = 128
A_CONV = 4
B_HEADS = 4
B_DH = 128
IDX_HEADS = 8
IDX_DIM = 64
TOPK_MAX = 256
TOPK_DIV = 4
FFN_CONV = 3
LANES = 128
KEY_CHUNK = 512
PLANE_ROWS = 256
QUERY_GROUPS = 4
ONES_ROWS = 16
FF_CHUNK = 256
FRAME_BLOCK = 512
GDN_SLOTS = 4
INT_MIN = -(2 ** 31)
VMEM_LIMIT_BYTES = 56 * 1024 * 1024

NT_DIMS = (((1,), (1,)), ((), ()))


def _dot(a, b):
    return jnp.dot(a, b, preferred_element_type=F32)


def _dot_nt(a, b):
    return lax.dot_general(a, b, NT_DIMS, preferred_element_type=F32)


def _dot_exact(a, b):
    return jnp.dot(a, b, preferred_element_type=F32, precision=lax.Precision.HIGHEST)


def _rms(x, gain):
    return x * lax.rsqrt(jnp.mean(x * x, axis=-1, keepdims=True) + EPS) * gain


def _sigmoid(x):
    return 0.5 * jnp.tanh(0.5 * x) + 0.5


def _silu(x):
    return x * _sigmoid(x)


def _softplus(x):
    return jnp.maximum(x, 0.0) + jnp.log(1.0 + jnp.exp(-jnp.abs(x)))


def _row_block(n_chunks):
    g = max(d for d in range(1, 12) if n_chunks % d == 0)
    return g * CHUNK


def _params(sem):
    return pltpu.CompilerParams(dimension_semantics=sem, vmem_limit_bytes=VMEM_LIMIT_BYTES)


def _const_spec(shape):
    nd = len(shape)
    return pl.BlockSpec(shape, lambda *_: (0,) * nd)


PROJ_WIDTHS = (1536, 512, 512, 256, 512, 256)


def _proj_body(h_ref, g_ref, w_ref, cosa_ref, sina_ref, cosi_ref, sinlo_ref, sinhi_ref,
               qn_ref, kn_ref, kin_ref, cw_ref,
               qkv_ref, z_ref, sm_ref, qr_ref, kr_ref, v_ref, iqr_ref, ike_ref, iko_ref, tail_ref, *, rb):
    @pl.when(pl.program_id(1) == 0)
    def _():
        tail_ref[...] = jnp.zeros_like(tail_ref)

    n = _rms(h_ref[0], g_ref[...]).astype(BF16)
    offs = dict(zip(("qkv", "z", "q", "kv", "iq", "sm_ik"), np.cumsum((0,) + PROJ_WIDTHS)[:-1]))
    rows8 = lax.broadcasted_iota(I32, (8, 1), 0)
    lane = lax.broadcasted_iota(I32, (1, LANES), 1)
    ca, sa = cosa_ref[...], sina_ref[...]
    ci, slo, shi = cosi_ref[...], sinlo_ref[...], sinhi_ref[...]

    def conv_silu(x, cols, r0, nrows):
        prev, cw = (tail_ref[:, cols] if r0 == 0 else x[r0 - 8:r0]), cw_ref[:, cols]
        xb = x[r0:r0 + nrows]
        acc = None
        for back in range(A_CONV - 1, 0, -1):
            shifted = pltpu.roll(xb, back, 0)
            top = shifted[:8]
            for r in range(back):
                top = jnp.where(rows8 == r, prev[8 - back + r:8 - back + r + 1], top)
            term = jnp.concatenate([top, shifted[8:]], axis=0) * cw[A_CONV - 1 - back:A_CONV - back]
            acc = term if acc is None else acc + term
        return _silu(acc + xb * cw[A_CONV - 1:A_CONV])

    half = rb // 2
    assert half % 8 == 0

    def gdn_piece(i):
        cols = slice(i * 256, (i + 1) * 256)

        def finish(x):
            def first():
                qkv_ref[0, 0:half, cols] = conv_silu(x, cols, 0, half)

            def second():
                qkv_ref[0, half:rb, cols] = conv_silu(x, cols, half, half)
                tail_ref[:, cols] = x[rb - 8:rb]
            return [first, second]
        return offs["qkv"] + i * 256, finish

    def plain_piece(i):
        def finish(x):
            def store():
                z_ref[0, :, i * 256:(i + 1) * 256] = x
            return [store]
        return offs["z"] + i * 256, finish

    def single(fn):
        return lambda x: [functools.partial(fn, x)]

    def q_piece(i):
        @single
        def finish(x):
            parts = []
            for hd in range(2):
                qh = _rms(x[:, hd * B_DH:(hd + 1) * B_DH], qn_ref[...])
                parts.append(qh * ca + pltpu.roll(qh, B_DH // 2, 1) * sa)
            qr_ref[0, :, i * 256:(i + 1) * 256] = (jnp.concatenate(parts, axis=1) * (B_DH ** -0.5)).astype(BF16)
        return offs["q"] + i * 256, finish

    def kv_piece():
        @single
        def finish(x):
            k = _rms(x[:, :B_DH], kn_ref[...])
            kr_ref[0] = (k * ca + pltpu.roll(k, B_DH // 2, 1) * sa).astype(BF16)
            v_ref[0] = x[:, B_DH:].astype(BF16)
        return offs["kv"], finish

    def iq_piece(i):
        @single
        def finish(x):
            parts = []
            for hp in range(2):
                xh = x[:, hp * LANES:(hp + 1) * LANES]
                parts.append(xh * ci + pltpu.roll(xh, LANES - IDX_DIM // 2, 1) * slo
                             + pltpu.roll(xh, IDX_DIM // 2, 1) * shi)
            iqr_ref[0, :, i * 256:(i + 1) * 256] = jnp.concatenate(parts, axis=1).astype(BF16)
        return offs["iq"] + i * 256, finish

    def small_ik_piece():
        @single
        def finish(x):
            sm_ref[0] = x[:, :LANES]
            ik = _rms(x[:, LANES:], kin_ref[...])
            ik = ik * ci + pltpu.roll(ik, IDX_DIM // 2, 1) * (slo + shi)
            ike_ref[0] = jnp.where(lane < IDX_DIM, ik, 0.0).astype(BF16)
            iko_ref[0] = jnp.where(lane >= IDX_DIM, ik, 0.0).astype(BF16)
        return offs["sm_ik"], finish

    pieces = [q_piece(0), gdn_piece(0), q_piece(1), gdn_piece(1), iq_piece(0), gdn_piece(2), iq_piece(1),
              gdn_piece(3), kv_piece(), gdn_piece(4), small_ik_piece(), gdn_piece(5), plain_piece(0), plain_piece(1)]
    ready, staged = [], []
    for col, finish in pieces:
        x = _dot(n, w_ref[:, col:col + 256])
        ready += staged
        staged = finish(x)
        for task in [ready.pop(0) for _ in range(min(1, len(ready)))]:
            task()
    for task in ready + staged:
        task()


def _proj_call(h, gain, w, tables, qn, kn, kin, cw, rb):
    bsz, tp, d = h.shape
    nt = tp // rb
    row = lambda w_, dt: (jax.ShapeDtypeStruct((bsz, tp, w_), dt),
                          pl.BlockSpec((1, rb, w_), lambda b, t: (b, t, 0)))
    outs = [row(1536, F32), row(512, F32), row(128, F32), row(512, BF16), row(128, BF16),
            row(128, BF16), row(512, BF16), row(128, BF16), row(128, BF16)]
    tab_spec = pl.BlockSpec((rb, LANES), lambda b, t: (t, 0))
    return pl.pallas_call(
        functools.partial(_proj_body, rb=rb),
        grid=(bsz, nt),
        in_specs=[pl.BlockSpec((1, rb, d), lambda b, t: (b, t, 0)), _const_spec((1, d)), _const_spec(w.shape)]
                 + [tab_spec] * 5 + [_const_spec((1, LANES))] * 3 + [_const_spec(cw.shape)],
        out_specs=[o[1] for o in outs],
        out_shape=[o[0] for o in outs],
        scratch_shapes=[pltpu.VMEM((8, PROJ_WIDTHS[0]), F32)],
        compiler_params=_params(("parallel", "arbitrary")),
        name="proj",
    )(h, gain, w, *tables, qn, kn, kin, cw)


def _split3(x):
    x1 = x.astype(BF16)
    r1 = x - x1.astype(F32)
    x2 = r1.astype(BF16)
    x3 = (r1 - x2.astype(F32)).astype(BF16)
    return jnp.concatenate([x1, x2, x3], axis=0)


def _gdn_body(qkv_ref, z_ref, sm_ref, alog_ref, dtb_ref, ng_ref, lt3_ref, expand_ref,
              out_ref, s01_ref, s23_ref, u_ref, wq_ref, qkd_ref, kdt_ref, egl_ref, *, rb):
    t = pl.program_id(1)

    @pl.when(t == 0)
    def _():
        s01_ref[...] = jnp.zeros_like(s01_ref)
        s23_ref[...] = jnp.zeros_like(s23_ref)

    hw = A_HEADS * A_DK
    pw = A_HEADS * CHUNK
    ii = lax.broadcasted_iota(I32, (CHUNK, pw), 0)
    jj = lax.broadcasted_iota(I32, (CHUNK, pw), 1) & (CHUNK - 1)
    colhead = lax.broadcasted_iota(I32, (CHUNK, pw), 1) >> 6
    eye_p = ii == jj
    bd_mask = (lax.broadcasted_iota(I32, (pw, pw), 0) >> 6) == (lax.broadcasted_iota(I32, (pw, pw), 1) >> 6)
    bdk_mask = (lax.broadcasted_iota(I32, (pw, hw), 0) >> 6) == (lax.broadcasted_iota(I32, (pw, hw), 1) >> 7)
    pair_mask = (lax.broadcasted_iota(I32, (pw, pw), 0) >> 7) == (lax.broadcasted_iota(I32, (pw, pw), 1) >> 7)
    lane = lax.broadcasted_iota(I32, (1, LANES), 1)
    rows64 = lax.broadcasted_iota(I32, (CHUNK, 1), 0)
    neg_a = -jnp.exp(alog_ref[...])
    dtb = dtb_ref[...]
    ng = ng_ref[...]

    def block_diag(xp):
        return jnp.where(bd_mask, jnp.concatenate([xp.astype(BF16)] * A_HEADS, axis=0), 0)

    def stack_heads(xp):
        return jnp.concatenate([jnp.where(colhead == hd, xp, 0.0) for hd in range(A_HEADS)], axis=0)

    def l2n(x):
        parts = []
        for hd in range(A_HEADS):
            xh = x[:, hd * A_DK:(hd + 1) * A_DK]
            parts.append(xh * lax.rsqrt(jnp.sum(xh * xh, axis=-1, keepdims=True) + EPS))
        return jnp.concatenate(parts, axis=1)

    zeros = jnp.zeros((CHUNK, 2 * A_DK), F32)

    def pair_lanes(x, a_, b_):
        return jnp.concatenate([x[a_ * CHUNK:(a_ + 1) * CHUNK], x[b_ * CHUNK:(b_ + 1) * CHUNK]], axis=1)

    def local_load(c):
        r0 = c * CHUNK if isinstance(c, int) else pl.multiple_of(c * CHUNK, CHUNK)
        return qkv_ref[0, pl.ds(r0, CHUNK), :], sm_ref[0, pl.ds(r0, CHUNK), :], r0

    def local_compute(xq, sm, r0):
        q = l2n(xq[:, :hw]) * (A_DK ** -0.5)
        k = l2n(xq[:, hw:2 * hw])
        v = xq[:, 2 * hw:]

        g = jnp.where(lane < A_HEADS, neg_a * _softplus(sm + dtb), 0.0)
        g = jnp.where(t * rb + r0 + rows64 >= PAD, g, 0.0)
        to_packed, to_wide = expand_ref[:, :pw], expand_ref[:, pw:]
        g3 = _split3(g)
        cg = _dot(g3, to_packed).astype(BF16)
        gcs = _dot(lt3_ref[...], g3)
        yield
        gp = _dot(lt3_ref[...], cg)
        in_head = lane < A_HEADS
        beta = jnp.where(in_head, pltpu.roll(_sigmoid(sm), LANES - A_HEADS, 1), 0.0)
        eg_s = jnp.where(in_head, jnp.exp(gcs), 0.0)
        to_last = jnp.where(in_head, jnp.exp(gcs[CHUNK - 1:CHUNK] - gcs), 0.0)
        wide = _dot(jnp.concatenate([beta, beta * eg_s, eg_s, to_last], axis=0).astype(BF16), to_wide)
        bp = _dot(beta.astype(BF16), to_packed)
        glast3 = _dot(_split3(gcs[CHUNK - 16:]), to_wide)
        yield
        bq, bq_eg, eg, kd_scale = (wide[i * CHUNK:(i + 1) * CHUNK] for i in range(4))
        glast = glast3[15:16] + glast3[31:32] + glast3[47:48]
        grow = jnp.sum(jnp.where(eye_p, gp, 0.0), axis=0, keepdims=True)
        decay = jnp.where(ii >= jj, jnp.exp(gp - grow), 0.0)

        kb = k.astype(BF16)
        bdk = jnp.where(bdk_mask, jnp.concatenate([kb] * A_HEADS, axis=0), 0)
        kq = _dot_nt(jnp.concatenate([kb, q.astype(BF16)], axis=0), bdk)
        yield
        kk_p, qk_p = kq[:CHUNK], kq[CHUNK:]

        a = -jnp.where(ii > jj, bp * kk_p * decay, 0.0)
        tinv = jnp.where(eye_p, 1.0, 0.0) + a
        pk = _dot(a.astype(BF16), block_diag(a))
        yield
        for _ in range(4):
            pt = _dot(jnp.concatenate([pk, tinv], axis=0).astype(BF16), block_diag(pk))
            yield
            tinv = tinv + pt[CHUNK:]
            pk = pt[:CHUNK]
        tinv = tinv + _dot(tinv.astype(BF16), block_diag(pk))
        yield

        vb = v * bq
        kbg = k * bq_eg
        rv = jnp.concatenate(
            [jnp.concatenate([vb[:, hd * A_DK:(hd + 1) * A_DK], kbg[:, hd * A_DK:(hd + 1) * A_DK]], axis=1)
             for hd in range(A_HEADS)], axis=0)
        uw = _dot(stack_heads(tinv).astype(BF16), rv.astype(BF16))
        yield
        u, w = uw[:, :A_DK], uw[:, A_DK:]

        qg = q * eg
        kd = k * kd_scale
        return (u,
                jnp.concatenate([pair_lanes(w, 0, 1), qg[:, :2 * A_DK]], axis=0).astype(BF16),
                jnp.concatenate([pair_lanes(w, 2, 3), qg[:, 2 * A_DK:]], axis=0).astype(BF16),
                stack_heads(qk_p * decay).astype(BF16),
                jnp.concatenate([kd[:, :2 * A_DK], zeros], axis=0).T.astype(BF16),
                jnp.concatenate([kd[:, 2 * A_DK:], zeros], axis=0).T.astype(BF16),
                jnp.broadcast_to(jnp.exp(glast), (8, hw)))

    def slot(c):
        return c % GDN_SLOTS if isinstance(c, int) else c & (GDN_SLOTS - 1)

    def local_store(c, vals):
        s = slot(c)
        u_ref[s], wq_ref[s, 0], wq_ref[s, 1], qkd_ref[s], kdt_ref[s, 0], kdt_ref[s, 1], egl_ref[s] = vals

    def local_chain(c):
        vals = yield from local_compute(*local_load(c))
        local_store(c, vals)

    def state_chain(chunks):
        def unpair(r):
            return [r[:, :A_DK], r[:, A_DK:]]

        def state_update(s_old, kdt, vn_pair, egl_pair):
            upd = _dot(kdt, jnp.concatenate([vn_pair, zeros], axis=0).astype(BF16))
            return s_old * egl_pair + jnp.where(pair_mask, upd, 0.0)

        s01 = s01_ref[...]
        s23 = s23_ref[...]
        for c in chunks:
            r0 = c * CHUNK if isinstance(c, int) else pl.multiple_of(c * CHUNK, CHUNK)
            s = slot(c)
            r01 = _dot(wq_ref[s, 0], s01.astype(BF16))
            r23 = _dot(wq_ref[s, 1], s23.astype(BF16))
            yield
            ws = jnp.concatenate(unpair(r01[:CHUNK]) + unpair(r23[:CHUNK]), axis=0)
            qs = jnp.concatenate(unpair(r01[CHUNK:]) + unpair(r23[CHUNK:]), axis=0)
            vn = u_ref[s] - ws
            o_rs = qs + _dot(qkd_ref[s], vn.astype(BF16))
            egl = egl_ref[s][0:1, :]
            s01 = state_update(s01, kdt_ref[s, 0], pair_lanes(vn, 0, 1), egl[:, :2 * A_DK])
            s23 = state_update(s23, kdt_ref[s, 1], pair_lanes(vn, 2, 3), egl[:, 2 * A_DK:])
            yield
            zz = z_ref[0, pl.ds(r0, CHUNK), :]
            parts = [_rms(o_rs[hd * CHUNK:(hd + 1) * CHUNK], ng) for hd in range(A_HEADS)]
            out_ref[0, pl.ds(r0, CHUNK), :] = jnp.concatenate(parts, axis=1) * _silu(zz)
        s01_ref[...] = s01
        s23_ref[...] = s23

    def run_lockstep(chains):
        live = list(chains)
        while live:
            for gen in list(live):
                try:
                    next(gen)
                except StopIteration:
                    live.remove(gen)

    nc = rb // CHUNK
    npairs = nc // 2
    if npairs == 0:
        run_lockstep([local_chain(0)])
        run_lockstep([state_chain([0])])
    else:
        run_lockstep([local_chain(0), local_chain(1)])

        def steady(i, carry):
            run_lockstep([local_chain(2 * i), local_chain(2 * i + 1), state_chain([2 * i - 2, 2 * i - 1])])
            return carry

        lax.fori_loop(1, npairs, steady, 0)
        last = [2 * npairs - 2, 2 * npairs - 1]
        if nc % 2:
            run_lockstep([local_chain(nc - 1), state_chain(last)])
            run_lockstep([state_chain([nc - 1])])
        else:
            run_lockstep([state_chain(last)])


def _gdn_call(qkv, z, sm, alog, dtb, ng, lt3, expand):
    bsz, tp, _ = qkv.shape
    rb = tp
    nt = tp // rb
    nc = GDN_SLOTS
    hw = A_HEADS * A_DK
    pw = A_HEADS * CHUNK
    row = lambda w_: pl.BlockSpec((1, rb, w_), lambda b, t: (b, t, 0))
    return pl.pallas_call(
        functools.partial(_gdn_body, rb=rb),
        grid=(bsz, nt),
        in_specs=[row(3 * hw), row(hw), row(LANES), _const_spec((1, LANES)),
                  _const_spec((1, LANES)), _const_spec((1, LANES)), _const_spec(lt3.shape),
                  _const_spec(expand.shape)],
        out_specs=row(hw),
        out_shape=jax.ShapeDtypeStruct((bsz, tp, hw), F32),
        scratch_shapes=[pltpu.VMEM((2 * A_DK, 2 * A_DK), F32), pltpu.VMEM((2 * A_DK, 2 * A_DK), F32),
                        pltpu.VMEM((nc, pw, A_DK), F32), pltpu.VMEM((nc, 2, 2 * CHUNK, 2 * A_DK), BF16),
                        pltpu.VMEM((nc, pw, pw), BF16), pltpu.VMEM((nc, 2, 2 * A_DK, 2 * CHUNK), BF16),
                        pltpu.VMEM((nc, 8, hw), F32)],
        compiler_params=_params(("parallel", "arbitrary")),
        name="gdn",
    )(qkv, z, sm, alog, dtb, ng, lt3, expand)


def _dsa_body(qr_ref, kr_ref, v_ref, iqr_ref, ike_ref, iko_ref, sm_ref, lstrict_ref, out_ref,
              vt_ref, vmt_ref, skey_ref, planes_ref, sel_ref, st_ref, ot_ref, *, seq, topk):
    g = pl.program_id(1)
    iw_scale = IDX_HEADS ** -0.5 * IDX_DIM ** -0.5
    qpair = 2 * CHUNK
    neg_inf = -jnp.inf

    @pl.when(g == 0)
    def _():
        planes_ref[...] = jnp.zeros_like(planes_ref)

        def vblk(i, carry):
            r = pl.multiple_of(CHUNK + i * LANES, CHUNK)
            vt_ref[i] = v_ref[0, pl.ds(r, LANES), :].astype(F32).T.astype(BF16)
            return carry

        lax.fori_loop(0, seq // LANES, vblk, 0)
        vmt_ref[...] = v_ref[0, 0:LANES, :].astype(F32).T.astype(BF16)

        qm = qr_ref[0, 0:CHUNK, :]
        km = kr_ref[0, 0:CHUNK, :]
        vm = v_ref[0, 0:CHUNK, :]
        colv = lax.broadcasted_iota(I32, (1, CHUNK), 1) >= PAD
        parts = []
        for hd in range(B_HEADS):
            s = _dot_nt(qm[:, hd * B_DH:(hd + 1) * B_DH], km)
            s = jnp.where(colv, s, neg_inf)
            e = jnp.exp(s - jnp.max(s, axis=-1, keepdims=True))
            pr = e / jnp.sum(e, axis=-1, keepdims=True)
            parts.append(_dot(pr.astype(BF16), vm))
        out_ref[0, 0:CHUNK, :] = jnp.concatenate(parts, axis=1)

    lane = lax.broadcasted_iota(I32, (1, LANES), 1)
    groups = []
    for j in range(QUERY_GROUPS):
        p = g * QUERY_GROUPS + j
        r0 = pl.multiple_of(CHUNK + p * qpair, CHUNK)
        qb = qr_ref[0, pl.ds(r0, qpair), :]
        ib = iqr_ref[0, pl.ds(r0, qpair), :]
        smt = sm_ref[0, pl.ds(r0, qpair), :].T
        groups.append(dict(
            j=j, r0=r0,
            q_hq=jnp.concatenate([qb[:, hd * B_DH:(hd + 1) * B_DH] for hd in range(B_HEADS)], axis=0),
            iq_hq=jnp.concatenate([ib[:, hp * LANES:(hp + 1) * LANES] for hp in range(IDX_HEADS // 2)], axis=0),
            wts=[smt[2 * A_HEADS + hd:2 * A_HEADS + hd + 1, :] * iw_scale for hd in range(IDX_HEADS)],
            limit=jnp.where(lane < CHUNK, (2 * p + 1) * CHUNK, (2 * p + 2) * CHUNK)))
    assert QUERY_GROUPS * qpair == KEY_CHUNK
    full_rows = [KEY_CHUNK] * QUERY_GROUPS
    diag_rows = [(j + 1) * qpair for j in range(QUERY_GROUPS)]
    diag_plane_rows = [-(-r // PLANE_ROWS) * PLANE_ROWS for r in diag_rows]
    krows = lax.broadcasted_iota(I32, (KEY_CHUNK, 1), 0)

    def score_chunk(kc, carry, nrows=full_rows):
        k0 = pl.multiple_of(CHUNK + kc * KEY_CHUNK, CHUNK)
        base = pl.multiple_of(kc * KEY_CHUNK, KEY_CHUNK)
        ke, ko, kk = (ref[0, pl.ds(k0, KEY_CHUNK), :] for ref in (ike_ref, iko_ref, kr_ref))
        logits = []
        for grp_, nr in zip(groups, nrows):
            logits.append((_dot_nt(ke[:nr], grp_["iq_hq"]), _dot_nt(ko[:nr], grp_["iq_hq"])))
            st_ref[grp_["j"], pl.ds(base, nr), :] = _dot_nt(kk[:nr], grp_["q_hq"])
        for grp_, nr, (le, lo) in zip(groups, nrows, logits):
            score_tail(kc, pl.ds(base, nr), nr, grp_, le, lo)
        return carry

    def score_tail(kc, rows, nr, grp_, le, lo):
        j, wts, limit = grp_["j"], grp_["wts"], grp_["limit"]
        sc = jnp.zeros((nr, LANES), F32)
        for hp in range(IDX_HEADS // 2):
            sc = sc + wts[2 * hp] * jnp.maximum(le[:, hp * LANES:(hp + 1) * LANES], 0.0)
            sc = sc + wts[2 * hp + 1] * jnp.maximum(lo[:, hp * LANES:(hp + 1) * LANES], 0.0)
        bits = lax.bitcast_convert_type(sc, I32)
        bits = jnp.where(sc == 0.0, 0, bits)
        key = bits ^ ((bits >> 31) & 0x7FFFFFFF)
        valid = kc * KEY_CHUNK + krows[:nr] < limit
        key = jnp.where(valid, key, INT_MIN)
        skey_ref[j, rows, :] = key
        for gi in range(nr // PLANE_ROWS):
            a = [key[gi * PLANE_ROWS + 8 * r:gi * PLANE_ROWS + 8 * r + 8, :] ^ INT_MIN for r in range(32)]
            sh, msk = 16, 0x0000FFFF
            while sh:
                k = 0
                while k < 32:
                    tt = (a[k] ^ lax.shift_right_logical(a[k + sh], jnp.int32(sh))) & msk
                    a[k] = a[k] ^ tt
                    a[k + sh] = a[k + sh] ^ (tt << sh)
                    k = (k + sh + 1) & ~sh
                sh >>= 1
                msk = (msk ^ (msk << sh)) & 0xFFFFFFFF
            g8 = pl.multiple_of((kc * (KEY_CHUNK // PLANE_ROWS) + gi) * 8, 8)
            for b in range(32):
                planes_ref[j, b, pl.ds(g8, 8), :] = a[b]

    lax.fori_loop(0, g, score_chunk, 0)
    score_chunk(g, 0, diag_plane_rows)

    words_per_chunk = KEY_CHUNK // 32

    def lane_count(words):
        c = lax.population_count(words)
        if words.shape[0] > 8:
            c = jnp.sum(c.reshape(words.shape[0] // 8, 8, LANES), axis=0)
        return jnp.sum(c, axis=0, keepdims=True)

    def select_threshold(step):
        nrow = (step + 1) * words_per_chunk
        prow = lax.broadcasted_iota(I32, (nrow, 1), 0)
        alive0 = [jnp.where(prow < step * words_per_chunk + nr // 32, jnp.full((nrow, LANES), -1, I32), 0)
                  for nr in diag_plane_rows]

        def bit_body(i, carry):
            bit = jnp.left_shift(jnp.int32(1), 31 - i)
            out = []
            for grp_, (tu, above, alive) in zip(groups, carry):
                hit = alive & planes_ref[grp_["j"], i, 0:nrow, :]
                c1 = lane_count(hit)
                take = above + c1 >= topk
                out.append((jnp.where(take, tu | bit, tu), jnp.where(take, above, above + c1),
                            jnp.where(take, hit, alive ^ hit)))
            return tuple(out)

        zero_row = jnp.zeros((1, LANES), I32)
        result = lax.fori_loop(0, 32, bit_body, tuple((zero_row, zero_row, a0) for a0 in alive0))
        for grp_, (tu, above, alive) in zip(groups, result):
            for n, row in enumerate((tu, above, lane_count(alive))):
                sel_ref[grp_["j"], n] = jnp.broadcast_to(row, (8, LANES))

    for step in range(seq // KEY_CHUNK):
        pl.when(g == step)(functools.partial(select_threshold, step))

    for grp_, nr_diag in zip(groups, diag_rows):
        tu, above, n_eq = (sel_ref[grp_["j"], n][0:1] for n in range(3))
        j, thr, found = grp_["j"], tu ^ INT_MIN, tu != 0
        grp_["thr"] = thr
        tie_lane = found & (above + n_eq > topk)

        @pl.when(jnp.max(jnp.where(tie_lane, 1, 0)) > 0)
        def _(j=j, thr=thr, found=found, above=above, nr_diag=nr_diag):
            need = (topk - above).astype(F32)

            def body(kc, seen, nr=KEY_CHUNK):
                base = pl.multiple_of(kc * KEY_CHUNK, KEY_CHUNK)
                for sub in range(nr // LANES):
                    rows = pl.ds(base + sub * LANES, LANES)
                    blk = skey_ref[j, rows, :]
                    tied = (blk == thr) & found
                    eq = jnp.where(tied, 1.0, 0.0)
                    rank = _dot(lstrict_ref[...], eq.astype(BF16)) + seen
                    skey_ref[j, rows, :] = jnp.where(tied & (rank >= need), INT_MIN, blk)
                    seen = seen + jnp.sum(eq, axis=0, keepdims=True)
                return seen

            body(g, lax.fori_loop(0, g, body, jnp.zeros((1, LANES), F32)), nr_diag)

    mrows = lax.broadcasted_iota(I32, (LANES, 1), 0)
    k_meta = kr_ref[0, 0:LANES, :]
    ones = jnp.ones((ONES_ROWS, LANES), BF16)
    v_meta = jnp.concatenate([vmt_ref[...], ones], axis=0)
    init = []
    for grp_ in groups:
        s_meta = _dot_nt(k_meta, grp_["q_hq"])
        s_meta = jnp.where((mrows >= PAD) & (mrows < CHUNK), s_meta, neg_inf)
        m0 = jnp.max(s_meta, axis=0, keepdims=True)
        ot_ref[grp_["j"]] = _dot(v_meta, jnp.exp(s_meta - m0).astype(BF16))
        init.append(m0)
        grp_["thr"] = jnp.maximum(grp_["thr"], INT_MIN + 1)

    def pv_chunk(kc, carry, nrows=full_rows):
        base = pl.multiple_of(kc * KEY_CHUNK, KEY_CHUNK)
        nblk = KEY_CHUNK // LANES
        vblk = jnp.concatenate([jnp.concatenate([vt_ref[nblk * kc + i], ones], axis=0) for i in range(nblk)], axis=1)
        out, probs = [], []
        for grp_, nr, m in zip(groups, nrows, carry):
            rows = pl.ds(base, nr)
            bias = jnp.where(skey_ref[grp_["j"], rows, :] >= grp_["thr"], 0.0, neg_inf)
            s = st_ref[grp_["j"], rows, :] + jnp.concatenate([bias] * B_HEADS, axis=1)
            m_new = jnp.maximum(m, jnp.max(jnp.max(s.reshape(nr // 8, 8, B_HEADS * LANES), axis=0),
                                           axis=0, keepdims=True))
            out.append(m_new)
            probs.append((jnp.exp(m - m_new), jnp.exp(s - m_new).astype(BF16)))
        for grp_, nr, (alpha, prb) in zip(groups, nrows, probs):
            ot_ref[grp_["j"]] = ot_ref[grp_["j"]] * alpha + _dot(vblk[:, :nr], prb)
        return tuple(out)

    pv_chunk(g, lax.fori_loop(0, g, pv_chunk, tuple(init)), diag_rows)
    for grp_ in groups:
        acc = ot_ref[grp_["j"]]
        o_hq = (acc[:B_DH] / acc[B_DH:B_DH + 1]).T
        for hd in range(B_HEADS):
            out_ref[0, pl.ds(grp_["r0"], qpair), hd * B_DH:(hd + 1) * B_DH] = o_hq[hd * LANES:(hd + 1) * LANES, :]


def _dsa_call(qr, kr, v, iqr, ike, iko, sm, lstrict, seq, topk):
    bsz, tp, _ = qr.shape
    nsteps, rem = divmod(seq, 2 * CHUNK * QUERY_GROUPS)
    assert rem == 0
    full = lambda w_: pl.BlockSpec((1, tp, w_), lambda b, p: (b, 0, 0))
    hw = B_HEADS * B_DH
    nq = QUERY_GROUPS
    return pl.pallas_call(
        functools.partial(_dsa_body, seq=seq, topk=topk),
        grid=(bsz, nsteps),
        in_specs=[full(hw), full(B_DH), full(B_DH), full(hw), full(LANES), full(LANES), full(LANES),
                  _const_spec(lstrict.shape)],
        out_specs=full(hw),
        out_shape=jax.ShapeDtypeStruct((bsz, tp, hw), F32),
        scratch_shapes=[pltpu.VMEM((seq // LANES, B_DH, LANES), BF16), pltpu.VMEM((B_DH, LANES), BF16),
                        pltpu.VMEM((nq, seq, LANES), I32), pltpu.VMEM((nq, 32, seq // 32, LANES), I32),
                        pltpu.VMEM((nq, 3, 8, LANES), I32),
                        pltpu.VMEM((nq, seq, hw), F32), pltpu.VMEM((nq, B_DH + ONES_ROWS, hw), F32)],
        compiler_params=_params(("parallel", "arbitrary")),
        name="dsa",
    )(qr, kr, v, iqr, ike, iko, sm, lstrict)


def _mix_body(h_ref, oa_ref, ob_ref, g_ref, wg_ref, bg_ref, wa_ref, wb_ref, wo_ref, out_ref, *, rb):
    x = h_ref[0]
    d = x.shape[-1]
    n = _rms(x, g_ref[...]).astype(BF16)
    gates = _sigmoid(_dot(n, wg_ref[...]) + bg_ref[...])
    y = gates[:, :d] * _dot(oa_ref[0].astype(BF16), wa_ref[...]) + gates[:, d:] * _dot(ob_ref[0].astype(BF16), wb_ref[...])
    out = x + _dot(y.astype(BF16), wo_ref[...])
    rows = pl.program_id(1) * rb + lax.broadcasted_iota(I32, (rb, 1), 0)
    out_ref[0] = jnp.where(rows >= PAD, out, 0.0)


def _mix_call(h, oa, ob, gain, wg, bg, wa, wb, wo, rb):
    bsz, tp, d = h.shape
    row = lambda w_: pl.BlockSpec((1, rb, w_), lambda b, t: (b, t, 0))
    return pl.pallas_call(
        functools.partial(_mix_body, rb=rb),
        grid=(bsz, tp // rb),
        in_specs=[row(d), row(oa.shape[-1]), row(ob.shape[-1]), _const_spec((1, d)), _const_spec(wg.shape),
                  _const_spec(bg.shape), _const_spec(wa.shape), _const_spec(wb.shape), _const_spec(wo.shape)],
        out_specs=row(d),
        out_shape=jax.ShapeDtypeStruct((bsz, tp, d), F32),
        compiler_params=_params(("parallel", "arbitrary")),
        name="mix",
    )(h, oa, ob, gain, wg, bg, wa, wb, wo)


def _ffn_body(h_ref, g_ref, wup_ref, cw_ref, wd_ref, out_ref, carry_ref, act_ref, *, rb, dff, frames_only):
    t = pl.program_id(1)

    @pl.when(t == 0)
    def _():
        carry_ref[...] = jnp.zeros_like(carry_ref)

    if frames_only:
        rb = rb + 8
    x = h_ref[0]
    n = _rms(x, g_ref[...]).astype(BF16)
    rows = lax.broadcasted_iota(I32, (rb, 1), 0)
    nch = dff // FF_CHUNK

    def gate_up(c):
        return (_dot(n, wup_ref[:, c * FF_CHUNK:(c + 1) * FF_CHUNK]),
                _dot(n, wup_ref[:, dff + c * FF_CHUNK:dff + (c + 1) * FF_CHUNK]))

    ahead = gate_up(0)
    for c in range(nch):
        cols = slice(c * FF_CHUNK, (c + 1) * FF_CHUNK)
        gate, up = ahead
        if c + 1 < nch:
            ahead = gate_up(c + 1)
        g1, g2 = pltpu.roll(gate, 1, 0), pltpu.roll(gate, 2, 0)
        if not frames_only:
            prev = carry_ref[:, cols]
            g1 = jnp.where(rows == 0, prev[7:8], g1)
            g2 = jnp.where(rows == 0, prev[6:7], jnp.where(rows == 1, prev[7:8], g2))
            carry_ref[:, cols] = gate[rb - 8:rb]
        cw = cw_ref[:, cols]
        conv = g2 * cw[0:1] + g1 * cw[1:2] + gate * cw[2:3]
        act_ref[:, cols] = (_silu(conv) * up).astype(BF16)
    out = x + _dot(act_ref[...], wd_ref[...])
    if frames_only:
        out_ref[0] = out[8:]
    else:
        out_ref[0] = jnp.where(t * rb + rows >= PAD, out, 0.0)


def _ffn_call(h, gain, wup, cw, wd, rb, frames_only=False):
    bsz, tp, d = h.shape
    dff = wd.shape[0]
    if frames_only:
        rb = FRAME_BLOCK
        nrows = tp - CHUNK
        in_rows = pl.BlockSpec((pl.Element(1), pl.Element(rb + 8), pl.Element(d)),
                               lambda b, t: (b, pl.multiple_of(CHUNK - 8 + t * rb, 8), 0))
    else:
        nrows = tp
        in_rows = pl.BlockSpec((1, rb, d), lambda b, t: (b, t, 0))
    return pl.pallas_call(
        functools.partial(_ffn_body, rb=rb, dff=dff, frames_only=frames_only),
        grid=(bsz, nrows // rb),
        in_specs=[in_rows, _const_spec((1, d)), _const_spec(wup.shape), _const_spec(cw.shape),
                  _const_spec(wd.shape)],
        out_specs=pl.BlockSpec((1, rb, d), lambda b, t: (b, t, 0)),
        out_shape=jax.ShapeDtypeStruct((bsz, nrows, d), F32),
        scratch_shapes=[pltpu.VMEM((8, dff), F32), pltpu.VMEM((rb + 8 * frames_only, dff), BF16)],
        compiler_params=_params(("parallel", "arbitrary")),
        name="ffn",
    )(h, gain, wup, cw, wd)


def _rope_tables(tp):
    pos = (jnp.arange(tp, dtype=F32) - PAD)[:, None]

    def cs(dim):
        inv = 1.0 / (ROPE_THETA ** (jnp.arange(0, dim, 2, dtype=F32) / dim))
        ang = pos * inv[None, :]
        return jnp.cos(ang), jnp.sin(ang)

    ca, sa = cs(B_DH)
    ci, si = cs(IDX_DIM)
    zi = jnp.zeros_like(si)
    return (jnp.concatenate([ca, ca], axis=1), jnp.concatenate([-sa, sa], axis=1),
            jnp.concatenate([ci] * 4, axis=1), jnp.concatenate([-si, zi, -si, zi], axis=1),
            jnp.concatenate([zi, si, zi, si], axis=1))


def _pad_lanes(vec):
    return jnp.zeros((1, LANES), F32).at[0, :vec.shape[0]].set(vec.astype(F32))


def _head_expander():
    pw, hw = A_HEADS * CHUNK, A_HEADS * A_DK
    head = np.concatenate([np.arange(pw) // CHUNK, np.arange(hw) // A_DK])
    return jnp.asarray(np.arange(LANES)[:, None] == head[None, :], BF16)


def kernel(x, meta_tokens, norm_mix, w_in, conv_a, a_log, dt_bias, a_out_norm, q_norm, k_norm, kidx_norm,
           w_branch_a, w_branch_b, w_gate, b_gate, w_out, norm_ffn, w_up, conv_ffn, w_down):
    bsz, seq, d = x.shape
    depth = w_in.shape[0]
    assert seq % KEY_CHUNK == 0
    tp = CHUNK + seq
    rb = _row_block(tp // CHUNK)
    topk = min(TOPK_MAX, seq // TOPK_DIV)

    meta = jnp.broadcast_to(meta_tokens.astype(x.dtype)[None], (bsz, N_META, d))
    h = jnp.concatenate([jnp.zeros((bsz, PAD, d), x.dtype), meta, x], axis=1)

    tables = _rope_tables(tp)
    expand = _head_expander()
    tri = np.arange(CHUNK)
    lt3 = jnp.asarray(np.tile(tri[:, None] >= tri[None, :], (1, 3)), BF16)
    trk = np.arange(LANES)
    lstrict = jnp.asarray(trk[:, None] > trk[None, :], BF16)

    hw = A_HEADS * A_DK
    o = np.cumsum((0, hw, hw, hw, hw, A_HEADS, A_HEADS, B_HEADS * B_DH, B_DH, B_DH,
                   IDX_HEADS * IDX_DIM, IDX_DIM, IDX_HEADS))
    for l in range(depth):
        w = w_in[l]
        small = jnp.concatenate([w[:, o[4]:o[6]], w[:, o[11]:o[12]],
                                 jnp.zeros((d, LANES - 2 * A_HEADS - IDX_HEADS), w.dtype)], axis=1)
        w_p = jnp.concatenate([w[:, o[0]:o[3]], w[:, o[3]:o[4]], w[:, o[6]:o[7]], w[:, o[7]:o[9]],
                               w[:, o[9]:o[10]], small, w[:, o[10]:o[11]], w[:, o[10]:o[11]]], axis=1).astype(BF16)
        qkv, z, sm, qr, kr, v, iqr, ike, iko = _proj_call(
            h, norm_mix[l][None], w_p, tables, q_norm[l][None], k_norm[l][None],
            jnp.concatenate([kidx_norm[l], kidx_norm[l]])[None], conv_a[l], rb)
        o_a = _gdn_call(qkv, z, sm, _pad_lanes(a_log[l]), _pad_lanes(dt_bias[l]),
                        a_out_norm[l][None], lt3, expand)
        o_b = _dsa_call(qr, kr, v, iqr, ike, iko, sm, lstrict, seq, topk)
        h = _mix_call(h, o_a, o_b, norm_mix[l][None], w_gate[l].astype(BF16), b_gate[l][None],
                      w_branch_a[l].astype(BF16), w_branch_b[l].astype(BF16), w_out[l].astype(BF16), rb)
        h = _ffn_call(h, norm_ffn[l][None], w_up[l].astype(BF16), conv_ffn[l], w_down[l].astype(BF16), rb,
                      frames_only=l == depth - 1)
    return h
```

```python
import functools
import math

import jax
import jax.numpy as jnp
import numpy as np
from jax import lax
from jax.experimental import pallas as pl
from jax.experimental.pallas import tpu as pltpu

F32 = jnp.float32
BF16 = jnp.bfloat16
I32 = jnp.int32

CHUNK = 64
N_META = 16
PAD = CHUNK - N_META
ROPE_THETA = 10000.0
EPS = 1e-6
A_HEADS = 4
A_DK = 128
A_CONV = 4
B_HEADS = 4
B_DH = 128
IDX_HEADS = 8
IDX_DIM = 64
TOPK_MAX = 256
TOPK_DIV = 4
FFN_CONV = 3
LANES = 128
KEY_CHUNK = 512
PLANE_ROWS = 256
QUERY_GROUPS = 4
ONES_ROWS = 16
FF_CHUNK = 256
FRAME_BLOCK = 512
GDN_SLOTS = 4
INT_MIN = -(2 ** 31)
VMEM_LIMIT_BYTES = 56 * 1024 * 1024

NT_DIMS = (((1,), (1,)), ((), ()))


def _dot(a, b):
    return jnp.dot(a, b, preferred_element_type=F32)


def _dot_nt(a, b):
    return lax.dot_general(a, b, NT_DIMS, preferred_element_type=F32)


def _dot_exact(a, b):
    return jnp.dot(a, b, preferred_element_type=F32, precision=lax.Precision.HIGHEST)


def _rms(x, gain):
    return x * lax.rsqrt(jnp.mean(x * x, axis=-1, keepdims=True) + EPS) * gain


def _sigmoid(x):
    return 0.5 * jnp.tanh(0.5 * x) + 0.5


def _silu(x):
    return x * _sigmoid(x)


def _softplus(x):
    return jnp.maximum(x, 0.0) + jnp.log(1.0 + jnp.exp(-jnp.abs(x)))


def _row_block(n_chunks):
    g = max(d for d in range(1, 12) if n_chunks % d == 0)
    return g * CHUNK


def _params(sem):
    return pltpu.CompilerParams(dimension_semantics=sem, vmem_limit_bytes=VMEM_LIMIT_BYTES)


def _const_spec(shape):
    nd = len(shape)
    return pl.BlockSpec(shape, lambda *_: (0,) * nd)


def _stream_spec(rb, d, from_frames):
    if not from_frames:
        return pl.BlockSpec((1, rb, d), lambda b, t: (b, t, 0))
    return pl.BlockSpec((pl.Element(1), pl.Element(rb), pl.Element(d)),
                        lambda b, t: (b, pl.multiple_of(jnp.maximum(t * rb - CHUNK, 0), 8), 0))


def _stream_block(h_ref, meta_ref, rb, from_frames):
    x = h_ref[0]
    if not from_frames:
        return x
    first = jnp.concatenate([jnp.zeros((PAD, x.shape[-1]), x.dtype), meta_ref[...], x[:rb - CHUNK]], axis=0)
    return jnp.where(pl.program_id(1) == 0, first, x)


PROJ_WIDTHS = (1536, 512, 512, 256, 512, 256)


def _proj_body(h_ref, meta_ref, g_ref, w_ref, cosa_ref, sina_ref, cosi_ref, sinlo_ref, sinhi_ref,
               qn_ref, kn_ref, kin_ref, cw_ref,
               qkv_ref, z_ref, sm_ref, qr_ref, kr_ref, v_ref, iqr_ref, ike_ref, iko_ref, tail_ref,
               *, rb, from_frames):
    @pl.when(pl.program_id(1) == 0)
    def _():
        tail_ref[...] = jnp.zeros_like(tail_ref)

    n = _rms(_stream_block(h_ref, meta_ref, rb, from_frames), g_ref[...]).astype(BF16)
    offs = dict(zip(("qkv", "z", "q", "kv", "iq", "sm_ik"), np.cumsum((0,) + PROJ_WIDTHS)[:-1]))
    rows8 = lax.broadcasted_iota(I32, (8, 1), 0)
    lane = lax.broadcasted_iota(I32, (1, LANES), 1)
    ca, sa = cosa_ref[...], sina_ref[...]
    ci, slo, shi = cosi_ref[...], sinlo_ref[...], sinhi_ref[...]

    def conv_silu(x, cols, r0, nrows):
        prev, cw = (tail_ref[:, cols] if r0 == 0 else x[r0 - 8:r0]), cw_ref[:, cols]
        xb = x[r0:r0 + nrows]
        acc = None
        for back in range(A_CONV - 1, 0, -1):
            shifted = pltpu.roll(xb, back, 0)
            top = shifted[:8]
            for r in range(back):
                top = jnp.where(rows8 == r, prev[8 - back + r:8 - back + r + 1], top)
            term = jnp.concatenate([top, shifted[8:]], axis=0) * cw[A_CONV - 1 - back:A_CONV - back]
            acc = term if acc is None else acc + term
        return _silu(acc + xb * cw[A_CONV - 1:A_CONV])

    half = rb // 2
    assert half % 8 == 0

    def gdn_piece(i):
        cols = slice(i * 256, (i + 1) * 256)

        def finish(x):
            def first():
                qkv_ref[0, 0:half, cols] = conv_silu(x, cols, 0, half)

            def second():
                qkv_ref[0, half:rb, cols] = conv_silu(x, cols, half, half)
                tail_ref[:, cols] = x[rb - 8:rb]
            return [first, second]
        return offs["qkv"] + i * 256, finish

    def plain_piece(i):
        def finish(x):
            def store():
                z_ref[0, :, i * 256:(i + 1) * 256] = x
            return [store]
        return offs["z"] + i * 256, finish

    def single(fn):
        return lambda x: [functools.partial(fn, x)]

    def q_piece(i):
        @single
        def finish(x):
            parts = []
            for hd in range(2):
                qh = _rms(x[:, hd * B_DH:(hd + 1) * B_DH], qn_ref[...])
                parts.append(qh * ca + pltpu.roll(qh, B_DH // 2, 1) * sa)
            qr_ref[0, :, i * 256:(i + 1) * 256] = (jnp.concatenate(parts, axis=1) * (B_DH ** -0.5)).astype(BF16)
        return offs["q"] + i * 256, finish

    def kv_piece():
        @single
        def finish(x):
            k = _rms(x[:, :B_DH], kn_ref[...])
            kr_ref[0] = (k * ca + pltpu.roll(k, B_DH // 2, 1) * sa).astype(BF16)
            v_ref[0] = x[:, B_DH:].astype(BF16)
        return offs["kv"], finish

    def iq_piece(i):
        @single
        def finish(x):
            parts = []
            for hp in range(2):
                xh = x[:, hp * LANES:(hp + 1) * LANES]
                parts.append(xh * ci + pltpu.roll(xh, LANES - IDX_DIM // 2, 1) * slo
                             + pltpu.roll(xh, IDX_DIM // 2, 1) * shi)
            iqr_ref[0, :, i * 256:(i + 1) * 256] = jnp.concatenate(parts, axis=1).astype(BF16)
        return offs["iq"] + i * 256, finish

    def small_ik_piece():
        @single
        def finish(x):
            sm_ref[0] = x[:, :LANES]
            ik = _rms(x[:, LANES:], kin_ref[...])
            ik = ik * ci + pltpu.roll(ik, IDX_DIM // 2, 1) * (slo + shi)
            ike_ref[0] = jnp.where(lane < IDX_DIM, ik, 0.0).astype(BF16)
            iko_ref[0] = jnp.where(lane >= IDX_DIM, ik, 0.0).astype(BF16)
        return offs["sm_ik"], finish

    pieces = [q_piece(0), gdn_piece(0), q_piece(1), gdn_piece(1), iq_piece(0), gdn_piece(2), iq_piece(1),
              gdn_piece(3), kv_piece(), gdn_piece(4), small_ik_piece(), gdn_piece(5), plain_piece(0), plain_piece(1)]
    ready, staged = [], []
    for col, finish in pieces:
        x = _dot(n, w_ref[:, col:col + 256])
        ready += staged
        staged = finish(x)
        for task in [ready.pop(0) for _ in range(min(1, len(ready)))]:
            task()
    for task in ready + staged:
        task()


def _proj_call(h, meta, gain, w, tables, qn, kn, kin, cw, rb, tp, from_frames):
    bsz, _, d = h.shape
    nt = tp // rb
    row = lambda w_, dt: (jax.ShapeDtypeStruct((bsz, tp, w_), dt),
                          pl.BlockSpec((1, rb, w_), lambda b, t: (b, t, 0)))
    outs = [row(1536, F32), row(512, F32), row(128, F32), row(512, BF16), row(128, BF16),
            row(128, BF16), row(512, BF16), row(128, BF16), row(128, BF16)]
    tab_spec = pl.BlockSpec((rb, LANES), lambda b, t: (t, 0))
    return pl.pallas_call(
        functools.partial(_proj_body, rb=rb, from_frames=from_frames),
        grid=(bsz, nt),
        in_specs=[_stream_spec(rb, d, from_frames), _const_spec(meta.shape), _const_spec((1, d)),
                  _const_spec(w.shape)]
                 + [tab_spec] * 5 + [_const_spec((1, LANES))] * 3 + [_const_spec(cw.shape)],
        out_specs=[o[1] for o in outs],
        out_shape=[o[0] for o in outs],
        scratch_shapes=[pltpu.VMEM((8, PROJ_WIDTHS[0]), F32)],
        compiler_params=_params(("parallel", "arbitrary")),
        name="proj",
    )(h, meta, gain, w, *tables, qn, kn, kin, cw)


def _split3(x):
    x1 = x.astype(BF16)
    r1 = x - x1.astype(F32)
    x2 = r1.astype(BF16)
    x3 = (r1 - x2.astype(F32)).astype(BF16)
    return jnp.concatenate([x1, x2, x3], axis=0)


def _gdn_body(qkv_ref, z_ref, sm_ref, alog_ref, dtb_ref, ng_ref, lt3_ref, expand_ref,
              out_ref, s01_ref, s23_ref, u_ref, wq_ref, qkd_ref, kdt_ref, egl_ref, *, rb):
    t = pl.program_id(1)

    @pl.when(t == 0)
    def _():
        s01_ref[...] = jnp.zeros_like(s01_ref)
        s23_ref[...] = jnp.zeros_like(s23_ref)

    hw = A_HEADS * A_DK
    pw = A_HEADS * CHUNK
    ii = lax.broadcasted_iota(I32, (CHUNK, pw), 0)
    jj = lax.broadcasted_iota(I32, (CHUNK, pw), 1) & (CHUNK - 1)
    colhead = lax.broadcasted_iota(I32, (CHUNK, pw), 1) >> 6
    eye_p = ii == jj
    bd_mask = (lax.broadcasted_iota(I32, (pw, pw), 0) >> 6) == (lax.broadcasted_iota(I32, (pw, pw), 1) >> 6)
    bdk_mask = (lax.broadcasted_iota(I32, (pw, hw), 0) >> 6) == (lax.broadcasted_iota(I32, (pw, hw), 1) >> 7)
    pair_mask = (lax.broadcasted_iota(I32, (pw, pw), 0) >> 7) == (lax.broadcasted_iota(I32, (pw, pw), 1) >> 7)
    lane = lax.broadcasted_iota(I32, (1, LANES), 1)
    rows64 = lax.broadcasted_iota(I32, (CHUNK, 1), 0)
    neg_a = -jnp.exp(alog_ref[...])
    dtb = dtb_ref[...]
    ng = ng_ref[...]

    def block_diag(xp):
        return jnp.where(bd_mask, jnp.concatenate([xp.astype(BF16)] * A_HEADS, axis=0), 0)

    def stack_heads(xp):
        return jnp.concatenate([jnp.where(colhead == hd, xp, 0.0) for hd in range(A_HEADS)], axis=0)

    def l2n(x):
        parts = []
        for hd in range(A_HEADS):
            xh = x[:, hd * A_DK:(hd + 1) * A_DK]
            parts.append(xh * lax.rsqrt(jnp.sum(xh * xh, axis=-1, keepdims=True) + EPS))
        return jnp.concatenate(parts, axis=1)

    zeros = jnp.zeros((CHUNK, 2 * A_DK), F32)

    def pair_lanes(x, a_, b_):
        return jnp.concatenate([x[a_ * CHUNK:(a_ + 1) * CHUNK], x[b_ * CHUNK:(b_ + 1) * CHUNK]], axis=1)

    def local_load(c):
        r0 = c * CHUNK if isinstance(c, int) else pl.multiple_of(c * CHUNK, CHUNK)
        return qkv_ref[0, pl.ds(r0, CHUNK), :], sm_ref[0, pl.ds(r0, CHUNK), :], r0

    def local_compute(xq, sm, r0):
        q = l2n(xq[:, :hw]) * (A_DK ** -0.5)
        k = l2n(xq[:, hw:2 * hw])
        v = xq[:, 2 * hw:]

        g = jnp.where(lane < A_HEADS, neg_a * _softplus(sm + dtb), 0.0)
        g = jnp.where(t * rb + r0 + rows64 >= PAD, g, 0.0)
        to_packed, to_wide = expand_ref[:, :pw], expand_ref[:, pw:]
        g3 = _split3(g)
        cg = _dot(g3, to_packed).astype(BF16)
        gcs = _dot(lt3_ref[...], g3)
        yield
        gp = _dot(lt3_ref[...], cg)
        in_head = lane < A_HEADS
        beta = jnp.where(in_head, pltpu.roll(_sigmoid(sm), LANES - A_HEADS, 1), 0.0)
        eg_s = jnp.where(in_head, jnp.exp(gcs), 0.0)
        to_last = jnp.where(in_head, jnp.exp(gcs[CHUNK - 1:CHUNK] - gcs), 0.0)
        wide = _dot(jnp.concatenate([beta, beta * eg_s, eg_s, to_last], axis=0).astype(BF16), to_wide)
        bp = _dot(beta.astype(BF16), to_packed)
        glast3 = _dot(_split3(gcs[CHUNK - 16:]), to_wide)
        yield
        bq, bq_eg, eg, kd_scale = (wide[i * CHUNK:(i + 1) * CHUNK] for i in range(4))
        glast = glast3[15:16] + glast3[31:32] + glast3[47:48]
        grow = jnp.sum(jnp.where(eye_p, gp, 0.0), axis=0, keepdims=True)
        decay = jnp.where(ii >= jj, jnp.exp(gp - grow), 0.0)

        kb = k.astype(BF16)
        bdk = jnp.where(bdk_mask, jnp.concatenate([kb] * A_HEADS, axis=0), 0)
        kq = _dot_nt(jnp.concatenate([kb, q.astype(BF16)], axis=0), bdk)
        yield
        kk_p, qk_p = kq[:CHUNK], kq[CHUNK:]

        a = -jnp.where(ii > jj, bp * kk_p * decay, 0.0)
        tinv = jnp.where(eye_p, 1.0, 0.0) + a
        pk = _dot(a.astype(BF16), block_diag(a))
        yield
        for _ in range(4):
            pt = _dot(jnp.concatenate([pk, tinv], axis=0).astype(BF16), block_diag(pk))
            yield
            tinv = tinv + pt[CHUNK:]
            pk = pt[:CHUNK]
        tinv = tinv + _dot(tinv.astype(BF16), block_diag(pk))
        yield

        vb = v * bq
        kbg = k * bq_eg
        rv = jnp.concatenate(
            [jnp.concatenate([vb[:, hd * A_DK:(hd + 1) * A_DK], kbg[:, hd * A_DK:(hd + 1) * A_DK]], axis=1)
             for hd in range(A_HEADS)], axis=0)
        uw = _dot(stack_heads(tinv).astype(BF16), rv.astype(BF16))
        yield
        u, w = uw[:, :A_DK], uw[:, A_DK:]

        qg = q * eg
        kd = k * kd_scale
        return (u,
                jnp.concatenate([pair_lanes(w, 0, 1), qg[:, :2 * A_DK]], axis=0).astype(BF16),
                jnp.concatenate([pair_lanes(w, 2, 3), qg[:, 2 * A_DK:]], axis=0).astype(BF16),
                stack_heads(qk_p * decay).astype(BF16),
                jnp.concatenate([kd[:, :2 * A_DK], zeros], axis=0).T.astype(BF16),
                jnp.concatenate([kd[:, 2 * A_DK:], zeros], axis=0).T.astype(BF16),
                jnp.broadcast_to(jnp.exp(glast), (8, hw)))

    def slot(c):
        return c % GDN_SLOTS if isinstance(c, int) else c & (GDN_SLOTS - 1)

    def local_store(c, vals):
        s = slot(c)
        u_ref[s], wq_ref[s, 0], wq_ref[s, 1], qkd_ref[s], kdt_ref[s, 0], kdt_ref[s, 1], egl_ref[s] = vals

    def local_chain(c):
        vals = yield from local_compute(*local_load(c))
        local_store(c, vals)

    def state_chain(chunks):
        def unpair(r):
            return [r[:, :A_DK], r[:, A_DK:]]

        def state_update(s_old, kdt, vn_pair, egl_pair):
            upd = _dot(kdt, jnp.concatenate([vn_pair, zeros], axis=0).astype(BF16))
            return s_old * egl_pair + jnp.where(pair_mask, upd, 0.0)

        s01 = s01_ref[...]
        s23 = s23_ref[...]
        for c in chunks:
            r0 = c * CHUNK if isinstance(c, int) else pl.multiple_of(c * CHUNK, CHUNK)
            s = slot(c)
            r01 = _dot(wq_ref[s, 0], s01.astype(BF16))
            r23 = _dot(wq_ref[s, 1], s23.astype(BF16))
            yield
            ws = jnp.concatenate(unpair(r01[:CHUNK]) + unpair(r23[:CHUNK]), axis=0)
            qs = jnp.concatenate(unpair(r01[CHUNK:]) + unpair(r23[CHUNK:]), axis=0)
            vn = u_ref[s] - ws
            o_rs = qs + _dot(qkd_ref[s], vn.astype(BF16))
            egl = egl_ref[s][0:1, :]
            s01 = state_update(s01, kdt_ref[s, 0], pair_lanes(vn, 0, 1), egl[:, :2 * A_DK])
            s23 = state_update(s23, kdt_ref[s, 1], pair_lanes(vn, 2, 3), egl[:, 2 * A_DK:])
            yield
            zz = z_ref[0, pl.ds(r0, CHUNK), :]
            parts = [_rms(o_rs[hd * CHUNK:(hd + 1) * CHUNK], ng) for hd in range(A_HEADS)]
            out_ref[0, pl.ds(r0, CHUNK), :] = jnp.concatenate(parts, axis=1) * _silu(zz)
        s01_ref[...] = s01
        s23_ref[...] = s23

    def run_lockstep(chains):
        live = list(chains)
        while live:
            for gen in list(live):
                try:
                    next(gen)
                except StopIteration:
                    live.remove(gen)

    nc = rb // CHUNK
    npairs = nc // 2
    if npairs == 0:
        run_lockstep([local_chain(0)])
        run_lockstep([state_chain([0])])
    else:
        run_lockstep([local_chain(0), local_chain(1)])

        def steady(i, carry):
            run_lockstep([local_chain(2 * i), local_chain(2 * i + 1), state_chain([2 * i - 2, 2 * i - 1])])
            return carry

        lax.fori_loop(1, npairs, steady, 0)
        last = [2 * npairs - 2, 2 * npairs - 1]
        if nc % 2:
            run_lockstep([local_chain(nc - 1), state_chain(last)])
            run_lockstep([state_chain([nc - 1])])
        else:
            run_lockstep([state_chain(last)])


def _gdn_call(qkv, z, sm, alog, dtb, ng, lt3, expand):
    bsz, tp, _ = qkv.shape
    rb = tp
    nt = tp // rb
    nc = GDN_SLOTS
    hw = A_HEADS * A_DK
    pw = A_HEADS * CHUNK
    row = lambda w_: pl.BlockSpec((1, rb, w_), lambda b, t: (b, t, 0))
    return pl.pallas_call(
        functools.partial(_gdn_body, rb=rb),
        grid=(bsz, nt),
        in_specs=[row(3 * hw), row(hw), row(LANES), _const_spec((1, LANES)),
                  _const_spec((1, LANES)), _const_spec((1, LANES)), _const_spec(lt3.shape),
                  _const_spec(expand.shape)],
        out_specs=row(hw),
        out_shape=jax.ShapeDtypeStruct((bsz, tp, hw), F32),
        scratch_shapes=[pltpu.VMEM((2 * A_DK, 2 * A_DK), F32), pltpu.VMEM((2 * A_DK, 2 * A_DK), F32),
                        pltpu.VMEM((nc, pw, A_DK), F32), pltpu.VMEM((nc, 2, 2 * CHUNK, 2 * A_DK), BF16),
                        pltpu.VMEM((nc, pw, pw), BF16), pltpu.VMEM((nc, 2, 2 * A_DK, 2 * CHUNK), BF16),
                        pltpu.VMEM((nc, 8, hw), F32)],
        compiler_params=_params(("parallel", "arbitrary")),
        name="gdn",
    )(qkv, z, sm, alog, dtb, ng, lt3, expand)


def _dsa_body(qr_ref, kr_ref, v_ref, iqr_ref, ike_ref, iko_ref, sm_ref, lstrict_ref, out_ref,
              vt_ref, vmt_ref, skey_ref, planes_ref, sel_ref, st_ref, ot_ref, *, seq, topk):
    g = pl.program_id(1)
    iw_scale = IDX_HEADS ** -0.5 * IDX_DIM ** -0.5
    qpair = 2 * CHUNK
    neg_inf = -jnp.inf

    @pl.when(g == 0)
    def _():
        planes_ref[...] = jnp.zeros_like(planes_ref)

        def vblk(i, carry):
            r = pl.multiple_of(CHUNK + i * LANES, CHUNK)
            vt_ref[i] = v_ref[0, pl.ds(r, LANES), :].astype(F32).T.astype(BF16)
            return carry

        lax.fori_loop(0, seq // LANES, vblk, 0)
        vmt_ref[...] = v_ref[0, 0:LANES, :].astype(F32).T.astype(BF16)

        qm = qr_ref[0, 0:CHUNK, :]
        km = kr_ref[0, 0:CHUNK, :]
        vm = v_ref[0, 0:CHUNK, :]
        colv = lax.broadcasted_iota(I32, (1, CHUNK), 1) >= PAD
        parts = []
        for hd in range(B_HEADS):
            s = _dot_nt(qm[:, hd * B_DH:(hd + 1) * B_DH], km)
            s = jnp.where(colv, s, neg_inf)
            e = jnp.exp(s - jnp.max(s, axis=-1, keepdims=True))
            pr = e / jnp.sum(e, axis=-1, keepdims=True)
            parts.append(_dot(pr.astype(BF16), vm))
        out_ref[0, 0:CHUNK, :] = jnp.concatenate(parts, axis=1)

    lane = lax.broadcasted_iota(I32, (1, LANES), 1)
    groups = []
    for j in range(QUERY_GROUPS):
        p = g * QUERY_GROUPS + j
        r0 = pl.multiple_of(CHUNK + p * qpair, CHUNK)
        qb = qr_ref[0, pl.ds(r0, qpair), :]
        ib = iqr_ref[0, pl.ds(r0, qpair), :]
        smt = sm_ref[0, pl.ds(r0, qpair), :].T
        groups.append(dict(
            j=j, r0=r0,
            q_hq=jnp.concatenate([qb[:, hd * B_DH:(hd + 1) * B_DH] for hd in range(B_HEADS)], axis=0),
            iq_hq=jnp.concatenate([ib[:, hp * LANES:(hp + 1) * LANES] for hp in range(IDX_HEADS // 2)], axis=0),
            wts=[smt[2 * A_HEADS + hd:2 * A_HEADS + hd + 1, :] * iw_scale for hd in range(IDX_HEADS)],
            limit=jnp.where(lane < CHUNK, (2 * p + 1) * CHUNK, (2 * p + 2) * CHUNK)))
    assert QUERY_GROUPS * qpair == KEY_CHUNK
    full_rows = [KEY_CHUNK] * QUERY_GROUPS
    diag_rows = [(j + 1) * qpair for j in range(QUERY_GROUPS)]
    diag_plane_rows = [-(-r // PLANE_ROWS) * PLANE_ROWS for r in diag_rows]
    krows = lax.broadcasted_iota(I32, (KEY_CHUNK, 1), 0)

    def score_chunk(kc, carry, nrows=full_rows):
        k0 = pl.multiple_of(CHUNK + kc * KEY_CHUNK, CHUNK)
        base = pl.multiple_of(kc * KEY_CHUNK, KEY_CHUNK)
        ke, ko, kk = (ref[0, pl.ds(k0, KEY_CHUNK), :] for ref in (ike_ref, iko_ref, kr_ref))
        logits = []
        for grp_, nr in zip(groups, nrows):
            logits.append((_dot_nt(ke[:nr], grp_["iq_hq"]), _dot_nt(ko[:nr], grp_["iq_hq"])))
            st_ref[grp_["j"], pl.ds(base, nr), :] = _dot_nt(kk[:nr], grp_["q_hq"])
        for grp_, nr, (le, lo) in zip(groups, nrows, logits):
            score_tail(kc, pl.ds(base, nr), nr, grp_, le, lo)
        return carry

    def score_tail(kc, rows, nr, grp_, le, lo):
        j, wts, limit = grp_["j"], grp_["wts"], grp_["limit"]
        sc = jnp.zeros((nr, LANES), F32)
        for hp in range(IDX_HEADS // 2):
            sc = sc + wts[2 * hp] * jnp.maximum(le[:, hp * LANES:(hp + 1) * LANES], 0.0)
            sc = sc + wts[2 * hp + 1] * jnp.maximum(lo[:, hp * LANES:(hp + 1) * LANES], 0.0)
        bits = lax.bitcast_convert_type(sc, I32)
        bits = jnp.where(sc == 0.0, 0, bits)
        key = bits ^ ((bits >> 31) & 0x7FFFFFFF)
        valid = kc * KEY_CHUNK + krows[:nr] < limit
        key = jnp.where(valid, key, INT_MIN)
        skey_ref[j, rows, :] = key
        for gi in range(nr // PLANE_ROWS):
            a = [key[gi * PLANE_ROWS + 8 * r:gi * PLANE_ROWS + 8 * r + 8, :] ^ INT_MIN for r in range(32)]
            sh, msk = 16, 0x0000FFFF
            while sh:
                k = 0
                while k < 32:
                    tt = (a[k] ^ lax.shift_right_logical(a[k + sh], jnp.int32(sh))) & msk
                    a[k] = a[k] ^ tt
                    a[k + sh] = a[k + sh] ^ (tt << sh)
                    k = (k + sh + 1) & ~sh
                sh >>= 1
                msk = (msk ^ (msk << sh)) & 0xFFFFFFFF
            g8 = pl.multiple_of((kc * (KEY_CHUNK // PLANE_ROWS) + gi) * 8, 8)
            for b in range(32):
                planes_ref[j, b, pl.ds(g8, 8), :] = a[b]

    lax.fori_loop(0, g, score_chunk, 0)
    score_chunk(g, 0, diag_plane_rows)

    words_per_chunk = KEY_CHUNK // 32

    def lane_count(words):
        c = lax.population_count(words)
        if words.shape[0] > 8:
            c = jnp.sum(c.reshape(words.shape[0] // 8, 8, LANES), axis=0)
        return jnp.sum(c, axis=0, keepdims=True)

    def select_threshold(step):
        nrow = (step + 1) * words_per_chunk
        prow = lax.broadcasted_iota(I32, (nrow, 1), 0)
        alive0 = [jnp.where(prow < step * words_per_chunk + nr // 32, jnp.full((nrow, LANES), -1, I32), 0)
                  for nr in diag_plane_rows]

        def bit_body(i, carry):
            bit = jnp.left_shift(jnp.int32(1), 31 - i)
            out = []
            for grp_, (tu, above, alive) in zip(groups, carry):
                hit = alive & planes_ref[grp_["j"], i, 0:nrow, :]
                c1 = lane_count(hit)
                take = above + c1 >= topk
                out.append((jnp.where(take, tu | bit, tu), jnp.where(take, above, above + c1),
                            jnp.where(take, hit, alive ^ hit)))
            return tuple(out)

        zero_row = jnp.zeros((1, LANES), I32)
        result = lax.fori_loop(0, 32, bit_body, tuple((zero_row, zero_row, a0) for a0 in alive0))
        for grp_, (tu, above, alive) in zip(groups, result):
            for n, row in enumerate((tu, above, lane_count(alive))):
                sel_ref[grp_["j"], n] = jnp.broadcast_to(row, (8, LANES))

    for step in range(seq // KEY_CHUNK):
        pl.when(g == step)(functools.partial(select_threshold, step))

    for grp_, nr_diag in zip(groups, diag_rows):
        tu, above, n_eq = (sel_ref[grp_["j"], n][0:1] for n in range(3))
        j, thr, found = grp_["j"], tu ^ INT_MIN, tu != 0
        grp_["thr"] = thr
        tie_lane = found & (above + n_eq > topk)

        @pl.when(jnp.max(jnp.where(tie_lane, 1, 0)) > 0)
        def _(j=j, thr=thr, found=found, above=above, nr_diag=nr_diag):
            need = (topk - above).astype(F32)

            def body(kc, seen, nr=KEY_CHUNK):
                base = pl.multiple_of(kc * KEY_CHUNK, KEY_CHUNK)
                for sub in range(nr // LANES):
                    rows = pl.ds(base + sub * LANES, LANES)
                    blk = skey_ref[j, rows, :]
                    tied = (blk == thr) & found
                    eq = jnp.where(tied, 1.0, 0.0)
                    rank = _dot(lstrict_ref[...], eq.astype(BF16)) + seen
                    skey_ref[j, rows, :] = jnp.where(tied & (rank >= need), INT_MIN, blk)
                    seen = seen + jnp.sum(eq, axis=0, keepdims=True)
                return seen

            body(g, lax.fori_loop(0, g, body, jnp.zeros((1, LANES), F32)), nr_diag)

    mrows = lax.broadcasted_iota(I32, (LANES, 1), 0)
    k_meta = kr_ref[0, 0:LANES, :]
    ones = jnp.ones((ONES_ROWS, LANES), BF16)
    v_meta = jnp.concatenate([vmt_ref[...], ones], axis=0)
    init = []
    for grp_ in groups:
        s_meta = _dot_nt(k_meta, grp_["q_hq"])
        s_meta = jnp.where((mrows >= PAD) & (mrows < CHUNK), s_meta, neg_inf)
        m0 = jnp.max(s_meta, axis=0, keepdims=True)
        ot_ref[grp_["j"]] = _dot(v_meta, jnp.exp(s_meta - m0).astype(BF16))
        init.append(m0)
        grp_["thr"] = jnp.maximum(grp_["thr"], INT_MIN + 1)

    def pv_chunk(kc, carry, nrows=full_rows):
        base = pl.multiple_of(kc * KEY_CHUNK, KEY_CHUNK)
        nblk = KEY_CHUNK // LANES
        vblk = jnp.concatenate([jnp.concatenate([vt_ref[nblk * kc + i], ones], axis=0) for i in range(nblk)], axis=1)
        out, probs = [], []
        for grp_, nr, m in zip(groups, nrows, carry):
            rows = pl.ds(base, nr)
            bias = jnp.where(skey_ref[grp_["j"], rows, :] >= grp_["thr"], 0.0, neg_inf)
            s = st_ref[grp_["j"], rows, :] + jnp.concatenate([bias] * B_HEADS, axis=1)
            m_new = jnp.maximum(m, jnp.max(jnp.max(s.reshape(nr // 8, 8, B_HEADS * LANES), axis=0),
                                           axis=0, keepdims=True))
            out.append(m_new)
            probs.append((jnp.exp(m - m_new), jnp.exp(s - m_new).astype(BF16)))
        for grp_, nr, (alpha, prb) in zip(groups, nrows, probs):
            ot_ref[grp_["j"]] = ot_ref[grp_["j"]] * alpha + _dot(vblk[:, :nr], prb)
        return tuple(out)

    pv_chunk(g, lax.fori_loop(0, g, pv_chunk, tuple(init)), diag_rows)
    for grp_ in groups:
        acc = ot_ref[grp_["j"]]
        o_hq = (acc[:B_DH] / acc[B_DH:B_DH + 1]).T
        for hd in range(B_HEADS):
            out_ref[0, pl.ds(grp_["r0"], qpair), hd * B_DH:(hd + 1) * B_DH] = o_hq[hd * LANES:(hd + 1) * LANES, :]


def _dsa_call(qr, kr, v, iqr, ike, iko, sm, lstrict, seq, topk):
    bsz, tp, _ = qr.shape
    nsteps, rem = divmod(seq, 2 * CHUNK * QUERY_GROUPS)
    assert rem == 0
    full = lambda w_: pl.BlockSpec((1, tp, w_), lambda b, p: (b, 0, 0))
    hw = B_HEADS * B_DH
    nq = QUERY_GROUPS
    return pl.pallas_call(
        functools.partial(_dsa_body, seq=seq, topk=topk),
        grid=(bsz, nsteps),
        in_specs=[full(hw), full(B_DH), full(B_DH), full(hw), full(LANES), full(LANES), full(LANES),
                  _const_spec(lstrict.shape)],
        out_specs=full(hw),
        out_shape=jax.ShapeDtypeStruct((bsz, tp, hw), F32),
        scratch_shapes=[pltpu.VMEM((seq // LANES, B_DH, LANES), BF16), pltpu.VMEM((B_DH, LANES), BF16),
                        pltpu.VMEM((nq, seq, LANES), I32), pltpu.VMEM((nq, 32, seq // 32, LANES), I32),
                        pltpu.VMEM((nq, 3, 8, LANES), I32),
                        pltpu.VMEM((nq, seq, hw), F32), pltpu.VMEM((nq, B_DH + ONES_ROWS, hw), F32)],
        compiler_params=_params(("parallel", "arbitrary")),
        name="dsa",
    )(qr, kr, v, iqr, ike, iko, sm, lstrict)


def _mix_body(h_ref, meta_ref, oa_ref, ob_ref, g_ref, wg_ref, bg_ref, wa_ref, wb_ref, wo_ref, out_ref,
              *, rb, from_frames):
    x = _stream_block(h_ref, meta_ref, rb, from_frames)
    d = x.shape[-1]
    n = _rms(x, g_ref[...]).astype(BF16)
    gates = _sigmoid(_dot(n, wg_ref[...]) + bg_ref[...])
    y = gates[:, :d] * _dot(oa_ref[0].astype(BF16), wa_ref[...]) + gates[:, d:] * _dot(ob_ref[0].astype(BF16), wb_ref[...])
    out = x + _dot(y.astype(BF16), wo_ref[...])
    rows = pl.program_id(1) * rb + lax.broadcasted_iota(I32, (rb, 1), 0)
    out_ref[0] = jnp.where(rows >= PAD, out, 0.0)


def _mix_call(h, meta, oa, ob, gain, wg, bg, wa, wb, wo, rb, from_frames):
    bsz, tp, _ = oa.shape
    d = h.shape[-1]
    row = lambda w_: pl.BlockSpec((1, rb, w_), lambda b, t: (b, t, 0))
    return pl.pallas_call(
        functools.partial(_mix_body, rb=rb, from_frames=from_frames),
        grid=(bsz, tp // rb),
        in_specs=[_stream_spec(rb, d, from_frames), _const_spec(meta.shape), row(oa.shape[-1]),
                  row(ob.shape[-1]), _const_spec((1, d)), _const_spec(wg.shape),
                  _const_spec(bg.shape), _const_spec(wa.shape), _const_spec(wb.shape), _const_spec(wo.shape)],
        out_specs=row(d),
        out_shape=jax.ShapeDtypeStruct((bsz, tp, d), F32),
        compiler_params=_params(("parallel", "arbitrary")),
        name="mix",
    )(h, meta, oa, ob, gain, wg, bg, wa, wb, wo)


def _ffn_body(h_ref, g_ref, wup_ref, cw_ref, wd_ref, out_ref, carry_ref, act_ref, *, rb, dff, frames_only):
    t = pl.program_id(1)

    @pl.when(t == 0)
    def _():
        carry_ref[...] = jnp.zeros_like(carry_ref)

    if frames_only:
        rb = rb + 8
    x = h_ref[0]
    n = _rms(x, g_ref[...]).astype(BF16)
    rows = lax.broadcasted_iota(I32, (rb, 1), 0)
    nch = dff // FF_CHUNK

    def gate_up(c):
        return (_dot(n, wup_ref[:, c * FF_CHUNK:(c + 1) * FF_CHUNK]),
                _dot(n, wup_ref[:, dff + c * FF_CHUNK:dff + (c + 1) * FF_CHUNK]))

    ahead = gate_up(0)
    for c in range(nch):
        cols = slice(c * FF_CHUNK, (c + 1) * FF_CHUNK)
        gate, up = ahead
        if c + 1 < nch:
            ahead = gate_up(c + 1)
        g1, g2 = pltpu.roll(gate, 1, 0), pltpu.roll(gate, 2, 0)
        if not frames_only:
            prev = carry_ref[:, cols]
            g1 = jnp.where(rows == 0, prev[7:8], g1)
            g2 = jnp.where(rows == 0, prev[6:7], jnp.where(rows == 1, prev[7:8], g2))
            carry_ref[:, cols] = gate[rb - 8:rb]
        cw = cw_ref[:, cols]
        conv = g2 * cw[0:1] + g1 * cw[1:2] + gate * cw[2:3]
        act_ref[:, cols] = (_silu(conv) * up).astype(BF16)
    out = x + _dot(act_ref[...], wd_ref[...])
    if frames_only:
        out_ref[0] = out[8:]
    else:
        out_ref[0] = jnp.where(t * rb + rows >= PAD, out, 0.0)


def _ffn_call(h, gain, wup, cw, wd, rb, frames_only=False):
    bsz, tp, d = h.shape
    dff = wd.shape[0]
    if frames_only:
        rb = FRAME_BLOCK
        nrows = tp - CHUNK
        in_rows = pl.BlockSpec((pl.Element(1), pl.Element(rb + 8), pl.Element(d)),
                               lambda b, t: (b, pl.multiple_of(CHUNK - 8 + t * rb, 8), 0))
    else:
        nrows = tp
        in_rows = pl.BlockSpec((1, rb, d), lambda b, t: (b, t, 0))
    return pl.pallas_call(
        functools.partial(_ffn_body, rb=rb, dff=dff, frames_only=frames_only),
        grid=(bsz, nrows // rb),
        in_specs=[in_rows, _const_spec((1, d)), _const_spec(wup.shape), _const_spec(cw.shape),
                  _const_spec(wd.shape)],
        out_specs=pl.BlockSpec((1, rb, d), lambda b, t: (b, t, 0)),
        out_shape=jax.ShapeDtypeStruct((bsz, nrows, d), F32),
        scratch_shapes=[pltpu.VMEM((8, dff), F32), pltpu.VMEM((rb + 8 * frames_only, dff), BF16)],
        compiler_params=_params(("parallel", "arbitrary")),
        name="ffn",
    )(h, gain, wup, cw, wd)


def _rope_tables(tp):
    pos = (jnp.arange(tp, dtype=F32) - PAD)[:, None]

    def cs(dim):
        inv = 1.0 / (ROPE_THETA ** (jnp.arange(0, dim, 2, dtype=F32) / dim))
        ang = pos * inv[None, :]
        return jnp.cos(ang), jnp.sin(ang)

    ca, sa = cs(B_DH)
    ci, si = cs(IDX_DIM)
    zi = jnp.zeros_like(si)
    return (jnp.concatenate([ca, ca], axis=1), jnp.concatenate([-sa, sa], axis=1),
            jnp.concatenate([ci] * 4, axis=1), jnp.concatenate([-si, zi, -si, zi], axis=1),
            jnp.concatenate([zi, si, zi, si], axis=1))


def _pad_lanes(vec):
    return jnp.zeros((1, LANES), F32).at[0, :vec.shape[0]].set(vec.astype(F32))


def _head_expander():
    pw, hw = A_HEADS * CHUNK, A_HEADS * A_DK
    head = np.concatenate([np.arange(pw) // CHUNK, np.arange(hw) // A_DK])
    return jnp.asarray(np.arange(LANES)[:, None] == head[None, :], BF16)


def kernel(x, meta_tokens, norm_mix, w_in, conv_a, a_log, dt_bias, a_out_norm, q_norm, k_norm, kidx_norm,
           w_branch_a, w_branch_b, w_gate, b_gate, w_out, norm_ffn, w_up, conv_ffn, w_down):
    bsz, seq, d = x.shape
    depth = w_in.shape[0]
    assert seq % KEY_CHUNK == 0
    tp = CHUNK + seq
    rb = _row_block(tp // CHUNK)
    topk = min(TOPK_MAX, seq // TOPK_DIV)

    meta = meta_tokens.astype(x.dtype)
    from_frames = rb <= seq
    if from_frames:
        h = x
    else:
        h = jnp.concatenate([jnp.zeros((bsz, PAD, d), x.dtype), jnp.broadcast_to(meta[None], (bsz, N_META, d)), x],
                            axis=1)

    tables = _rope_tables(tp)
    expand = _head_expander()
    tri = np.arange(CHUNK)
    lt3 = jnp.asarray(np.tile(tri[:, None] >= tri[None, :], (1, 3)), BF16)
    trk = np.arange(LANES)
    lstrict = jnp.asarray(trk[:, None] > trk[None, :], BF16)

    hw = A_HEADS * A_DK
    o = np.cumsum((0, hw, hw, hw, hw, A_HEADS, A_HEADS, B_HEADS * B_DH, B_DH, B_DH,
                   IDX_HEADS * IDX_DIM, IDX_DIM, IDX_HEADS))
    for l in range(depth):
        w = w_in[l]
        small = jnp.concatenate([w[:, o[4]:o[6]], w[:, o[11]:o[12]],
                                 jnp.zeros((d, LANES - 2 * A_HEADS - IDX_HEADS), w.dtype)], axis=1)
        w_p = jnp.concatenate([w[:, o[0]:o[3]], w[:, o[3]:o[4]], w[:, o[6]:o[7]], w[:, o[7]:o[9]],
                               w[:, o[9]:o[10]], small, w[:, o[10]:o[11]], w[:, o[10]:o[11]]], axis=1).astype(BF16)
        qkv, z, sm, qr, kr, v, iqr, ike, iko = _proj_call(
            h, meta, norm_mix[l][None], w_p, tables, q_norm[l][None], k_norm[l][None],
            jnp.concatenate([kidx_norm[l], kidx_norm[l]])[None], conv_a[l], rb, tp, from_frames)
        o_a = _gdn_call(qkv, z, sm, _pad_lanes(a_log[l]), _pad_lanes(dt_bias[l]),
                        a_out_norm[l][None], lt3, expand)
        o_b = _dsa_call(qr, kr, v, iqr, ike, iko, sm, lstrict, seq, topk)
        h = _mix_call(h, meta, o_a, o_b, norm_mix[l][None], w_gate[l].astype(BF16), b_gate[l][None],
                      w_branch_a[l].astype(BF16), w_branch_b[l].astype(BF16), w_out[l].astype(BF16), rb,
                      from_frames)
        from_frames = False
        h = _ffn_call(h, norm_ffn[l][None], w_up[l].astype(BF16), conv_ffn[l], w_down[l].astype(BF16), rb,
                      frames_only=l == depth - 1)
    return h
```

```python
import functools
import math

import jax
import jax.numpy as jnp
import numpy as np
from jax import lax
from jax.experimental import pallas as pl
from jax.experimental.pallas import tpu as pltpu

F32 = jnp.float32
BF16 = jnp.bfloat16
I32 = jnp.int32

CHUNK = 64
N_META = 16
PAD = CHUNK - N_META
ROPE_THETA = 10000.0
EPS = 1e-6
A_HEADS = 4
A_DK = 128
A_CONV = 4
B_HEADS = 4
B_DH = 128
IDX_HEADS = 8
IDX_DIM = 64
TOPK_MAX = 256
TOPK_DIV = 4
FFN_CONV = 3
LANES = 128
KEY_CHUNK = 512
PLANE_ROWS = 256
QUERY_GROUPS = 4
ONES_ROWS = 16
FF_CHUNK = 256
FRAME_BLOCK = 512
GDN_SLOTS = 4
Q_SCALE = B_DH ** -0.5 * math.log2(math.e)
INT_MIN = -(2 ** 31)
VMEM_LIMIT_BYTES = 56 * 1024 * 1024

NT_DIMS = (((1,), (1,)), ((), ()))


def _dot(a, b):
    return jnp.dot(a, b, preferred_element_type=F32)


def _dot_nt(a, b):
    return lax.dot_general(a, b, NT_DIMS, preferred_element_type=F32)


def _dot_exact(a, b):
    return jnp.dot(a, b, preferred_element_type=F32, precision=lax.Precision.HIGHEST)


def _rms(x, gain):
    return x * lax.rsqrt(jnp.mean(x * x, axis=-1, keepdims=True) + EPS) * gain


def _sigmoid(x):
    return 0.5 * jnp.tanh(0.5 * x) + 0.5


def _silu(x):
    return x * _sigmoid(x)


def _softplus(x):
    return jnp.maximum(x, 0.0) + jnp.log(1.0 + jnp.exp(-jnp.abs(x)))


def _row_block(n_chunks):
    g = max(d for d in range(1, 12) if n_chunks % d == 0)
    return g * CHUNK


def _params(sem):
    return pltpu.CompilerParams(dimension_semantics=sem, vmem_limit_bytes=VMEM_LIMIT_BYTES)


def _const_spec(shape):
    nd = len(shape)
    return pl.BlockSpec(shape, lambda *_: (0,) * nd)


def _stream_spec(rb, d, from_frames):
    if not from_frames:
        return pl.BlockSpec((1, rb, d), lambda b, t: (b, t, 0))
    return pl.BlockSpec((pl.Element(1), pl.Element(rb), pl.Element(d)),
                        lambda b, t: (b, pl.multiple_of(jnp.maximum(t * rb - CHUNK, 0), 8), 0))


def _stream_block(h_ref, meta_ref, rb, from_frames):
    x = h_ref[0]
    if not from_frames:
        return x
    first = jnp.concatenate([jnp.zeros((PAD, x.shape[-1]), x.dtype), meta_ref[...], x[:rb - CHUNK]], axis=0)
    return jnp.where(pl.program_id(1) == 0, first, x)


PROJ_WIDTHS = (1536, 512, 512, 256, 512, 256)


def _proj_body(h_ref, meta_ref, g_ref, w_ref, cosa_ref, sina_ref, cosi_ref, sinlo_ref, sinhi_ref,
               qn_ref, kn_ref, kin_ref, cw_ref,
               qkv_ref, z_ref, sm_ref, qr_ref, kr_ref, v_ref, iqr_ref, ike_ref, iko_ref, tail_ref,
               *, rb, from_frames):
    @pl.when(pl.program_id(1) == 0)
    def _():
        tail_ref[...] = jnp.zeros_like(tail_ref)

    n = _rms(_stream_block(h_ref, meta_ref, rb, from_frames), g_ref[...]).astype(BF16)
    offs = dict(zip(("qkv", "z", "q", "kv", "iq", "sm_ik"), np.cumsum((0,) + PROJ_WIDTHS)[:-1]))
    rows8 = lax.broadcasted_iota(I32, (8, 1), 0)
    lane = lax.broadcasted_iota(I32, (1, LANES), 1)
    ca, sa = cosa_ref[...], sina_ref[...]
    ci, slo, shi = cosi_ref[...], sinlo_ref[...], sinhi_ref[...]

    def conv_silu(x, cols, r0, nrows):
        prev, cw = (tail_ref[:, cols] if r0 == 0 else x[r0 - 8:r0]), cw_ref[:, cols]
        xb = x[r0:r0 + nrows]
        acc = None
        for back in range(A_CONV - 1, 0, -1):
            shifted = pltpu.roll(xb, back, 0)
            top = shifted[:8]
            for r in range(back):
                top = jnp.where(rows8 == r, prev[8 - back + r:8 - back + r + 1], top)
            term = jnp.concatenate([top, shifted[8:]], axis=0) * cw[A_CONV - 1 - back:A_CONV - back]
            acc = term if acc is None else acc + term
        return _silu(acc + xb * cw[A_CONV - 1:A_CONV])

    half = rb // 2
    assert half % 8 == 0

    def gdn_piece(i):
        cols = slice(i * 256, (i + 1) * 256)

        def finish(x):
            def first():
                qkv_ref[0, 0:half, cols] = conv_silu(x, cols, 0, half)

            def second():
                qkv_ref[0, half:rb, cols] = conv_silu(x, cols, half, half)
                tail_ref[:, cols] = x[rb - 8:rb]
            return [first, second]
        return offs["qkv"] + i * 256, finish

    def plain_piece(i):
        def finish(x):
            def store():
                z_ref[0, :, i * 256:(i + 1) * 256] = x
            return [store]
        return offs["z"] + i * 256, finish

    def single(fn):
        return lambda x: [functools.partial(fn, x)]

    def q_piece(i):
        @single
        def finish(x):
            parts = []
            for hd in range(2):
                qh = _rms(x[:, hd * B_DH:(hd + 1) * B_DH], qn_ref[...])
                parts.append(qh * ca + pltpu.roll(qh, B_DH // 2, 1) * sa)
            qr_ref[0, :, i * 256:(i + 1) * 256] = (jnp.concatenate(parts, axis=1) * Q_SCALE).astype(BF16)
        return offs["q"] + i * 256, finish

    def kv_piece():
        @single
        def finish(x):
            k = _rms(x[:, :B_DH], kn_ref[...])
            kr_ref[0] = (k * ca + pltpu.roll(k, B_DH // 2, 1) * sa).astype(BF16)
            v_ref[0] = x[:, B_DH:].astype(BF16)
        return offs["kv"], finish

    def iq_piece(i):
        @single
        def finish(x):
            parts = []
            for hp in range(2):
                xh = x[:, hp * LANES:(hp + 1) * LANES]
                parts.append(xh * ci + pltpu.roll(xh, LANES - IDX_DIM // 2, 1) * slo
                             + pltpu.roll(xh, IDX_DIM // 2, 1) * shi)
            iqr_ref[0, :, i * 256:(i + 1) * 256] = jnp.concatenate(parts, axis=1).astype(BF16)
        return offs["iq"] + i * 256, finish

    def small_ik_piece():
        @single
        def finish(x):
            sm_ref[0] = x[:, :LANES]
            ik = _rms(x[:, LANES:], kin_ref[...])
            ik = ik * ci + pltpu.roll(ik, IDX_DIM // 2, 1) * (slo + shi)
            ike_ref[0] = jnp.where(lane < IDX_DIM, ik, 0.0).astype(BF16)
            iko_ref[0] = jnp.where(lane >= IDX_DIM, ik, 0.0).astype(BF16)
        return offs["sm_ik"], finish

    pieces = [q_piece(0), gdn_piece(0), q_piece(1), gdn_piece(1), iq_piece(0), gdn_piece(2), iq_piece(1),
              gdn_piece(3), kv_piece(), gdn_piece(4), small_ik_piece(), gdn_piece(5), plain_piece(0), plain_piece(1)]
    ready, staged = [], []
    for col, finish in pieces:
        x = _dot(n, w_ref[:, col:col + 256])
        ready += staged
        staged = finish(x)
        for task in [ready.pop(0) for _ in range(min(1, len(ready)))]:
            task()
    for task in ready + staged:
        task()


def _proj_call(h, meta, gain, w, tables, qn, kn, kin, cw, rb, tp, from_frames):
    bsz, _, d = h.shape
    nt = tp // rb
    row = lambda w_, dt: (jax.ShapeDtypeStruct((bsz, tp, w_), dt),
                          pl.BlockSpec((1, rb, w_), lambda b, t: (b, t, 0)))
    outs = [row(1536, F32), row(512, F32), row(128, F32), row(512, BF16), row(128, BF16),
            row(128, BF16), row(512, BF16), row(128, BF16), row(128, BF16)]
    tab_spec = pl.BlockSpec((rb, LANES), lambda b, t: (t, 0))
    return pl.pallas_call(
        functools.partial(_proj_body, rb=rb, from_frames=from_frames),
        grid=(bsz, nt),
        in_specs=[_stream_spec(rb, d, from_frames), _const_spec(meta.shape), _const_spec((1, d)),
                  _const_spec(w.shape)]
                 + [tab_spec] * 5 + [_const_spec((1, LANES))] * 3 + [_const_spec(cw.shape)],
        out_specs=[o[1] for o in outs],
        out_shape=[o[0] for o in outs],
        scratch_shapes=[pltpu.VMEM((8, PROJ_WIDTHS[0]), F32)],
        compiler_params=_params(("parallel", "arbitrary")),
        name="proj",
    )(h, meta, gain, w, *tables, qn, kn, kin, cw)


def _split3(x):
    x1 = x.astype(BF16)
    r1 = x - x1.astype(F32)
    x2 = r1.astype(BF16)
    x3 = (r1 - x2.astype(F32)).astype(BF16)
    return jnp.concatenate([x1, x2, x3], axis=0)


def _gdn_body(qkv_ref, z_ref, sm_ref, alog_ref, dtb_ref, ng_ref, lt3_ref, expand_ref,
              out_ref, s01_ref, s23_ref, u_ref, wq_ref, qkd_ref, kdt_ref, egl_ref, *, rb):
    t = pl.program_id(1)

    @pl.when(t == 0)
    def _():
        s01_ref[...] = jnp.zeros_like(s01_ref)
        s23_ref[...] = jnp.zeros_like(s23_ref)

    hw = A_HEADS * A_DK
    pw = A_HEADS * CHUNK
    ii = lax.broadcasted_iota(I32, (CHUNK, pw), 0)
    jj = lax.broadcasted_iota(I32, (CHUNK, pw), 1) & (CHUNK - 1)
    colhead = lax.broadcasted_iota(I32, (CHUNK, pw), 1) >> 6
    eye_p = ii == jj
    bd_mask = (lax.broadcasted_iota(I32, (pw, pw), 0) >> 6) == (lax.broadcasted_iota(I32, (pw, pw), 1) >> 6)
    bdk_mask = (lax.broadcasted_iota(I32, (pw, hw), 0) >> 6) == (lax.broadcasted_iota(I32, (pw, hw), 1) >> 7)
    pair_mask = (lax.broadcasted_iota(I32, (pw, pw), 0) >> 7) == (lax.broadcasted_iota(I32, (pw, pw), 1) >> 7)
    lane = lax.broadcasted_iota(I32, (1, LANES), 1)
    rows64 = lax.broadcasted_iota(I32, (CHUNK, 1), 0)
    neg_a = -jnp.exp(alog_ref[...])
    dtb = dtb_ref[...]
    ng = ng_ref[...]

    def block_diag(xp):
        return jnp.where(bd_mask, jnp.concatenate([xp.astype(BF16)] * A_HEADS, axis=0), 0)

    def stack_heads(xp):
        return jnp.concatenate([jnp.where(colhead == hd, xp, 0.0) for hd in range(A_HEADS)], axis=0)

    def l2n(x):
        parts = []
        for hd in range(A_HEADS):
            xh = x[:, hd * A_DK:(hd + 1) * A_DK]
            parts.append(xh * lax.rsqrt(jnp.sum(xh * xh, axis=-1, keepdims=True) + EPS))
        return jnp.concatenate(parts, axis=1)

    zeros = jnp.zeros((CHUNK, 2 * A_DK), F32)

    def pair_lanes(x, a_, b_):
        return jnp.concatenate([x[a_ * CHUNK:(a_ + 1) * CHUNK], x[b_ * CHUNK:(b_ + 1) * CHUNK]], axis=1)

    def local_load(c):
        r0 = c * CHUNK if isinstance(c, int) else pl.multiple_of(c * CHUNK, CHUNK)
        return qkv_ref[0, pl.ds(r0, CHUNK), :], sm_ref[0, pl.ds(r0, CHUNK), :], r0

    def local_compute(xq, sm, r0):
        q = l2n(xq[:, :hw]) * (A_DK ** -0.5)
        k = l2n(xq[:, hw:2 * hw])
        v = xq[:, 2 * hw:]

        g = jnp.where(lane < A_HEADS, neg_a * _softplus(sm + dtb), 0.0)
        g = jnp.where(t * rb + r0 + rows64 >= PAD, g, 0.0)
        to_packed, to_wide = expand_ref[:, :pw], expand_ref[:, pw:]
        g3 = _split3(g)
        cg = _dot(g3, to_packed).astype(BF16)
        gcs = _dot(lt3_ref[...], g3)
        yield
        gp = _dot(lt3_ref[...], cg)
        in_head = lane < A_HEADS
        beta = jnp.where(in_head, pltpu.roll(_sigmoid(sm), LANES - A_HEADS, 1), 0.0)
        eg_s = jnp.where(in_head, jnp.exp(gcs), 0.0)
        to_last = jnp.where(in_head, jnp.exp(gcs[CHUNK - 1:CHUNK] - gcs), 0.0)
        wide = _dot(jnp.concatenate([beta, beta * eg_s, eg_s, to_last], axis=0).astype(BF16), to_wide)
        bp = _dot(beta.astype(BF16), to_packed)
        glast3 = _dot(_split3(gcs[CHUNK - 16:]), to_wide)
        yield
        bq, bq_eg, eg, kd_scale = (wide[i * CHUNK:(i + 1) * CHUNK] for i in range(4))
        glast = glast3[15:16] + glast3[31:32] + glast3[47:48]
        grow = jnp.sum(jnp.where(eye_p, gp, 0.0), axis=0, keepdims=True)
        decay = jnp.where(ii >= jj, jnp.exp(gp - grow), 0.0)

        kb = k.astype(BF16)
        bdk = jnp.where(bdk_mask, jnp.concatenate([kb] * A_HEADS, axis=0), 0)
        kq = _dot_nt(jnp.concatenate([kb, q.astype(BF16)], axis=0), bdk)
        yield
        kk_p, qk_p = kq[:CHUNK], kq[CHUNK:]

        a = -jnp.where(ii > jj, bp * kk_p * decay, 0.0)
        tinv = jnp.where(eye_p, 1.0, 0.0) + a
        pk = _dot(a.astype(BF16), block_diag(a))
        yield
        for _ in range(4):
            pt = _dot(jnp.concatenate([pk, tinv], axis=0).astype(BF16), block_diag(pk))
            yield
            tinv = tinv + pt[CHUNK:]
            pk = pt[:CHUNK]
        tinv = tinv + _dot(tinv.astype(BF16), block_diag(pk))
        yield

        vb = v * bq
        kbg = k * bq_eg
        rv = jnp.concatenate(
            [jnp.concatenate([vb[:, hd * A_DK:(hd + 1) * A_DK], kbg[:, hd * A_DK:(hd + 1) * A_DK]], axis=1)
             for hd in range(A_HEADS)], axis=0)
        uw = _dot(stack_heads(tinv).astype(BF16), rv.astype(BF16))
        yield
        u, w = uw[:, :A_DK], uw[:, A_DK:]

        qg = q * eg
        kd = k * kd_scale
        return (u,
                jnp.concatenate([pair_lanes(w, 0, 1), qg[:, :2 * A_DK]], axis=0).astype(BF16),
                jnp.concatenate([pair_lanes(w, 2, 3), qg[:, 2 * A_DK:]], axis=0).astype(BF16),
                stack_heads(qk_p * decay).astype(BF16),
                jnp.concatenate([kd[:, :2 * A_DK], zeros], axis=0).T.astype(BF16),
                jnp.concatenate([kd[:, 2 * A_DK:], zeros], axis=0).T.astype(BF16),
                jnp.broadcast_to(jnp.exp(glast), (8, hw)))

    def slot(c):
        return c % GDN_SLOTS if isinstance(c, int) else c & (GDN_SLOTS - 1)

    def local_store(c, vals):
        s = slot(c)
        u_ref[s], wq_ref[s, 0], wq_ref[s, 1], qkd_ref[s], kdt_ref[s, 0], kdt_ref[s, 1], egl_ref[s] = vals

    def local_chain(c):
        vals = yield from local_compute(*local_load(c))
        local_store(c, vals)

    def state_chain(chunks):
        def unpair(r):
            return [r[:, :A_DK], r[:, A_DK:]]

        def state_update(s_old, kdt, vn_pair, egl_pair):
            upd = _dot(kdt, jnp.concatenate([vn_pair, zeros], axis=0).astype(BF16))
            return s_old * egl_pair + jnp.where(pair_mask, upd, 0.0)

        s01 = s01_ref[...]
        s23 = s23_ref[...]
        for c in chunks:
            r0 = c * CHUNK if isinstance(c, int) else pl.multiple_of(c * CHUNK, CHUNK)
            s = slot(c)
            r01 = _dot(wq_ref[s, 0], s01.astype(BF16))
            r23 = _dot(wq_ref[s, 1], s23.astype(BF16))
            yield
            ws = jnp.concatenate(unpair(r01[:CHUNK]) + unpair(r23[:CHUNK]), axis=0)
            qs = jnp.concatenate(unpair(r01[CHUNK:]) + unpair(r23[CHUNK:]), axis=0)
            vn = u_ref[s] - ws
            o_rs = qs + _dot(qkd_ref[s], vn.astype(BF16))
            egl = egl_ref[s][0:1, :]
            s01 = state_update(s01, kdt_ref[s, 0], pair_lanes(vn, 0, 1), egl[:, :2 * A_DK])
            s23 = state_update(s23, kdt_ref[s, 1], pair_lanes(vn, 2, 3), egl[:, 2 * A_DK:])
            yield
            zz = z_ref[0, pl.ds(r0, CHUNK), :]
            parts = [_rms(o_rs[hd * CHUNK:(hd + 1) * CHUNK], ng) for hd in range(A_HEADS)]
            out_ref[0, pl.ds(r0, CHUNK), :] = jnp.concatenate(parts, axis=1) * _silu(zz)
        s01_ref[...] = s01
        s23_ref[...] = s23

    def run_lockstep(chains):
        live = list(chains)
        while live:
            for gen in list(live):
                try:
                    next(gen)
                except StopIteration:
                    live.remove(gen)

    nc = rb // CHUNK
    npairs = nc // 2
    if npairs == 0:
        run_lockstep([local_chain(0)])
        run_lockstep([state_chain([0])])
    else:
        run_lockstep([local_chain(0), local_chain(1)])

        def steady(i, carry):
            run_lockstep([local_chain(2 * i), local_chain(2 * i + 1), state_chain([2 * i - 2, 2 * i - 1])])
            return carry

        lax.fori_loop(1, npairs, steady, 0)
        last = [2 * npairs - 2, 2 * npairs - 1]
        if nc % 2:
            run_lockstep([local_chain(nc - 1), state_chain(last)])
            run_lockstep([state_chain([nc - 1])])
        else:
            run_lockstep([state_chain(last)])


def _gdn_call(qkv, z, sm, alog, dtb, ng, lt3, expand):
    bsz, tp, _ = qkv.shape
    rb = tp
    nt = tp // rb
    nc = GDN_SLOTS
    hw = A_HEADS * A_DK
    pw = A_HEADS * CHUNK
    row = lambda w_: pl.BlockSpec((1, rb, w_), lambda b, t: (b, t, 0))
    return pl.pallas_call(
        functools.partial(_gdn_body, rb=rb),
        grid=(bsz, nt),
        in_specs=[row(3 * hw), row(hw), row(LANES), _const_spec((1, LANES)),
                  _const_spec((1, LANES)), _const_spec((1, LANES)), _const_spec(lt3.shape),
                  _const_spec(expand.shape)],
        out_specs=row(hw),
        out_shape=jax.ShapeDtypeStruct((bsz, tp, hw), F32),
        scratch_shapes=[pltpu.VMEM((2 * A_DK, 2 * A_DK), F32), pltpu.VMEM((2 * A_DK, 2 * A_DK), F32),
                        pltpu.VMEM((nc, pw, A_DK), F32), pltpu.VMEM((nc, 2, 2 * CHUNK, 2 * A_DK), BF16),
                        pltpu.VMEM((nc, pw, pw), BF16), pltpu.VMEM((nc, 2, 2 * A_DK, 2 * CHUNK), BF16),
                        pltpu.VMEM((nc, 8, hw), F32)],
        compiler_params=_params(("parallel", "arbitrary")),
        name="gdn",
    )(qkv, z, sm, alog, dtb, ng, lt3, expand)


def _dsa_body(qr_ref, kr_ref, v_ref, iqr_ref, ike_ref, iko_ref, sm_ref, lstrict_ref, out_ref,
              vt_ref, vmt_ref, skey_ref, planes_ref, sel_ref, st_ref, ot_ref, *, seq, topk):
    g = pl.program_id(1)
    iw_scale = IDX_HEADS ** -0.5 * IDX_DIM ** -0.5
    qpair = 2 * CHUNK
    neg_inf = -jnp.inf

    @pl.when(g == 0)
    def _():
        planes_ref[...] = jnp.zeros_like(planes_ref)

        def vblk(i, carry):
            r = pl.multiple_of(CHUNK + i * LANES, CHUNK)
            vt_ref[i] = v_ref[0, pl.ds(r, LANES), :].astype(F32).T.astype(BF16)
            return carry

        lax.fori_loop(0, seq // LANES, vblk, 0)
        vmt_ref[...] = v_ref[0, 0:LANES, :].astype(F32).T.astype(BF16)

        qm = qr_ref[0, 0:CHUNK, :]
        km = kr_ref[0, 0:CHUNK, :]
        vm = v_ref[0, 0:CHUNK, :]
        colv = lax.broadcasted_iota(I32, (1, CHUNK), 1) >= PAD
        parts = []
        for hd in range(B_HEADS):
            s = _dot_nt(qm[:, hd * B_DH:(hd + 1) * B_DH], km)
            s = jnp.where(colv, s, neg_inf)
            e = jnp.exp2(s - jnp.max(s, axis=-1, keepdims=True))
            pr = e / jnp.sum(e, axis=-1, keepdims=True)
            parts.append(_dot(pr.astype(BF16), vm))
        out_ref[0, 0:CHUNK, :] = jnp.concatenate(parts, axis=1)

    lane = lax.broadcasted_iota(I32, (1, LANES), 1)
    groups = []
    for j in range(QUERY_GROUPS):
        p = g * QUERY_GROUPS + j
        r0 = pl.multiple_of(CHUNK + p * qpair, CHUNK)
        qb = qr_ref[0, pl.ds(r0, qpair), :]
        ib = iqr_ref[0, pl.ds(r0, qpair), :]
        smt = sm_ref[0, pl.ds(r0, qpair), :].T
        groups.append(dict(
            j=j, r0=r0,
            q_hq=jnp.concatenate([qb[:, hd * B_DH:(hd + 1) * B_DH] for hd in range(B_HEADS)], axis=0),
            iq_hq=jnp.concatenate([ib[:, hp * LANES:(hp + 1) * LANES] for hp in range(IDX_HEADS // 2)], axis=0),
            wts=[smt[2 * A_HEADS + hd:2 * A_HEADS + hd + 1, :] * iw_scale for hd in range(IDX_HEADS)],
            limit=jnp.where(lane < CHUNK, (2 * p + 1) * CHUNK, (2 * p + 2) * CHUNK)))
    assert QUERY_GROUPS * qpair == KEY_CHUNK
    full_rows = [KEY_CHUNK] * QUERY_GROUPS
    diag_rows = [(j + 1) * qpair for j in range(QUERY_GROUPS)]
    diag_plane_rows = [-(-r // PLANE_ROWS) * PLANE_ROWS for r in diag_rows]
    krows = lax.broadcasted_iota(I32, (KEY_CHUNK, 1), 0)

    def score_chunk(kc, carry, nrows=full_rows):
        k0 = pl.multiple_of(CHUNK + kc * KEY_CHUNK, CHUNK)
        base = pl.multiple_of(kc * KEY_CHUNK, KEY_CHUNK)
        ke, ko, kk = (ref[0, pl.ds(k0, KEY_CHUNK), :] for ref in (ike_ref, iko_ref, kr_ref))
        logits = []
        for grp_, nr in zip(groups, nrows):
            logits.append((_dot_nt(ke[:nr], grp_["iq_hq"]), _dot_nt(ko[:nr], grp_["iq_hq"])))
            st_ref[grp_["j"], pl.ds(base, nr), :] = _dot_nt(kk[:nr], grp_["q_hq"])
        for grp_, nr, (le, lo) in zip(groups, nrows, logits):
            score_tail(kc, pl.ds(base, nr), nr, grp_, le, lo)
        return carry

    def score_tail(kc, rows, nr, grp_, le, lo):
        j, wts, limit = grp_["j"], grp_["wts"], grp_["limit"]
        sc = jnp.zeros((nr, LANES), F32)
        for hp in range(IDX_HEADS // 2):
            sc = sc + wts[2 * hp] * jnp.maximum(le[:, hp * LANES:(hp + 1) * LANES], 0.0)
            sc = sc + wts[2 * hp + 1] * jnp.maximum(lo[:, hp * LANES:(hp + 1) * LANES], 0.0)
        bits = lax.bitcast_convert_type(sc, I32)
        bits = jnp.where(sc == 0.0, 0, bits)
        key = bits ^ ((bits >> 31) & 0x7FFFFFFF)
        valid = kc * KEY_CHUNK + krows[:nr] < limit
        key = jnp.where(valid, key, INT_MIN)
        skey_ref[j, rows, :] = key
        for gi in range(nr // PLANE_ROWS):
            a = [key[gi * PLANE_ROWS + 8 * r:gi * PLANE_ROWS + 8 * r + 8, :] ^ INT_MIN for r in range(32)]
            sh, msk = 16, 0x0000FFFF
            while sh:
                k = 0
                while k < 32:
                    tt = (a[k] ^ lax.shift_right_logical(a[k + sh], jnp.int32(sh))) & msk
                    a[k] = a[k] ^ tt
                    a[k + sh] = a[k + sh] ^ (tt << sh)
                    k = (k + sh + 1) & ~sh
                sh >>= 1
                msk = (msk ^ (msk << sh)) & 0xFFFFFFFF
            g8 = pl.multiple_of((kc * (KEY_CHUNK // PLANE_ROWS) + gi) * 8, 8)
            for b in range(32):
                planes_ref[j, b, pl.ds(g8, 8), :] = a[b]

    lax.fori_loop(0, g, score_chunk, 0)
    score_chunk(g, 0, diag_plane_rows)

    words_per_chunk = KEY_CHUNK // 32

    def lane_count(words):
        c = lax.population_count(words)
        if words.shape[0] > 8:
            c = jnp.sum(c.reshape(words.shape[0] // 8, 8, LANES), axis=0)
        return jnp.sum(c, axis=0, keepdims=True)

    def select_threshold(step):
        nrow = (step + 1) * words_per_chunk
        prow = lax.broadcasted_iota(I32, (nrow, 1), 0)
        alive0 = [jnp.where(prow < step * words_per_chunk + nr // 32, jnp.full((nrow, LANES), -1, I32), 0)
                  for nr in diag_plane_rows]

        def bit_body(i, carry):
            bit = jnp.left_shift(jnp.int32(1), 31 - i)
            out = []
            for grp_, (tu, above, alive) in zip(groups, carry):
                hit = alive & planes_ref[grp_["j"], i, 0:nrow, :]
                c1 = lane_count(hit)
                take = above + c1 >= topk
                out.append((jnp.where(take, tu | bit, tu), jnp.where(take, above, above + c1),
                            jnp.where(take, hit, alive ^ hit)))
            return tuple(out)

        zero_row = jnp.zeros((1, LANES), I32)
        result = lax.fori_loop(0, 32, bit_body, tuple((zero_row, zero_row, a0) for a0 in alive0))
        for grp_, (tu, above, alive) in zip(groups, result):
            for n, row in enumerate((tu, above, lane_count(alive))):
                sel_ref[grp_["j"], n] = jnp.broadcast_to(row, (8, LANES))

    for step in range(seq // KEY_CHUNK):
        pl.when(g == step)(functools.partial(select_threshold, step))

    for grp_, nr_diag in zip(groups, diag_rows):
        tu, above, n_eq = (sel_ref[grp_["j"], n][0:1] for n in range(3))
        j, thr, found = grp_["j"], tu ^ INT_MIN, tu != 0
        grp_["thr"] = thr
        tie_lane = found & (above + n_eq > topk)

        @pl.when(jnp.max(jnp.where(tie_lane, 1, 0)) > 0)
        def _(j=j, thr=thr, found=found, above=above, nr_diag=nr_diag):
            need = (topk - above).astype(F32)

            def body(kc, seen, nr=KEY_CHUNK):
                base = pl.multiple_of(kc * KEY_CHUNK, KEY_CHUNK)
                for sub in range(nr // LANES):
                    rows = pl.ds(base + sub * LANES, LANES)
                    blk = skey_ref[j, rows, :]
                    tied = (blk == thr) & found
                    eq = jnp.where(tied, 1.0, 0.0)
                    rank = _dot(lstrict_ref[...], eq.astype(BF16)) + seen
                    skey_ref[j, rows, :] = jnp.where(tied & (rank >= need), INT_MIN, blk)
                    seen = seen + jnp.sum(eq, axis=0, keepdims=True)
                return seen

            body(g, lax.fori_loop(0, g, body, jnp.zeros((1, LANES), F32)), nr_diag)

    mrows = lax.broadcasted_iota(I32, (LANES, 1), 0)
    k_meta = kr_ref[0, 0:LANES, :]
    ones = jnp.ones((ONES_ROWS, LANES), BF16)
    v_meta = jnp.concatenate([vmt_ref[...], ones], axis=0)
    init = []
    for grp_ in groups:
        s_meta = _dot_nt(k_meta, grp_["q_hq"])
        s_meta = jnp.where((mrows >= PAD) & (mrows < CHUNK), s_meta, neg_inf)
        m0 = jnp.max(s_meta, axis=0, keepdims=True)
        ot_ref[grp_["j"]] = _dot(v_meta, jnp.exp2(s_meta - m0).astype(BF16))
        init.append(m0)
        grp_["thr"] = jnp.maximum(grp_["thr"], INT_MIN + 1)

    def pv_chunk(kc, carry, nrows=full_rows):
        base = pl.multiple_of(kc * KEY_CHUNK, KEY_CHUNK)
        nblk = KEY_CHUNK // LANES
        vblk = jnp.concatenate([jnp.concatenate([vt_ref[nblk * kc + i], ones], axis=0) for i in range(nblk)], axis=1)
        out, probs = [], []
        for grp_, nr, m in zip(groups, nrows, carry):
            rows = pl.ds(base, nr)
            bias = jnp.where(skey_ref[grp_["j"], rows, :] >= grp_["thr"], 0.0, neg_inf)
            s = st_ref[grp_["j"], rows, :] + jnp.concatenate([bias] * B_HEADS, axis=1)
            m_new = jnp.maximum(m, jnp.max(jnp.max(s.reshape(nr // 8, 8, B_HEADS * LANES), axis=0),
                                           axis=0, keepdims=True))
            out.append(m_new)
            probs.append((jnp.exp2(m - m_new), jnp.exp2(s - m_new).astype(BF16)))
        for grp_, nr, (alpha, prb) in zip(groups, nrows, probs):
            ot_ref[grp_["j"]] = ot_ref[grp_["j"]] * alpha + _dot(vblk[:, :nr], prb)
        return tuple(out)

    pv_chunk(g, lax.fori_loop(0, g, pv_chunk, tuple(init)), diag_rows)
    for grp_ in groups:
        acc = ot_ref[grp_["j"]]
        o_hq = (acc[:B_DH] / acc[B_DH:B_DH + 1]).T
        for hd in range(B_HEADS):
            out_ref[0, pl.ds(grp_["r0"], qpair), hd * B_DH:(hd + 1) * B_DH] = o_hq[hd * LANES:(hd + 1) * LANES, :]


def _dsa_call(qr, kr, v, iqr, ike, iko, sm, lstrict, seq, topk):
    bsz, tp, _ = qr.shape
    nsteps, rem = divmod(seq, 2 * CHUNK * QUERY_GROUPS)
    assert rem == 0
    full = lambda w_: pl.BlockSpec((1, tp, w_), lambda b, p: (b, 0, 0))
    hw = B_HEADS * B_DH
    nq = QUERY_GROUPS
    return pl.pallas_call(
        functools.partial(_dsa_body, seq=seq, topk=topk),
        grid=(bsz, nsteps),
        in_specs=[full(hw), full(B_DH), full(B_DH), full(hw), full(LANES), full(LANES), full(LANES),
                  _const_spec(lstrict.shape)],
        out_specs=full(hw),
        out_shape=jax.ShapeDtypeStruct((bsz, tp, hw), F32),
        scratch_shapes=[pltpu.VMEM((seq // LANES, B_DH, LANES), BF16), pltpu.VMEM((B_DH, LANES), BF16),
                        pltpu.VMEM((nq, seq, LANES), I32), pltpu.VMEM((nq, 32, seq // 32, LANES), I32),
                        pltpu.VMEM((nq, 3, 8, LANES), I32),
                        pltpu.VMEM((nq, seq, hw), F32), pltpu.VMEM((nq, B_DH + ONES_ROWS, hw), F32)],
        compiler_params=_params(("parallel", "arbitrary")),
        name="dsa",
    )(qr, kr, v, iqr, ike, iko, sm, lstrict)


def _mix_body(h_ref, meta_ref, oa_ref, ob_ref, g_ref, wg_ref, bg_ref, wa_ref, wb_ref, wo_ref, out_ref,
              *, rb, from_frames):
    x = _stream_block(h_ref, meta_ref, rb, from_frames)
    d = x.shape[-1]
    n = _rms(x, g_ref[...]).astype(BF16)
    gates = _sigmoid(_dot(n, wg_ref[...]) + bg_ref[...])
    y = gates[:, :d] * _dot(oa_ref[0].astype(BF16), wa_ref[...]) + gates[:, d:] * _dot(ob_ref[0].astype(BF16), wb_ref[...])
    out = x + _dot(y.astype(BF16), wo_ref[...])
    rows = pl.program_id(1) * rb + lax.broadcasted_iota(I32, (rb, 1), 0)
    out_ref[0] = jnp.where(rows >= PAD, out, 0.0)


def _mix_call(h, meta, oa, ob, gain, wg, bg, wa, wb, wo, rb, from_frames):
    bsz, tp, _ = oa.shape
    d = h.shape[-1]
    row = lambda w_: pl.BlockSpec((1, rb, w_), lambda b, t: (b, t, 0))
    return pl.pallas_call(
        functools.partial(_mix_body, rb=rb, from_frames=from_frames),
        grid=(bsz, tp // rb),
        in_specs=[_stream_spec(rb, d, from_frames), _const_spec(meta.shape), row(oa.shape[-1]),
                  row(ob.shape[-1]), _const_spec((1, d)), _const_spec(wg.shape),
                  _const_spec(bg.shape), _const_spec(wa.shape), _const_spec(wb.shape), _const_spec(wo.shape)],
        out_specs=row(d),
        out_shape=jax.ShapeDtypeStruct((bsz, tp, d), F32),
        compiler_params=_params(("parallel", "arbitrary")),
        name="mix",
    )(h, meta, oa, ob, gain, wg, bg, wa, wb, wo)


def _ffn_body(h_ref, g_ref, wup_ref, cw_ref, wd_ref, out_ref, carry_ref, act_ref, *, rb, dff, frames_only):
    t = pl.program_id(1)

    @pl.when(t == 0)
    def _():
        carry_ref[...] = jnp.zeros_like(carry_ref)

    if frames_only:
        rb = rb + 8
    x = h_ref[0]
    n = _rms(x, g_ref[...]).astype(BF16)
    rows = lax.broadcasted_iota(I32, (rb, 1), 0)
    nch = dff // FF_CHUNK

    def gate_up(c):
        return (_dot(n, wup_ref[:, c * FF_CHUNK:(c + 1) * FF_CHUNK]),
                _dot(n, wup_ref[:, dff + c * FF_CHUNK:dff + (c + 1) * FF_CHUNK]))

    ahead = gate_up(0)
    for c in range(nch):
        cols = slice(c * FF_CHUNK, (c + 1) * FF_CHUNK)
        gate, up = ahead
        if c + 1 < nch:
            ahead = gate_up(c + 1)
        g1, g2 = pltpu.roll(gate, 1, 0), pltpu.roll(gate, 2, 0)
        if not frames_only:
            prev = carry_ref[:, cols]
            g1 = jnp.where(rows == 0, prev[7:8], g1)
            g2 = jnp.where(rows == 0, prev[6:7], jnp.where(rows == 1, prev[7:8], g2))
            carry_ref[:, cols] = gate[rb - 8:rb]
        cw = cw_ref[:, cols]
        conv = g2 * cw[0:1] + g1 * cw[1:2] + gate * cw[2:3]
        act_ref[:, cols] = (_silu(conv) * up).astype(BF16)
    out = x + _dot(act_ref[...], wd_ref[...])
    if frames_only:
        out_ref[0] = out[8:]
    else:
        out_ref[0] = jnp.where(t * rb + rows >= PAD, out, 0.0)


def _ffn_call(h, gain, wup, cw, wd, rb, frames_only=False):
    bsz, tp, d = h.shape
    dff = wd.shape[0]
    if frames_only:
        rb = FRAME_BLOCK
        nrows = tp - CHUNK
        in_rows = pl.BlockSpec((pl.Element(1), pl.Element(rb + 8), pl.Element(d)),
                               lambda b, t: (b, pl.multiple_of(CHUNK - 8 + t * rb, 8), 0))
    else:
        nrows = tp
        in_rows = pl.BlockSpec((1, rb, d), lambda b, t: (b, t, 0))
    return pl.pallas_call(
        functools.partial(_ffn_body, rb=rb, dff=dff, frames_only=frames_only),
        grid=(bsz, nrows // rb),
        in_specs=[in_rows, _const_spec((1, d)), _const_spec(wup.shape), _const_spec(cw.shape),
                  _const_spec(wd.shape)],
        out_specs=pl.BlockSpec((1, rb, d), lambda b, t: (b, t, 0)),
        out_shape=jax.ShapeDtypeStruct((bsz, nrows, d), F32),
        scratch_shapes=[pltpu.VMEM((8, dff), F32), pltpu.VMEM((rb + 8 * frames_only, dff), BF16)],
        compiler_params=_params(("parallel", "arbitrary")),
        name="ffn",
    )(h, gain, wup, cw, wd)


def _rope_tables(tp):
    pos = (jnp.arange(tp, dtype=F32) - PAD)[:, None]

    def cs(dim):
        inv = 1.0 / (ROPE_THETA ** (jnp.arange(0, dim, 2, dtype=F32) / dim))
        ang = pos * inv[None, :]
        return jnp.cos(ang), jnp.sin(ang)

    ca, sa = cs(B_DH)
    ci, si = cs(IDX_DIM)
    zi = jnp.zeros_like(si)
    return (jnp.concatenate([ca, ca], axis=1), jnp.concatenate([-sa, sa], axis=1),
            jnp.concatenate([ci] * 4, axis=1), jnp.concatenate([-si, zi, -si, zi], axis=1),
            jnp.concatenate([zi, si, zi, si], axis=1))


def _pad_lanes(vec):
    return jnp.zeros((1, LANES), F32).at[0, :vec.shape[0]].set(vec.astype(F32))


def _head_expander():
    pw, hw = A_HEADS * CHUNK, A_HEADS * A_DK
    head = np.concatenate([np.arange(pw) // CHUNK, np.arange(hw) // A_DK])
    return jnp.asarray(np.arange(LANES)[:, None] == head[None, :], BF16)


def kernel(x, meta_tokens, norm_mix, w_in, conv_a, a_log, dt_bias, a_out_norm, q_norm, k_norm, kidx_norm,
           w_branch_a, w_branch_b, w_gate, b_gate, w_out, norm_ffn, w_up, conv_ffn, w_down):
    bsz, seq, d = x.shape
    depth = w_in.shape[0]
    assert seq % KEY_CHUNK == 0
    tp = CHUNK + seq
    rb = _row_block(tp // CHUNK)
    topk = min(TOPK_MAX, seq // TOPK_DIV)

    meta = meta_tokens.astype(x.dtype)
    from_frames = rb <= seq
    if from_frames:
        h = x
    else:
        h = jnp.concatenate([jnp.zeros((bsz, PAD, d), x.dtype), jnp.broadcast_to(meta[None], (bsz, N_META, d)), x],
                            axis=1)

    tables = _rope_tables(tp)
    expand = _head_expander()
    tri = np.arange(CHUNK)
    lt3 = jnp.asarray(np.tile(tri[:, None] >= tri[None, :], (1, 3)), BF16)
    trk = np.arange(LANES)
    lstrict = jnp.asarray(trk[:, None] > trk[None, :], BF16)

    hw = A_HEADS * A_DK
    o = np.cumsum((0, hw, hw, hw, hw, A_HEADS, A_HEADS, B_HEADS * B_DH, B_DH, B_DH,
                   IDX_HEADS * IDX_DIM, IDX_DIM, IDX_HEADS))
    for l in range(depth):
        w = w_in[l]
        small = jnp.concatenate([w[:, o[4]:o[6]], w[:, o[11]:o[12]],
                                 jnp.zeros((d, LANES - 2 * A_HEADS - IDX_HEADS), w.dtype)], axis=1)
        w_p = jnp.concatenate([w[:, o[0]:o[3]], w[:, o[3]:o[4]], w[:, o[6]:o[7]], w[:, o[7]:o[9]],
                               w[:, o[9]:o[10]], small, w[:, o[10]:o[11]], w[:, o[10]:o[11]]], axis=1).astype(BF16)
        qkv, z, sm, qr, kr, v, iqr, ike, iko = _proj_call(
            h, meta, norm_mix[l][None], w_p, tables, q_norm[l][None], k_norm[l][None],
            jnp.concatenate([kidx_norm[l], kidx_norm[l]])[None], conv_a[l], rb, tp, from_frames)
        o_a = _gdn_call(qkv, z, sm, _pad_lanes(a_log[l]), _pad_lanes(dt_bias[l]),
                        a_out_norm[l][None], lt3, expand)
        o_b = _dsa_call(qr, kr, v, iqr, ike, iko, sm, lstrict, seq, topk)
        h = _mix_call(h, meta, o_a, o_b, norm_mix[l][None], w_gate[l].astype(BF16), b_gate[l][None],
                      w_branch_a[l].astype(BF16), w_branch_b[l].astype(BF16), w_out[l].astype(BF16), rb,
                      from_frames)
        from_frames = False
        h = _ffn_call(h, norm_ffn[l][None], w_up[l].astype(BF16), conv_ffn[l], w_down[l].astype(BF16), rb,
                      frames_only=l == depth - 1)
    return h
```

```python
import functools
import math

import jax
import jax.numpy as jnp
import numpy as np
from jax import lax
from jax.experimental import pallas as pl
from jax.experimental.pallas import tpu as pltpu

F32 = jnp.float32
BF16 = jnp.bfloat16
I32 = jnp.int32

CHUNK = 64
N_META = 16
PAD = CHUNK - N_META
ROPE_THETA = 10000.0
EPS = 1e-6
A_HEADS = 4
A_DK = 128
A_CONV = 4
B_HEADS = 4
B_DH = 128
IDX_HEADS = 8
IDX_DIM = 64
TOPK_MAX = 256
TOPK_DIV = 4
FFN_CONV = 3
LANES = 128
KEY_CHUNK = 512
PLANE_ROWS = 256
QUERY_GROUPS = 4
ONES_ROWS = 16
FF_CHUNK = 256
FRAME_BLOCK = 1024
GDN_SLOTS = 4
Q_SCALE = B_DH ** -0.5 * math.log2(math.e)
INT_MIN = -(2 ** 31)
VMEM_LIMIT_BYTES = 56 * 1024 * 1024

NT_DIMS = (((1,), (1,)), ((), ()))


def _dot(a, b):
    return jnp.dot(a, b, preferred_element_type=F32)


def _dot_nt(a, b):
    return lax.dot_general(a, b, NT_DIMS, preferred_element_type=F32)


def _dot_exact(a, b):
    return jnp.dot(a, b, preferred_element_type=F32, precision=lax.Precision.HIGHEST)


def _rms(x, gain):
    return x * lax.rsqrt(jnp.mean(x * x, axis=-1, keepdims=True) + EPS) * gain


def _sigmoid(x):
    return 0.5 * jnp.tanh(0.5 * x) + 0.5


def _silu(x):
    return x * _sigmoid(x)


def _softplus(x):
    return jnp.maximum(x, 0.0) + jnp.log(1.0 + jnp.exp(-jnp.abs(x)))


def _row_block(n_chunks):
    g = max(d for d in range(1, 12) if n_chunks % d == 0)
    return g * CHUNK


def _params(sem):
    return pltpu.CompilerParams(dimension_semantics=sem, vmem_limit_bytes=VMEM_LIMIT_BYTES)


def _const_spec(shape):
    nd = len(shape)
    return pl.BlockSpec(shape, lambda *_: (0,) * nd, pipeline_mode=pl.Buffered(1))


def _stream_spec(rb, d, from_frames):
    if not from_frames:
        return pl.BlockSpec((1, rb, d), lambda b, t: (b, t, 0))
    return pl.BlockSpec((pl.Element(1), pl.Element(rb), pl.Element(d)),
                        lambda b, t: (b, pl.multiple_of(jnp.maximum(t * rb - CHUNK, 0), 8), 0))


def _stream_block(h_ref, meta_ref, rb, from_frames):
    x = h_ref[0]
    if not from_frames:
        return x
    first = jnp.concatenate([jnp.zeros((PAD, x.shape[-1]), x.dtype), meta_ref[...], x[:rb - CHUNK]], axis=0)
    return jnp.where(pl.program_id(1) == 0, first, x)


PROJ_WIDTHS = (1536, 512, 512, 256, 512, 256)


def _proj_body(h_ref, meta_ref, g_ref, w_ref, cosa_ref, sina_ref, cosi_ref, sinlo_ref, sinhi_ref,
               qn_ref, kn_ref, kin_ref, cw_ref,
               qkv_ref, z_ref, sm_ref, qr_ref, kr_ref, v_ref, iqr_ref, ike_ref, iko_ref, tail_ref,
               *, rb, from_frames):
    @pl.when(pl.program_id(1) == 0)
    def _():
        tail_ref[...] = jnp.zeros_like(tail_ref)

    n = _rms(_stream_block(h_ref, meta_ref, rb, from_frames), g_ref[...]).astype(BF16)
    offs = dict(zip(("qkv", "z", "q", "kv", "iq", "sm_ik"), np.cumsum((0,) + PROJ_WIDTHS)[:-1]))
    rows8 = lax.broadcasted_iota(I32, (8, 1), 0)
    lane = lax.broadcasted_iota(I32, (1, LANES), 1)
    ca, sa = cosa_ref[...], sina_ref[...]
    ci, slo, shi = cosi_ref[...], sinlo_ref[...], sinhi_ref[...]

    def conv_silu(x, cols, r0, nrows):
        prev, cw = (tail_ref[:, cols] if r0 == 0 else x[r0 - 8:r0]), cw_ref[:, cols]
        xb = x[r0:r0 + nrows]
        acc = None
        for back in range(A_CONV - 1, 0, -1):
            shifted = pltpu.roll(xb, back, 0)
            top = shifted[:8]
            for r in range(back):
                top = jnp.where(rows8 == r, prev[8 - back + r:8 - back + r + 1], top)
            term = jnp.concatenate([top, shifted[8:]], axis=0) * cw[A_CONV - 1 - back:A_CONV - back]
            acc = term if acc is None else acc + term
        return _silu(acc + xb * cw[A_CONV - 1:A_CONV])

    half = rb // 2
    assert half % 8 == 0

    def gdn_piece(i):
        cols = slice(i * 256, (i + 1) * 256)

        def finish(x):
            def first():
                qkv_ref[0, 0:half, cols] = conv_silu(x, cols, 0, half)

            def second():
                qkv_ref[0, half:rb, cols] = conv_silu(x, cols, half, half)
                tail_ref[:, cols] = x[rb - 8:rb]
            return [first, second]
        return offs["qkv"] + i * 256, finish

    def plain_piece(i):
        def finish(x):
            def store():
                z_ref[0, :, i * 256:(i + 1) * 256] = x
            return [store]
        return offs["z"] + i * 256, finish

    def single(fn):
        return lambda x: [functools.partial(fn, x)]

    def q_piece(i):
        @single
        def finish(x):
            parts = []
            for hd in range(2):
                qh = _rms(x[:, hd * B_DH:(hd + 1) * B_DH], qn_ref[...])
                parts.append(qh * ca + pltpu.roll(qh, B_DH // 2, 1) * sa)
            qr_ref[0, :, i * 256:(i + 1) * 256] = (jnp.concatenate(parts, axis=1) * Q_SCALE).astype(BF16)
        return offs["q"] + i * 256, finish

    def kv_piece():
        @single
        def finish(x):
            k = _rms(x[:, :B_DH], kn_ref[...])
            kr_ref[0] = (k * ca + pltpu.roll(k, B_DH // 2, 1) * sa).astype(BF16)
            v_ref[0] = x[:, B_DH:].astype(BF16)
        return offs["kv"], finish

    def iq_piece(i):
        @single
        def finish(x):
            parts = []
            for hp in range(2):
                xh = x[:, hp * LANES:(hp + 1) * LANES]
                parts.append(xh * ci + pltpu.roll(xh, LANES - IDX_DIM // 2, 1) * slo
                             + pltpu.roll(xh, IDX_DIM // 2, 1) * shi)
            iqr_ref[0, :, i * 256:(i + 1) * 256] = jnp.concatenate(parts, axis=1).astype(BF16)
        return offs["iq"] + i * 256, finish

    def small_ik_piece():
        @single
        def finish(x):
            sm_ref[0] = x[:, :LANES]
            ik = _rms(x[:, LANES:], kin_ref[...])
            ik = ik * ci + pltpu.roll(ik, IDX_DIM // 2, 1) * (slo + shi)
            ike_ref[0] = jnp.where(lane < IDX_DIM, ik, 0.0).astype(BF16)
            iko_ref[0] = jnp.where(lane >= IDX_DIM, ik, 0.0).astype(BF16)
        return offs["sm_ik"], finish

    pieces = [q_piece(0), gdn_piece(0), q_piece(1), gdn_piece(1), iq_piece(0), gdn_piece(2), iq_piece(1),
              gdn_piece(3), kv_piece(), gdn_piece(4), small_ik_piece(), gdn_piece(5), plain_piece(0), plain_piece(1)]
    ready, staged = [], []
    for col, finish in pieces:
        x = _dot(n, w_ref[:, col:col + 256])
        ready += staged
        staged = finish(x)
        for task in [ready.pop(0) for _ in range(min(1, len(ready)))]:
            task()
    for task in ready + staged:
        task()


def _proj_call(h, meta, gain, w, tables, qn, kn, kin, cw, rb, tp, from_frames):
    bsz, _, d = h.shape
    nt = tp // rb
    row = lambda w_, dt: (jax.ShapeDtypeStruct((bsz, tp, w_), dt),
                          pl.BlockSpec((1, rb, w_), lambda b, t: (b, t, 0)))
    outs = [row(1536, F32), row(512, F32), row(128, F32), row(512, BF16), row(128, BF16),
            row(128, BF16), row(512, BF16), row(128, BF16), row(128, BF16)]
    tab_spec = pl.BlockSpec((rb, LANES), lambda b, t: (t, 0))
    return pl.pallas_call(
        functools.partial(_proj_body, rb=rb, from_frames=from_frames),
        grid=(bsz, nt),
        in_specs=[_stream_spec(rb, d, from_frames), _const_spec(meta.shape), _const_spec((1, d)),
                  _const_spec(w.shape)]
                 + [tab_spec] * 5 + [_const_spec((1, LANES))] * 3 + [_const_spec(cw.shape)],
        out_specs=[o[1] for o in outs],
        out_shape=[o[0] for o in outs],
        scratch_shapes=[pltpu.VMEM((8, PROJ_WIDTHS[0]), F32)],
        compiler_params=_params(("parallel", "arbitrary")),
        name="proj",
    )(h, meta, gain, w, *tables, qn, kn, kin, cw)


def _split3(x):
    x1 = x.astype(BF16)
    r1 = x - x1.astype(F32)
    x2 = r1.astype(BF16)
    x3 = (r1 - x2.astype(F32)).astype(BF16)
    return jnp.concatenate([x1, x2, x3], axis=0)


def _gdn_body(qkv_ref, z_ref, sm_ref, alog_ref, dtb_ref, ng_ref, lt3_ref, expand_ref,
              out_ref, s01_ref, s23_ref, u_ref, wq_ref, qkd_ref, kdt_ref, egl_ref, *, rb):
    t = pl.program_id(1)

    @pl.when(t == 0)
    def _():
        s01_ref[...] = jnp.zeros_like(s01_ref)
        s23_ref[...] = jnp.zeros_like(s23_ref)

    hw = A_HEADS * A_DK
    pw = A_HEADS * CHUNK
    ii = lax.broadcasted_iota(I32, (CHUNK, pw), 0)
    jj = lax.broadcasted_iota(I32, (CHUNK, pw), 1) & (CHUNK - 1)
    colhead = lax.broadcasted_iota(I32, (CHUNK, pw), 1) >> 6
    eye_p = ii == jj
    bd_mask = (lax.broadcasted_iota(I32, (pw, pw), 0) >> 6) == (lax.broadcasted_iota(I32, (pw, pw), 1) >> 6)
    bdk_mask = (lax.broadcasted_iota(I32, (pw, hw), 0) >> 6) == (lax.broadcasted_iota(I32, (pw, hw), 1) >> 7)
    pair_mask = (lax.broadcasted_iota(I32, (pw, pw), 0) >> 7) == (lax.broadcasted_iota(I32, (pw, pw), 1) >> 7)
    lane = lax.broadcasted_iota(I32, (1, LANES), 1)
    rows64 = lax.broadcasted_iota(I32, (CHUNK, 1), 0)
    neg_a = -jnp.exp(alog_ref[...])
    dtb = dtb_ref[...]
    ng = ng_ref[...]

    def block_diag(xp):
        return jnp.where(bd_mask, jnp.concatenate([xp.astype(BF16)] * A_HEADS, axis=0), 0)

    def stack_heads(xp):
        return jnp.concatenate([jnp.where(colhead == hd, xp, 0.0) for hd in range(A_HEADS)], axis=0)

    def l2n(x):
        parts = []
        for hd in range(A_HEADS):
            xh = x[:, hd * A_DK:(hd + 1) * A_DK]
            parts.append(xh * lax.rsqrt(jnp.sum(xh * xh, axis=-1, keepdims=True) + EPS))
        return jnp.concatenate(parts, axis=1)

    zeros = jnp.zeros((CHUNK, 2 * A_DK), F32)

    def pair_lanes(x, a_, b_):
        return jnp.concatenate([x[a_ * CHUNK:(a_ + 1) * CHUNK], x[b_ * CHUNK:(b_ + 1) * CHUNK]], axis=1)

    def local_load(c):
        r0 = c * CHUNK if isinstance(c, int) else pl.multiple_of(c * CHUNK, CHUNK)
        return qkv_ref[0, pl.ds(r0, CHUNK), :], sm_ref[0, pl.ds(r0, CHUNK), :], r0

    def local_compute(xq, sm, r0):
        q = l2n(xq[:, :hw]) * (A_DK ** -0.5)
        k = l2n(xq[:, hw:2 * hw])
        v = xq[:, 2 * hw:]

        g = jnp.where(lane < A_HEADS, neg_a * _softplus(sm + dtb), 0.0)
        g = jnp.where(t * rb + r0 + rows64 >= PAD, g, 0.0)
        to_packed, to_wide = expand_ref[:, :pw], expand_ref[:, pw:]
        g3 = _split3(g)
        cg = _dot(g3, to_packed).astype(BF16)
        gcs = _dot(lt3_ref[...], g3)
        yield
        gp = _dot(lt3_ref[...], cg)
        in_head = lane < A_HEADS
        beta = jnp.where(in_head, pltpu.roll(_sigmoid(sm), LANES - A_HEADS, 1), 0.0)
        eg_s = jnp.where(in_head, jnp.exp(gcs), 0.0)
        to_last = jnp.where(in_head, jnp.exp(gcs[CHUNK - 1:CHUNK] - gcs), 0.0)
        wide = _dot(jnp.concatenate([beta, beta * eg_s, eg_s, to_last], axis=0).astype(BF16), to_wide)
        bp = _dot(beta.astype(BF16), to_packed)
        glast3 = _dot(_split3(gcs[CHUNK - 16:]), to_wide)
        yield
        bq, bq_eg, eg, kd_scale = (wide[i * CHUNK:(i + 1) * CHUNK] for i in range(4))
        glast = glast3[15:16] + glast3[31:32] + glast3[47:48]
        grow = jnp.sum(jnp.where(eye_p, gp, 0.0), axis=0, keepdims=True)
        decay = jnp.where(ii >= jj, jnp.exp(gp - grow), 0.0)

        kb = k.astype(BF16)
        bdk = jnp.where(bdk_mask, jnp.concatenate([kb] * A_HEADS, axis=0), 0)
        kq = _dot_nt(jnp.concatenate([kb, q.astype(BF16)], axis=0), bdk)
        yield
        kk_p, qk_p = kq[:CHUNK], kq[CHUNK:]

        a = -jnp.where(ii > jj, bp * kk_p * decay, 0.0)
        tinv = jnp.where(eye_p, 1.0, 0.0) + a
        pk = _dot(a.astype(BF16), block_diag(a))
        yield
        for _ in range(4):
            pt = _dot(jnp.concatenate([pk, tinv], axis=0).astype(BF16), block_diag(pk))
            yield
            tinv = tinv + pt[CHUNK:]
            pk = pt[:CHUNK]
        tinv = tinv + _dot(tinv.astype(BF16), block_diag(pk))
        yield

        vb = v * bq
        kbg = k * bq_eg
        rv = jnp.concatenate(
            [jnp.concatenate([vb[:, hd * A_DK:(hd + 1) * A_DK], kbg[:, hd * A_DK:(hd + 1) * A_DK]], axis=1)
             for hd in range(A_HEADS)], axis=0)
        uw = _dot(stack_heads(tinv).astype(BF16), rv.astype(BF16))
        yield
        u, w = uw[:, :A_DK], uw[:, A_DK:]

        qg = q * eg
        kd = k * kd_scale
        return (u,
                jnp.concatenate([pair_lanes(w, 0, 1), qg[:, :2 * A_DK]], axis=0).astype(BF16),
                jnp.concatenate([pair_lanes(w, 2, 3), qg[:, 2 * A_DK:]], axis=0).astype(BF16),
                stack_heads(qk_p * decay).astype(BF16),
                jnp.concatenate([kd[:, :2 * A_DK], zeros], axis=0).T.astype(BF16),
                jnp.concatenate([kd[:, 2 * A_DK:], zeros], axis=0).T.astype(BF16),
                jnp.broadcast_to(jnp.exp(glast), (8, hw)))

    def slot(c):
        return c % GDN_SLOTS if isinstance(c, int) else c & (GDN_SLOTS - 1)

    def local_store(c, vals):
        s = slot(c)
        u_ref[s], wq_ref[s, 0], wq_ref[s, 1], qkd_ref[s], kdt_ref[s, 0], kdt_ref[s, 1], egl_ref[s] = vals

    def local_chain(c):
        vals = yield from local_compute(*local_load(c))
        local_store(c, vals)

    def state_chain(chunks):
        def unpair(r):
            return [r[:, :A_DK], r[:, A_DK:]]

        def state_update(s_old, kdt, vn_pair, egl_pair):
            upd = _dot(kdt, jnp.concatenate([vn_pair, zeros], axis=0).astype(BF16))
            return s_old * egl_pair + jnp.where(pair_mask, upd, 0.0)

        s01 = s01_ref[...]
        s23 = s23_ref[...]
        for c in chunks:
            r0 = c * CHUNK if isinstance(c, int) else pl.multiple_of(c * CHUNK, CHUNK)
            s = slot(c)
            r01 = _dot(wq_ref[s, 0], s01.astype(BF16))
            r23 = _dot(wq_ref[s, 1], s23.astype(BF16))
            yield
            ws = jnp.concatenate(unpair(r01[:CHUNK]) + unpair(r23[:CHUNK]), axis=0)
            qs = jnp.concatenate(unpair(r01[CHUNK:]) + unpair(r23[CHUNK:]), axis=0)
            vn = u_ref[s] - ws
            o_rs = qs + _dot(qkd_ref[s], vn.astype(BF16))
            egl = egl_ref[s][0:1, :]
            s01 = state_update(s01, kdt_ref[s, 0], pair_lanes(vn, 0, 1), egl[:, :2 * A_DK])
            s23 = state_update(s23, kdt_ref[s, 1], pair_lanes(vn, 2, 3), egl[:, 2 * A_DK:])
            yield
            zz = z_ref[0, pl.ds(r0, CHUNK), :]
            parts = [_rms(o_rs[hd * CHUNK:(hd + 1) * CHUNK], ng) for hd in range(A_HEADS)]
            out_ref[0, pl.ds(r0, CHUNK), :] = jnp.concatenate(parts, axis=1) * _silu(zz)
        s01_ref[...] = s01
        s23_ref[...] = s23

    def run_lockstep(chains):
        live = list(chains)
        while live:
            for gen in list(live):
                try:
                    next(gen)
                except StopIteration:
                    live.remove(gen)

    nc = rb // CHUNK
    npairs = nc // 2
    if npairs == 0:
        run_lockstep([local_chain(0)])
        run_lockstep([state_chain([0])])
    else:
        run_lockstep([local_chain(0), local_chain(1)])

        def steady(i, carry):
            run_lockstep([local_chain(2 * i), local_chain(2 * i + 1), state_chain([2 * i - 2, 2 * i - 1])])
            return carry

        lax.fori_loop(1, npairs, steady, 0)
        last = [2 * npairs - 2, 2 * npairs - 1]
        if nc % 2:
            run_lockstep([local_chain(nc - 1), state_chain(last)])
            run_lockstep([state_chain([nc - 1])])
        else:
            run_lockstep([state_chain(last)])


def _gdn_call(qkv, z, sm, alog, dtb, ng, lt3, expand):
    bsz, tp, _ = qkv.shape
    rb = tp
    nt = tp // rb
    nc = GDN_SLOTS
    hw = A_HEADS * A_DK
    pw = A_HEADS * CHUNK
    row = lambda w_: pl.BlockSpec((1, rb, w_), lambda b, t: (b, t, 0))
    return pl.pallas_call(
        functools.partial(_gdn_body, rb=rb),
        grid=(bsz, nt),
        in_specs=[row(3 * hw), row(hw), row(LANES), _const_spec((1, LANES)),
                  _const_spec((1, LANES)), _const_spec((1, LANES)), _const_spec(lt3.shape),
                  _const_spec(expand.shape)],
        out_specs=row(hw),
        out_shape=jax.ShapeDtypeStruct((bsz, tp, hw), F32),
        scratch_shapes=[pltpu.VMEM((2 * A_DK, 2 * A_DK), F32), pltpu.VMEM((2 * A_DK, 2 * A_DK), F32),
                        pltpu.VMEM((nc, pw, A_DK), F32), pltpu.VMEM((nc, 2, 2 * CHUNK, 2 * A_DK), BF16),
                        pltpu.VMEM((nc, pw, pw), BF16), pltpu.VMEM((nc, 2, 2 * A_DK, 2 * CHUNK), BF16),
                        pltpu.VMEM((nc, 8, hw), F32)],
        compiler_params=_params(("parallel", "arbitrary")),
        name="gdn",
    )(qkv, z, sm, alog, dtb, ng, lt3, expand)


def _dsa_body(qr_ref, kr_ref, v_ref, iqr_ref, ike_ref, iko_ref, sm_ref, lstrict_ref, out_ref,
              vt_ref, vmt_ref, skey_ref, planes_ref, sel_ref, st_ref, ot_ref, *, seq, topk):
    g = pl.program_id(1)
    iw_scale = IDX_HEADS ** -0.5 * IDX_DIM ** -0.5
    qpair = 2 * CHUNK
    neg_inf = -jnp.inf

    @pl.when(g == 0)
    def _():
        planes_ref[...] = jnp.zeros_like(planes_ref)

        def vblk(i, carry):
            r = pl.multiple_of(CHUNK + i * LANES, CHUNK)
            vt_ref[i] = v_ref[0, pl.ds(r, LANES), :].astype(F32).T.astype(BF16)
            return carry

        lax.fori_loop(0, seq // LANES, vblk, 0)
        vmt_ref[...] = v_ref[0, 0:LANES, :].astype(F32).T.astype(BF16)

        qm = qr_ref[0, 0:CHUNK, :]
        km = kr_ref[0, 0:CHUNK, :]
        vm = v_ref[0, 0:CHUNK, :]
        colv = lax.broadcasted_iota(I32, (1, CHUNK), 1) >= PAD
        parts = []
        for hd in range(B_HEADS):
            s = _dot_nt(qm[:, hd * B_DH:(hd + 1) * B_DH], km)
            s = jnp.where(colv, s, neg_inf)
            e = jnp.exp2(s - jnp.max(s, axis=-1, keepdims=True))
            pr = e / jnp.sum(e, axis=-1, keepdims=True)
            parts.append(_dot(pr.astype(BF16), vm))
        out_ref[0, 0:CHUNK, :] = jnp.concatenate(parts, axis=1)

    lane = lax.broadcasted_iota(I32, (1, LANES), 1)
    groups = []
    for j in range(QUERY_GROUPS):
        p = g * QUERY_GROUPS + j
        r0 = pl.multiple_of(CHUNK + p * qpair, CHUNK)
        qb = qr_ref[0, pl.ds(r0, qpair), :]
        ib = iqr_ref[0, pl.ds(r0, qpair), :]
        smt = sm_ref[0, pl.ds(r0, qpair), :].T
        groups.append(dict(
            j=j, r0=r0,
            q_hq=jnp.concatenate([qb[:, hd * B_DH:(hd + 1) * B_DH] for hd in range(B_HEADS)], axis=0),
            iq_hq=jnp.concatenate([ib[:, hp * LANES:(hp + 1) * LANES] for hp in range(IDX_HEADS // 2)], axis=0),
            wts=[smt[2 * A_HEADS + hd:2 * A_HEADS + hd + 1, :] * iw_scale for hd in range(IDX_HEADS)],
            limit=jnp.where(lane < CHUNK, (2 * p + 1) * CHUNK, (2 * p + 2) * CHUNK)))
    assert QUERY_GROUPS * qpair == KEY_CHUNK
    full_rows = [KEY_CHUNK] * QUERY_GROUPS
    diag_rows = [(j + 1) * qpair for j in range(QUERY_GROUPS)]
    diag_plane_rows = [-(-r // PLANE_ROWS) * PLANE_ROWS for r in diag_rows]
    krows = lax.broadcasted_iota(I32, (KEY_CHUNK, 1), 0)

    def score_chunk(kc, carry, nrows=full_rows):
        k0 = pl.multiple_of(CHUNK + kc * KEY_CHUNK, CHUNK)
        base = pl.multiple_of(kc * KEY_CHUNK, KEY_CHUNK)
        ke, ko, kk = (ref[0, pl.ds(k0, KEY_CHUNK), :] for ref in (ike_ref, iko_ref, kr_ref))
        logits = []
        for grp_, nr in zip(groups, nrows):
            logits.append((_dot_nt(ke[:nr], grp_["iq_hq"]), _dot_nt(ko[:nr], grp_["iq_hq"])))
            st_ref[grp_["j"], pl.ds(base, nr), :] = _dot_nt(kk[:nr], grp_["q_hq"])
        for grp_, nr, (le, lo) in zip(groups, nrows, logits):
            score_tail(kc, pl.ds(base, nr), nr, grp_, le, lo)
        return carry

    def score_tail(kc, rows, nr, grp_, le, lo):
        j, wts, limit = grp_["j"], grp_["wts"], grp_["limit"]
        sc = jnp.zeros((nr, LANES), F32)
        for hp in range(IDX_HEADS // 2):
            sc = sc + wts[2 * hp] * jnp.maximum(le[:, hp * LANES:(hp + 1) * LANES], 0.0)
            sc = sc + wts[2 * hp + 1] * jnp.maximum(lo[:, hp * LANES:(hp + 1) * LANES], 0.0)
        bits = lax.bitcast_convert_type(sc, I32)
        bits = jnp.where(sc == 0.0, 0, bits)
        key = bits ^ ((bits >> 31) & 0x7FFFFFFF)
        valid = kc * KEY_CHUNK + krows[:nr] < limit
        key = jnp.where(valid, key, INT_MIN)
        skey_ref[j, rows, :] = key
        for gi in range(nr // PLANE_ROWS):
            a = [key[gi * PLANE_ROWS + 8 * r:gi * PLANE_ROWS + 8 * r + 8, :] ^ INT_MIN for r in range(32)]
            sh, msk = 16, 0x0000FFFF
            while sh:
                k = 0
                while k < 32:
                    tt = (a[k] ^ lax.shift_right_logical(a[k + sh], jnp.int32(sh))) & msk
                    a[k] = a[k] ^ tt
                    a[k + sh] = a[k + sh] ^ (tt << sh)
                    k = (k + sh + 1) & ~sh
                sh >>= 1
                msk = (msk ^ (msk << sh)) & 0xFFFFFFFF
            g8 = pl.multiple_of((kc * (KEY_CHUNK // PLANE_ROWS) + gi) * 8, 8)
            for b in range(32):
                planes_ref[j, b, pl.ds(g8, 8), :] = a[b]

    lax.fori_loop(0, g, score_chunk, 0)
    score_chunk(g, 0, diag_plane_rows)

    words_per_chunk = KEY_CHUNK // 32

    def lane_count(words):
        c = lax.population_count(words)
        if words.shape[0] > 8:
            c = jnp.sum(c.reshape(words.shape[0] // 8, 8, LANES), axis=0)
        return jnp.sum(c, axis=0, keepdims=True)

    def select_threshold(step):
        nrow = (step + 1) * words_per_chunk
        prow = lax.broadcasted_iota(I32, (nrow, 1), 0)
        alive0 = [jnp.where(prow < step * words_per_chunk + nr // 32, jnp.full((nrow, LANES), -1, I32), 0)
                  for nr in diag_plane_rows]

        def bit_body(i, carry):
            bit = jnp.left_shift(jnp.int32(1), 31 - i)
            out = []
            for grp_, (tu, above, alive) in zip(groups, carry):
                hit = alive & planes_ref[grp_["j"], i, 0:nrow, :]
                c1 = lane_count(hit)
                take = above + c1 >= topk
                out.append((jnp.where(take, tu | bit, tu), jnp.where(take, above, above + c1),
                            jnp.where(take, hit, alive ^ hit)))
            return tuple(out)

        zero_row = jnp.zeros((1, LANES), I32)
        result = lax.fori_loop(0, 32, bit_body, tuple((zero_row, zero_row, a0) for a0 in alive0))
        for grp_, (tu, above, alive) in zip(groups, result):
            for n, row in enumerate((tu, above, lane_count(alive))):
                sel_ref[grp_["j"], n] = jnp.broadcast_to(row, (8, LANES))

    for step in range(seq // KEY_CHUNK):
        pl.when(g == step)(functools.partial(select_threshold, step))

    for grp_, nr_diag in zip(groups, diag_rows):
        tu, above, n_eq = (sel_ref[grp_["j"], n][0:1] for n in range(3))
        j, thr, found = grp_["j"], tu ^ INT_MIN, tu != 0
        grp_["thr"] = thr
        tie_lane = found & (above + n_eq > topk)

        @pl.when(jnp.max(jnp.where(tie_lane, 1, 0)) > 0)
        def _(j=j, thr=thr, found=found, above=above, nr_diag=nr_diag):
            need = (topk - above).astype(F32)

            def body(kc, seen, nr=KEY_CHUNK):
                base = pl.multiple_of(kc * KEY_CHUNK, KEY_CHUNK)
                for sub in range(nr // LANES):
                    rows = pl.ds(base + sub * LANES, LANES)
                    blk = skey_ref[j, rows, :]
                    tied = (blk == thr) & found
                    eq = jnp.where(tied, 1.0, 0.0)
                    rank = _dot(lstrict_ref[...], eq.astype(BF16)) + seen
                    skey_ref[j, rows, :] = jnp.where(tied & (rank >= need), INT_MIN, blk)
                    seen = seen + jnp.sum(eq, axis=0, keepdims=True)
                return seen

            body(g, lax.fori_loop(0, g, body, jnp.zeros((1, LANES), F32)), nr_diag)

    mrows = lax.broadcasted_iota(I32, (LANES, 1), 0)
    k_meta = kr_ref[0, 0:LANES, :]
    ones = jnp.ones((ONES_ROWS, LANES), BF16)
    v_meta = jnp.concatenate([vmt_ref[...], ones], axis=0)
    init = []
    for grp_ in groups:
        s_meta = _dot_nt(k_meta, grp_["q_hq"])
        s_meta = jnp.where((mrows >= PAD) & (mrows < CHUNK), s_meta, neg_inf)
        m0 = jnp.max(s_meta, axis=0, keepdims=True)
        ot_ref[grp_["j"]] = _dot(v_meta, jnp.exp2(s_meta - m0).astype(BF16))
        init.append(m0)
        grp_["thr"] = jnp.maximum(grp_["thr"], INT_MIN + 1)

    def pv_chunk(kc, carry, nrows=full_rows):
        base = pl.multiple_of(kc * KEY_CHUNK, KEY_CHUNK)
        nblk = KEY_CHUNK // LANES
        vblk = jnp.concatenate([jnp.concatenate([vt_ref[nblk * kc + i], ones], axis=0) for i in range(nblk)], axis=1)
        out, probs = [], []
        for grp_, nr, m in zip(groups, nrows, carry):
            rows = pl.ds(base, nr)
            bias = jnp.where(skey_ref[grp_["j"], rows, :] >= grp_["thr"], 0.0, neg_inf)
            s = st_ref[grp_["j"], rows, :] + jnp.concatenate([bias] * B_HEADS, axis=1)
            m_new = jnp.maximum(m, jnp.max(jnp.max(s.reshape(nr // 8, 8, B_HEADS * LANES), axis=0),
                                           axis=0, keepdims=True))
            out.append(m_new)
            probs.append((jnp.exp2(m - m_new), jnp.exp2(s - m_new).astype(BF16)))
        for grp_, nr, (alpha, prb) in zip(groups, nrows, probs):
            ot_ref[grp_["j"]] = ot_ref[grp_["j"]] * alpha + _dot(vblk[:, :nr], prb)
        return tuple(out)

    pv_chunk(g, lax.fori_loop(0, g, pv_chunk, tuple(init)), diag_rows)
    for grp_ in groups:
        acc = ot_ref[grp_["j"]]
        o_hq = (acc[:B_DH] / acc[B_DH:B_DH + 1]).T
        for hd in range(B_HEADS):
            out_ref[0, pl.ds(grp_["r0"], qpair), hd * B_DH:(hd + 1) * B_DH] = o_hq[hd * LANES:(hd + 1) * LANES, :]


def _dsa_call(qr, kr, v, iqr, ike, iko, sm, lstrict, seq, topk):
    bsz, tp, _ = qr.shape
    nsteps, rem = divmod(seq, 2 * CHUNK * QUERY_GROUPS)
    assert rem == 0
    full = lambda w_: pl.BlockSpec((1, tp, w_), lambda b, p: (b, 0, 0))
    hw = B_HEADS * B_DH
    nq = QUERY_GROUPS
    return pl.pallas_call(
        functools.partial(_dsa_body, seq=seq, topk=topk),
        grid=(bsz, nsteps),
        in_specs=[full(hw), full(B_DH), full(B_DH), full(hw), full(LANES), full(LANES), full(LANES),
                  _const_spec(lstrict.shape)],
        out_specs=full(hw),
        out_shape=jax.ShapeDtypeStruct((bsz, tp, hw), F32),
        scratch_shapes=[pltpu.VMEM((seq // LANES, B_DH, LANES), BF16), pltpu.VMEM((B_DH, LANES), BF16),
                        pltpu.VMEM((nq, seq, LANES), I32), pltpu.VMEM((nq, 32, seq // 32, LANES), I32),
                        pltpu.VMEM((nq, 3, 8, LANES), I32),
                        pltpu.VMEM((nq, seq, hw), F32), pltpu.VMEM((nq, B_DH + ONES_ROWS, hw), F32)],
        compiler_params=_params(("parallel", "arbitrary")),
        name="dsa",
    )(qr, kr, v, iqr, ike, iko, sm, lstrict)


def _mix_body(h_ref, meta_ref, oa_ref, ob_ref, g_ref, wg_ref, bg_ref, wa_ref, wb_ref, wo_ref, out_ref,
              *, rb, from_frames):
    x = _stream_block(h_ref, meta_ref, rb, from_frames)
    d = x.shape[-1]
    n = _rms(x, g_ref[...]).astype(BF16)
    gates = _sigmoid(_dot(n, wg_ref[...]) + bg_ref[...])
    y = gates[:, :d] * _dot(oa_ref[0].astype(BF16), wa_ref[...]) + gates[:, d:] * _dot(ob_ref[0].astype(BF16), wb_ref[...])
    out = x + _dot(y.astype(BF16), wo_ref[...])
    rows = pl.program_id(1) * rb + lax.broadcasted_iota(I32, (rb, 1), 0)
    out_ref[0] = jnp.where(rows >= PAD, out, 0.0)


def _mix_call(h, meta, oa, ob, gain, wg, bg, wa, wb, wo, rb, from_frames):
    bsz, tp, _ = oa.shape
    d = h.shape[-1]
    row = lambda w_: pl.BlockSpec((1, rb, w_), lambda b, t: (b, t, 0))
    return pl.pallas_call(
        functools.partial(_mix_body, rb=rb, from_frames=from_frames),
        grid=(bsz, tp // rb),
        in_specs=[_stream_spec(rb, d, from_frames), _const_spec(meta.shape), row(oa.shape[-1]),
                  row(ob.shape[-1]), _const_spec((1, d)), _const_spec(wg.shape),
                  _const_spec(bg.shape), _const_spec(wa.shape), _const_spec(wb.shape), _const_spec(wo.shape)],
        out_specs=row(d),
        out_shape=jax.ShapeDtypeStruct((bsz, tp, d), F32),
        compiler_params=_params(("parallel", "arbitrary")),
        name="mix",
    )(h, meta, oa, ob, gain, wg, bg, wa, wb, wo)


def _ffn_body(h_ref, g_ref, wup_ref, cw_ref, wd_ref, out_ref, carry_ref, act_ref, *, rb, dff, frames_only):
    t = pl.program_id(1)

    @pl.when(t == 0)
    def _():
        carry_ref[...] = jnp.zeros_like(carry_ref)

    if frames_only:
        rb = rb + 8
    x = h_ref[0]
    n = _rms(x, g_ref[...]).astype(BF16)
    rows = lax.broadcasted_iota(I32, (rb, 1), 0)
    nch = dff // FF_CHUNK

    def gate_up(c):
        return (_dot(n, wup_ref[:, c * FF_CHUNK:(c + 1) * FF_CHUNK]),
                _dot(n, wup_ref[:, dff + c * FF_CHUNK:dff + (c + 1) * FF_CHUNK]))

    ahead = gate_up(0)
    for c in range(nch):
        cols = slice(c * FF_CHUNK, (c + 1) * FF_CHUNK)
        gate, up = ahead
        if c + 1 < nch:
            ahead = gate_up(c + 1)
        g1, g2 = pltpu.roll(gate, 1, 0), pltpu.roll(gate, 2, 0)
        if not frames_only:
            prev = carry_ref[:, cols]
            g1 = jnp.where(rows == 0, prev[7:8], g1)
            g2 = jnp.where(rows == 0, prev[6:7], jnp.where(rows == 1, prev[7:8], g2))
            carry_ref[:, cols] = gate[rb - 8:rb]
        cw = cw_ref[:, cols]
        conv = g2 * cw[0:1] + g1 * cw[1:2] + gate * cw[2:3]
        act_ref[:, cols] = (_silu(conv) * up).astype(BF16)
    out = x + _dot(act_ref[...], wd_ref[...])
    if frames_only:
        out_ref[0] = out[8:]
    else:
        out_ref[0] = jnp.where(t * rb + rows >= PAD, out, 0.0)


def _ffn_call(h, gain, wup, cw, wd, rb, frames_only=False):
    bsz, tp, d = h.shape
    dff = wd.shape[0]
    if frames_only:
        nrows = tp - CHUNK
        rb = min(FRAME_BLOCK, nrows)
        in_rows = pl.BlockSpec((pl.Element(1), pl.Element(rb + 8), pl.Element(d)),
                               lambda b, t: (b, pl.multiple_of(CHUNK - 8 + t * rb, 8), 0))
    else:
        nrows = tp
        in_rows = pl.BlockSpec((1, rb, d), lambda b, t: (b, t, 0))
    return pl.pallas_call(
        functools.partial(_ffn_body, rb=rb, dff=dff, frames_only=frames_only),
        grid=(bsz, nrows // rb),
        in_specs=[in_rows, _const_spec((1, d)), _const_spec(wup.shape), _const_spec(cw.shape),
                  _const_spec(wd.shape)],
        out_specs=pl.BlockSpec((1, rb, d), lambda b, t: (b, t, 0)),
        out_shape=jax.ShapeDtypeStruct((bsz, nrows, d), F32),
        scratch_shapes=[pltpu.VMEM((8, dff), F32), pltpu.VMEM((rb + 8 * frames_only, dff), BF16)],
        compiler_params=_params(("parallel", "arbitrary")),
        name="ffn",
    )(h, gain, wup, cw, wd)


def _rope_tables(tp):
    pos = (jnp.arange(tp, dtype=F32) - PAD)[:, None]

    def cs(dim):
        inv = 1.0 / (ROPE_THETA ** (jnp.arange(0, dim, 2, dtype=F32) / dim))
        ang = pos * inv[None, :]
        return jnp.cos(ang), jnp.sin(ang)

    ca, sa = cs(B_DH)
    ci, si = cs(IDX_DIM)
    zi = jnp.zeros_like(si)
    return (jnp.concatenate([ca, ca], axis=1), jnp.concatenate([-sa, sa], axis=1),
            jnp.concatenate([ci] * 4, axis=1), jnp.concatenate([-si, zi, -si, zi], axis=1),
            jnp.concatenate([zi, si, zi, si], axis=1))


def _pad_lanes(vec):
    return jnp.zeros((1, LANES), F32).at[0, :vec.shape[0]].set(vec.astype(F32))


def _head_expander():
    pw, hw = A_HEADS * CHUNK, A_HEADS * A_DK
    head = np.concatenate([np.arange(pw) // CHUNK, np.arange(hw) // A_DK])
    return jnp.asarray(np.arange(LANES)[:, None] == head[None, :], BF16)


def kernel(x, meta_tokens, norm_mix, w_in, conv_a, a_log, dt_bias, a_out_norm, q_norm, k_norm, kidx_norm,
           w_branch_a, w_branch_b, w_gate, b_gate, w_out, norm_ffn, w_up, conv_ffn, w_down):
    bsz, seq, d = x.shape
    depth = w_in.shape[0]
    assert seq % KEY_CHUNK == 0
    tp = CHUNK + seq
    rb = _row_block(tp // CHUNK)
    topk = min(TOPK_MAX, seq // TOPK_DIV)

    meta = meta_tokens.astype(x.dtype)
    from_frames = rb <= seq
    if from_frames:
        h = x
    else:
        h = jnp.concatenate([jnp.zeros((bsz, PAD, d), x.dtype), jnp.broadcast_to(meta[None], (bsz, N_META, d)), x],
                            axis=1)

    tables = _rope_tables(tp)
    expand = _head_expander()
    tri = np.arange(CHUNK)
    lt3 = jnp.asarray(np.tile(tri[:, None] >= tri[None, :], (1, 3)), BF16)
    trk = np.arange(LANES)
    lstrict = jnp.asarray(trk[:, None] > trk[None, :], BF16)

    hw = A_HEADS * A_DK
    o = np.cumsum((0, hw, hw, hw, hw, A_HEADS, A_HEADS, B_HEADS * B_DH, B_DH, B_DH,
                   IDX_HEADS * IDX_DIM, IDX_DIM, IDX_HEADS))
    for l in range(depth):
        w = w_in[l]
        small = jnp.concatenate([w[:, o[4]:o[6]], w[:, o[11]:o[12]],
                                 jnp.zeros((d, LANES - 2 * A_HEADS - IDX_HEADS), w.dtype)], axis=1)
        w_p = jnp.concatenate([w[:, o[0]:o[3]], w[:, o[3]:o[4]], w[:, o[6]:o[7]], w[:, o[7]:o[9]],
                               w[:, o[9]:o[10]], small, w[:, o[10]:o[11]], w[:, o[10]:o[11]]], axis=1).astype(BF16)
        qkv, z, sm, qr, kr, v, iqr, ike, iko = _proj_call(
            h, meta, norm_mix[l][None], w_p, tables, q_norm[l][None], k_norm[l][None],
            jnp.concatenate([kidx_norm[l], kidx_norm[l]])[None], conv_a[l], rb, tp, from_frames)
        o_a = _gdn_call(qkv, z, sm, _pad_lanes(a_log[l]), _pad_lanes(dt_bias[l]),
                        a_out_norm[l][None], lt3, expand)
        o_b = _dsa_call(qr, kr, v, iqr, ike, iko, sm, lstrict, seq, topk)
        h = _mix_call(h, meta, o_a, o_b, norm_mix[l][None], w_gate[l].astype(BF16), b_gate[l][None],
                      w_branch_a[l].astype(BF16), w_branch_b[l].astype(BF16), w_out[l].astype(BF16), rb,
                      from_frames)
        from_frames = False
        h = _ffn_call(h, norm_ffn[l][None], w_up[l].astype(BF16), conv_ffn[l], w_down[l].astype(BF16), rb,
                      frames_only=l == depth - 1)
    return h
```
